```python
import math, functools
import jax, jax.numpy as jnp
from jax import lax
import numpy as np

D_MODEL = 1024
BATCH = 8
SEQ = 4096
DEPTH = 4

CTX_LEN = 256
GRID_W = 64
N_MIXERS = 4
D_MIX = D_MODEL
GROUP_W = D_MIX // N_MIXERS
HEAD_DIM = 64
FFN_DIM = ((8 * D_MODEL // 3 + 255) // 256) * 256
MACARON = 0.5
N_MOD = 9
NORM_EPS = 1e-6
SHORT_CONV = 3
S5_P = 16
S5_GROUPS = GROUP_W // S5_P
S5_STATE = 64
S5_DT_MIN = 1e-3
S5_DT_MAX = 1e-1
MLA_HEADS = GROUP_W // HEAD_DIM
MLA_NOPE = HEAD_DIM
MLA_ROPE = HEAD_DIM // 2
MLA_V = HEAD_DIM
MLA_Q_RANK = D_MODEL // 4
MLA_KV_RANK = D_MODEL // 8
ROPE_AXIS = MLA_ROPE // 2
ROPE_BASE = 10000.0
ATTN_BLOCK = 128
RWKV_HEADS = GROUP_W // HEAD_DIM
RWKV_DECAY_LORA = 32
RWKV_A_LORA = 32
RWKV_GATE_LORA = 64
RWKV_GN_EPS = HEAD_DIM * 1e-5
MLSTM_HEADS = GROUP_W // HEAD_DIM
MLSTM_CHUNK = 64
S5_COLS = GROUP_W
MLA_COLS = MLA_Q_RANK + MLA_KV_RANK + MLA_ROPE
RWKV_COLS = 3 * GROUP_W + RWKV_DECAY_LORA + RWKV_A_LORA + RWKV_GATE_LORA
MLSTM_COLS = 4 * GROUP_W + 4 * MLSTM_HEADS
IN_COLS = S5_COLS + MLA_COLS + RWKV_COLS + MLSTM_COLS

kernel_name = 'hybrid_parallel_head_flow_block'


def rms_norm(x, g):
    xf = x.astype(jnp.float32)
    y = xf * lax.rsqrt(jnp.mean(xf * xf, axis=-1, keepdims=True) + NORM_EPS)
    return (y * g.astype(jnp.float32)).astype(x.dtype)


def head_norm(y, heads, eps):
    B, T, W = y.shape
    yh = y.astype(jnp.float32).reshape(B, T, heads, W // heads)
    mu = jnp.mean(yh, axis=-1, keepdims=True)
    var = jnp.mean(jnp.square(yh - mu), axis=-1, keepdims=True)
    return ((yh - mu) * lax.rsqrt(var + eps)).reshape(B, T, W)


def modulate(h, shift, scale):
    return h * (1 + scale) + shift


def swiglu(h, w_gate, w_up, w_down):
    return (jax.nn.silu(h @ w_gate) * (h @ w_up)) @ w_down


def half_ffn(h, shift, scale, gate, g_pre, g_post, w_gate, w_up, w_down):
    y = swiglu(modulate(rms_norm(h, g_pre), shift, scale), w_gate, w_up, w_down)
    return MACARON * gate * rms_norm(y, g_post)


def short_conv(x, w, b):
    pad = SHORT_CONV // 2
    T = x.shape[1]
    xp = jnp.pad(x, ((0, 0), (pad, pad), (0, 0)))
    y = b
    for j in range(SHORT_CONV):
        y = y + xp[:, j:j + T] * w[j]
    return y


def axial_rope_tables(seq_len):
    rows = seq_len // GRID_W
    r_idx, c_idx = jnp.meshgrid(jnp.arange(rows), jnp.arange(GRID_W), indexing='ij')
    inv_freq = 1.0 / (ROPE_BASE ** (jnp.arange(0, ROPE_AXIS, 2, dtype=jnp.float32) / ROPE_AXIS))
    ang_r = r_idx.reshape(-1, 1).astype(jnp.float32) * inv_freq
    ang_c = c_idx.reshape(-1, 1).astype(jnp.float32) * inv_freq
    ang = jnp.concatenate([ang_r, ang_r, ang_c, ang_c], axis=-1)
    return jnp.cos(ang), jnp.sin(ang)


def apply_axial_rope(x, cos, sin):
    h = ROPE_AXIS // 2
    row, col = x[..., :ROPE_AXIS], x[..., ROPE_AXIS:]
    rot = jnp.concatenate([-row[..., h:], row[..., :h], -col[..., h:], col[..., :h]], axis=-1)
    return x * cos.astype(x.dtype) + rot * sin.astype(x.dtype)


def bidirectional(run_fwd, run_bwd, ctx_in, lat_in, h0):
    flip = lambda t: jax.tree.map(lambda a: jnp.flip(a, axis=1), t)
    yc_f, hc_f = run_fwd(ctx_in, h0)
    yl_f, _ = run_fwd(lat_in, hc_f)
    yc_b, hc_b = run_bwd(flip(ctx_in), h0)
    yl_b, _ = run_bwd(flip(lat_in), hc_b)
    return yl_f + jnp.flip(yl_b, axis=1), yc_f + jnp.flip(yc_b, axis=1)


def _complex_combine(e1, e2):
    ar1, ai1, br1, bi1 = e1
    ar2, ai2, br2, bi2 = e2
    return (ar2 * ar1 - ai2 * ai1, ar2 * ai1 + ai2 * ar1,
            ar2 * br1 - ai2 * bi1 + br2, ar2 * bi1 + ai2 * br1 + bi2)


def s5_scan(u, h0, lam_re, lam_im, log_dt, b_re, b_im, c_re, c_im):
    dt = jnp.exp(log_dt)[:, None]
    mag = jnp.exp(lam_re * dt)
    ar, ai = mag * jnp.cos(lam_im * dt), mag * jnp.sin(lam_im * dt)
    den = lam_re * lam_re + lam_im * lam_im
    fr = ((ar - 1) * lam_re + ai * lam_im) / den
    fi = (ai * lam_re - (ar - 1) * lam_im) / den
    bu_re = jnp.einsum('btgp,gnp->btgn', u, b_re)
    bu_im = jnp.einsum('btgp,gnp->btgn', u, b_im)
    xr = fr * bu_re - fi * bu_im
    xi = fr * bu_im + fi * bu_re
    h0r, h0i = h0
    xr = xr.at[:, 0].add(ar * h0r - ai * h0i)
    xi = xi.at[:, 0].add(ar * h0i + ai * h0r)
    elems = (jnp.broadcast_to(ar, xr.shape), jnp.broadcast_to(ai, xr.shape), xr, xi)
    _, _, sr, si = lax.associative_scan(_complex_combine, elems, axis=1)
    y = jnp.einsum('btgn,gpn->btgp', sr, c_re) - jnp.einsum('btgn,gpn->btgp', si, c_im)
    return y, (sr[:, -1], si[:, -1])


def s5_mixer(u_lat, u_ctx, lam_re, lam_im, log_dt, b_re, b_im, c_re, c_im, d_skip, w_glu, b_glu):
    groups = lambda u: u.astype(jnp.float32).reshape(u.shape[0], u.shape[1], S5_GROUPS, S5_P)
    runs = [functools.partial(s5_scan, lam_re=lam_re[d], lam_im=lam_im[d], log_dt=log_dt[d],
                              b_re=b_re[d], b_im=b_im[d], c_re=c_re[d], c_im=c_im[d]) for d in range(2)]
    B = u_lat.shape[0]
    h0 = (jnp.zeros((B, S5_GROUPS, S5_STATE), jnp.float32), jnp.zeros((B, S5_GROUPS, S5_STATE), jnp.float32))
    y_l, y_c = bidirectional(runs[0], runs[1], groups(u_ctx), groups(u_lat), h0)

    def finish(y, u):
        y = y.reshape(u.shape) + d_skip * u
        z = jax.nn.gelu(y)
        return z * jax.nn.sigmoid(z @ w_glu + b_glu)
    return finish(y_l, u_lat), finish(y_c, u_ctx)


def mla_mixer(z_lat, z_ctx, cos, sin, q_norm, kv_norm, w_uq, w_ukv):
    def project(z):
        B, T, _ = z.shape
        cq, ckv, k_rope = jnp.split(z, [MLA_Q_RANK, MLA_Q_RANK + MLA_KV_RANK], axis=-1)
        q = (rms_norm(cq, q_norm) @ w_uq).reshape(B, T, MLA_HEADS, MLA_NOPE + MLA_ROPE)
        kv = (rms_norm(ckv, kv_norm) @ w_ukv).reshape(B, T, MLA_HEADS, MLA_NOPE + MLA_V)
        return q, kv[..., :MLA_NOPE], kv[..., MLA_NOPE:], k_rope

    def full_key(k_nope, k_rope):
        kr = jnp.broadcast_to(k_rope[:, :, None, :], k_nope.shape[:3] + (MLA_ROPE,))
        return jnp.concatenate([k_nope, kr], axis=-1)

    scale = (MLA_NOPE + MLA_ROPE) ** -0.5

    def attend(q, k, v):
        s = jnp.einsum('bqhd,bkhd->bhqk', q, k).astype(jnp.float32) * scale
        p = jax.nn.softmax(s, axis=-1).astype(v.dtype)
        return jnp.einsum('bhqk,bkhd->bqhd', p, v)

    q_c, kn_c, v_c, kr_c = project(z_ctx)
    k_c = full_key(kn_c, kr_c)
    y_ctx = attend(q_c, k_c, v_c)
    q_l, kn_l, v_l, kr_l = project(z_lat)
    q_l = jnp.concatenate([q_l[..., :MLA_NOPE],
                           apply_axial_rope(q_l[..., MLA_NOPE:], cos[:, None], sin[:, None])], axis=-1)
    k_l = full_key(kn_l, apply_axial_rope(kr_l, cos, sin))
    k_all = jnp.concatenate([k_c, k_l], axis=1)
    v_all = jnp.concatenate([v_c, v_l], axis=1)
    B, T = q_l.shape[:2]
    nb = T // ATTN_BLOCK
    qb = jnp.moveaxis(q_l.reshape(B, nb, ATTN_BLOCK, MLA_HEADS, MLA_NOPE + MLA_ROPE), 1, 0)
    yb = lax.map(lambda qblk: attend(qblk, k_all, v_all), qb)
    y_lat = jnp.moveaxis(yb, 0, 1).reshape(B, T, MLA_HEADS * MLA_V)
    return y_lat, y_ctx.reshape(B, -1, MLA_HEADS * MLA_V)


def rwkv_direction(inp, state, w0, w_up, a0, a_up, k_a):
    r, k, v, kk, wd, ad = inp
    B, T, _ = r.shape
    w_log = -jax.nn.softplus(-(w0 + wd @ w_up)) - 0.5
    decay = jnp.exp(-jnp.exp(w_log.astype(jnp.float32)))
    a = jax.nn.sigmoid(a0 + ad @ a_up)
    k_eff = k * (1 + (a - 1) * k_a)
    heads = lambda t: jnp.moveaxis(t.astype(jnp.float32).reshape(B, T, RWKV_HEADS, HEAD_DIM), 1, 0)

    def step(S, xs):
        rt, wt, kt, vt, kkt, at = xs
        sa = jnp.einsum('bhvk,bhk->bhv', S, kkt)
        S = S * wt[:, :, None, :] - sa[..., :, None] * (kkt * at)[:, :, None, :] + vt[..., :, None] * kt[:, :, None, :]
        return S, jnp.einsum('bhvk,bhk->bhv', S, rt)
    S, y = lax.scan(step, state, tuple(heads(t) for t in (r, decay, k_eff, v, kk, a)))
    return jnp.moveaxis(y, 0, 1).reshape(B, T, GROUP_W), S


def rwkv_mixer(z_lat, z_ctx, conv_w, conv_b, w0, w_up, a0, a_up, g_up, k_k, k_a, r_k, ln_w, ln_b):
    def prepare(z):
        B, T, _ = z.shape
        o1 = 3 * GROUP_W
        rkv, wd, ad, gd = jnp.split(z, [o1, o1 + RWKV_DECAY_LORA, o1 + RWKV_DECAY_LORA + RWKV_A_LORA], axis=-1)
        r, k, v = jnp.split(short_conv(rkv, conv_w, conv_b), 3, axis=-1)
        kk = (k * k_k).astype(jnp.float32).reshape(B, T, RWKV_HEADS, HEAD_DIM)
        kk = kk / jnp.maximum(jnp.sqrt(jnp.sum(kk * kk, axis=-1, keepdims=True)), 1e-12)
        return (r, k, v, kk.reshape(B, T, GROUP_W), jnp.tanh(wd), ad), jax.nn.sigmoid(gd) @ g_up

    inp_l, g_l = prepare(z_lat)
    inp_c, g_c = prepare(z_ctx)
    runs = [functools.partial(rwkv_direction, w0=w0[d], w_up=w_up[d], a0=a0[d], a_up=a_up[d], k_a=k_a)
            for d in range(2)]
    B = z_lat.shape[0]
    s0 = jnp.zeros((B, RWKV_HEADS, HEAD_DIM, HEAD_DIM), jnp.float32)
    y_l, y_c = bidirectional(runs[0], runs[1], inp_c, inp_l, s0)

    def finish(y, inp, g):
        r, k, v = inp[0], inp[1], inp[2]
        B, T, _ = r.shape
        hs = lambda t: t.astype(jnp.float32).reshape(B, T, RWKV_HEADS, HEAD_DIM)
        bonus = jnp.sum(hs(r) * hs(k) * r_k, axis=-1, keepdims=True) * hs(v)
        y = head_norm(y, RWKV_HEADS, RWKV_GN_EPS) * ln_w + ln_b + bonus.reshape(B, T, GROUP_W)
        return y * g
    return finish(y_l, inp_l, g_l), finish(y_c, inp_c, g_c)


def mlstm_chunkwise(q, k, v, log_i, log_f, state):
    B, T, H, D = q.shape
    L = MLSTM_CHUNK
    nc = T // L
    to_chunks = lambda t: jnp.moveaxis(t.reshape((B, nc, L) + t.shape[2:]), (1, 3), (0, 2))
    k = k * (D ** -0.5)
    tril = jnp.tril(jnp.ones((L, L), dtype=bool))

    def step(carry, xs):
        C, n, m = carry
        qc, kc, vc, li, lf = xs
        b = jnp.cumsum(lf, axis=-1)
        log_d = jnp.where(tril, b[..., :, None] - b[..., None, :] + li[..., None, :], -jnp.inf)
        inter = b + m[..., None]
        m_t = jnp.maximum(inter, jnp.max(log_d, axis=-1))
        s = jnp.einsum('bhld,bhsd->bhls', qc, kc) * jnp.exp(log_d - m_t[..., None])
        w_inter = jnp.exp(inter - m_t)
        num = jnp.einsum('bhls,bhsd->bhld', s, vc) + w_inter[..., None] * jnp.einsum('bhed,bhld->bhle', C, qc)
        den = jnp.sum(s, axis=-1) + w_inter * jnp.einsum('bhd,bhld->bhl', n, qc)
        h = num / jnp.maximum(jnp.abs(den), jnp.exp(-m_t))[..., None]
        b_last = b[..., -1]
        lw = b_last[..., None] - b + li
        m_new = jnp.maximum(b_last + m, jnp.max(lw, axis=-1))
        wk = jnp.exp(lw - m_new[..., None])
        sc = jnp.exp(b_last + m - m_new)
        C_new = sc[..., None, None] * C + jnp.einsum('bhl,bhle,bhld->bhed', wk, vc, kc)
        n_new = sc[..., None] * n + jnp.einsum('bhl,bhld->bhd', wk, kc)
        return (C_new, n_new, m_new), h

    state, hs = lax.scan(step, state, tuple(to_chunks(t) for t in (q, k, v, log_i, log_f)))
    return jnp.moveaxis(hs, (0, 2), (1, 3)).reshape(B, T, H, D), state


def mlstm_direction(inp, state, direction):
    q, k, v, gates = inp
    B, T, _ = q.shape
    g = gates.astype(jnp.float32).reshape(B, T, 2, 2, MLSTM_HEADS)[:, :, direction]
    log_i, log_f = g[:, :, 0], jax.nn.log_sigmoid(g[:, :, 1])
    heads = lambda t: t.astype(jnp.float32).reshape(B, T, MLSTM_HEADS, HEAD_DIM)
    h, state = mlstm_chunkwise(heads(q), heads(k), heads(v), log_i, log_f, state)
    return h.reshape(B, T, GROUP_W), state


def mlstm_mixer(z_lat, z_ctx, conv_w, conv_b, gate_b, norm_w):
    def prepare(z):
        qk, v, o, gates = jnp.split(z, [2 * GROUP_W, 3 * GROUP_W, 4 * GROUP_W], axis=-1)
        q, k = jnp.split(jax.nn.silu(short_conv(qk, conv_w, conv_b)), 2, axis=-1)
        return (q, k, v, gates + gate_b), jax.nn.sigmoid(o)

    inp_l, o_l = prepare(z_lat)
    inp_c, o_c = prepare(z_ctx)
    B = z_lat.shape[0]
    state0 = (jnp.zeros((B, MLSTM_HEADS, HEAD_DIM, HEAD_DIM), jnp.float32),
              jnp.zeros((B, MLSTM_HEADS, HEAD_DIM), jnp.float32),
              jnp.zeros((B, MLSTM_HEADS), jnp.float32))
    runs = [functools.partial(mlstm_direction, direction=d) for d in range(2)]
    y_l, y_c = bidirectional(runs[0], runs[1], inp_c, inp_l, state0)
    finish = lambda y, o: head_norm(y, MLSTM_HEADS, NORM_EPS) * norm_w * o
    return finish(y_l, o_l), finish(y_c, o_c)


def setup_inputs(seed: int = 0) -> dict:
    key = jax.random.key(seed)
    ks = iter(jax.random.split(key, 64))
    nrm = lambda shape, s=1.0: s * jax.random.normal(next(ks), shape, jnp.float32)
    gain = lambda shape: 1.0 + nrm(shape, 0.05)
    L, D, W, H = DEPTH, D_MODEL, GROUP_W, MLSTM_HEADS
    n_idx = jnp.arange(S5_STATE, dtype=jnp.float32)
    gate_i = nrm((L, 2, H), 0.1)
    gate_f = jnp.linspace(3.0, 6.0, H, dtype=jnp.float32) + nrm((L, 2, H), 0.1)
    return {
        'x': nrm((BATCH, SEQ, D)),
        'c': nrm((BATCH, D)),
        'ctx': nrm((BATCH, CTX_LEN, D)),
        'c_ctx': nrm((D,)),
        'w_ada': nrm((L, D, N_MOD * D), 0.5 * D ** -0.5),
        'b_ada': nrm((L, N_MOD * D), 0.01),
        'norm_pre': gain((L, 3, D)),
        'norm_post': gain((L, 3, D)),
        'ffn_w_gate': nrm((L, 2, D, FFN_DIM), D ** -0.5),
        'ffn_w_up': nrm((L, 2, D, FFN_DIM), D ** -0.5),
        'ffn_w_down': nrm((L, 2, FFN_DIM, D), FFN_DIM ** -0.5),
        'w_in': nrm((L, D, IN_COLS), D ** -0.5),
        'w_out': nrm((L, D_MIX, D), D_MIX ** -0.5),
        's5_lam_re': -0.5 + nrm((L, 2, S5_GROUPS, S5_STATE), 0.01),
        's5_lam_im': math.pi * n_idx + nrm((L, 2, S5_GROUPS, S5_STATE), 0.01),
        's5_log_dt': jax.random.uniform(next(ks), (L, 2, S5_GROUPS), jnp.float32,
                                        math.log(S5_DT_MIN), math.log(S5_DT_MAX)),
        's5_b_re': nrm((L, 2, S5_GROUPS, S5_STATE, S5_P), (2 * S5_P) ** -0.5),
        's5_b_im': nrm((L, 2, S5_GROUPS, S5_STATE, S5_P), (2 * S5_P) ** -0.5),
        's5_c_re': nrm((L, 2, S5_GROUPS, S5_P, S5_STATE), S5_STATE ** -0.5),
        's5_c_im': nrm((L, 2, S5_GROUPS, S5_P, S5_STATE), S5_STATE ** -0.5),
        's5_d': nrm((L, W)),
        's5_w_glu': nrm((L, W, W), W ** -0.5),
        's5_b_glu': nrm((L, W), 0.01),
        'mla_q_norm': gain((L, MLA_Q_RANK)),
        'mla_kv_norm': gain((L, MLA_KV_RANK)),
        'mla_w_uq': nrm((L, MLA_Q_RANK, MLA_HEADS * (MLA_NOPE + MLA_ROPE)), MLA_Q_RANK ** -0.5),
        'mla_w_ukv': nrm((L, MLA_KV_RANK, MLA_HEADS * (MLA_NOPE + MLA_V)), MLA_KV_RANK ** -0.5),
        'rwkv_conv_w': nrm((L, SHORT_CONV, 3 * W), SHORT_CONV ** -0.5),
        'rwkv_conv_b': nrm((L, 3 * W), 0.01),
        'rwkv_w0': jax.random.uniform(next(ks), (L, 2, W), jnp.float32, -6.0, 1.0),
        'rwkv_w_up': nrm((L, 2, RWKV_DECAY_LORA, W), 0.5 * RWKV_DECAY_LORA ** -0.5),
        'rwkv_a0': nrm((L, 2, W), 0.5),
        'rwkv_a_up': nrm((L, 2, RWKV_A_LORA, W), RWKV_A_LORA ** -0.5),
        'rwkv_g_up': nrm((L, RWKV_GATE_LORA, W), RWKV_GATE_LORA ** -0.5),
        'rwkv_k_k': 0.85 + nrm((L, W), 0.05),
        'rwkv_k_a': 1.0 + nrm((L, W), 0.05),
        'rwkv_r_k': nrm((L, RWKV_HEADS, HEAD_DIM), 0.1),
        'rwkv_ln_w': gain((L, W)),
        'rwkv_ln_b': nrm((L, W), 0.01),
        'mlstm_conv_w': nrm((L, SHORT_CONV, 2 * W), SHORT_CONV ** -0.5),
        'mlstm_conv_b': nrm((L, 2 * W), 0.01),
        'mlstm_gate_b': jnp.stack([gate_i, gate_f], axis=2).reshape(L, 4 * H),
        'mlstm_norm': gain((L, W)),
    }


def reference(x, c, ctx, c_ctx, w_ada, b_ada, norm_pre, norm_post, ffn_w_gate, ffn_w_up, ffn_w_down,
              w_in, w_out, s5_lam_re, s5_lam_im, s5_log_dt, s5_b_re, s5_b_im, s5_c_re, s5_c_im, s5_d,
              s5_w_glu, s5_b_glu, mla_q_norm, mla_kv_norm, mla_w_uq, mla_w_ukv, rwkv_conv_w, rwkv_conv_b,
              rwkv_w0, rwkv_w_up, rwkv_a0, rwkv_a_up, rwkv_g_up, rwkv_k_k, rwkv_k_a, rwkv_r_k, rwkv_ln_w,
              rwkv_ln_b, mlstm_conv_w, mlstm_conv_b, mlstm_gate_b, mlstm_norm):
    cos, sin = axial_rope_tables(x.shape[1])
    split_at = [S5_COLS, S5_COLS + MLA_COLS, S5_COLS + MLA_COLS + RWKV_COLS]
    xc = ctx
    for l in range(DEPTH):
        last = l == DEPTH - 1
        mod_x = jnp.split((jax.nn.silu(c) @ w_ada[l] + b_ada[l])[:, None, :], N_MOD, axis=-1)
        mod_c = jnp.split(jax.nn.silu(c_ctx) @ w_ada[l] + b_ada[l], N_MOD, axis=-1)
        ffn0 = (norm_pre[l, 0], norm_post[l, 0], ffn_w_gate[l, 0], ffn_w_up[l, 0], ffn_w_down[l, 0])
        ffn1 = (norm_pre[l, 2], norm_post[l, 2], ffn_w_gate[l, 1], ffn_w_up[l, 1], ffn_w_down[l, 1])
        x = x + half_ffn(x, *mod_x[0:3], *ffn0)
        xc = xc + half_ffn(xc, *mod_c[0:3], *ffn0)
        hx = modulate(rms_norm(x, norm_pre[l, 1]), mod_x[3], mod_x[4])
        hc = modulate(rms_norm(xc, norm_pre[l, 1]), mod_c[3], mod_c[4])
        zx = jnp.split(hx @ w_in[l], split_at, axis=-1)
        zc = jnp.split(hc @ w_in[l], split_at, axis=-1)
        outs = [
            s5_mixer(zx[0], zc[0], s5_lam_re[l], s5_lam_im[l], s5_log_dt[l], s5_b_re[l], s5_b_im[l],
                     s5_c_re[l], s5_c_im[l], s5_d[l], s5_w_glu[l], s5_b_glu[l]),
            mla_mixer(zx[1], zc[1], cos, sin, mla_q_norm[l], mla_kv_norm[l], mla_w_uq[l], mla_w_ukv[l]),
            rwkv_mixer(zx[2], zc[2], rwkv_conv_w[l], rwkv_conv_b[l], rwkv_w0[l], rwkv_w_up[l], rwkv_a0[l],
                       rwkv_a_up[l], rwkv_g_up[l], rwkv_k_k[l], rwkv_k_a[l], rwkv_r_k[l], rwkv_ln_w[l],
                       rwkv_ln_b[l]),
            mlstm_mixer(zx[3], zc[3], mlstm_conv_w[l], mlstm_conv_b[l], mlstm_gate_b[l], mlstm_norm[l]),
        ]
        y_x = jnp.concatenate([o[0].astype(x.dtype) for o in outs], axis=-1) @ w_out[l]
        x = x + mod_x[5] * rms_norm(y_x, norm_post[l, 1])
        x = x + half_ffn(x, *mod_x[6:9], *ffn1)
        if not last:
            y_c = jnp.concatenate([o[1].astype(xc.dtype) for o in outs], axis=-1) @ w_out[l]
            xc = xc + mod_c[5] * rms_norm(y_c, norm_post[l, 1])
            xc = xc + half_ffn(xc, *mod_c[6:9], *ffn1)
    return x
```

```python
import functools
import math

import numpy as np
import jax
import jax.numpy as jnp
from jax import lax
from jax.experimental import pallas as pl
from jax.experimental.pallas import tpu as pltpu

F32 = jnp.float32
BF16 = jnp.bfloat16

GROUP_W = 256
HEAD_DIM = 64
N_HEADS = GROUP_W // HEAD_DIM
CHUNK = 64
N_MOD = 9
NORM_EPS = 1e-6
RWKV_GN_EPS = HEAD_DIM * 1e-5
GRID_W = 64
ROPE_BASE = 10000.0
ROPE_DIM = 32
ROPE_AXIS = 16
S5_P = 16
S5_STATE = 64
MACARON = 0.5
VMEM_LIMIT_BYTES = 56 * 1024 * 1024

ZB = 256
Z_MLA, Z_RKV, Z_S5, Z_LORA, Z_MQK, Z_MVO, Z_GATE = 0, 3, 6, 7, 8, 10, 12
Z_NBLK = 13
Z_COLS = Z_NBLK * ZB


def _nn(a, b):
    return lax.dot_general(a, b, (((1,), (0,)), ((), ())), preferred_element_type=F32)


def _nt(a, b):
    return lax.dot_general(a, b, (((1,), (1,)), ((), ())), preferred_element_type=F32)


def _tn(a, b):
    return lax.dot_general(a, b, (((0,), (0,)), ((), ())), preferred_element_type=F32)


def _split2(x):
    hi = x.astype(BF16)
    lo = (x - hi.astype(F32)).astype(BF16)
    return hi, lo


def _split3(x):
    p1 = x.astype(BF16)
    r = x - p1.astype(F32)
    p2 = r.astype(BF16)
    p3 = (r - p2.astype(F32)).astype(BF16)
    return p1, p2, p3


def _mm3(dotf, a, b):
    ah, al = _split2(a)
    bh, bl = _split2(b)
    return dotf(ah, bh) + (dotf(ah, bl) + dotf(al, bh))


def _mm1(dotf, a, b):
    return dotf(a.astype(BF16), b.astype(BF16))


def _mm_exact_rhs(a, e):
    eb = e.astype(BF16)
    p1, p2, p3 = _split3(a)
    return _nn(p1, eb) + (_nn(p2, eb) + _nn(p3, eb))


def _mm_exact_lhs(e, a):
    eb = e.astype(BF16)
    p1, p2, p3 = _split3(a)
    return _nn(eb, p1) + (_nn(eb, p2) + _nn(eb, p3))


def _rms(x, g):
    return x * lax.rsqrt(jnp.mean(x * x, axis=-1, keepdims=True) + NORM_EPS) * g


def _sigmoid(x):
    return 1.0 / (1.0 + jnp.exp(-x))


def _softplus(x):
    return jnp.maximum(x, 0.0) + jnp.log(1.0 + jnp.exp(-jnp.abs(x)))


def _iota(shape, dim):
    return lax.broadcasted_iota(jnp.int32, shape, dim)


def _head_masks():
    lane = _iota((1, GROUP_W), 1)
    return [lane // HEAD_DIM == h for h in range(N_HEADS)]


def _block_diag_mask():
    r = _iota((GROUP_W, GROUP_W), 0) // HEAD_DIM
    c = _iota((GROUP_W, GROUP_W), 1) // HEAD_DIM
    return (r == c).astype(F32)


def _row_stack(x, hms):
    return jnp.concatenate([jnp.where(m, x, 0.0) for m in hms], axis=0)


def _pick_tile(n, target, mult=16):
    best = None
    for t in range(mult, min(n, target) + 1, mult):
        if n % t == 0:
            best = t
    if best is None:
        raise ValueError(f"no tile for {n}")
    return best


def _cparams(*sem):
    return pltpu.CompilerParams(dimension_semantics=sem, vmem_limit_bytes=VMEM_LIMIT_BYTES)


def _ada_kernel(c_ref, w_ref, b_ref, o_ref):
    c = c_ref[...]
    s = c * _sigmoid(c)
    o_ref[0] = _mm3(_nn, s, w_ref[0]) + b_ref[0]


def _ada_call(cvec, w_ada, b_ada):
    L, D, N = w_ada.shape
    R = cvec.shape[0]
    tn = _pick_tile(N, 1152, 128)
    return pl.pallas_call(
        _ada_kernel,
        out_shape=jax.ShapeDtypeStruct((L, R, N), F32),
        grid=(L, N // tn),
        in_specs=[pl.BlockSpec((R, D), lambda l, j: (0, 0)),
                  pl.BlockSpec((1, D, tn), lambda l, j: (l, 0, j)),
                  pl.BlockSpec((1, 1, tn), lambda l, j: (l, 0, j))],
        out_specs=pl.BlockSpec((1, R, tn), lambda l, j: (l, 0, j)),
        compiler_params=_cparams("parallel", "parallel"),
        name="ada_mod",
    )(cvec, w_ada, b_ada.reshape(L, 1, N))


def _ffn_kernel(x_ref, mods_ref, gpre_ref, gpost_ref, gmix_ref, wg_ref, wu_ref, wd_ref,
                o_ref, hmix_ref, h_scr, acc_scr, *, tm, tiles_per_batch, tc, n_batch, koff, emit_hmix):
    i = pl.program_id(0)
    f = pl.program_id(1)
    nf = pl.num_programs(1)
    d = x_ref.shape[-1]
    b = i // tiles_per_batch
    t0 = (i % tiles_per_batch) * tm
    is_ctx = (t0 + _iota((tm, 1), 0)) < tc

    def mod(k):
        mx = mods_ref[pl.ds(b, 1), pl.ds(k * d, d)]
        mc = mods_ref[pl.ds(n_batch, 1), pl.ds(k * d, d)]
        return jnp.where(is_ctx, mc, mx)

    @pl.when(f == 0)
    def _():
        h = _rms(x_ref[...], gpre_ref[...]) * (1.0 + mod(koff + 1)) + mod(koff)
        h_scr[...] = h.astype(BF16)
        acc_scr[...] = jnp.zeros_like(acc_scr)

    hb = h_scr[...]
    g = _nn(hb, wg_ref[...])
    u = _nn(hb, wu_ref[...])
    a = (g * _sigmoid(g) * u).astype(BF16)
    acc_scr[...] += _nn(a, wd_ref[...])

    @pl.when(f == nf - 1)
    def _():
        y = _rms(acc_scr[...], gpost_ref[...])
        xn = x_ref[...] + MACARON * mod(koff + 2) * y
        o_ref[...] = xn
        if emit_hmix:
            hm = _rms(xn, gmix_ref[...]) * (1.0 + mod(4)) + mod(3)
            hmix_ref[...] = hm.astype(BF16)
        else:
            hmix_ref[...] = jnp.zeros_like(hmix_ref)


def _ffn_call(xa2, mods, gpre, gpost, gmix, wg, wu, wd, *, ta, tc, n_batch, koff, emit_hmix):
    M, D = xa2.shape
    Fd = wg.shape[1]
    tm = _pick_tile(ta, 544)
    tf = _pick_tile(Fd, 1408, 128)
    kern = functools.partial(_ffn_kernel, tm=tm, tiles_per_batch=ta // tm, tc=tc, n_batch=n_batch,
                             koff=koff, emit_hmix=emit_hmix)
    hm_rows = tm if emit_hmix else 16
    return pl.pallas_call(
        kern,
        out_shape=(jax.ShapeDtypeStruct((M, D), F32),
                   jax.ShapeDtypeStruct((M if emit_hmix else 16 * (M // tm), D), BF16)),
        grid=(M // tm, Fd // tf),
        in_specs=[pl.BlockSpec((tm, D), lambda i, f: (i, 0)),
                  pl.BlockSpec(mods.shape, lambda i, f: (0, 0)),
                  pl.BlockSpec((1, D), lambda i, f: (0, 0)),
                  pl.BlockSpec((1, D), lambda i, f: (0, 0)),
                  pl.BlockSpec((1, D), lambda i, f: (0, 0)),
                  pl.BlockSpec((D, tf), lambda i, f: (0, f)),
                  pl.BlockSpec((D, tf), lambda i, f: (0, f)),
                  pl.BlockSpec((tf, D), lambda i, f: (f, 0))],
        out_specs=(pl.BlockSpec((tm, D), lambda i, f: (i, 0)),
                   pl.BlockSpec((hm_rows, D), lambda i, f: (i, 0))),
        scratch_shapes=[pltpu.VMEM((tm, D), BF16), pltpu.VMEM((tm, D), F32)],
        compiler_params=_cparams("parallel", "arbitrary"),
        name="half_ffn",
    )(xa2, mods, gpre, gpost, gmix, wg, wu, wd)


def _inproj_kernel(h_ref, w_ref, cw_ref, o_ref, *, tc, conv_lo, conv_hi, silu_lo, silu_hi):
    nb = pl.program_id(1)
    z = _nn(h_ref[0], w_ref[...])
    ta = z.shape[0]
    is_conv = ((nb >= conv_lo[0]) & (nb < conv_hi[0])) | ((nb >= conv_lo[1]) & (nb < conv_hi[1]))
    is_silu = (nb >= silu_lo) & (nb < silu_hi)

    @pl.when(jnp.logical_not(is_conv))
    def _():
        o_ref[0] = z

    def conv():
        row = _iota((ta, 1), 0)
        zp = jnp.where((row == 0) | (row == tc), 0.0, pltpu.roll(z, 1, 0))
        zn = jnp.where((row == tc - 1) | (row == ta - 1), 0.0, pltpu.roll(z, ta - 1, 0))
        cw = cw_ref[0]
        return cw[3:4] + zp * cw[0:1] + z * cw[1:2] + zn * cw[2:3]

    @pl.when(is_conv & jnp.logical_not(is_silu))
    def _():
        o_ref[0] = conv()

    @pl.when(is_conv & is_silu)
    def _():
        y = conv()
        o_ref[0] = y * _sigmoid(y)


def _inproj_call(hmix3, w_re, cw, *, tc):
    B, Ta, D = hmix3.shape
    kern = functools.partial(_inproj_kernel, tc=tc, conv_lo=(Z_RKV, Z_MQK), conv_hi=(Z_RKV + 3, Z_MQK + 2),
                             silu_lo=Z_MQK, silu_hi=Z_MQK + 2)
    return pl.pallas_call(
        kern,
        out_shape=jax.ShapeDtypeStruct((B, Ta, Z_COLS), F32),
        grid=(B, Z_NBLK),
        in_specs=[pl.BlockSpec((1, Ta, D), lambda b, n: (b, 0, 0)),
                  pl.BlockSpec((D, ZB), lambda b, n: (0, n)),
                  pl.BlockSpec((1, 8, ZB), lambda b, n: (n, 0, 0))],
        out_specs=pl.BlockSpec((1, Ta, ZB), lambda b, n: (b, 0, n)),
        compiler_params=_cparams("parallel", "arbitrary"),
        name="in_proj",
    )(hmix3, w_re, cw)


def _bwd_chunk(i, n_ctx, n_all):
    return jnp.where(i < n_ctx, n_ctx - 1 - i, n_all - 1 - (i - n_ctx))


def _s5_kernel(uf_ref, ub_ref, lre_ref, lim_ref, ldt_ref, wbre_ref, wbim_ref, wcre_ref, wcim_ref,
               yf_ref, yb_ref, wb_scr, coef_scr, st_scr, x_scr, *, lc, nb):
    i = pl.program_id(0)
    gn = lre_ref.shape[-1]

    @pl.when(i == 0)
    def _():
        for d in range(2):
            dt = jnp.exp(ldt_ref[d])
            lre = lre_ref[d]
            lim = lim_ref[d]
            mag = jnp.exp(lre * dt)
            ar = mag * jnp.cos(lim * dt)
            ai = mag * jnp.sin(lim * dt)
            den = lre * lre + lim * lim
            fr = ((ar - 1.0) * lre + ai * lim) / den
            fi = (ai * lre - (ar - 1.0) * lim) / den
            coef_scr[d, 0:nb, :] = jnp.broadcast_to(ar, (nb, gn))
            coef_scr[d, nb:2 * nb, :] = jnp.broadcast_to(ai, (nb, gn))
            wre = wbre_ref[d]
            wim = wbim_ref[d]
            wb_scr[d, :, 0:gn] = (wre * fr - wim * fi).astype(BF16)
            wb_scr[d, :, gn:2 * gn] = (wim * fr + wre * fi).astype(BF16)
        st_scr[...] = jnp.zeros_like(st_scr)

    for d, (u_ref, y_ref) in enumerate(((uf_ref, yf_ref), (ub_ref, yb_ref))):
        x_scr[...] = _nn(u_ref[...].astype(BF16), wb_scr[d])
        ar = coef_scr[d, 0:nb, :]
        ai = coef_scr[d, nb:2 * nb, :]

        def body(t, carry, d=d, ar=ar, ai=ai):
            sr, si = carry
            tt = t if d == 0 else lc - 1 - t
            r0 = pl.multiple_of(tt * nb, nb)
            xr = x_scr[pl.ds(r0, nb), 0:gn]
            xi = x_scr[pl.ds(r0, nb), gn:2 * gn]
            nsr = ar * sr - ai * si + xr
            nsi = ar * si + ai * sr + xi
            x_scr[pl.ds(r0, nb), 0:gn] = nsr
            x_scr[pl.ds(r0, nb), gn:2 * gn] = nsi
            return nsr, nsi

        sr, si = lax.fori_loop(0, lc, body, (st_scr[d, 0:nb, :], st_scr[d, nb:2 * nb, :]))
        st_scr[d, 0:nb, :] = sr
        st_scr[d, nb:2 * nb, :] = si
        y_ref[...] = (_nn(x_scr[:, 0:gn].astype(BF16), wcre_ref[d])
                      - _nn(x_scr[:, gn:2 * gn].astype(BF16), wcim_ref[d]))


def _s5_call(u_tm, lre, lim, ldt, wbre, wbim, wcre, wcim, *, nb, n_ctx, n_all):
    rows, W = u_tm.shape
    lc = CHUNK
    gn = lre.shape[-1]
    blk = lc * nb
    kern = functools.partial(_s5_kernel, lc=lc, nb=nb)
    full = lambda a: pl.BlockSpec(a.shape, lambda i: (0,) * a.ndim)
    return pl.pallas_call(
        kern,
        out_shape=(jax.ShapeDtypeStruct((rows, W), F32), jax.ShapeDtypeStruct((rows, W), F32)),
        grid=(n_all,),
        in_specs=[pl.BlockSpec((blk, W), lambda i: (i, 0)),
                  pl.BlockSpec((blk, W), lambda i: (_bwd_chunk(i, n_ctx, n_all), 0)),
                  full(lre), full(lim), full(ldt), full(wbre), full(wbim), full(wcre), full(wcim)],
        out_specs=(pl.BlockSpec((blk, W), lambda i: (i, 0)),
                   pl.BlockSpec((blk, W), lambda i: (_bwd_chunk(i, n_ctx, n_all), 0))),
        scratch_shapes=[pltpu.VMEM((2, W, 2 * gn), BF16),
                        pltpu.VMEM((2, 2 * nb, gn), F32),
                        pltpu.VMEM((2, 2 * nb, gn), F32),
                        pltpu.VMEM((blk, 2 * gn), F32)],
        compiler_params=_cparams("arbitrary"),
        name="s5_scan",
    )(u_tm, u_tm, lre, lim, ldt, wbre, wbim, wcre, wcim)


def _mla_proj_kernel(z_ref, qn_ref, kvn_ref, wq_ref, wqr_ref, wk_ref, wv_ref, cos_ref, sin_ref,
                     q_ref, k_ref, v_ref, *, scale):
    z = z_ref[0]
    cq = z[:, 0:256]
    ckv = z[:, 256:384]
    kr = z[:, 384:512]
    krr = z[:, 512:640]
    cos = cos_ref[...]
    sin = sin_ref[...]
    cqb = _rms(cq, qn_ref[...]).astype(BF16)
    ckvb = _rms(ckv, kvn_ref[...]).astype(BF16)
    q = _nn(cqb, wq_ref[...])
    qr = _nn(cqb, wqr_ref[...])
    kn = _nn(ckvb, wk_ref[...])
    krp = kr * cos + krr * sin
    for h in range(N_HEADS):
        sl = slice(h * 128, (h + 1) * 128)
        q_ref[0, h] = ((q[:, sl] * cos + qr[:, sl] * sin) * scale).astype(BF16)
        k_ref[0, h] = (kn[:, sl] + krp).astype(BF16)
    v_ref[0] = _nn(ckvb, wv_ref[...]).astype(BF16)


def _mla_proj_call(z, qn, kvn, wq, wqr, wk, wv, cos_t, sin_t, *, scale):
    B, Ta, _ = z.shape
    tm = _pick_tile(Ta, 544)
    kern = functools.partial(_mla_proj_kernel, scale=scale)
    full = lambda a: pl.BlockSpec(a.shape, lambda b, t: (0,) * a.ndim)
    return pl.pallas_call(
        kern,
        out_shape=(jax.ShapeDtypeStruct((B, N_HEADS, Ta, 128), BF16),
                   jax.ShapeDtypeStruct((B, N_HEADS, Ta, 128), BF16),
                   jax.ShapeDtypeStruct((B, Ta, GROUP_W), BF16)),
        grid=(B, Ta // tm),
        in_specs=[pl.BlockSpec((1, tm, 3 * ZB), lambda b, t: (b, t, 0)),
                  full(qn), full(kvn), full(wq), full(wqr), full(wk), full(wv),
                  pl.BlockSpec((tm, 128), lambda b, t: (t, 0)),
                  pl.BlockSpec((tm, 128), lambda b, t: (t, 0))],
        out_specs=(pl.BlockSpec((1, N_HEADS, tm, 128), lambda b, t: (b, 0, t, 0)),
                   pl.BlockSpec((1, N_HEADS, tm, 128), lambda b, t: (b, 0, t, 0)),
                   pl.BlockSpec((1, tm, GROUP_W), lambda b, t: (b, t, 0))),
        compiler_params=_cparams("parallel", "parallel"),
        name="mla_proj",
    )(z, qn, kvn, wq, wqr, wk, wv, cos_t, sin_t)


def _attn_kernel(q_ref, k_ref, v_ref, o_ref, *, tc, ta, n_ctx_tiles):
    i = pl.program_id(2)

    def run(nk):
        outs = []
        for j in range(2):
            q = q_ref[0, j]
            k = k_ref[0, j, 0:nk, :]
            v = v_ref[0, 0:nk, :]
            s = _nt(q, k)
            m = jnp.max(s, axis=-1, keepdims=True)
            p = jnp.exp(s - m)
            l = jnp.sum(p, axis=-1, keepdims=True)
            outs.append(_nn(p.astype(BF16), v) * (1.0 / l))
        lane = _iota((1, 128), 1)
        o_ref[0] = jnp.where(lane < HEAD_DIM, outs[0], outs[1])

    @pl.when(i < n_ctx_tiles)
    def _():
        run(tc)

    @pl.when(i >= n_ctx_tiles)
    def _():
        run(ta)


def _attn_call(q, k, v, *, tc):
    B, H, Ta, _ = q.shape
    tq = _pick_tile(math.gcd(tc, Ta), 256)
    kern = functools.partial(_attn_kernel, tc=tc, ta=Ta, n_ctx_tiles=tc // tq)
    return pl.pallas_call(
        kern,
        out_shape=jax.ShapeDtypeStruct((B, Ta, GROUP_W), F32),
        grid=(B, H // 2, Ta // tq),
        in_specs=[pl.BlockSpec((1, 2, tq, 128), lambda b, h, i: (b, h, i, 0)),
                  pl.BlockSpec((1, 2, Ta, 128), lambda b, h, i: (b, h, 0, 0)),
                  pl.BlockSpec((1, Ta, 128), lambda b, h, i: (b, 0, h))],
        out_specs=pl.BlockSpec((1, tq, 128), lambda b, h, i: (b, i, h)),
        compiler_params=_cparams("parallel", "parallel", "arbitrary"),
        name="mla_attn",
    )(q, k, v)


def _chunk_masks(d):
    L = CHUNK
    row = _iota((L, GROUP_W), 0)
    s_idx = _iota((L, GROUP_W), 1) % L
    tr = _iota((L, L), 0)
    tcol = _iota((L, L), 1)
    if d == 0:
        return (tcol <= tr).astype(F32), s_idx < row, s_idx <= row, s_idx == row
    return (tcol >= tr).astype(F32), s_idx > row, s_idx >= row, s_idx == row


def _rwkv_kernel(rf_ref, rb_ref, lf_ref, lb_ref, wup_ref, aup_ref, gup_ref, w0_ref, a0_ref,
                 kk_ref, ka_ref, rk_ref, yf_ref, yb_ref, g_ref, bon_ref, s_scr):
    i = pl.program_id(1)
    L = CHUNK
    W = GROUP_W

    @pl.when(i == 0)
    def _():
        s_scr[...] = jnp.zeros_like(s_scr)

    hms = _head_masks()
    bd = _block_diag_mask()
    rs = lambda x: _row_stack(x, hms)
    bdiag = lambda x: jnp.concatenate([x] * N_HEADS, axis=0) * bd

    for d, (r_ref, l_ref, y_ref) in enumerate(((rf_ref, lf_ref, yf_ref), (rb_ref, lb_ref, yb_ref))):
        rkv = r_ref[0]
        lora = l_ref[0]
        r = rkv[:, 0:W]
        k = rkv[:, W:2 * W]
        v = rkv[:, 2 * W:3 * W]
        tri, strict, incl, eye = _chunk_masks(d)

        w_log = -_softplus(-(w0_ref[d] + _mm3(_nn, jnp.tanh(lora), wup_ref[d]))) - 0.5
        lw = -jnp.exp(w_log)
        a = _sigmoid(a0_ref[d] + _mm3(_nn, lora, aup_ref[d]))
        kkv = k * kk_ref[...]
        kkn = kkv / jnp.maximum(jnp.sqrt(_mm_exact_rhs(kkv * kkv, bd)), 1e-12)
        keff = k * (1.0 + (a - 1.0) * ka_ref[...])

        cum = _mm_exact_lhs(tri, lw)
        tot = jnp.sum(lw, axis=0, keepdims=True)
        e_dn = jnp.exp(-cum)
        e_tc = jnp.exp(tot - cum)
        kka = kkn * a
        al = -kkn * jnp.exp(cum - lw)
        rt = r * jnp.exp(cum)
        rsv = rs(v)
        ar = jnp.concatenate([al, rt], axis=0)
        a_all = _mm3(_nt, ar, jnp.concatenate([rs(kka * e_dn), rs(keff * e_dn)], axis=0))
        a_ab = jnp.where(strict, a_all[0:L, 0:W], 0.0)
        a_ak = jnp.where(strict, a_all[0:L, W:2 * W], 0.0)
        a_rb = jnp.where(incl, a_all[L:2 * L, 0:W], 0.0)
        a_rk = jnp.where(incl, a_all[L:2 * L, W:2 * W], 0.0)

        p = jnp.where(eye, 1.0, 0.0) + a_ab
        sq = _mm3(_nn, a_ab, bdiag(a_ab))
        n_sq = int(math.log2(L)) - 1
        for it in range(n_sq):
            if it < n_sq - 1:
                ps = _mm3(_nn, jnp.concatenate([p, sq], axis=0), bdiag(sq))
                p = p + ps[0:L]
                sq = ps[L:2 * L]
            else:
                p = p + _mm3(_nn, p, bdiag(sq))

        st = s_scr[d]
        ah = _mm3(_nt, ar, st)
        z = ah[0:L] + _mm3(_nn, a_ak, rsv)
        u = _mm3(_nn, p, rs(z))
        y_ref[0] = ah[L:2 * L] + _mm3(_nn, jnp.concatenate([a_rb, a_rk], axis=1),
                                      jnp.concatenate([rs(u), rsv], axis=0))
        upd = _mm3(_tn, jnp.concatenate([u, v], axis=0),
                   jnp.concatenate([kka * e_tc, keff * e_tc], axis=0))
        s_scr[d] = st * jnp.exp(tot) + bd * upd

        if d == 0:
            g_ref[0] = _mm3(_nn, _sigmoid(lora), gup_ref[...])
            bon_ref[0] = _mm_exact_rhs(r * k * rk_ref[...], bd) * v


def _rwkv_call(z, wup, aup, gup, w0, a0, kk, ka, rk, *, n_ctx, n_all):
    B, Ta, _ = z.shape
    L = CHUNK
    W = GROUP_W
    full = lambda a: pl.BlockSpec(a.shape, lambda b, i: (0,) * a.ndim)
    fwd = lambda blk: (lambda b, i: (b, i, blk))
    bwd = lambda blk: (lambda b, i: (b, _bwd_chunk(i, n_ctx, n_all), blk))
    out = jax.ShapeDtypeStruct((B, Ta, W), F32)
    return pl.pallas_call(
        _rwkv_kernel,
        out_shape=(out, out, out, out),
        grid=(B, n_all),
        in_specs=[pl.BlockSpec((1, L, 3 * W), fwd(Z_RKV // 3)),
                  pl.BlockSpec((1, L, 3 * W), bwd(Z_RKV // 3)),
                  pl.BlockSpec((1, L, W), fwd(Z_LORA)),
                  pl.BlockSpec((1, L, W), bwd(Z_LORA)),
                  full(wup), full(aup), full(gup), full(w0), full(a0), full(kk), full(ka), full(rk)],
        out_specs=(pl.BlockSpec((1, L, W), fwd(0)), pl.BlockSpec((1, L, W), bwd(0)),
                   pl.BlockSpec((1, L, W), fwd(0)), pl.BlockSpec((1, L, W), fwd(0))),
        scratch_shapes=[pltpu.VMEM((2, W, W), F32)],
        compiler_params=_cparams("parallel", "arbitrary"),
        name="rwkv7",
    )(z, z, z, z, wup, aup, gup, w0, a0, kk, ka, rk)


def _mlstm_kernel(qf_ref, qb_ref, vf_ref, vb_ref, gf_ref, gb_ref, gbias_ref,
                  yf_ref, yb_ref, c_scr, n_scr, m_scr):
    i = pl.program_id(1)
    L = CHUNK
    W = GROUP_W

    @pl.when(i == 0)
    def _():
        c_scr[...] = jnp.zeros_like(c_scr)
        n_scr[...] = jnp.zeros_like(n_scr)
        m_scr[...] = jnp.zeros_like(m_scr)

    hms = _head_masks()
    bd = _block_diag_mask()
    rs = lambda x: _row_stack(x, hms)
    ci = _iota((W, W), 0)
    cj = _iota((W, W), 1) // HEAD_DIM
    neg_inf = -jnp.inf

    for d, (q_ref, v_ref, g_ref, y_ref) in enumerate(((qf_ref, vf_ref, gf_ref, yf_ref),
                                                      (qb_ref, vb_ref, gb_ref, yb_ref))):
        qk = q_ref[0]
        q = qk[:, 0:W]
        k = qk[:, W:2 * W] * (HEAD_DIM ** -0.5)
        v = v_ref[0][:, 0:W]
        g = g_ref[0] + gbias_ref[...]
        tri, strict, incl, eye = _chunk_masks(d)

        li = _mm_exact_rhs(g, (ci == d * 2 * N_HEADS + cj).astype(F32))
        lf = -_softplus(-_mm_exact_rhs(g, (ci == d * 2 * N_HEADS + N_HEADS + cj).astype(F32)))
        bcol = _mm_exact_lhs(tri, lf)
        brow = jnp.sum(jnp.where(eye, bcol, 0.0), axis=0, keepdims=True)
        lirow = jnp.sum(jnp.where(eye, li, 0.0), axis=0, keepdims=True)
        logd = jnp.where(incl, bcol - brow + lirow, neg_inf)
        m_row = m_scr[d, 0:1, :]
        n_row = n_scr[d, 0:1, :]
        c_st = c_scr[d]
        inter = bcol + m_row
        mx = jnp.zeros((L, W), F32)
        for hm in hms:
            mh = jnp.max(jnp.where(hm, logd, neg_inf), axis=1, keepdims=True)
            mx = jnp.where(hm, mh, mx)
        mt = jnp.maximum(inter, mx)
        sp = _mm3(_nt, q, rs(k)) * jnp.exp(logd - mt)
        w_int = jnp.exp(inter - mt)
        rsv = rs(v)
        num = _mm3(_nn, sp, rsv) + w_int * _mm3(_nt, q, c_st)
        den = _mm_exact_rhs(sp, bd) + w_int * _mm_exact_rhs(q * n_row, bd)
        y_ref[0] = num / jnp.maximum(jnp.abs(den), jnp.exp(-mt))

        blast = bcol[L - 1:L, :] if d == 0 else bcol[0:1, :]
        lwc = blast - bcol + li
        m_new = jnp.maximum(blast + m_row, jnp.max(lwc, axis=0, keepdims=True))
        wk = jnp.exp(lwc - m_new)
        sc = jnp.exp(blast + m_row - m_new)
        kw = k * wk
        c_scr[d] = sc * c_st + bd * _mm3(_tn, v, kw)
        n_scr[d, 0:1, :] = sc * n_row + jnp.sum(kw, axis=0, keepdims=True)
        m_scr[d, 0:1, :] = m_new


def _mlstm_call(z, gbias, *, n_ctx, n_all):
    B, Ta, _ = z.shape
    L = CHUNK
    W = GROUP_W
    fwd = lambda blk: (lambda b, i: (b, i, blk))
    bwd = lambda blk: (lambda b, i: (b, _bwd_chunk(i, n_ctx, n_all), blk))
    out = jax.ShapeDtypeStruct((B, Ta, W), F32)
    return pl.pallas_call(
        _mlstm_kernel,
        out_shape=(out, out),
        grid=(B, n_all),
        in_specs=[pl.BlockSpec((1, L, 2 * W), fwd(Z_MQK // 2)),
                  pl.BlockSpec((1, L, 2 * W), bwd(Z_MQK // 2)),
                  pl.BlockSpec((1, L, 2 * W), fwd(Z_MVO // 2)),
                  pl.BlockSpec((1, L, 2 * W), bwd(Z_MVO // 2)),
                  pl.BlockSpec((1, L, W), fwd(Z_GATE)),
                  pl.BlockSpec((1, L, W), bwd(Z_GATE)),
                  pl.BlockSpec(gbias.shape, lambda b, i: (0, 0))],
        out_specs=(pl.BlockSpec((1, L, W), fwd(0)), pl.BlockSpec((1, L, W), bwd(0))),
        scratch_shapes=[pltpu.VMEM((2, W, W), F32), pltpu.VMEM((2, 8, W), F32), pltpu.VMEM((2, 8, W), F32)],
        compiler_params=_cparams("parallel", "arbitrary"),
        name="mlstm",
    )(z, z, z, z, z, z, gbias)


def _head_norm(y, bd, eps):
    mu = _mm_exact_rhs(y, bd) * (1.0 / HEAD_DIM)
    yc = y - mu
    var = _mm_exact_rhs(yc * yc, bd) * (1.0 / HEAD_DIM)
    return yc * lax.rsqrt(var + eps)


def _outproj_kernel(x_ref, mods_ref, s5f_ref, s5b_ref, u_ref, at_ref, rf_ref, rb_ref, rg_ref, rbon_ref,
                    mf_ref, mb_ref, vo_ref, s5d_ref, wglu_ref, bglu_ref, lnw_ref, lnb_ref, mnw_ref,
                    gpost_ref, wout_ref, o_ref, *, tm, tc, n_batch):
    b = pl.program_id(0)
    t = pl.program_id(1)
    d = x_ref.shape[-1]
    W = GROUP_W
    is_ctx = (t * tm + _iota((tm, 1), 0)) < tc
    gate = jnp.where(is_ctx, mods_ref[pl.ds(n_batch, 1), pl.ds(5 * d, d)],
                     mods_ref[pl.ds(b, 1), pl.ds(5 * d, d)])
    bd = _block_diag_mask()

    y = s5f_ref[...] + s5b_ref[...] + s5d_ref[...] * u_ref[0]
    zg = 0.5 * y * (1.0 + jnp.tanh(math.sqrt(2.0 / math.pi) * (y + 0.044715 * (y * y * y))))
    s5o = zg * _sigmoid(_mm1(_nn, zg, wglu_ref[...]) + bglu_ref[...])

    yr = _head_norm(rf_ref[0] + rb_ref[0], bd, RWKV_GN_EPS)
    rwo = (yr * lnw_ref[...] + lnb_ref[...] + rbon_ref[0]) * rg_ref[0]

    ym = _head_norm(mf_ref[0] + mb_ref[0], bd, NORM_EPS)
    mlo = ym * mnw_ref[...] * _sigmoid(vo_ref[0][:, W:2 * W])

    cat = jnp.concatenate([s5o, at_ref[0], rwo, mlo], axis=1).astype(BF16)
    yx = _nn(cat, wout_ref[...])
    o_ref[0] = x_ref[0] + gate * _rms(yx, gpost_ref[...])


def _outproj_call(xa, mods, s5f, s5b, z, attn, rf, rb, rg, rbon, mf, mb,
                  s5d, wglu, bglu, lnw, lnb, mnw, gpost, wout, *, tc):
    B, Ta, D = xa.shape
    W = GROUP_W
    tm = _pick_tile(Ta, 544)
    kern = functools.partial(_outproj_kernel, tm=tm, tc=tc, n_batch=B)
    full = lambda a: pl.BlockSpec(a.shape, lambda b, t: (0,) * a.ndim)
    tok = pl.BlockSpec((1, tm, W), lambda b, t: (b, t, 0))
    tmaj = pl.BlockSpec((tm, W), lambda b, t: (t, b))
    return pl.pallas_call(
        kern,
        out_shape=jax.ShapeDtypeStruct((B, Ta, D), F32),
        grid=(B, Ta // tm),
        in_specs=[pl.BlockSpec((1, tm, D), lambda b, t: (b, t, 0)), full(mods),
                  tmaj, tmaj, pl.BlockSpec((1, tm, W), lambda b, t: (b, t, Z_S5)),
                  tok, tok, tok, tok, tok, tok, tok,
                  pl.BlockSpec((1, tm, 2 * W), lambda b, t: (b, t, Z_MVO // 2)),
                  full(s5d), full(wglu), full(bglu), full(lnw), full(lnb), full(mnw), full(gpost), full(wout)],
        out_specs=pl.BlockSpec((1, tm, D), lambda b, t: (b, t, 0)),
        compiler_params=_cparams("parallel", "parallel"),
        name="mix_out",
    )(xa, mods, s5f, s5b, z, attn, rf, rb, rg, rbon, mf, mb, z, s5d, wglu, bglu, lnw, lnb, mnw, gpost, wout)


def _inproj_column_map():
    src = np.zeros((Z_COLS,), np.int32)
    sgn = np.zeros((Z_COLS,), np.float32)

    def put(dst, s0, n):
        src[dst:dst + n] = np.arange(s0, s0 + n)
        sgn[dst:dst + n] = 1.0

    o_s5, o_mla, o_rw, o_ml = 0, 256, 672, 1568
    put(0, o_mla, 256)
    put(256, o_mla + 256, 128)
    put(384 + 64, o_mla + 384, ROPE_DIM)
    h = ROPE_AXIS // 2
    perm = np.concatenate([np.arange(h, 2 * h), np.arange(0, h), np.arange(3 * h, 4 * h), np.arange(2 * h, 3 * h)])
    sign = np.concatenate([-np.ones(h), np.ones(h), -np.ones(h), np.ones(h)])
    src[512 + 64:512 + 64 + ROPE_DIM] = o_mla + 384 + perm
    sgn[512 + 64:512 + 64 + ROPE_DIM] = sign
    put(Z_RKV * ZB, o_rw, 768)
    put(Z_S5 * ZB, o_s5, 256)
    put(Z_LORA * ZB, o_rw + 768, 128)
    put(Z_MQK * ZB, o_ml, 512)
    put(Z_MVO * ZB, o_ml + 512, 512)
    put(Z_GATE * ZB, o_ml + 1024, 16)
    return src, sgn, perm, sign


def _rope_tables(T, tc):
    rows = T // GRID_W
    r_idx, c_idx = jnp.meshgrid(jnp.arange(rows), jnp.arange(GRID_W), indexing='ij')
    inv_freq = 1.0 / (ROPE_BASE ** (jnp.arange(0, ROPE_AXIS, 2, dtype=F32) / ROPE_AXIS))
    ang_r = r_idx.reshape(-1, 1).astype(F32) * inv_freq
    ang_c = c_idx.reshape(-1, 1).astype(F32) * inv_freq
    ang = jnp.concatenate([ang_r, ang_r, ang_c, ang_c], axis=-1)
    cos = jnp.concatenate([jnp.ones((tc, ROPE_DIM), F32), jnp.cos(ang)], axis=0)
    sin = jnp.concatenate([jnp.zeros((tc, ROPE_DIM), F32), jnp.sin(ang)], axis=0)
    ta = T + tc
    cos_t = jnp.concatenate([jnp.ones((ta, 64), F32), cos, jnp.zeros((ta, 32), F32)], axis=1)
    sin_t = jnp.concatenate([jnp.zeros((ta, 64), F32), sin, jnp.zeros((ta, 32), F32)], axis=1)
    return cos_t, sin_t


def _pad_rows(w, r0, total):
    pad = [(0, 0)] * (w.ndim - 2) + [(r0, total - r0 - w.shape[-2]), (0, 0)]
    return jnp.pad(w, pad)


def kernel(x, c, ctx, c_ctx, w_ada, b_ada, norm_pre, norm_post, ffn_w_gate, ffn_w_up, ffn_w_down, w_in, w_out, s5_lam_re, s5_lam_im, s5_log_dt, s5_b_re, s5_b_im, s5_c_re, s5_c_im, s5_d, s5_w_glu, s5_b_glu, mla_q_norm, mla_kv_norm, mla_w_uq, mla_w_ukv, rwkv_conv_w, rwkv_conv_b, rwkv_w0, rwkv_w_up, rwkv_a0, rwkv_a_up, rwkv_g_up, rwkv_k_k, rwkv_k_a, rwkv_r_k, rwkv_ln_w, rwkv_ln_b, mlstm_conv_w, mlstm_conv_b, mlstm_gate_b, mlstm_norm):
    B, T, D = x.shape
    Tc = ctx.shape[1]
    Ta = T + Tc
    L = w_ada.shape[0]
    W = GROUP_W
    assert T % CHUNK == 0 and Tc % CHUNK == 0 and B % 8 == 0 and B <= 8
    n_ctx, n_all = Tc // CHUNK, Ta // CHUNK

    rows = 16
    cvec = jnp.concatenate([c, c_ctx[None, :], jnp.zeros((rows - B - 1, D), F32)], axis=0)
    mods_all = _ada_call(cvec, w_ada, b_ada)

    src, sgn, perm, sign = _inproj_column_map()
    w_in_re = (jnp.take(w_in, jnp.asarray(src), axis=2) * jnp.asarray(sgn)).astype(BF16)
    cw = jnp.zeros((L, Z_NBLK, 8, ZB), F32)
    rc = jnp.concatenate([rwkv_conv_w, rwkv_conv_b[:, None, :]], axis=1).reshape(L, 4, 3, ZB).transpose(0, 2, 1, 3)
    mc = jnp.concatenate([mlstm_conv_w, mlstm_conv_b[:, None, :]], axis=1).reshape(L, 4, 2, ZB).transpose(0, 2, 1, 3)
    cw = cw.at[:, Z_RKV:Z_RKV + 3, 0:4].set(rc).at[:, Z_MQK:Z_MQK + 2, 0:4].set(mc)

    wg = ffn_w_gate.astype(BF16)
    wu = ffn_w_up.astype(BF16)
    wd = ffn_w_down.astype(BF16)
    wout = w_out.astype(BF16)

    G = s5_lam_re.shape[2]
    N = s5_lam_re.shape[3]
    eye_g = jnp.eye(G, dtype=F32)
    lre = s5_lam_re.reshape(L, 2, 1, G * N)
    lim = s5_lam_im.reshape(L, 2, 1, G * N)
    ldt = jnp.repeat(s5_log_dt, N, axis=-1).reshape(L, 2, 1, G * N)
    wbre = jnp.einsum('ldgnp,gh->ldgphn', s5_b_re, eye_g).reshape(L, 2, G * S5_P, G * N)
    wbim = jnp.einsum('ldgnp,gh->ldgphn', s5_b_im, eye_g).reshape(L, 2, G * S5_P, G * N)
    wcre = jnp.einsum('ldgpn,gh->ldgnhp', s5_c_re, eye_g).reshape(L, 2, G * N, G * S5_P).astype(BF16)
    wcim = jnp.einsum('ldgpn,gh->ldgnhp', s5_c_im, eye_g).reshape(L, 2, G * N, G * S5_P).astype(BF16)

    nope = HEAD_DIM
    qd = nope + ROPE_DIM
    wq4 = mla_w_uq.reshape(L, -1, N_HEADS, qd)
    wq = jnp.pad(wq4, ((0, 0), (0, 0), (0, 0), (0, 128 - qd))).reshape(L, -1, N_HEADS * 128).astype(BF16)
    wq_rot = wq4[..., nope + jnp.asarray(perm)] * jnp.asarray(sign, F32)
    wqr = jnp.pad(wq_rot, ((0, 0), (0, 0), (0, 0), (nope, 128 - qd))).reshape(L, -1, N_HEADS * 128).astype(BF16)
    wkv4 = mla_w_ukv.reshape(L, -1, N_HEADS, 2 * HEAD_DIM)
    wk = jnp.pad(wkv4[..., :HEAD_DIM], ((0, 0), (0, 0), (0, 0), (0, 64))).reshape(L, -1, N_HEADS * 128).astype(BF16)
    wv = wkv4[..., HEAD_DIM:].reshape(L, -1, W).astype(BF16)
    cos_t, sin_t = _rope_tables(T, Tc)
    scale = float(qd) ** -0.5

    wup = _pad_rows(rwkv_w_up, 0, W)
    aup = _pad_rows(rwkv_a_up, 32, W)
    gup = _pad_rows(rwkv_g_up, 64, W)
    gbias = jnp.pad(mlstm_gate_b, ((0, 0), (0, W - mlstm_gate_b.shape[1])))

    xa = jnp.concatenate([ctx, x], axis=1)
    r1 = lambda a: a.reshape(1, -1)

    for l in range(L):
        mods = mods_all[l]
        xa2, hmix = _ffn_call(xa.reshape(B * Ta, D), mods, r1(norm_pre[l, 0]), r1(norm_post[l, 0]),
                              r1(norm_pre[l, 1]), wg[l, 0], wu[l, 0], wd[l, 0],
                              ta=Ta, tc=Tc, n_batch=B, koff=0, emit_hmix=True)
        xa = xa2.reshape(B, Ta, D)
        z = _inproj_call(hmix.reshape(B, Ta, D), w_in_re[l], cw[l], tc=Tc)

        u_tm = jnp.transpose(z[:, :, Z_S5 * ZB:(Z_S5 + 1) * ZB], (1, 0, 2)).reshape(Ta * B, W)
        s5f, s5b = _s5_call(u_tm, lre[l], lim[l], ldt[l], wbre[l], wbim[l], wcre[l], wcim[l],
                            nb=B, n_ctx=n_ctx, n_all=n_all)

        q, k, v = _mla_proj_call(z, r1(mla_q_norm[l]), r1(mla_kv_norm[l]), wq[l], wqr[l], wk[l], wv[l],
                                 cos_t, sin_t, scale=scale)
        attn = _attn_call(q, k, v, tc=Tc)

        rf, rb, rg, rbon = _rwkv_call(z, wup[l], aup[l], gup[l], rwkv_w0[l][:, None, :], rwkv_a0[l][:, None, :],
                                      r1(rwkv_k_k[l]), r1(rwkv_k_a[l]), r1(rwkv_r_k[l]), n_ctx=n_ctx, n_all=n_all)
        mf, mb = _mlstm_call(z, gbias[l:l + 1], n_ctx=n_ctx, n_all=n_all)

        xa = _outproj_call(xa, mods, s5f.reshape(Ta, B * W), s5b.reshape(Ta, B * W), z, attn, rf, rb, rg, rbon,
                           mf, mb, r1(s5_d[l]), s5_w_glu[l].astype(BF16), r1(s5_b_glu[l]), r1(rwkv_ln_w[l]),
                           r1(rwkv_ln_b[l]), r1(mlstm_norm[l]), r1(norm_post[l, 1]), wout[l], tc=Tc)

        xa2, _ = _ffn_call(xa.reshape(B * Ta, D), mods, r1(norm_pre[l, 2]), r1(norm_post[l, 2]),
                           r1(norm_pre[l, 1]), wg[l, 1], wu[l, 1], wd[l, 1],
                           ta=Ta, tc=Tc, n_batch=B, koff=6, emit_hmix=False)
        xa = xa2.reshape(B, Ta, D)

    return xa[:, Tc:, :]
```

```python
import functools
import math

import numpy as np
import jax
import jax.numpy as jnp
from jax import lax
from jax.experimental import pallas as pl
from jax.experimental.pallas import tpu as pltpu

F32 = jnp.float32
BF16 = jnp.bfloat16

GROUP_W = 256
HEAD_DIM = 64
N_HEADS = GROUP_W // HEAD_DIM
CHUNK = 64
N_MOD = 9
NORM_EPS = 1e-6
RWKV_GN_EPS = HEAD_DIM * 1e-5
GRID_W = 64
ROPE_BASE = 10000.0
ROPE_DIM = 32
ROPE_AXIS = 16
S5_P = 16
S5_STATE = 64
MACARON = 0.5
VMEM_LIMIT_BYTES = 56 * 1024 * 1024

ZB = 256
Z_MLA, Z_RKV, Z_S5, Z_LORA, Z_MQK, Z_MVO, Z_GATE = 0, 3, 6, 7, 8, 10, 12
Z_NBLK = 13
Z_COLS = Z_NBLK * ZB


def _nn(a, b):
    return lax.dot_general(a, b, (((1,), (0,)), ((), ())), preferred_element_type=F32)


def _nt(a, b):
    return lax.dot_general(a, b, (((1,), (1,)), ((), ())), preferred_element_type=F32)


def _tn(a, b):
    return lax.dot_general(a, b, (((0,), (0,)), ((), ())), preferred_element_type=F32)


def _split2(x):
    hi = x.astype(BF16)
    lo = (x - hi.astype(F32)).astype(BF16)
    return hi, lo


def _split3(x):
    p1 = x.astype(BF16)
    r = x - p1.astype(F32)
    p2 = r.astype(BF16)
    p3 = (r - p2.astype(F32)).astype(BF16)
    return p1, p2, p3


def _mm3(dotf, a, b):
    ah, al = _split2(a)
    bh, bl = _split2(b)
    return dotf(ah, bh) + (dotf(ah, bl) + dotf(al, bh))


def _mm1(dotf, a, b):
    return dotf(a.astype(BF16), b.astype(BF16))


_mmn = _mm1


def _mm_exact_rhs(a, e):
    eb = e.astype(BF16)
    p1, p2, p3 = _split3(a)
    return _nn(p1, eb) + (_nn(p2, eb) + _nn(p3, eb))


def _mm_exact_lhs(e, a):
    eb = e.astype(BF16)
    p1, p2, p3 = _split3(a)
    return _nn(eb, p1) + (_nn(eb, p2) + _nn(eb, p3))


def _rms(x, g):
    return x * lax.rsqrt(jnp.mean(x * x, axis=-1, keepdims=True) + NORM_EPS) * g


def _sigmoid(x):
    return 1.0 / (1.0 + jnp.exp(-x))


def _softplus(x):
    return jnp.maximum(x, 0.0) + jnp.log(1.0 + jnp.exp(-jnp.abs(x)))


def _iota(shape, dim):
    return lax.broadcasted_iota(jnp.int32, shape, dim)


def _head_masks():
    lane = _iota((1, GROUP_W), 1)
    return [lane // HEAD_DIM == h for h in range(N_HEADS)]


def _block_diag_mask():
    r = _iota((GROUP_W, GROUP_W), 0) // HEAD_DIM
    c = _iota((GROUP_W, GROUP_W), 1) // HEAD_DIM
    return (r == c).astype(F32)


def _row_stack(x, hms):
    return jnp.concatenate([jnp.where(m, x, 0.0) for m in hms], axis=0)


def _pick_tile(n, target, mult=16):
    best = None
    for t in range(mult, min(n, target) + 1, mult):
        if n % t == 0:
            best = t
    if best is None:
        raise ValueError(f"no tile for {n}")
    return best


def _cparams(*sem):
    return pltpu.CompilerParams(dimension_semantics=sem, vmem_limit_bytes=VMEM_LIMIT_BYTES)


def _ada_kernel(c_ref, w_ref, b_ref, o_ref):
    c = c_ref[...]
    s = c * _sigmoid(c)
    o_ref[0] = _mm3(_nn, s, w_ref[0]) + b_ref[0]


def _ada_call(cvec, w_ada, b_ada):
    L, D, N = w_ada.shape
    R = cvec.shape[0]
    tn = _pick_tile(N, 1152, 128)
    return pl.pallas_call(
        _ada_kernel,
        out_shape=jax.ShapeDtypeStruct((L, R, N), F32),
        grid=(L, N // tn),
        in_specs=[pl.BlockSpec((R, D), lambda l, j: (0, 0)),
                  pl.BlockSpec((1, D, tn), lambda l, j: (l, 0, j)),
                  pl.BlockSpec((1, 1, tn), lambda l, j: (l, 0, j))],
        out_specs=pl.BlockSpec((1, R, tn), lambda l, j: (l, 0, j)),
        compiler_params=_cparams("parallel", "parallel"),
        name="ada_mod",
    )(cvec, w_ada, b_ada.reshape(L, 1, N))


def _ffn_kernel(x_ref, mods_ref, gpre_ref, gpost_ref, gmix_ref, wg_ref, wu_ref, wd_ref,
                o_ref, hmix_ref, h_scr, acc_scr, *, tm, tiles_per_batch, tc, n_batch, koff, emit_hmix):
    i = pl.program_id(0)
    f = pl.program_id(1)
    nf = pl.num_programs(1)
    d = x_ref.shape[-1]
    b = i // tiles_per_batch
    t0 = (i % tiles_per_batch) * tm
    is_ctx = (t0 + _iota((tm, 1), 0)) < tc

    def mod(k):
        mx = mods_ref[pl.ds(b, 1), pl.ds(k * d, d)]
        mc = mods_ref[pl.ds(n_batch, 1), pl.ds(k * d, d)]
        return jnp.where(is_ctx, mc, mx)

    @pl.when(f == 0)
    def _():
        h = _rms(x_ref[...], gpre_ref[...]) * (1.0 + mod(koff + 1)) + mod(koff)
        h_scr[...] = h.astype(BF16)
        acc_scr[...] = jnp.zeros_like(acc_scr)

    hb = h_scr[...]
    g = _nn(hb, wg_ref[...])
    u = _nn(hb, wu_ref[...])
    a = (g * _sigmoid(g) * u).astype(BF16)
    acc_scr[...] += _nn(a, wd_ref[...])

    @pl.when(f == nf - 1)
    def _():
        y = _rms(acc_scr[...], gpost_ref[...])
        xn = x_ref[...] + MACARON * mod(koff + 2) * y
        o_ref[...] = xn
        if emit_hmix:
            hm = _rms(xn, gmix_ref[...]) * (1.0 + mod(4)) + mod(3)
            hmix_ref[...] = hm.astype(BF16)
        else:
            hmix_ref[...] = jnp.zeros_like(hmix_ref)


def _ffn_call(xa2, mods, gpre, gpost, gmix, wg, wu, wd, *, ta, tc, n_batch, koff, emit_hmix):
    M, D = xa2.shape
    Fd = wg.shape[1]
    tm = _pick_tile(ta, 544)
    tf = _pick_tile(Fd, 1408, 128)
    kern = functools.partial(_ffn_kernel, tm=tm, tiles_per_batch=ta // tm, tc=tc, n_batch=n_batch,
                             koff=koff, emit_hmix=emit_hmix)
    hm_rows = tm if emit_hmix else 16
    return pl.pallas_call(
        kern,
        out_shape=(jax.ShapeDtypeStruct((M, D), F32),
                   jax.ShapeDtypeStruct((M if emit_hmix else 16 * (M // tm), D), BF16)),
        grid=(M // tm, Fd // tf),
        in_specs=[pl.BlockSpec((tm, D), lambda i, f: (i, 0)),
                  pl.BlockSpec(mods.shape, lambda i, f: (0, 0)),
                  pl.BlockSpec((1, D), lambda i, f: (0, 0)),
                  pl.BlockSpec((1, D), lambda i, f: (0, 0)),
                  pl.BlockSpec((1, D), lambda i, f: (0, 0)),
                  pl.BlockSpec((D, tf), lambda i, f: (0, f)),
                  pl.BlockSpec((D, tf), lambda i, f: (0, f)),
                  pl.BlockSpec((tf, D), lambda i, f: (f, 0))],
        out_specs=(pl.BlockSpec((tm, D), lambda i, f: (i, 0)),
                   pl.BlockSpec((hm_rows, D), lambda i, f: (i, 0))),
        scratch_shapes=[pltpu.VMEM((tm, D), BF16), pltpu.VMEM((tm, D), F32)],
        compiler_params=_cparams("parallel", "arbitrary"),
        name="half_ffn",
    )(xa2, mods, gpre, gpost, gmix, wg, wu, wd)


def _inproj_kernel(h_ref, w_ref, cw_ref, o_ref, *, tc, conv_lo, conv_hi, silu_lo, silu_hi):
    nb = pl.program_id(1)
    z = _nn(h_ref[0], w_ref[...])
    ta = z.shape[0]
    is_conv = ((nb >= conv_lo[0]) & (nb < conv_hi[0])) | ((nb >= conv_lo[1]) & (nb < conv_hi[1]))
    is_silu = (nb >= silu_lo) & (nb < silu_hi)

    @pl.when(jnp.logical_not(is_conv))
    def _():
        o_ref[0] = z

    def conv():
        row = _iota((ta, 1), 0)
        zp = jnp.where((row == 0) | (row == tc), 0.0, pltpu.roll(z, 1, 0))
        zn = jnp.where((row == tc - 1) | (row == ta - 1), 0.0, pltpu.roll(z, ta - 1, 0))
        cw = cw_ref[0]
        return cw[3:4] + zp * cw[0:1] + z * cw[1:2] + zn * cw[2:3]

    @pl.when(is_conv & jnp.logical_not(is_silu))
    def _():
        o_ref[0] = conv()

    @pl.when(is_conv & is_silu)
    def _():
        y = conv()
        o_ref[0] = y * _sigmoid(y)


def _inproj_call(hmix3, w_re, cw, *, tc):
    B, Ta, D = hmix3.shape
    kern = functools.partial(_inproj_kernel, tc=tc, conv_lo=(Z_RKV, Z_MQK), conv_hi=(Z_RKV + 3, Z_MQK + 2),
                             silu_lo=Z_MQK, silu_hi=Z_MQK + 2)
    return pl.pallas_call(
        kern,
        out_shape=jax.ShapeDtypeStruct((B, Ta, Z_COLS), F32),
        grid=(B, Z_NBLK),
        in_specs=[pl.BlockSpec((1, Ta, D), lambda b, n: (b, 0, 0)),
                  pl.BlockSpec((D, ZB), lambda b, n: (0, n)),
                  pl.BlockSpec((1, 8, ZB), lambda b, n: (n, 0, 0))],
        out_specs=pl.BlockSpec((1, Ta, ZB), lambda b, n: (b, 0, n)),
        compiler_params=_cparams("parallel", "arbitrary"),
        name="in_proj",
    )(hmix3, w_re, cw)


def _bwd_chunk(i, n_ctx, n_all):
    return jnp.where(i < n_ctx, n_ctx - 1 - i, n_all - 1 - (i - n_ctx))


def _s5_kernel(uf_ref, ub_ref, lre_ref, lim_ref, ldt_ref, wbre_ref, wbim_ref, wcre_ref, wcim_ref,
               yf_ref, yb_ref, wb_scr, coef_scr, st_scr, x_scr, *, lc, nb):
    i = pl.program_id(0)
    gn = lre_ref.shape[-1]

    @pl.when(i == 0)
    def _():
        for d in range(2):
            dt = jnp.exp(ldt_ref[d])
            lre = lre_ref[d]
            lim = lim_ref[d]
            mag = jnp.exp(lre * dt)
            ar = mag * jnp.cos(lim * dt)
            ai = mag * jnp.sin(lim * dt)
            den = lre * lre + lim * lim
            fr = ((ar - 1.0) * lre + ai * lim) / den
            fi = (ai * lre - (ar - 1.0) * lim) / den
            coef_scr[d, 0:nb, :] = jnp.broadcast_to(ar, (nb, gn))
            coef_scr[d, nb:2 * nb, :] = jnp.broadcast_to(ai, (nb, gn))
            wre = wbre_ref[d]
            wim = wbim_ref[d]
            wb_scr[d, :, 0:gn] = (wre * fr - wim * fi).astype(BF16)
            wb_scr[d, :, gn:2 * gn] = (wim * fr + wre * fi).astype(BF16)
        st_scr[...] = jnp.zeros_like(st_scr)

    for d, (u_ref, y_ref) in enumerate(((uf_ref, yf_ref), (ub_ref, yb_ref))):
        x_scr[...] = _nn(u_ref[...].astype(BF16), wb_scr[d])
        ar = coef_scr[d, 0:nb, :]
        ai = coef_scr[d, nb:2 * nb, :]

        def body(t, carry, d=d, ar=ar, ai=ai):
            sr, si = carry
            tt = t if d == 0 else lc - 1 - t
            r0 = pl.multiple_of(tt * nb, nb)
            xr = x_scr[pl.ds(r0, nb), 0:gn]
            xi = x_scr[pl.ds(r0, nb), gn:2 * gn]
            nsr = ar * sr - ai * si + xr
            nsi = ar * si + ai * sr + xi
            x_scr[pl.ds(r0, nb), 0:gn] = nsr
            x_scr[pl.ds(r0, nb), gn:2 * gn] = nsi
            return nsr, nsi

        sr, si = lax.fori_loop(0, lc, body, (st_scr[d, 0:nb, :], st_scr[d, nb:2 * nb, :]))
        st_scr[d, 0:nb, :] = sr
        st_scr[d, nb:2 * nb, :] = si
        y_ref[...] = (_nn(x_scr[:, 0:gn].astype(BF16), wcre_ref[d])
                      - _nn(x_scr[:, gn:2 * gn].astype(BF16), wcim_ref[d]))


def _s5_call(u_tm, lre, lim, ldt, wbre, wbim, wcre, wcim, *, nb, n_ctx, n_all):
    rows, W = u_tm.shape
    lc = CHUNK
    gn = lre.shape[-1]
    blk = lc * nb
    kern = functools.partial(_s5_kernel, lc=lc, nb=nb)
    full = lambda a: pl.BlockSpec(a.shape, lambda i: (0,) * a.ndim)
    return pl.pallas_call(
        kern,
        out_shape=(jax.ShapeDtypeStruct((rows, W), F32), jax.ShapeDtypeStruct((rows, W), F32)),
        grid=(n_all,),
        in_specs=[pl.BlockSpec((blk, W), lambda i: (i, 0)),
                  pl.BlockSpec((blk, W), lambda i: (_bwd_chunk(i, n_ctx, n_all), 0)),
                  full(lre), full(lim), full(ldt), full(wbre), full(wbim), full(wcre), full(wcim)],
        out_specs=(pl.BlockSpec((blk, W), lambda i: (i, 0)),
                   pl.BlockSpec((blk, W), lambda i: (_bwd_chunk(i, n_ctx, n_all), 0))),
        scratch_shapes=[pltpu.VMEM((2, W, 2 * gn), BF16),
                        pltpu.VMEM((2, 2 * nb, gn), F32),
                        pltpu.VMEM((2, 2 * nb, gn), F32),
                        pltpu.VMEM((blk, 2 * gn), F32)],
        compiler_params=_cparams("arbitrary"),
        name="s5_scan",
    )(u_tm, u_tm, lre, lim, ldt, wbre, wbim, wcre, wcim)


def _mla_proj_kernel(z_ref, qn_ref, kvn_ref, wq_ref, wqr_ref, wk_ref, wv_ref, cos_ref, sin_ref,
                     q_ref, k_ref, v_ref, *, scale):
    z = z_ref[0]
    cq = z[:, 0:256]
    ckv = z[:, 256:384]
    kr = z[:, 384:512]
    krr = z[:, 512:640]
    cos = cos_ref[...]
    sin = sin_ref[...]
    cqb = _rms(cq, qn_ref[...]).astype(BF16)
    ckvb = _rms(ckv, kvn_ref[...]).astype(BF16)
    q = _nn(cqb, wq_ref[...])
    qr = _nn(cqb, wqr_ref[...])
    kn = _nn(ckvb, wk_ref[...])
    krp = kr * cos + krr * sin
    for h in range(N_HEADS):
        sl = slice(h * 128, (h + 1) * 128)
        q_ref[0, h] = ((q[:, sl] * cos + qr[:, sl] * sin) * scale).astype(BF16)
        k_ref[0, h] = (kn[:, sl] + krp).astype(BF16)
    v_ref[0] = _nn(ckvb, wv_ref[...]).astype(BF16)


def _mla_proj_call(z, qn, kvn, wq, wqr, wk, wv, cos_t, sin_t, *, scale):
    B, Ta, _ = z.shape
    tm = _pick_tile(Ta, 544)
    kern = functools.partial(_mla_proj_kernel, scale=scale)
    full = lambda a: pl.BlockSpec(a.shape, lambda b, t: (0,) * a.ndim)
    return pl.pallas_call(
        kern,
        out_shape=(jax.ShapeDtypeStruct((B, N_HEADS, Ta, 128), BF16),
                   jax.ShapeDtypeStruct((B, N_HEADS, Ta, 128), BF16),
                   jax.ShapeDtypeStruct((B, Ta, GROUP_W), BF16)),
        grid=(B, Ta // tm),
        in_specs=[pl.BlockSpec((1, tm, 3 * ZB), lambda b, t: (b, t, 0)),
                  full(qn), full(kvn), full(wq), full(wqr), full(wk), full(wv),
                  pl.BlockSpec((tm, 128), lambda b, t: (t, 0)),
                  pl.BlockSpec((tm, 128), lambda b, t: (t, 0))],
        out_specs=(pl.BlockSpec((1, N_HEADS, tm, 128), lambda b, t: (b, 0, t, 0)),
                   pl.BlockSpec((1, N_HEADS, tm, 128), lambda b, t: (b, 0, t, 0)),
                   pl.BlockSpec((1, tm, GROUP_W), lambda b, t: (b, t, 0))),
        compiler_params=_cparams("parallel", "parallel"),
        name="mla_proj",
    )(z, qn, kvn, wq, wqr, wk, wv, cos_t, sin_t)


def _attn_kernel(q_ref, k_ref, v_ref, o_ref, *, tc, ta, n_ctx_tiles):
    i = pl.program_id(2)

    def run(nk):
        outs = []
        for j in range(2):
            q = q_ref[0, j]
            k = k_ref[0, j, 0:nk, :]
            v = v_ref[0, 0:nk, :]
            s = _nt(q, k)
            m = jnp.max(s, axis=-1, keepdims=True)
            p = jnp.exp(s - m)
            l = jnp.sum(p, axis=-1, keepdims=True)
            outs.append(_nn(p.astype(BF16), v) * (1.0 / l))
        lane = _iota((1, 128), 1)
        o_ref[0] = jnp.where(lane < HEAD_DIM, outs[0], outs[1])

    @pl.when(i < n_ctx_tiles)
    def _():
        run(tc)

    @pl.when(i >= n_ctx_tiles)
    def _():
        run(ta)


def _attn_call(q, k, v, *, tc):
    B, H, Ta, _ = q.shape
    tq = _pick_tile(math.gcd(tc, Ta), 256)
    kern = functools.partial(_attn_kernel, tc=tc, ta=Ta, n_ctx_tiles=tc // tq)
    return pl.pallas_call(
        kern,
        out_shape=jax.ShapeDtypeStruct((B, Ta, GROUP_W), F32),
        grid=(B, H // 2, Ta // tq),
        in_specs=[pl.BlockSpec((1, 2, tq, 128), lambda b, h, i: (b, h, i, 0)),
                  pl.BlockSpec((1, 2, Ta, 128), lambda b, h, i: (b, h, 0, 0)),
                  pl.BlockSpec((1, Ta, 128), lambda b, h, i: (b, 0, h))],
        out_specs=pl.BlockSpec((1, tq, 128), lambda b, h, i: (b, i, h)),
        compiler_params=_cparams("parallel", "parallel", "arbitrary"),
        name="mla_attn",
    )(q, k, v)


def _chunk_masks(d):
    L = CHUNK
    row = _iota((L, GROUP_W), 0)
    s_idx = _iota((L, GROUP_W), 1) % L
    tr = _iota((L, L), 0)
    tcol = _iota((L, L), 1)
    if d == 0:
        return (tcol <= tr).astype(F32), s_idx < row, s_idx <= row, s_idx == row
    return (tcol >= tr).astype(F32), s_idx > row, s_idx >= row, s_idx == row


def _rwkv_kernel(rf_ref, rb_ref, lf_ref, lb_ref, wup_ref, aup_ref, gup_ref, w0_ref, a0_ref,
                 kk_ref, ka_ref, rk_ref, yf_ref, yb_ref, g_ref, bon_ref, s_scr, *, cps):
    i = pl.program_id(1)
    L = CHUNK
    W = GROUP_W

    @pl.when(i == 0)
    def _():
        s_scr[...] = jnp.zeros_like(s_scr)

    hms = _head_masks()
    bd = _block_diag_mask()
    rs = lambda x: _row_stack(x, hms)
    bdiag = lambda x: jnp.concatenate([x] * N_HEADS, axis=0) * bd
    eye_w = (_iota((W, W), 0) == _iota((W, W), 1)).astype(F32)
    rows = cps * L
    grp_r = _iota((rows, rows), 0)
    grp_c = _iota((rows, rows), 1)
    same_chunk = (grp_r // L) == (grp_c // L)

    per_dir = []
    for d, (r_ref, l_ref) in enumerate(((rf_ref, lf_ref), (rb_ref, lb_ref))):
        rkv = r_ref[0]
        lora = l_ref[0]
        r = rkv[:, 0:W]
        k = rkv[:, W:2 * W]
        v = rkv[:, 2 * W:3 * W]
        _, strict, incl, eye = _chunk_masks(d)
        tri = (same_chunk & ((grp_c <= grp_r) if d == 0 else (grp_c >= grp_r))).astype(BF16)

        w_log = -_softplus(-(w0_ref[d] + _mm1(_nn, jnp.tanh(lora), wup_ref[d]))) - 0.5
        lw = -jnp.exp(w_log)
        a = _sigmoid(a0_ref[d] + _mm1(_nn, lora, aup_ref[d]))
        kkv = k * kk_ref[...]
        kkn = kkv / jnp.maximum(jnp.sqrt(_mm_exact_rhs(kkv * kkv, bd)), 1e-12)
        keff = k * (1.0 + (a - 1.0) * ka_ref[...])
        kka = kkn * a
        lw_hi, lw_lo = _split2(lw)
        cum2 = _nn(tri, jnp.concatenate([lw_hi, lw_lo], axis=1))
        cum = cum2[:, 0:W] + cum2[:, W:2 * W]
        e_dn = jnp.exp(-cum)
        per_dir.append(dict(v=v, lw=lw, cum=cum, kka=kka, keff=keff, strict=strict, incl=incl, eye=eye,
                            al=-kkn * jnp.exp(cum - lw), rt=r * jnp.exp(cum), bh=kka * e_dn, kh=keff * e_dn))
        if d == 0:
            g_ref[0] = _mm1(_nn, _sigmoid(lora), gup_ref[...])
            bon_ref[0] = _mm_exact_rhs(r * k * rk_ref[...], bd) * v

    chains = []
    for c in range(cps):
        for d in range(2):
            pd = per_dir[d]
            sl = slice(c * L, (c + 1) * L)
            tot = jnp.sum(pd["lw"][sl], axis=0, keepdims=True)
            e_tc = jnp.exp(tot - pd["cum"][sl])
            ch = dict(d=d, c=c, al=pd["al"][sl], rt=pd["rt"][sl], v=pd["v"][sl], tot=tot,
                      bt=pd["kka"][sl] * e_tc, kt=pd["keff"][sl] * e_tc)
            ch["rsv"] = rs(ch["v"])
            a_all = _mm1(_nt, jnp.concatenate([ch["al"], ch["rt"]], axis=0),
                         jnp.concatenate([rs(pd["bh"][sl]), rs(pd["kh"][sl])], axis=0))
            ch["a_ab"] = jnp.where(pd["strict"], a_all[0:L, 0:W], 0.0)
            ch["a_ak"] = jnp.where(pd["strict"], a_all[0:L, W:2 * W], 0.0)
            ch["a_rb"] = jnp.where(pd["incl"], a_all[L:2 * L, 0:W], 0.0)
            ch["a_rk"] = jnp.where(pd["incl"], a_all[L:2 * L, W:2 * W], 0.0)
            ch["p"] = jnp.where(pd["eye"], 1.0, 0.0) + ch["a_ab"]
            chains.append(ch)

    for ch in chains:
        ch["sq"] = _mmn(_nn, ch["a_ab"], bdiag(ch["a_ab"]))
        ch["zk"] = _mm1(_nn, ch["a_ak"], ch["rsv"])
        ch["y0k"] = _mm1(_nn, ch["a_rk"], ch["rsv"])
    n_sq = int(math.log2(L)) - 1
    for it in range(n_sq):
        for ch in chains:
            if it < n_sq - 1:
                ps = _mmn(_nn, jnp.concatenate([ch["p"], ch["sq"]], axis=0), bdiag(ch["sq"]))
                ch["p"] = ch["p"] + ps[0:L]
                ch["sq"] = ps[L:2 * L]
            else:
                ch["p"] = ch["p"] + _mmn(_nn, ch["p"], bdiag(ch["sq"]))
    for ch in chains:
        pu = _mm1(_nn, ch["p"], jnp.concatenate([rs(ch["al"]), rs(ch["zk"])], axis=1))
        ch["w"], ch["uk"] = pu[:, 0:W], pu[:, W:2 * W]
    for ch in chains:
        gy = _mm1(_nn, ch["a_rb"], jnp.concatenate([rs(ch["w"]), rs(ch["uk"])], axis=1))
        ch["g"] = ch["rt"] + gy[:, 0:W]
        ch["y0"] = gy[:, W:2 * W] + ch["y0k"]
        ch["pm"] = eye_w * jnp.exp(ch["tot"]) + bd * _mm1(_tn, ch["w"], ch["bt"])
        ch["q0"] = bd * _mm1(_tn, jnp.concatenate([ch["uk"], ch["v"]], axis=0),
                             jnp.concatenate([ch["bt"], ch["kt"]], axis=0))

    by_key = {(ch["d"], ch["c"]): ch for ch in chains}
    st = [s_scr[0], s_scr[1]]
    ys = [[None] * cps, [None] * cps]
    for step in range(cps):
        for d in range(2):
            c = step if d == 0 else cps - 1 - step
            ch = by_key[(d, c)]
            ys[d][c] = _mm1(_nt, ch["g"], st[d]) + ch["y0"]
            st[d] = _mm3(_nn, st[d], ch["pm"]) + ch["q0"]
    for d, y_ref in enumerate((yf_ref, yb_ref)):
        s_scr[d] = st[d]
        y_ref[0] = jnp.concatenate(ys[d], axis=0)


def _chunks_per_step(n_ctx, n_all):
    for cps in (4, 2, 1):
        if n_ctx % cps == 0 and (n_all - n_ctx) % cps == 0:
            return cps


def _rwkv_call(z, wup, aup, gup, w0, a0, kk, ka, rk, *, n_ctx, n_all):
    B, Ta, _ = z.shape
    W = GROUP_W
    cps = _chunks_per_step(n_ctx, n_all)
    rows = cps * CHUNK
    gc, ga = n_ctx // cps, n_all // cps
    full = lambda a: pl.BlockSpec(a.shape, lambda b, i: (0,) * a.ndim)
    fwd = lambda blk: (lambda b, i: (b, i, blk))
    bwd = lambda blk: (lambda b, i: (b, _bwd_chunk(i, gc, ga), blk))
    out = jax.ShapeDtypeStruct((B, Ta, W), F32)
    return pl.pallas_call(
        functools.partial(_rwkv_kernel, cps=cps),
        out_shape=(out, out, out, out),
        grid=(B, ga),
        in_specs=[pl.BlockSpec((1, rows, 3 * W), fwd(Z_RKV // 3)),
                  pl.BlockSpec((1, rows, 3 * W), bwd(Z_RKV // 3)),
                  pl.BlockSpec((1, rows, W), fwd(Z_LORA)),
                  pl.BlockSpec((1, rows, W), bwd(Z_LORA)),
                  full(wup), full(aup), full(gup), full(w0), full(a0), full(kk), full(ka), full(rk)],
        out_specs=(pl.BlockSpec((1, rows, W), fwd(0)), pl.BlockSpec((1, rows, W), bwd(0)),
                   pl.BlockSpec((1, rows, W), fwd(0)), pl.BlockSpec((1, rows, W), fwd(0))),
        scratch_shapes=[pltpu.VMEM((2, W, W), F32)],
        compiler_params=_cparams("parallel", "arbitrary"),
        name="rwkv7",
    )(z, z, z, z, wup, aup, gup, w0, a0, kk, ka, rk)


def _mlstm_kernel(qf_ref, qb_ref, vf_ref, vb_ref, gf_ref, gb_ref, gbias_ref,
                  yf_ref, yb_ref, c_scr, n_scr, m_scr, *, cps):
    i = pl.program_id(1)
    L = CHUNK
    W = GROUP_W

    @pl.when(i == 0)
    def _():
        c_scr[...] = jnp.zeros_like(c_scr)
        n_scr[...] = jnp.zeros_like(n_scr)
        m_scr[...] = jnp.zeros_like(m_scr)

    hms = _head_masks()
    bd = _block_diag_mask()
    rs = lambda x: _row_stack(x, hms)
    ci = _iota((W, W), 0)
    cj = _iota((W, W), 1) // HEAD_DIM
    neg_inf = -jnp.inf
    rows = cps * L
    grp_r = _iota((rows, rows), 0)
    grp_c = _iota((rows, rows), 1)
    same_chunk = (grp_r // L) == (grp_c // L)

    def sum2(x, e):
        hi, lo = _split2(x)
        eb = e.astype(BF16)
        return _nn(hi, eb) + _nn(lo, eb)

    per_dir = []
    for d, (q_ref, v_ref, g_ref) in enumerate(((qf_ref, vf_ref, gf_ref), (qb_ref, vb_ref, gb_ref))):
        qk = q_ref[0]
        g = g_ref[0] + gbias_ref[...]
        _, strict, incl, eye = _chunk_masks(d)
        tri = (same_chunk & ((grp_c <= grp_r) if d == 0 else (grp_c >= grp_r))).astype(BF16)
        li = _mm_exact_rhs(g, (ci == d * 2 * N_HEADS + cj).astype(F32))
        lf = -_softplus(-_mm_exact_rhs(g, (ci == d * 2 * N_HEADS + N_HEADS + cj).astype(F32)))
        lf_hi, lf_lo = _split2(lf)
        b2 = _nn(tri, jnp.concatenate([lf_hi, lf_lo], axis=1))
        per_dir.append(dict(q=qk[:, 0:W], k=qk[:, W:2 * W] * (HEAD_DIM ** -0.5), v=v_ref[0][:, 0:W],
                            li=li, bcol=b2[:, 0:W] + b2[:, W:2 * W], incl=incl, eye=eye))

    chains = []
    for c in range(cps):
        for d in range(2):
            pd = per_dir[d]
            sl = slice(c * L, (c + 1) * L)
            q, k, v, li, bcol = pd["q"][sl], pd["k"][sl], pd["v"][sl], pd["li"][sl], pd["bcol"][sl]
            brow = jnp.sum(jnp.where(pd["eye"], bcol, 0.0), axis=0, keepdims=True)
            lirow = jnp.sum(jnp.where(pd["eye"], li, 0.0), axis=0, keepdims=True)
            logd = jnp.where(pd["incl"], bcol - brow + lirow, neg_inf)
            mx = jnp.zeros((L, W), F32)
            for hm in hms:
                mh = jnp.max(jnp.where(hm, logd, neg_inf), axis=1, keepdims=True)
                mx = jnp.where(hm, mh, mx)
            blast = bcol[L - 1:L, :] if d == 0 else bcol[0:1, :]
            lwc = blast - bcol + li
            mlw = jnp.max(lwc, axis=0, keepdims=True)
            kw = k * jnp.exp(lwc - mlw)
            chains.append(dict(d=d, c=c, q=q, v=v, bcol=bcol, mx=mx, blast=blast, mlw=mlw, kw=kw,
                               dexp=jnp.exp(logd - mx), rsk=rs(k), rsv=rs(v),
                               nu0=jnp.sum(kw, axis=0, keepdims=True)))
    for ch in chains:
        ch["sp"] = _mm1(_nt, ch["q"], ch["rsk"]) * ch["dexp"]
        ch["cu0"] = bd * _mm1(_tn, ch["v"], ch["kw"])
    for ch in chains:
        ch["num0"] = _mm1(_nn, ch["sp"], ch["rsv"])
        ch["den0"] = sum2(ch["sp"], bd)

    by_key = {(ch["d"], ch["c"]): ch for ch in chains}
    c_st = [c_scr[0], c_scr[1]]
    n_row = [n_scr[0, 0:1, :], n_scr[1, 0:1, :]]
    m_row = [m_scr[0, 0:1, :], m_scr[1, 0:1, :]]
    ys = [[None] * cps, [None] * cps]
    for step in range(cps):
        for d in range(2):
            c = step if d == 0 else cps - 1 - step
            ch = by_key[(d, c)]
            inter = ch["bcol"] + m_row[d]
            mt = jnp.maximum(inter, ch["mx"])
            f_in = jnp.exp(ch["mx"] - mt)
            w_int = jnp.exp(inter - mt)
            num = f_in * ch["num0"] + w_int * _mm1(_nt, ch["q"], c_st[d])
            den = f_in * ch["den0"] + w_int * sum2(ch["q"] * n_row[d], bd)
            ys[d][c] = num / jnp.maximum(jnp.abs(den), jnp.exp(-mt))
            m_new = jnp.maximum(ch["blast"] + m_row[d], ch["mlw"])
            sc = jnp.exp(ch["blast"] + m_row[d] - m_new)
            e2 = jnp.exp(ch["mlw"] - m_new)
            c_st[d] = sc * c_st[d] + e2 * ch["cu0"]
            n_row[d] = sc * n_row[d] + e2 * ch["nu0"]
            m_row[d] = m_new
    for d, y_ref in enumerate((yf_ref, yb_ref)):
        c_scr[d] = c_st[d]
        n_scr[d, 0:1, :] = n_row[d]
        m_scr[d, 0:1, :] = m_row[d]
        y_ref[0] = jnp.concatenate(ys[d], axis=0)


def _mlstm_call(z, gbias, *, n_ctx, n_all):
    B, Ta, _ = z.shape
    W = GROUP_W
    cps = _chunks_per_step(n_ctx, n_all)
    L = cps * CHUNK
    n_ctx, n_all = n_ctx // cps, n_all // cps
    fwd = lambda blk: (lambda b, i: (b, i, blk))
    bwd = lambda blk: (lambda b, i: (b, _bwd_chunk(i, n_ctx, n_all), blk))
    out = jax.ShapeDtypeStruct((B, Ta, W), F32)
    return pl.pallas_call(
        functools.partial(_mlstm_kernel, cps=cps),
        out_shape=(out, out),
        grid=(B, n_all),
        in_specs=[pl.BlockSpec((1, L, 2 * W), fwd(Z_MQK // 2)),
                  pl.BlockSpec((1, L, 2 * W), bwd(Z_MQK // 2)),
                  pl.BlockSpec((1, L, 2 * W), fwd(Z_MVO // 2)),
                  pl.BlockSpec((1, L, 2 * W), bwd(Z_MVO // 2)),
                  pl.BlockSpec((1, L, W), fwd(Z_GATE)),
                  pl.BlockSpec((1, L, W), bwd(Z_GATE)),
                  pl.BlockSpec(gbias.shape, lambda b, i: (0, 0))],
        out_specs=(pl.BlockSpec((1, L, W), fwd(0)), pl.BlockSpec((1, L, W), bwd(0))),
        scratch_shapes=[pltpu.VMEM((2, W, W), F32), pltpu.VMEM((2, 8, W), F32), pltpu.VMEM((2, 8, W), F32)],
        compiler_params=_cparams("parallel", "arbitrary"),
        name="mlstm",
    )(z, z, z, z, z, z, gbias)


def _head_norm(y, bd, eps):
    mu = _mm_exact_rhs(y, bd) * (1.0 / HEAD_DIM)
    yc = y - mu
    var = _mm_exact_rhs(yc * yc, bd) * (1.0 / HEAD_DIM)
    return yc * lax.rsqrt(var + eps)


def _outproj_kernel(x_ref, mods_ref, s5f_ref, s5b_ref, u_ref, at_ref, rf_ref, rb_ref, rg_ref, rbon_ref,
                    mf_ref, mb_ref, vo_ref, s5d_ref, wglu_ref, bglu_ref, lnw_ref, lnb_ref, mnw_ref,
                    gpost_ref, wout_ref, o_ref, *, tm, tc, n_batch):
    b = pl.program_id(0)
    t = pl.program_id(1)
    d = x_ref.shape[-1]
    W = GROUP_W
    is_ctx = (t * tm + _iota((tm, 1), 0)) < tc
    gate = jnp.where(is_ctx, mods_ref[pl.ds(n_batch, 1), pl.ds(5 * d, d)],
                     mods_ref[pl.ds(b, 1), pl.ds(5 * d, d)])
    bd = _block_diag_mask()

    y = s5f_ref[...] + s5b_ref[...] + s5d_ref[...] * u_ref[0]
    zg = 0.5 * y * (1.0 + jnp.tanh(math.sqrt(2.0 / math.pi) * (y + 0.044715 * (y * y * y))))
    s5o = zg * _sigmoid(_mm1(_nn, zg, wglu_ref[...]) + bglu_ref[...])

    yr = _head_norm(rf_ref[0] + rb_ref[0], bd, RWKV_GN_EPS)
    rwo = (yr * lnw_ref[...] + lnb_ref[...] + rbon_ref[0]) * rg_ref[0]

    ym = _head_norm(mf_ref[0] + mb_ref[0], bd, NORM_EPS)
    mlo = ym * mnw_ref[...] * _sigmoid(vo_ref[0][:, W:2 * W])

    cat = jnp.concatenate([s5o, at_ref[0], rwo, mlo], axis=1).astype(BF16)
    yx = _nn(cat, wout_ref[...])
    o_ref[0] = x_ref[0] + gate * _rms(yx, gpost_ref[...])


def _outproj_call(xa, mods, s5f, s5b, z, attn, rf, rb, rg, rbon, mf, mb,
                  s5d, wglu, bglu, lnw, lnb, mnw, gpost, wout, *, tc):
    B, Ta, D = xa.shape
    W = GROUP_W
    tm = _pick_tile(Ta, 544)
    kern = functools.partial(_outproj_kernel, tm=tm, tc=tc, n_batch=B)
    full = lambda a: pl.BlockSpec(a.shape, lambda b, t: (0,) * a.ndim)
    tok = pl.BlockSpec((1, tm, W), lambda b, t: (b, t, 0))
    tmaj = pl.BlockSpec((tm, W), lambda b, t: (t, b))
    return pl.pallas_call(
        kern,
        out_shape=jax.ShapeDtypeStruct((B, Ta, D), F32),
        grid=(B, Ta // tm),
        in_specs=[pl.BlockSpec((1, tm, D), lambda b, t: (b, t, 0)), full(mods),
                  tmaj, tmaj, pl.BlockSpec((1, tm, W), lambda b, t: (b, t, Z_S5)),
                  tok, tok, tok, tok, tok, tok, tok,
                  pl.BlockSpec((1, tm, 2 * W), lambda b, t: (b, t, Z_MVO // 2)),
                  full(s5d), full(wglu), full(bglu), full(lnw), full(lnb), full(mnw), full(gpost), full(wout)],
        out_specs=pl.BlockSpec((1, tm, D), lambda b, t: (b, t, 0)),
        compiler_params=_cparams("parallel", "parallel"),
        name="mix_out",
    )(xa, mods, s5f, s5b, z, attn, rf, rb, rg, rbon, mf, mb, z, s5d, wglu, bglu, lnw, lnb, mnw, gpost, wout)


def _inproj_column_map():
    src = np.zeros((Z_COLS,), np.int32)
    sgn = np.zeros((Z_COLS,), np.float32)

    def put(dst, s0, n):
        src[dst:dst + n] = np.arange(s0, s0 + n)
        sgn[dst:dst + n] = 1.0

    o_s5, o_mla, o_rw, o_ml = 0, 256, 672, 1568
    put(0, o_mla, 256)
    put(256, o_mla + 256, 128)
    put(384 + 64, o_mla + 384, ROPE_DIM)
    h = ROPE_AXIS // 2
    perm = np.concatenate([np.arange(h, 2 * h), np.arange(0, h), np.arange(3 * h, 4 * h), np.arange(2 * h, 3 * h)])
    sign = np.concatenate([-np.ones(h), np.ones(h), -np.ones(h), np.ones(h)])
    src[512 + 64:512 + 64 + ROPE_DIM] = o_mla + 384 + perm
    sgn[512 + 64:512 + 64 + ROPE_DIM] = sign
    put(Z_RKV * ZB, o_rw, 768)
    put(Z_S5 * ZB, o_s5, 256)
    put(Z_LORA * ZB, o_rw + 768, 128)
    put(Z_MQK * ZB, o_ml, 512)
    put(Z_MVO * ZB, o_ml + 512, 512)
    put(Z_GATE * ZB, o_ml + 1024, 16)
    return src, sgn, perm, sign


def _rope_tables(T, tc):
    rows = T // GRID_W
    r_idx, c_idx = jnp.meshgrid(jnp.arange(rows), jnp.arange(GRID_W), indexing='ij')
    inv_freq = 1.0 / (ROPE_BASE ** (jnp.arange(0, ROPE_AXIS, 2, dtype=F32) / ROPE_AXIS))
    ang_r = r_idx.reshape(-1, 1).astype(F32) * inv_freq
    ang_c = c_idx.reshape(-1, 1).astype(F32) * inv_freq
    ang = jnp.concatenate([ang_r, ang_r, ang_c, ang_c], axis=-1)
    cos = jnp.concatenate([jnp.ones((tc, ROPE_DIM), F32), jnp.cos(ang)], axis=0)
    sin = jnp.concatenate([jnp.zeros((tc, ROPE_DIM), F32), jnp.sin(ang)], axis=0)
    ta = T + tc
    cos_t = jnp.concatenate([jnp.ones((ta, 64), F32), cos, jnp.zeros((ta, 32), F32)], axis=1)
    sin_t = jnp.concatenate([jnp.zeros((ta, 64), F32), sin, jnp.zeros((ta, 32), F32)], axis=1)
    return cos_t, sin_t


def _pad_rows(w, r0, total):
    pad = [(0, 0)] * (w.ndim - 2) + [(r0, total - r0 - w.shape[-2]), (0, 0)]
    return jnp.pad(w, pad)


def kernel(x, c, ctx, c_ctx, w_ada, b_ada, norm_pre, norm_post, ffn_w_gate, ffn_w_up, ffn_w_down, w_in, w_out, s5_lam_re, s5_lam_im, s5_log_dt, s5_b_re, s5_b_im, s5_c_re, s5_c_im, s5_d, s5_w_glu, s5_b_glu, mla_q_norm, mla_kv_norm, mla_w_uq, mla_w_ukv, rwkv_conv_w, rwkv_conv_b, rwkv_w0, rwkv_w_up, rwkv_a0, rwkv_a_up, rwkv_g_up, rwkv_k_k, rwkv_k_a, rwkv_r_k, rwkv_ln_w, rwkv_ln_b, mlstm_conv_w, mlstm_conv_b, mlstm_gate_b, mlstm_norm):
    B, T, D = x.shape
    Tc = ctx.shape[1]
    Ta = T + Tc
    L = w_ada.shape[0]
    W = GROUP_W
    assert T % CHUNK == 0 and Tc % CHUNK == 0 and B % 8 == 0 and B <= 8
    n_ctx, n_all = Tc // CHUNK, Ta // CHUNK

    rows = 16
    cvec = jnp.concatenate([c, c_ctx[None, :], jnp.zeros((rows - B - 1, D), F32)], axis=0)
    mods_all = _ada_call(cvec, w_ada, b_ada)

    src, sgn, perm, sign = _inproj_column_map()
    w_in_re = (jnp.take(w_in, jnp.asarray(src), axis=2) * jnp.asarray(sgn)).astype(BF16)
    cw = jnp.zeros((L, Z_NBLK, 8, ZB), F32)
    rc = jnp.concatenate([rwkv_conv_w, rwkv_conv_b[:, None, :]], axis=1).reshape(L, 4, 3, ZB).transpose(0, 2, 1, 3)
    mc = jnp.concatenate([mlstm_conv_w, mlstm_conv_b[:, None, :]], axis=1).reshape(L, 4, 2, ZB).transpose(0, 2, 1, 3)
    cw = cw.at[:, Z_RKV:Z_RKV + 3, 0:4].set(rc).at[:, Z_MQK:Z_MQK + 2, 0:4].set(mc)

    wg = ffn_w_gate.astype(BF16)
    wu = ffn_w_up.astype(BF16)
    wd = ffn_w_down.astype(BF16)
    wout = w_out.astype(BF16)

    G = s5_lam_re.shape[2]
    N = s5_lam_re.shape[3]
    eye_g = jnp.eye(G, dtype=F32)
    lre = s5_lam_re.reshape(L, 2, 1, G * N)
    lim = s5_lam_im.reshape(L, 2, 1, G * N)
    ldt = jnp.repeat(s5_log_dt, N, axis=-1).reshape(L, 2, 1, G * N)
    wbre = jnp.einsum('ldgnp,gh->ldgphn', s5_b_re, eye_g).reshape(L, 2, G * S5_P, G * N)
    wbim = jnp.einsum('ldgnp,gh->ldgphn', s5_b_im, eye_g).reshape(L, 2, G * S5_P, G * N)
    wcre = jnp.einsum('ldgpn,gh->ldgnhp', s5_c_re, eye_g).reshape(L, 2, G * N, G * S5_P).astype(BF16)
    wcim = jnp.einsum('ldgpn,gh->ldgnhp', s5_c_im, eye_g).reshape(L, 2, G * N, G * S5_P).astype(BF16)

    nope = HEAD_DIM
    qd = nope + ROPE_DIM
    wq4 = mla_w_uq.reshape(L, -1, N_HEADS, qd)
    wq = jnp.pad(wq4, ((0, 0), (0, 0), (0, 0), (0, 128 - qd))).reshape(L, -1, N_HEADS * 128).astype(BF16)
    wq_rot = wq4[..., nope + jnp.asarray(perm)] * jnp.asarray(sign, F32)
    wqr = jnp.pad(wq_rot, ((0, 0), (0, 0), (0, 0), (nope, 128 - qd))).reshape(L, -1, N_HEADS * 128).astype(BF16)
    wkv4 = mla_w_ukv.reshape(L, -1, N_HEADS, 2 * HEAD_DIM)
    wk = jnp.pad(wkv4[..., :HEAD_DIM], ((0, 0), (0, 0), (0, 0), (0, 64))).reshape(L, -1, N_HEADS * 128).astype(BF16)
    wv = wkv4[..., HEAD_DIM:].reshape(L, -1, W).astype(BF16)
    cos_t, sin_t = _rope_tables(T, Tc)
    scale = float(qd) ** -0.5

    wup = _pad_rows(rwkv_w_up, 0, W)
    aup = _pad_rows(rwkv_a_up, 32, W)
    gup = _pad_rows(rwkv_g_up, 64, W)
    gbias = jnp.pad(mlstm_gate_b, ((0, 0), (0, W - mlstm_gate_b.shape[1])))

    xa = jnp.concatenate([ctx, x], axis=1)
    r1 = lambda a: a.reshape(1, -1)

    for l in range(L):
        mods = mods_all[l]
        xa2, hmix = _ffn_call(xa.reshape(B * Ta, D), mods, r1(norm_pre[l, 0]), r1(norm_post[l, 0]),
                              r1(norm_pre[l, 1]), wg[l, 0], wu[l, 0], wd[l, 0],
                              ta=Ta, tc=Tc, n_batch=B, koff=0, emit_hmix=True)
        xa = xa2.reshape(B, Ta, D)
        z = _inproj_call(hmix.reshape(B, Ta, D), w_in_re[l], cw[l], tc=Tc)

        u_tm = jnp.transpose(z[:, :, Z_S5 * ZB:(Z_S5 + 1) * ZB], (1, 0, 2)).reshape(Ta * B, W)
        s5f, s5b = _s5_call(u_tm, lre[l], lim[l], ldt[l], wbre[l], wbim[l], wcre[l], wcim[l],
                            nb=B, n_ctx=n_ctx, n_all=n_all)

        q, k, v = _mla_proj_call(z, r1(mla_q_norm[l]), r1(mla_kv_norm[l]), wq[l], wqr[l], wk[l], wv[l],
                                 cos_t, sin_t, scale=scale)
        attn = _attn_call(q, k, v, tc=Tc)

        rf, rb, rg, rbon = _rwkv_call(z, wup[l], aup[l], gup[l], rwkv_w0[l][:, None, :], rwkv_a0[l][:, None, :],
                                      r1(rwkv_k_k[l]), r1(rwkv_k_a[l]), r1(rwkv_r_k[l]), n_ctx=n_ctx, n_all=n_all)
        mf, mb = _mlstm_call(z, gbias[l:l + 1], n_ctx=n_ctx, n_all=n_all)

        xa = _outproj_call(xa, mods, s5f.reshape(Ta, B * W), s5b.reshape(Ta, B * W), z, attn, rf, rb, rg, rbon,
                           mf, mb, r1(s5_d[l]), s5_w_glu[l].astype(BF16), r1(s5_b_glu[l]), r1(rwkv_ln_w[l]),
                           r1(rwkv_ln_b[l]), r1(mlstm_norm[l]), r1(norm_post[l, 1]), wout[l], tc=Tc)

        xa2, _ = _ffn_call(xa.reshape(B * Ta, D), mods, r1(norm_pre[l, 2]), r1(norm_post[l, 2]),
                           r1(norm_pre[l, 1]), wg[l, 1], wu[l, 1], wd[l, 1],
                           ta=Ta, tc=Tc, n_batch=B, koff=6, emit_hmix=False)
        xa = xa2.reshape(B, Ta, D)

    return xa[:, Tc:, :]
```

```python
import functools
import math

import numpy as np
import jax
import jax.numpy as jnp
from jax import lax
from jax.experimental import pallas as pl
from jax.experimental.pallas import tpu as pltpu

F32 = jnp.float32
BF16 = jnp.bfloat16

GROUP_W = 256
HEAD_DIM = 64
N_HEADS = GROUP_W // HEAD_DIM
CHUNK = 64
N_MOD = 9
NORM_EPS = 1e-6
RWKV_GN_EPS = HEAD_DIM * 1e-5
GRID_W = 64
ROPE_BASE = 10000.0
ROPE_DIM = 32
ROPE_AXIS = 16
S5_P = 16
S5_STATE = 64
MACARON = 0.5
VMEM_LIMIT_BYTES = 56 * 1024 * 1024

ZB = 256
Z_MLA, Z_RKV, Z_S5, Z_LORA, Z_MQK, Z_MVO, Z_GATE = 0, 3, 6, 7, 8, 10, 12
Z_NBLK = 13
Z_COLS = Z_NBLK * ZB


def _nn(a, b):
    return lax.dot_general(a, b, (((1,), (0,)), ((), ())), preferred_element_type=F32)


def _nt(a, b):
    return lax.dot_general(a, b, (((1,), (1,)), ((), ())), preferred_element_type=F32)


def _tn(a, b):
    return lax.dot_general(a, b, (((0,), (0,)), ((), ())), preferred_element_type=F32)


def _split2(x):
    hi = x.astype(BF16)
    lo = (x - hi.astype(F32)).astype(BF16)
    return hi, lo


def _split3(x):
    p1 = x.astype(BF16)
    r = x - p1.astype(F32)
    p2 = r.astype(BF16)
    p3 = (r - p2.astype(F32)).astype(BF16)
    return p1, p2, p3


def _mm3(dotf, a, b):
    ah, al = _split2(a)
    bh, bl = _split2(b)
    return dotf(ah, bh) + (dotf(ah, bl) + dotf(al, bh))


def _mm1(dotf, a, b):
    return dotf(a.astype(BF16), b.astype(BF16))


_mmn = _mm1


def _mm_exact_rhs(a, e):
    eb = e.astype(BF16)
    p1, p2, p3 = _split3(a)
    return _nn(p1, eb) + (_nn(p2, eb) + _nn(p3, eb))


def _mm_exact_lhs(e, a):
    eb = e.astype(BF16)
    p1, p2, p3 = _split3(a)
    return _nn(eb, p1) + (_nn(eb, p2) + _nn(eb, p3))


def _rms(x, g):
    return x * lax.rsqrt(jnp.mean(x * x, axis=-1, keepdims=True) + NORM_EPS) * g


def _sigmoid(x):
    return 1.0 / (1.0 + jnp.exp(-x))


def _softplus(x):
    return jnp.maximum(x, 0.0) + jnp.log(1.0 + jnp.exp(-jnp.abs(x)))


def _iota(shape, dim):
    return lax.broadcasted_iota(jnp.int32, shape, dim)


def _head_masks():
    lane = _iota((1, GROUP_W), 1)
    return [lane // HEAD_DIM == h for h in range(N_HEADS)]


def _block_diag_mask():
    r = _iota((GROUP_W, GROUP_W), 0) // HEAD_DIM
    c = _iota((GROUP_W, GROUP_W), 1) // HEAD_DIM
    return (r == c).astype(F32)


def _row_stack(x, hms):
    return jnp.concatenate([jnp.where(m, x, 0.0) for m in hms], axis=0)


def _pick_tile(n, target, mult=16):
    best = None
    for t in range(mult, min(n, target) + 1, mult):
        if n % t == 0:
            best = t
    if best is None:
        raise ValueError(f"no tile for {n}")
    return best


def _cparams(*sem):
    return pltpu.CompilerParams(dimension_semantics=sem, vmem_limit_bytes=VMEM_LIMIT_BYTES)


def _ada_kernel(c_ref, w_ref, b_ref, o_ref):
    c = c_ref[...]
    s = c * _sigmoid(c)
    o_ref[0] = _mm3(_nn, s, w_ref[0]) + b_ref[0]


def _ada_call(cvec, w_ada, b_ada):
    L, D, N = w_ada.shape
    R = cvec.shape[0]
    tn = _pick_tile(N, 1152, 128)
    return pl.pallas_call(
        _ada_kernel,
        out_shape=jax.ShapeDtypeStruct((L, R, N), F32),
        grid=(L, N // tn),
        in_specs=[pl.BlockSpec((R, D), lambda l, j: (0, 0)),
                  pl.BlockSpec((1, D, tn), lambda l, j: (l, 0, j)),
                  pl.BlockSpec((1, 1, tn), lambda l, j: (l, 0, j))],
        out_specs=pl.BlockSpec((1, R, tn), lambda l, j: (l, 0, j)),
        compiler_params=_cparams("parallel", "parallel"),
        name="ada_mod",
    )(cvec, w_ada, b_ada.reshape(L, 1, N))


FFN_COLS = 256


def _ffn_kernel(xn_ref, xp_ref, mods_ref, gpre_ref, gpost_ref, gmix_ref, wg_ref, wu_ref, wd_ref,
                o_ref, hmix_ref, h_scr, acc_scr, *, tm, tiles_per_batch, tc, n_batch, koff, emit_hmix,
                n_tiles, f_split):
    i = pl.program_id(0)
    f = pl.program_id(1)
    d = xn_ref.shape[-1]
    fd = wg_ref.shape[1]

    def mod_of(tile):
        b = tile // tiles_per_batch
        t0 = (tile % tiles_per_batch) * tm
        is_ctx = (t0 + _iota((tm, 1), 0)) < tc

        def mod(k):
            mx = mods_ref[pl.ds(b, 1), pl.ds(k * d, d)]
            mc = mods_ref[pl.ds(n_batch, 1), pl.ds(k * d, d)]
            return jnp.where(is_ctx, mc, mx)
        return mod

    def pre_norm(tile, x):
        mod = mod_of(tile)
        return (_rms(x, gpre_ref[...]) * (1.0 + mod(koff + 1)) + mod(koff)).astype(BF16)

    def finish(tile, acc, x):
        mod = mod_of(tile)
        xn = x + MACARON * mod(koff + 2) * _rms(acc, gpost_ref[...])
        o_ref[...] = xn
        if emit_hmix:
            hmix_ref[...] = (_rms(xn, gmix_ref[...]) * (1.0 + mod(4)) + mod(3)).astype(BF16)
        else:
            hmix_ref[...] = jnp.zeros_like(hmix_ref)

    def hidden_cols(hb, lo, hi):
        acc = None
        for c0 in range(lo, hi, FFN_COLS):
            g = _nn(hb, wg_ref[:, c0:c0 + FFN_COLS])
            u = _nn(hb, wu_ref[:, c0:c0 + FFN_COLS])
            t = _nn((g * _sigmoid(g) * u).astype(BF16), wd_ref[c0:c0 + FFN_COLS, :])
            acc = t if acc is None else acc + t
        return acc

    @pl.when(jnp.logical_and(i == 0, f == 0))
    def _():
        h_scr[0] = pre_norm(0, xn_ref[...])
        acc_scr[1] = jnp.zeros((tm, d), F32)

    for s in range(2):
        tile = 2 * i + s

        @pl.when(jnp.logical_and(tile < n_tiles, f == 2 * s))
        def _(s=s, tile=tile):
            finish(jnp.maximum(tile - 1, 0), acc_scr[1 - s], xp_ref[...])
            acc_scr[s] = hidden_cols(h_scr[s], 0, f_split)

        @pl.when(jnp.logical_and(tile < n_tiles, f == 2 * s + 1))
        def _(s=s, tile=tile):
            h_scr[1 - s] = pre_norm(jnp.minimum(tile + 1, n_tiles - 1), xn_ref[...])
            acc_scr[s] += hidden_cols(h_scr[s], f_split, fd)

    @pl.when(jnp.logical_and(2 * i == n_tiles, f == 0))
    def _():
        finish(n_tiles - 1, acc_scr[1], xp_ref[...])


def _ffn_call(xa2, mods, gpre, gpost, gmix, wg, wu, wd, *, ta, tc, n_batch, koff, emit_hmix):
    M, D = xa2.shape
    Fd = wg.shape[1]
    assert Fd % FFN_COLS == 0
    tm = _pick_tile(ta, 544)
    n_tiles = M // tm
    assert n_tiles % 2 == 0
    f_split = ((Fd // FFN_COLS + 1) // 2) * FFN_COLS
    kern = functools.partial(_ffn_kernel, tm=tm, tiles_per_batch=ta // tm, tc=tc, n_batch=n_batch,
                             koff=koff, emit_hmix=emit_hmix, n_tiles=n_tiles, f_split=f_split)
    hm_rows = tm if emit_hmix else 16
    last = n_tiles - 1
    resident = lambda a: pl.BlockSpec(a.shape, lambda i, f: (0, 0), pipeline_mode=pl.Buffered(1))
    vec = pl.BlockSpec((1, D), lambda i, f: (0, 0))
    return pl.pallas_call(
        kern,
        out_shape=(jax.ShapeDtypeStruct((M, D), F32),
                   jax.ShapeDtypeStruct((M if emit_hmix else 16 * n_tiles, D), BF16)),
        grid=(n_tiles // 2 + 1, 4),
        in_specs=[pl.BlockSpec((tm, D), lambda i, f: (jnp.minimum(2 * i + (f + 1) // 2, last), 0)),
                  pl.BlockSpec((tm, D), lambda i, f: (jnp.clip(2 * i - 1 + f // 2, 0, last), 0)),
                  pl.BlockSpec(mods.shape, lambda i, f: (0, 0)),
                  vec, vec, vec, resident(wg), resident(wu), resident(wd)],
        out_specs=(pl.BlockSpec((tm, D), lambda i, f: (jnp.clip(2 * i - 1 + f // 2, 0, last), 0)),
                   pl.BlockSpec((hm_rows, D), lambda i, f: (jnp.clip(2 * i - 1 + f // 2, 0, last), 0))),
        scratch_shapes=[pltpu.VMEM((2, tm, D), BF16), pltpu.VMEM((2, tm, D), F32)],
        compiler_params=_cparams("arbitrary", "arbitrary"),
        name="half_ffn",
    )(xa2, xa2, mods, gpre, gpost, gmix, wg, wu, wd)


def _inproj_kernel(h_ref, w_ref, cw_ref, o_ref, *, tc, conv_lo, conv_hi, silu_lo, silu_hi):
    nb = pl.program_id(1)
    z = _nn(h_ref[0], w_ref[...])
    ta = z.shape[0]
    is_conv = ((nb >= conv_lo[0]) & (nb < conv_hi[0])) | ((nb >= conv_lo[1]) & (nb < conv_hi[1]))
    is_silu = (nb >= silu_lo) & (nb < silu_hi)

    @pl.when(jnp.logical_not(is_conv))
    def _():
        o_ref[0] = z

    def conv():
        row = _iota((ta, 1), 0)
        zp = jnp.where((row == 0) | (row == tc), 0.0, pltpu.roll(z, 1, 0))
        zn = jnp.where((row == tc - 1) | (row == ta - 1), 0.0, pltpu.roll(z, ta - 1, 0))
        cw = cw_ref[0]
        return cw[3:4] + zp * cw[0:1] + z * cw[1:2] + zn * cw[2:3]

    @pl.when(is_conv & jnp.logical_not(is_silu))
    def _():
        o_ref[0] = conv()

    @pl.when(is_conv & is_silu)
    def _():
        y = conv()
        o_ref[0] = y * _sigmoid(y)


def _inproj_call(hmix3, w_re, cw, *, tc):
    B, Ta, D = hmix3.shape
    kern = functools.partial(_inproj_kernel, tc=tc, conv_lo=(Z_RKV, Z_MQK), conv_hi=(Z_RKV + 3, Z_MQK + 2),
                             silu_lo=Z_MQK, silu_hi=Z_MQK + 2)
    return pl.pallas_call(
        kern,
        out_shape=jax.ShapeDtypeStruct((B, Ta, Z_COLS), F32),
        grid=(B, Z_NBLK),
        in_specs=[pl.BlockSpec((1, Ta, D), lambda b, n: (b, 0, 0)),
                  pl.BlockSpec((D, ZB), lambda b, n: (0, n)),
                  pl.BlockSpec((1, 8, ZB), lambda b, n: (n, 0, 0))],
        out_specs=pl.BlockSpec((1, Ta, ZB), lambda b, n: (b, 0, n)),
        compiler_params=_cparams("parallel", "arbitrary"),
        name="in_proj",
    )(hmix3, w_re, cw)


def _bwd_chunk(i, n_ctx, n_all):
    return jnp.where(i < n_ctx, n_ctx - 1 - i, n_all - 1 - (i - n_ctx))


def _s5_kernel(uf_ref, ub_ref, lre_ref, lim_ref, ldt_ref, wbre_ref, wbim_ref, wcre_ref, wcim_ref,
               yf_ref, yb_ref, wb_scr, coef_scr, st_scr, x_scr, *, lc, nb):
    i = pl.program_id(0)
    gn = lre_ref.shape[-1]

    @pl.when(i == 0)
    def _():
        for d in range(2):
            dt = jnp.exp(ldt_ref[d])
            lre = lre_ref[d]
            lim = lim_ref[d]
            mag = jnp.exp(lre * dt)
            ar = mag * jnp.cos(lim * dt)
            ai = mag * jnp.sin(lim * dt)
            den = lre * lre + lim * lim
            fr = ((ar - 1.0) * lre + ai * lim) / den
            fi = (ai * lre - (ar - 1.0) * lim) / den
            coef_scr[d, 0:nb, :] = jnp.broadcast_to(ar, (nb, gn))
            coef_scr[d, nb:2 * nb, :] = jnp.broadcast_to(ai, (nb, gn))
            wre = wbre_ref[d]
            wim = wbim_ref[d]
            wb_scr[d, :, 0:gn] = (wre * fr - wim * fi).astype(BF16)
            wb_scr[d, :, gn:2 * gn] = (wim * fr + wre * fi).astype(BF16)
        st_scr[...] = jnp.zeros_like(st_scr)

    for d, (u_ref, y_ref) in enumerate(((uf_ref, yf_ref), (ub_ref, yb_ref))):
        x_scr[...] = _nn(u_ref[...].astype(BF16), wb_scr[d])
        ar = coef_scr[d, 0:nb, :]
        ai = coef_scr[d, nb:2 * nb, :]

        def body(t, carry, d=d, ar=ar, ai=ai):
            sr, si = carry
            tt = t if d == 0 else lc - 1 - t
            r0 = pl.multiple_of(tt * nb, nb)
            xr = x_scr[pl.ds(r0, nb), 0:gn]
            xi = x_scr[pl.ds(r0, nb), gn:2 * gn]
            nsr = ar * sr - ai * si + xr
            nsi = ar * si + ai * sr + xi
            x_scr[pl.ds(r0, nb), 0:gn] = nsr
            x_scr[pl.ds(r0, nb), gn:2 * gn] = nsi
            return nsr, nsi

        sr, si = lax.fori_loop(0, lc, body, (st_scr[d, 0:nb, :], st_scr[d, nb:2 * nb, :]))
        st_scr[d, 0:nb, :] = sr
        st_scr[d, nb:2 * nb, :] = si
        y_ref[...] = (_nn(x_scr[:, 0:gn].astype(BF16), wcre_ref[d])
                      - _nn(x_scr[:, gn:2 * gn].astype(BF16), wcim_ref[d]))


def _s5_call(u_tm, lre, lim, ldt, wbre, wbim, wcre, wcim, *, nb, n_ctx, n_all):
    rows, W = u_tm.shape
    lc = CHUNK
    gn = lre.shape[-1]
    blk = lc * nb
    kern = functools.partial(_s5_kernel, lc=lc, nb=nb)
    full = lambda a: pl.BlockSpec(a.shape, lambda i: (0,) * a.ndim)
    return pl.pallas_call(
        kern,
        out_shape=(jax.ShapeDtypeStruct((rows, W), F32), jax.ShapeDtypeStruct((rows, W), F32)),
        grid=(n_all,),
        in_specs=[pl.BlockSpec((blk, W), lambda i: (i, 0)),
                  pl.BlockSpec((blk, W), lambda i: (_bwd_chunk(i, n_ctx, n_all), 0)),
                  full(lre), full(lim), full(ldt), full(wbre), full(wbim), full(wcre), full(wcim)],
        out_specs=(pl.BlockSpec((blk, W), lambda i: (i, 0)),
                   pl.BlockSpec((blk, W), lambda i: (_bwd_chunk(i, n_ctx, n_all), 0))),
        scratch_shapes=[pltpu.VMEM((2, W, 2 * gn), BF16),
                        pltpu.VMEM((2, 2 * nb, gn), F32),
                        pltpu.VMEM((2, 2 * nb, gn), F32),
                        pltpu.VMEM((blk, 2 * gn), F32)],
        compiler_params=_cparams("arbitrary"),
        name="s5_scan",
    )(u_tm, u_tm, lre, lim, ldt, wbre, wbim, wcre, wcim)


def _mla_proj_kernel(z_ref, qn_ref, kvn_ref, wq_ref, wqr_ref, wk_ref, wv_ref, cos_ref, sin_ref,
                     q_ref, k_ref, v_ref, *, scale):
    z = z_ref[0]
    cq = z[:, 0:256]
    ckv = z[:, 256:384]
    kr = z[:, 384:512]
    krr = z[:, 512:640]
    cos = cos_ref[...]
    sin = sin_ref[...]
    cqb = _rms(cq, qn_ref[...]).astype(BF16)
    ckvb = _rms(ckv, kvn_ref[...]).astype(BF16)
    q = _nn(cqb, wq_ref[...])
    qr = _nn(cqb, wqr_ref[...])
    kn = _nn(ckvb, wk_ref[...])
    krp = kr * cos + krr * sin
    for h in range(N_HEADS):
        sl = slice(h * 128, (h + 1) * 128)
        q_ref[0, h] = ((q[:, sl] * cos + qr[:, sl] * sin) * scale).astype(BF16)
        k_ref[0, h] = (kn[:, sl] + krp).astype(BF16)
    v_ref[0] = _nn(ckvb, wv_ref[...]).astype(BF16)


def _mla_proj_call(z, qn, kvn, wq, wqr, wk, wv, cos_t, sin_t, *, scale):
    B, Ta, _ = z.shape
    tm = _pick_tile(Ta, 544)
    kern = functools.partial(_mla_proj_kernel, scale=scale)
    full = lambda a: pl.BlockSpec(a.shape, lambda b, t: (0,) * a.ndim)
    return pl.pallas_call(
        kern,
        out_shape=(jax.ShapeDtypeStruct((B, N_HEADS, Ta, 128), BF16),
                   jax.ShapeDtypeStruct((B, N_HEADS, Ta, 128), BF16),
                   jax.ShapeDtypeStruct((B, Ta, GROUP_W), BF16)),
        grid=(B, Ta // tm),
        in_specs=[pl.BlockSpec((1, tm, 3 * ZB), lambda b, t: (b, t, 0)),
                  full(qn), full(kvn), full(wq), full(wqr), full(wk), full(wv),
                  pl.BlockSpec((tm, 128), lambda b, t: (t, 0)),
                  pl.BlockSpec((tm, 128), lambda b, t: (t, 0))],
        out_specs=(pl.BlockSpec((1, N_HEADS, tm, 128), lambda b, t: (b, 0, t, 0)),
                   pl.BlockSpec((1, N_HEADS, tm, 128), lambda b, t: (b, 0, t, 0)),
                   pl.BlockSpec((1, tm, GROUP_W), lambda b, t: (b, t, 0))),
        compiler_params=_cparams("parallel", "parallel"),
        name="mla_proj",
    )(z, qn, kvn, wq, wqr, wk, wv, cos_t, sin_t)


def _attn_kernel(q_ref, k_ref, v_ref, o_ref, *, tc, ta, n_ctx_tiles, key_chunk):
    i = pl.program_id(2)
    lane = _iota((1, 128), 1)

    @pl.when(i < n_ctx_tiles)
    def _():
        outs = []
        for j in range(2):
            s = _nt(q_ref[0, j], k_ref[0, j, 0:tc, :])
            p = jnp.exp2(s - jnp.max(s, axis=-1, keepdims=True))
            l = jnp.sum(p, axis=-1, keepdims=True)
            outs.append(_nn(p.astype(BF16), v_ref[0, 0:tc, :]) * (1.0 / l))
        o_ref[0] = jnp.where(lane < HEAD_DIM, outs[0], outs[1])

    @pl.when(i >= n_ctx_tiles)
    def _():
        bounds = [0, tc] + list(range(tc + key_chunk, ta + 1, key_chunk))
        units = [(c, j) for c in range(len(bounds) - 1) for j in range(2)]
        qs = [q_ref[0, 0], q_ref[0, 1]]
        score = lambda c, j: _nt(qs[j], k_ref[0, j, bounds[c]:bounds[c + 1], :])
        m = [None, None]
        l = [None, None]
        acc = [None, None]

        def weighted_values(pend):
            c, j, pb, alpha = pend
            pv = _nn(pb, v_ref[0, bounds[c]:bounds[c + 1], :])
            acc[j] = pv if alpha is None else alpha * acc[j] + pv

        pending = None
        s_next = score(*units[0])
        for idx, (c, j) in enumerate(units):
            s = s_next
            if idx + 1 < len(units):
                s_next = score(*units[idx + 1])
            mc = jnp.max(s, axis=-1, keepdims=True)
            if c == 0:
                alpha = None
                m[j] = mc
                p = jnp.exp2(s - mc)
                l[j] = jnp.sum(p, axis=-1, keepdims=True)
            else:
                m_new = jnp.maximum(m[j], mc)
                alpha = jnp.exp2(m[j] - m_new)
                p = jnp.exp2(s - m_new)
                l[j] = alpha * l[j] + jnp.sum(p, axis=-1, keepdims=True)
                m[j] = m_new
            if pending is not None:
                weighted_values(pending)
            pending = (c, j, p.astype(BF16), alpha)
        weighted_values(pending)
        o_ref[0] = jnp.where(lane < HEAD_DIM, acc[0] * (1.0 / l[0]), acc[1] * (1.0 / l[1]))


def _attn_call(q, k, v, *, tc):
    B, H, Ta, _ = q.shape
    tq = _pick_tile(math.gcd(tc, Ta), 256)
    kern = functools.partial(_attn_kernel, tc=tc, ta=Ta, n_ctx_tiles=tc // tq,
                             key_chunk=_pick_tile(Ta - tc, 1024, 128))
    return pl.pallas_call(
        kern,
        out_shape=jax.ShapeDtypeStruct((B, Ta, GROUP_W), F32),
        grid=(B, H // 2, Ta // tq),
        in_specs=[pl.BlockSpec((1, 2, tq, 128), lambda b, h, i: (b, h, i, 0)),
                  pl.BlockSpec((1, 2, Ta, 128), lambda b, h, i: (b, h, 0, 0)),
                  pl.BlockSpec((1, Ta, 128), lambda b, h, i: (b, 0, h))],
        out_specs=pl.BlockSpec((1, tq, 128), lambda b, h, i: (b, i, h)),
        compiler_params=_cparams("parallel", "parallel", "arbitrary"),
        name="mla_attn",
    )(q, k, v)


def _chunk_masks(d):
    L = CHUNK
    row = _iota((L, GROUP_W), 0)
    s_idx = _iota((L, GROUP_W), 1) % L
    tr = _iota((L, L), 0)
    tcol = _iota((L, L), 1)
    if d == 0:
        return (tcol <= tr).astype(F32), s_idx < row, s_idx <= row, s_idx == row
    return (tcol >= tr).astype(F32), s_idx > row, s_idx >= row, s_idx == row


def _rwkv_kernel(rf_ref, rb_ref, lf_ref, lb_ref, wup_ref, aup_ref, gup_ref, w0_ref, a0_ref,
                 kk_ref, ka_ref, rk_ref, yf_ref, yb_ref, g_ref, bon_ref, s_scr, *, cps):
    i = pl.program_id(1)
    L = CHUNK
    W = GROUP_W

    @pl.when(i == 0)
    def _():
        s_scr[...] = jnp.zeros_like(s_scr)

    hms = _head_masks()
    bd = _block_diag_mask()
    rs = lambda x: _row_stack(x, hms)
    bdiag = lambda x: jnp.concatenate([x] * N_HEADS, axis=0) * bd
    eye_w = (_iota((W, W), 0) == _iota((W, W), 1)).astype(F32)
    rows = cps * L
    grp_r = _iota((rows, rows), 0)
    grp_c = _iota((rows, rows), 1)
    same_chunk = (grp_r // L) == (grp_c // L)

    per_dir = []
    for d, (r_ref, l_ref) in enumerate(((rf_ref, lf_ref), (rb_ref, lb_ref))):
        rkv = r_ref[0]
        lora = l_ref[0]
        r = rkv[:, 0:W]
        k = rkv[:, W:2 * W]
        v = rkv[:, 2 * W:3 * W]
        _, strict, incl, eye = _chunk_masks(d)
        tri = (same_chunk & ((grp_c <= grp_r) if d == 0 else (grp_c >= grp_r))).astype(BF16)

        w_log = -_softplus(-(w0_ref[d] + _mm1(_nn, jnp.tanh(lora), wup_ref[d]))) - 0.5
        lw = -jnp.exp(w_log)
        a = _sigmoid(a0_ref[d] + _mm1(_nn, lora, aup_ref[d]))
        kkv = k * kk_ref[...]
        kkn = kkv / jnp.maximum(jnp.sqrt(_mm_exact_rhs(kkv * kkv, bd)), 1e-12)
        keff = k * (1.0 + (a - 1.0) * ka_ref[...])
        kka = kkn * a
        lw_hi, lw_lo = _split2(lw)
        cum2 = _nn(tri, jnp.concatenate([lw_hi, lw_lo], axis=1))
        cum = cum2[:, 0:W] + cum2[:, W:2 * W]
        e_dn = jnp.exp(-cum)
        per_dir.append(dict(v=v, lw=lw, cum=cum, kka=kka, keff=keff, strict=strict, incl=incl, eye=eye,
                            al=-kkn * jnp.exp(cum - lw), rt=r * jnp.exp(cum), bh=kka * e_dn, kh=keff * e_dn))
        if d == 0:
            g_ref[0] = _mm1(_nn, _sigmoid(lora), gup_ref[...])
            bon_ref[0] = _mm_exact_rhs(r * k * rk_ref[...], bd) * v

    chains = []
    for c in range(cps):
        for d in range(2):
            pd = per_dir[d]
            sl = slice(c * L, (c + 1) * L)
            tot = jnp.sum(pd["lw"][sl], axis=0, keepdims=True)
            e_tc = jnp.exp(tot - pd["cum"][sl])
            ch = dict(d=d, c=c, al=pd["al"][sl], rt=pd["rt"][sl], v=pd["v"][sl], tot=tot,
                      bt=pd["kka"][sl] * e_tc, kt=pd["keff"][sl] * e_tc)
            ch["rsv"] = rs(ch["v"])
            a_all = _mm1(_nt, jnp.concatenate([ch["al"], ch["rt"]], axis=0),
                         jnp.concatenate([rs(pd["bh"][sl]), rs(pd["kh"][sl])], axis=0))
            ch["a_ab"] = jnp.where(pd["strict"], a_all[0:L, 0:W], 0.0)
            ch["a_ak"] = jnp.where(pd["strict"], a_all[0:L, W:2 * W], 0.0)
            ch["a_rb"] = jnp.where(pd["incl"], a_all[L:2 * L, 0:W], 0.0)
            ch["a_rk"] = jnp.where(pd["incl"], a_all[L:2 * L, W:2 * W], 0.0)
            ch["p"] = jnp.where(pd["eye"], 1.0, 0.0) + ch["a_ab"]
            chains.append(ch)

    for ch in chains:
        ch["sq"] = _mmn(_nn, ch["a_ab"], bdiag(ch["a_ab"]))
        ch["zk"] = _mm1(_nn, ch["a_ak"], ch["rsv"])
        ch["y0k"] = _mm1(_nn, ch["a_rk"], ch["rsv"])
    n_sq = int(math.log2(L)) - 1
    for it in range(n_sq):
        for ch in chains:
            if it < n_sq - 1:
                ps = _mmn(_nn, jnp.concatenate([ch["p"], ch["sq"]], axis=0), bdiag(ch["sq"]))
                ch["p"] = ch["p"] + ps[0:L]
                ch["sq"] = ps[L:2 * L]
            else:
                ch["p"] = ch["p"] + _mmn(_nn, ch["p"], bdiag(ch["sq"]))
    for ch in chains:
        pu = _mm1(_nn, ch["p"], jnp.concatenate([rs(ch["al"]), rs(ch["zk"])], axis=1))
        ch["w"], ch["uk"] = pu[:, 0:W], pu[:, W:2 * W]
    for ch in chains:
        gy = _mm1(_nn, ch["a_rb"], jnp.concatenate([rs(ch["w"]), rs(ch["uk"])], axis=1))
        ch["g"] = ch["rt"] + gy[:, 0:W]
        ch["y0"] = gy[:, W:2 * W] + ch["y0k"]
        ch["pm"] = eye_w * jnp.exp(ch["tot"]) + bd * _mm1(_tn, ch["w"], ch["bt"])
        ch["q0"] = bd * _mm1(_tn, jnp.concatenate([ch["uk"], ch["v"]], axis=0),
                             jnp.concatenate([ch["bt"], ch["kt"]], axis=0))

    by_key = {(ch["d"], ch["c"]): ch for ch in chains}
    st = [s_scr[0], s_scr[1]]
    ys = [[None] * cps, [None] * cps]
    for step in range(cps):
        for d in range(2):
            c = step if d == 0 else cps - 1 - step
            ch = by_key[(d, c)]
            ys[d][c] = _mm1(_nt, ch["g"], st[d]) + ch["y0"]
            st[d] = _mm3(_nn, st[d], ch["pm"]) + ch["q0"]
    for d, y_ref in enumerate((yf_ref, yb_ref)):
        s_scr[d] = st[d]
        y_ref[0] = jnp.concatenate(ys[d], axis=0)


def _chunks_per_step(n_ctx, n_all):
    for cps in (4, 2, 1):
        if n_ctx % cps == 0 and (n_all - n_ctx) % cps == 0:
            return cps


def _rwkv_call(z, wup, aup, gup, w0, a0, kk, ka, rk, *, n_ctx, n_all):
    B, Ta, _ = z.shape
    W = GROUP_W
    cps = _chunks_per_step(n_ctx, n_all)
    rows = cps * CHUNK
    gc, ga = n_ctx // cps, n_all // cps
    full = lambda a: pl.BlockSpec(a.shape, lambda b, i: (0,) * a.ndim)
    fwd = lambda blk: (lambda b, i: (b, i, blk))
    bwd = lambda blk: (lambda b, i: (b, _bwd_chunk(i, gc, ga), blk))
    out = jax.ShapeDtypeStruct((B, Ta, W), F32)
    return pl.pallas_call(
        functools.partial(_rwkv_kernel, cps=cps),
        out_shape=(out, out, out, out),
        grid=(B, ga),
        in_specs=[pl.BlockSpec((1, rows, 3 * W), fwd(Z_RKV // 3)),
                  pl.BlockSpec((1, rows, 3 * W), bwd(Z_RKV // 3)),
                  pl.BlockSpec((1, rows, W), fwd(Z_LORA)),
                  pl.BlockSpec((1, rows, W), bwd(Z_LORA)),
                  full(wup), full(aup), full(gup), full(w0), full(a0), full(kk), full(ka), full(rk)],
        out_specs=(pl.BlockSpec((1, rows, W), fwd(0)), pl.BlockSpec((1, rows, W), bwd(0)),
                   pl.BlockSpec((1, rows, W), fwd(0)), pl.BlockSpec((1, rows, W), fwd(0))),
        scratch_shapes=[pltpu.VMEM((2, W, W), F32)],
        compiler_params=_cparams("parallel", "arbitrary"),
        name="rwkv7",
    )(z, z, z, z, wup, aup, gup, w0, a0, kk, ka, rk)


def _mlstm_kernel(qf_ref, qb_ref, vf_ref, vb_ref, gf_ref, gb_ref, gbias_ref,
                  yf_ref, yb_ref, c_scr, n_scr, m_scr, *, cps):
    i = pl.program_id(1)
    L = CHUNK
    W = GROUP_W

    @pl.when(i == 0)
    def _():
        c_scr[...] = jnp.zeros_like(c_scr)
        n_scr[...] = jnp.zeros_like(n_scr)
        m_scr[...] = jnp.zeros_like(m_scr)

    hms = _head_masks()
    bd = _block_diag_mask()
    rs = lambda x: _row_stack(x, hms)
    ci = _iota((W, W), 0)
    cj = _iota((W, W), 1) // HEAD_DIM
    neg_inf = -jnp.inf
    rows = cps * L
    grp_r = _iota((rows, rows), 0)
    grp_c = _iota((rows, rows), 1)
    same_chunk = (grp_r // L) == (grp_c // L)

    def sum2(x, e):
        hi, lo = _split2(x)
        eb = e.astype(BF16)
        return _nn(hi, eb) + _nn(lo, eb)

    per_dir = []
    for d, (q_ref, v_ref, g_ref) in enumerate(((qf_ref, vf_ref, gf_ref), (qb_ref, vb_ref, gb_ref))):
        qk = q_ref[0]
        g = g_ref[0] + gbias_ref[...]
        _, strict, incl, eye = _chunk_masks(d)
        tri = (same_chunk & ((grp_c <= grp_r) if d == 0 else (grp_c >= grp_r))).astype(BF16)
        li = _mm_exact_rhs(g, (ci == d * 2 * N_HEADS + cj).astype(F32))
        lf = -_softplus(-_mm_exact_rhs(g, (ci == d * 2 * N_HEADS + N_HEADS + cj).astype(F32)))
        lf_hi, lf_lo = _split2(lf)
        b2 = _nn(tri, jnp.concatenate([lf_hi, lf_lo], axis=1))
        per_dir.append(dict(q=qk[:, 0:W], k=qk[:, W:2 * W] * (HEAD_DIM ** -0.5), v=v_ref[0][:, 0:W],
                            li=li, bcol=b2[:, 0:W] + b2[:, W:2 * W], incl=incl, eye=eye))

    chains = []
    for c in range(cps):
        for d in range(2):
            pd = per_dir[d]
            sl = slice(c * L, (c + 1) * L)
            q, k, v, li, bcol = pd["q"][sl], pd["k"][sl], pd["v"][sl], pd["li"][sl], pd["bcol"][sl]
            brow = jnp.sum(jnp.where(pd["eye"], bcol, 0.0), axis=0, keepdims=True)
            lirow = jnp.sum(jnp.where(pd["eye"], li, 0.0), axis=0, keepdims=True)
            logd = jnp.where(pd["incl"], bcol - brow + lirow, neg_inf)
            mx = jnp.zeros((L, W), F32)
            for hm in hms:
                mh = jnp.max(jnp.where(hm, logd, neg_inf), axis=1, keepdims=True)
                mx = jnp.where(hm, mh, mx)
            blast = bcol[L - 1:L, :] if d == 0 else bcol[0:1, :]
            lwc = blast - bcol + li
            mlw = jnp.max(lwc, axis=0, keepdims=True)
            kw = k * jnp.exp(lwc - mlw)
            chains.append(dict(d=d, c=c, q=q, v=v, bcol=bcol, mx=mx, blast=blast, mlw=mlw, kw=kw,
                               dexp=jnp.exp(logd - mx), rsk=rs(k), rsv=rs(v),
                               nu0=jnp.sum(kw, axis=0, keepdims=True)))
    for ch in chains:
        ch["sp"] = _mm1(_nt, ch["q"], ch["rsk"]) * ch["dexp"]
        ch["cu0"] = bd * _mm1(_tn, ch["v"], ch["kw"])
    for ch in chains:
        ch["num0"] = _mm1(_nn, ch["sp"], ch["rsv"])
        ch["den0"] = sum2(ch["sp"], bd)

    by_key = {(ch["d"], ch["c"]): ch for ch in chains}
    c_st = [c_scr[0], c_scr[1]]
    n_row = [n_scr[0, 0:1, :], n_scr[1, 0:1, :]]
    m_row = [m_scr[0, 0:1, :], m_scr[1, 0:1, :]]
    ys = [[None] * cps, [None] * cps]
    for step in range(cps):
        for d in range(2):
            c = step if d == 0 else cps - 1 - step
            ch = by_key[(d, c)]
            inter = ch["bcol"] + m_row[d]
            mt = jnp.maximum(inter, ch["mx"])
            f_in = jnp.exp(ch["mx"] - mt)
            w_int = jnp.exp(inter - mt)
            num = f_in * ch["num0"] + w_int * _mm1(_nt, ch["q"], c_st[d])
            den = f_in * ch["den0"] + w_int * sum2(ch["q"] * n_row[d], bd)
            ys[d][c] = num / jnp.maximum(jnp.abs(den), jnp.exp(-mt))
            m_new = jnp.maximum(ch["blast"] + m_row[d], ch["mlw"])
            sc = jnp.exp(ch["blast"] + m_row[d] - m_new)
            e2 = jnp.exp(ch["mlw"] - m_new)
            c_st[d] = sc * c_st[d] + e2 * ch["cu0"]
            n_row[d] = sc * n_row[d] + e2 * ch["nu0"]
            m_row[d] = m_new
    for d, y_ref in enumerate((yf_ref, yb_ref)):
        c_scr[d] = c_st[d]
        n_scr[d, 0:1, :] = n_row[d]
        m_scr[d, 0:1, :] = m_row[d]
        y_ref[0] = jnp.concatenate(ys[d], axis=0)


def _mlstm_call(z, gbias, *, n_ctx, n_all):
    B, Ta, _ = z.shape
    W = GROUP_W
    cps = _chunks_per_step(n_ctx, n_all)
    L = cps * CHUNK
    n_ctx, n_all = n_ctx // cps, n_all // cps
    fwd = lambda blk: (lambda b, i: (b, i, blk))
    bwd = lambda blk: (lambda b, i: (b, _bwd_chunk(i, n_ctx, n_all), blk))
    out = jax.ShapeDtypeStruct((B, Ta, W), F32)
    return pl.pallas_call(
        functools.partial(_mlstm_kernel, cps=cps),
        out_shape=(out, out),
        grid=(B, n_all),
        in_specs=[pl.BlockSpec((1, L, 2 * W), fwd(Z_MQK // 2)),
                  pl.BlockSpec((1, L, 2 * W), bwd(Z_MQK // 2)),
                  pl.BlockSpec((1, L, 2 * W), fwd(Z_MVO // 2)),
                  pl.BlockSpec((1, L, 2 * W), bwd(Z_MVO // 2)),
                  pl.BlockSpec((1, L, W), fwd(Z_GATE)),
                  pl.BlockSpec((1, L, W), bwd(Z_GATE)),
                  pl.BlockSpec(gbias.shape, lambda b, i: (0, 0))],
        out_specs=(pl.BlockSpec((1, L, W), fwd(0)), pl.BlockSpec((1, L, W), bwd(0))),
        scratch_shapes=[pltpu.VMEM((2, W, W), F32), pltpu.VMEM((2, 8, W), F32), pltpu.VMEM((2, 8, W), F32)],
        compiler_params=_cparams("parallel", "arbitrary"),
        name="mlstm",
    )(z, z, z, z, z, z, gbias)


def _head_norm(y, bd, eps):
    mu = _mm_exact_rhs(y, bd) * (1.0 / HEAD_DIM)
    yc = y - mu
    var = _mm_exact_rhs(yc * yc, bd) * (1.0 / HEAD_DIM)
    return yc * lax.rsqrt(var + eps)


def _outproj_kernel(x_ref, mods_ref, s5f_ref, s5b_ref, u_ref, at_ref, rf_ref, rb_ref, rg_ref, rbon_ref,
                    mf_ref, mb_ref, vo_ref, s5d_ref, wglu_ref, bglu_ref, lnw_ref, lnb_ref, mnw_ref,
                    gpost_ref, wout_ref, o_ref, *, tm, tc, n_batch):
    b = pl.program_id(0)
    t = pl.program_id(1)
    d = x_ref.shape[-1]
    W = GROUP_W
    is_ctx = (t * tm + _iota((tm, 1), 0)) < tc
    gate = jnp.where(is_ctx, mods_ref[pl.ds(n_batch, 1), pl.ds(5 * d, d)],
                     mods_ref[pl.ds(b, 1), pl.ds(5 * d, d)])
    bd = _block_diag_mask()

    y = s5f_ref[...] + s5b_ref[...] + s5d_ref[...] * u_ref[0]
    zg = 0.5 * y * (1.0 + jnp.tanh(math.sqrt(2.0 / math.pi) * (y + 0.044715 * (y * y * y))))
    s5o = zg * _sigmoid(_mm1(_nn, zg, wglu_ref[...]) + bglu_ref[...])

    yr = _head_norm(rf_ref[0] + rb_ref[0], bd, RWKV_GN_EPS)
    rwo = (yr * lnw_ref[...] + lnb_ref[...] + rbon_ref[0]) * rg_ref[0]

    ym = _head_norm(mf_ref[0] + mb_ref[0], bd, NORM_EPS)
    mlo = ym * mnw_ref[...] * _sigmoid(vo_ref[0][:, W:2 * W])

    cat = jnp.concatenate([s5o, at_ref[0], rwo, mlo], axis=1).astype(BF16)
    yx = _nn(cat, wout_ref[...])
    o_ref[0] = x_ref[0] + gate * _rms(yx, gpost_ref[...])


def _outproj_call(xa, mods, s5f, s5b, z, attn, rf, rb, rg, rbon, mf, mb,
                  s5d, wglu, bglu, lnw, lnb, mnw, gpost, wout, *, tc):
    B, Ta, D = xa.shape
    W = GROUP_W
    tm = _pick_tile(Ta, 544)
    kern = functools.partial(_outproj_kernel, tm=tm, tc=tc, n_batch=B)
    full = lambda a: pl.BlockSpec(a.shape, lambda b, t: (0,) * a.ndim)
    tok = pl.BlockSpec((1, tm, W), lambda b, t: (b, t, 0))
    tmaj = pl.BlockSpec((tm, W), lambda b, t: (t, b))
    return pl.pallas_call(
        kern,
        out_shape=jax.ShapeDtypeStruct((B, Ta, D), F32),
        grid=(B, Ta // tm),
        in_specs=[pl.BlockSpec((1, tm, D), lambda b, t: (b, t, 0)), full(mods),
                  tmaj, tmaj, pl.BlockSpec((1, tm, W), lambda b, t: (b, t, Z_S5)),
                  tok, tok, tok, tok, tok, tok, tok,
                  pl.BlockSpec((1, tm, 2 * W), lambda b, t: (b, t, Z_MVO // 2)),
                  full(s5d), full(wglu), full(bglu), full(lnw), full(lnb), full(mnw), full(gpost), full(wout)],
        out_specs=pl.BlockSpec((1, tm, D), lambda b, t: (b, t, 0)),
        compiler_params=_cparams("parallel", "parallel"),
        name="mix_out",
    )(xa, mods, s5f, s5b, z, attn, rf, rb, rg, rbon, mf, mb, z, s5d, wglu, bglu, lnw, lnb, mnw, gpost, wout)


def _inproj_column_map():
    src = np.zeros((Z_COLS,), np.int32)
    sgn = np.zeros((Z_COLS,), np.float32)

    def put(dst, s0, n):
        src[dst:dst + n] = np.arange(s0, s0 + n)
        sgn[dst:dst + n] = 1.0

    o_s5, o_mla, o_rw, o_ml = 0, 256, 672, 1568
    put(0, o_mla, 256)
    put(256, o_mla + 256, 128)
    put(384 + 64, o_mla + 384, ROPE_DIM)
    h = ROPE_AXIS // 2
    perm = np.concatenate([np.arange(h, 2 * h), np.arange(0, h), np.arange(3 * h, 4 * h), np.arange(2 * h, 3 * h)])
    sign = np.concatenate([-np.ones(h), np.ones(h), -np.ones(h), np.ones(h)])
    src[512 + 64:512 + 64 + ROPE_DIM] = o_mla + 384 + perm
    sgn[512 + 64:512 + 64 + ROPE_DIM] = sign
    put(Z_RKV * ZB, o_rw, 768)
    put(Z_S5 * ZB, o_s5, 256)
    put(Z_LORA * ZB, o_rw + 768, 128)
    put(Z_MQK * ZB, o_ml, 512)
    put(Z_MVO * ZB, o_ml + 512, 512)
    put(Z_GATE * ZB, o_ml + 1024, 16)
    return src, sgn, perm, sign


def _rope_tables(T, tc):
    rows = T // GRID_W
    r_idx, c_idx = jnp.meshgrid(jnp.arange(rows), jnp.arange(GRID_W), indexing='ij')
    inv_freq = 1.0 / (ROPE_BASE ** (jnp.arange(0, ROPE_AXIS, 2, dtype=F32) / ROPE_AXIS))
    ang_r = r_idx.reshape(-1, 1).astype(F32) * inv_freq
    ang_c = c_idx.reshape(-1, 1).astype(F32) * inv_freq
    ang = jnp.concatenate([ang_r, ang_r, ang_c, ang_c], axis=-1)
    cos = jnp.concatenate([jnp.ones((tc, ROPE_DIM), F32), jnp.cos(ang)], axis=0)
    sin = jnp.concatenate([jnp.zeros((tc, ROPE_DIM), F32), jnp.sin(ang)], axis=0)
    ta = T + tc
    cos_t = jnp.concatenate([jnp.ones((ta, 64), F32), cos, jnp.zeros((ta, 32), F32)], axis=1)
    sin_t = jnp.concatenate([jnp.zeros((ta, 64), F32), sin, jnp.zeros((ta, 32), F32)], axis=1)
    return cos_t, sin_t


def _pad_rows(w, r0, total):
    pad = [(0, 0)] * (w.ndim - 2) + [(r0, total - r0 - w.shape[-2]), (0, 0)]
    return jnp.pad(w, pad)


def kernel(x, c, ctx, c_ctx, w_ada, b_ada, norm_pre, norm_post, ffn_w_gate, ffn_w_up, ffn_w_down, w_in, w_out, s5_lam_re, s5_lam_im, s5_log_dt, s5_b_re, s5_b_im, s5_c_re, s5_c_im, s5_d, s5_w_glu, s5_b_glu, mla_q_norm, mla_kv_norm, mla_w_uq, mla_w_ukv, rwkv_conv_w, rwkv_conv_b, rwkv_w0, rwkv_w_up, rwkv_a0, rwkv_a_up, rwkv_g_up, rwkv_k_k, rwkv_k_a, rwkv_r_k, rwkv_ln_w, rwkv_ln_b, mlstm_conv_w, mlstm_conv_b, mlstm_gate_b, mlstm_norm):
    B, T, D = x.shape
    Tc = ctx.shape[1]
    Ta = T + Tc
    L = w_ada.shape[0]
    W = GROUP_W
    assert T % CHUNK == 0 and Tc % CHUNK == 0 and B % 8 == 0 and B <= 8
    n_ctx, n_all = Tc // CHUNK, Ta // CHUNK

    rows = 16
    cvec = jnp.concatenate([c, c_ctx[None, :], jnp.zeros((rows - B - 1, D), F32)], axis=0)
    mods_all = _ada_call(cvec, w_ada, b_ada)

    src, sgn, perm, sign = _inproj_column_map()
    w_in_re = (jnp.take(w_in, jnp.asarray(src), axis=2) * jnp.asarray(sgn)).astype(BF16)
    cw = jnp.zeros((L, Z_NBLK, 8, ZB), F32)
    rc = jnp.concatenate([rwkv_conv_w, rwkv_conv_b[:, None, :]], axis=1).reshape(L, 4, 3, ZB).transpose(0, 2, 1, 3)
    mc = jnp.concatenate([mlstm_conv_w, mlstm_conv_b[:, None, :]], axis=1).reshape(L, 4, 2, ZB).transpose(0, 2, 1, 3)
    cw = cw.at[:, Z_RKV:Z_RKV + 3, 0:4].set(rc).at[:, Z_MQK:Z_MQK + 2, 0:4].set(mc)

    wg = ffn_w_gate.astype(BF16)
    wu = ffn_w_up.astype(BF16)
    wd = ffn_w_down.astype(BF16)
    wout = w_out.astype(BF16)

    G = s5_lam_re.shape[2]
    N = s5_lam_re.shape[3]
    eye_g = jnp.eye(G, dtype=F32)
    lre = s5_lam_re.reshape(L, 2, 1, G * N)
    lim = s5_lam_im.reshape(L, 2, 1, G * N)
    ldt = jnp.repeat(s5_log_dt, N, axis=-1).reshape(L, 2, 1, G * N)
    wbre = jnp.einsum('ldgnp,gh->ldgphn', s5_b_re, eye_g).reshape(L, 2, G * S5_P, G * N)
    wbim = jnp.einsum('ldgnp,gh->ldgphn', s5_b_im, eye_g).reshape(L, 2, G * S5_P, G * N)
    wcre = jnp.einsum('ldgpn,gh->ldgnhp', s5_c_re, eye_g).reshape(L, 2, G * N, G * S5_P).astype(BF16)
    wcim = jnp.einsum('ldgpn,gh->ldgnhp', s5_c_im, eye_g).reshape(L, 2, G * N, G * S5_P).astype(BF16)

    nope = HEAD_DIM
    qd = nope + ROPE_DIM
    wq4 = mla_w_uq.reshape(L, -1, N_HEADS, qd)
    wq = jnp.pad(wq4, ((0, 0), (0, 0), (0, 0), (0, 128 - qd))).reshape(L, -1, N_HEADS * 128).astype(BF16)
    wq_rot = wq4[..., nope + jnp.asarray(perm)] * jnp.asarray(sign, F32)
    wqr = jnp.pad(wq_rot, ((0, 0), (0, 0), (0, 0), (nope, 128 - qd))).reshape(L, -1, N_HEADS * 128).astype(BF16)
    wkv4 = mla_w_ukv.reshape(L, -1, N_HEADS, 2 * HEAD_DIM)
    wk = jnp.pad(wkv4[..., :HEAD_DIM], ((0, 0), (0, 0), (0, 0), (0, 64))).reshape(L, -1, N_HEADS * 128).astype(BF16)
    wv = wkv4[..., HEAD_DIM:].reshape(L, -1, W).astype(BF16)
    cos_t, sin_t = _rope_tables(T, Tc)
    scale = float(qd) ** -0.5 * math.log2(math.e)

    wup = _pad_rows(rwkv_w_up, 0, W)
    aup = _pad_rows(rwkv_a_up, 32, W)
    gup = _pad_rows(rwkv_g_up, 64, W)
    gbias = jnp.pad(mlstm_gate_b, ((0, 0), (0, W - mlstm_gate_b.shape[1])))

    xa = jnp.concatenate([ctx, x], axis=1)
    r1 = lambda a: a.reshape(1, -1)

    for l in range(L):
        mods = mods_all[l]
        xa2, hmix = _ffn_call(xa.reshape(B * Ta, D), mods, r1(norm_pre[l, 0]), r1(norm_post[l, 0]),
                              r1(norm_pre[l, 1]), wg[l, 0], wu[l, 0], wd[l, 0],
                              ta=Ta, tc=Tc, n_batch=B, koff=0, emit_hmix=True)
        xa = xa2.reshape(B, Ta, D)
        z = _inproj_call(hmix.reshape(B, Ta, D), w_in_re[l], cw[l], tc=Tc)

        u_tm = jnp.transpose(z[:, :, Z_S5 * ZB:(Z_S5 + 1) * ZB], (1, 0, 2)).reshape(Ta * B, W)
        s5f, s5b = _s5_call(u_tm, lre[l], lim[l], ldt[l], wbre[l], wbim[l], wcre[l], wcim[l],
                            nb=B, n_ctx=n_ctx, n_all=n_all)

        q, k, v = _mla_proj_call(z, r1(mla_q_norm[l]), r1(mla_kv_norm[l]), wq[l], wqr[l], wk[l], wv[l],
                                 cos_t, sin_t, scale=scale)
        attn = _attn_call(q, k, v, tc=Tc)

        rf, rb, rg, rbon = _rwkv_call(z, wup[l], aup[l], gup[l], rwkv_w0[l][:, None, :], rwkv_a0[l][:, None, :],
                                      r1(rwkv_k_k[l]), r1(rwkv_k_a[l]), r1(rwkv_r_k[l]), n_ctx=n_ctx, n_all=n_all)
        mf, mb = _mlstm_call(z, gbias[l:l + 1], n_ctx=n_ctx, n_all=n_all)

        xa = _outproj_call(xa, mods, s5f.reshape(Ta, B * W), s5b.reshape(Ta, B * W), z, attn, rf, rb, rg, rbon,
                           mf, mb, r1(s5_d[l]), s5_w_glu[l].astype(BF16), r1(s5_b_glu[l]), r1(rwkv_ln_w[l]),
                           r1(rwkv_ln_b[l]), r1(mlstm_norm[l]), r1(norm_post[l, 1]), wout[l], tc=Tc)

        xa2, _ = _ffn_call(xa.reshape(B * Ta, D), mods, r1(norm_pre[l, 2]), r1(norm_post[l, 2]),
                           r1(norm_pre[l, 1]), wg[l, 1], wu[l, 1], wd[l, 1],
                           ta=Ta, tc=Tc, n_batch=B, koff=6, emit_hmix=False)
        xa = xa2.reshape(B, Ta, D)

    return xa[:, Tc:, :]
```

```python
import functools
import math

import numpy as np
import jax
import jax.numpy as jnp
from jax import lax
from jax.experimental import pallas as pl
from jax.experimental.pallas import tpu as pltpu

F32 = jnp.float32
BF16 = jnp.bfloat16

GROUP_W = 256
HEAD_DIM = 64
N_HEADS = GROUP_W // HEAD_DIM
CHUNK = 64
N_MOD = 9
NORM_EPS = 1e-6
RWKV_GN_EPS = HEAD_DIM * 1e-5
GRID_W = 64
ROPE_BASE = 10000.0
ROPE_DIM = 32
ROPE_AXIS = 16
S5_P = 16
S5_STATE = 64
MACARON = 0.5
VMEM_LIMIT_BYTES = 56 * 1024 * 1024

ZB = 256
Z_MLA, Z_RKV, Z_S5, Z_LORA, Z_MQK, Z_MVO, Z_GATE = 0, 3, 6, 7, 8, 10, 12
Z_NBLK = 13
Z_COLS = Z_NBLK * ZB


def _nn(a, b):
    return lax.dot_general(a, b, (((1,), (0,)), ((), ())), preferred_element_type=F32)


def _nt(a, b):
    return lax.dot_general(a, b, (((1,), (1,)), ((), ())), preferred_element_type=F32)


def _tn(a, b):
    return lax.dot_general(a, b, (((0,), (0,)), ((), ())), preferred_element_type=F32)


def _split2(x):
    hi = x.astype(BF16)
    lo = (x - hi.astype(F32)).astype(BF16)
    return hi, lo


def _split3(x):
    p1 = x.astype(BF16)
    r = x - p1.astype(F32)
    p2 = r.astype(BF16)
    p3 = (r - p2.astype(F32)).astype(BF16)
    return p1, p2, p3


def _mm3(dotf, a, b):
    ah, al = _split2(a)
    bh, bl = _split2(b)
    return dotf(ah, bh) + (dotf(ah, bl) + dotf(al, bh))


def _mm1(dotf, a, b):
    return dotf(a.astype(BF16), b.astype(BF16))


_mmn = _mm1


def _mm_exact_rhs(a, e):
    eb = e.astype(BF16)
    p1, p2, p3 = _split3(a)
    return _nn(p1, eb) + (_nn(p2, eb) + _nn(p3, eb))


def _mm_exact_lhs(e, a):
    eb = e.astype(BF16)
    p1, p2, p3 = _split3(a)
    return _nn(eb, p1) + (_nn(eb, p2) + _nn(eb, p3))


def _rms(x, g):
    return x * lax.rsqrt(jnp.mean(x * x, axis=-1, keepdims=True) + NORM_EPS) * g


def _sigmoid(x):
    return 1.0 / (1.0 + jnp.exp(-x))


def _softplus(x):
    return jnp.maximum(x, 0.0) + jnp.log(1.0 + jnp.exp(-jnp.abs(x)))


def _iota(shape, dim):
    return lax.broadcasted_iota(jnp.int32, shape, dim)


def _head_masks():
    lane = _iota((1, GROUP_W), 1)
    return [lane // HEAD_DIM == h for h in range(N_HEADS)]


def _block_diag_mask():
    r = _iota((GROUP_W, GROUP_W), 0) // HEAD_DIM
    c = _iota((GROUP_W, GROUP_W), 1) // HEAD_DIM
    return (r == c).astype(F32)


def _row_stack(x, hms):
    xb = x.astype(BF16)
    return jnp.concatenate([jnp.where(m, xb, 0.0) for m in hms], axis=0)


def _pick_tile(n, target, mult=16):
    best = None
    for t in range(mult, min(n, target) + 1, mult):
        if n % t == 0:
            best = t
    if best is None:
        raise ValueError(f"no tile for {n}")
    return best


def _cparams(*sem):
    return pltpu.CompilerParams(dimension_semantics=sem, vmem_limit_bytes=VMEM_LIMIT_BYTES)


def _ada_kernel(c_ref, w_ref, b_ref, o_ref):
    c = c_ref[...]
    s = c * _sigmoid(c)
    o_ref[0] = _mm3(_nn, s, w_ref[0]) + b_ref[0]


def _ada_call(cvec, w_ada, b_ada):
    L, D, N = w_ada.shape
    R = cvec.shape[0]
    tn = _pick_tile(N, 1152, 128)
    return pl.pallas_call(
        _ada_kernel,
        out_shape=jax.ShapeDtypeStruct((L, R, N), F32),
        grid=(L, N // tn),
        in_specs=[pl.BlockSpec((R, D), lambda l, j: (0, 0)),
                  pl.BlockSpec((1, D, tn), lambda l, j: (l, 0, j)),
                  pl.BlockSpec((1, 1, tn), lambda l, j: (l, 0, j))],
        out_specs=pl.BlockSpec((1, R, tn), lambda l, j: (l, 0, j)),
        compiler_params=_cparams("parallel", "parallel"),
        name="ada_mod",
    )(cvec, w_ada, b_ada.reshape(L, 1, N))


FFN_COLS = 256


def _ffn_kernel(xn_ref, xp_ref, mods_ref, gpre_ref, gpost_ref, gmix_ref, wg_ref, wu_ref, wd_ref,
                o_ref, hmix_ref, h_scr, acc_scr, *, tm, tiles_per_batch, tc, n_batch, koff, emit_hmix,
                n_tiles, f_split):
    i = pl.program_id(0)
    f = pl.program_id(1)
    d = xn_ref.shape[-1]
    fd = wg_ref.shape[1]

    def mod_of(tile):
        b = tile // tiles_per_batch
        t0 = (tile % tiles_per_batch) * tm
        is_ctx = (t0 + _iota((tm, 1), 0)) < tc

        def mod(k):
            mx = mods_ref[pl.ds(b, 1), pl.ds(k * d, d)]
            mc = mods_ref[pl.ds(n_batch, 1), pl.ds(k * d, d)]
            return jnp.where(is_ctx, mc, mx)
        return mod

    def pre_norm(tile, x):
        mod = mod_of(tile)
        return (_rms(x, gpre_ref[...]) * (1.0 + mod(koff + 1)) + mod(koff)).astype(BF16)

    def finish(tile, acc, x):
        mod = mod_of(tile)
        xn = x + MACARON * mod(koff + 2) * _rms(acc, gpost_ref[...])
        o_ref[...] = xn
        if emit_hmix:
            hmix_ref[...] = (_rms(xn, gmix_ref[...]) * (1.0 + mod(4)) + mod(3)).astype(BF16)
        else:
            hmix_ref[...] = jnp.zeros_like(hmix_ref)

    def hidden_cols(hb, lo, hi):
        acc = None
        for c0 in range(lo, hi, FFN_COLS):
            g = _nn(hb, wg_ref[:, c0:c0 + FFN_COLS])
            u = _nn(hb, wu_ref[:, c0:c0 + FFN_COLS])
            t = _nn((g * _sigmoid(g) * u).astype(BF16), wd_ref[c0:c0 + FFN_COLS, :])
            acc = t if acc is None else acc + t
        return acc

    @pl.when(jnp.logical_and(i == 0, f == 0))
    def _():
        h_scr[0] = pre_norm(0, xn_ref[...])
        acc_scr[1] = jnp.zeros((tm, d), F32)

    for s in range(2):
        tile = 2 * i + s

        @pl.when(jnp.logical_and(tile < n_tiles, f == 2 * s))
        def _(s=s, tile=tile):
            finish(jnp.maximum(tile - 1, 0), acc_scr[1 - s], xp_ref[...])
            acc_scr[s] = hidden_cols(h_scr[s], 0, f_split)

        @pl.when(jnp.logical_and(tile < n_tiles, f == 2 * s + 1))
        def _(s=s, tile=tile):
            h_scr[1 - s] = pre_norm(jnp.minimum(tile + 1, n_tiles - 1), xn_ref[...])
            acc_scr[s] += hidden_cols(h_scr[s], f_split, fd)

    @pl.when(jnp.logical_and(2 * i == n_tiles, f == 0))
    def _():
        finish(n_tiles - 1, acc_scr[1], xp_ref[...])


def _ffn_call(xa2, mods, gpre, gpost, gmix, wg, wu, wd, *, ta, tc, n_batch, koff, emit_hmix):
    M, D = xa2.shape
    Fd = wg.shape[1]
    assert Fd % FFN_COLS == 0
    tm = _pick_tile(ta, 544)
    n_tiles = M // tm
    assert n_tiles % 2 == 0
    f_split = ((Fd // FFN_COLS + 1) // 2) * FFN_COLS
    kern = functools.partial(_ffn_kernel, tm=tm, tiles_per_batch=ta // tm, tc=tc, n_batch=n_batch,
                             koff=koff, emit_hmix=emit_hmix, n_tiles=n_tiles, f_split=f_split)
    hm_rows = tm if emit_hmix else 16
    last = n_tiles - 1
    resident = lambda a: pl.BlockSpec(a.shape, lambda i, f: (0, 0), pipeline_mode=pl.Buffered(1))
    vec = pl.BlockSpec((1, D), lambda i, f: (0, 0))
    return pl.pallas_call(
        kern,
        out_shape=(jax.ShapeDtypeStruct((M, D), F32),
                   jax.ShapeDtypeStruct((M if emit_hmix else 16 * n_tiles, D), BF16)),
        grid=(n_tiles // 2 + 1, 4),
        in_specs=[pl.BlockSpec((tm, D), lambda i, f: (jnp.minimum(2 * i + (f + 1) // 2, last), 0)),
                  pl.BlockSpec((tm, D), lambda i, f: (jnp.clip(2 * i - 1 + f // 2, 0, last), 0)),
                  pl.BlockSpec(mods.shape, lambda i, f: (0, 0)),
                  vec, vec, vec, resident(wg), resident(wu), resident(wd)],
        out_specs=(pl.BlockSpec((tm, D), lambda i, f: (jnp.clip(2 * i - 1 + f // 2, 0, last), 0)),
                   pl.BlockSpec((hm_rows, D), lambda i, f: (jnp.clip(2 * i - 1 + f // 2, 0, last), 0))),
        scratch_shapes=[pltpu.VMEM((2, tm, D), BF16), pltpu.VMEM((2, tm, D), F32)],
        compiler_params=_cparams("arbitrary", "arbitrary"),
        name="half_ffn",
    )(xa2, xa2, mods, gpre, gpost, gmix, wg, wu, wd)


def _inproj_kernel(h_ref, w_ref, cw_ref, o_ref, *, tc, conv_lo, conv_hi, silu_lo, silu_hi):
    nb = pl.program_id(1)
    z = _nn(h_ref[0], w_ref[...])
    ta = z.shape[0]
    is_conv = ((nb >= conv_lo[0]) & (nb < conv_hi[0])) | ((nb >= conv_lo[1]) & (nb < conv_hi[1]))
    is_silu = (nb >= silu_lo) & (nb < silu_hi)

    @pl.when(jnp.logical_not(is_conv))
    def _():
        o_ref[0] = z

    def conv():
        row = _iota((ta, 1), 0)
        zp = jnp.where((row == 0) | (row == tc), 0.0, pltpu.roll(z, 1, 0))
        zn = jnp.where((row == tc - 1) | (row == ta - 1), 0.0, pltpu.roll(z, ta - 1, 0))
        cw = cw_ref[0]
        return cw[3:4] + zp * cw[0:1] + z * cw[1:2] + zn * cw[2:3]

    @pl.when(is_conv & jnp.logical_not(is_silu))
    def _():
        o_ref[0] = conv()

    @pl.when(is_conv & is_silu)
    def _():
        y = conv()
        o_ref[0] = y * _sigmoid(y)


def _inproj_call(hmix3, w_re, cw, *, tc):
    B, Ta, D = hmix3.shape
    kern = functools.partial(_inproj_kernel, tc=tc, conv_lo=(Z_RKV, Z_MQK), conv_hi=(Z_RKV + 3, Z_MQK + 2),
                             silu_lo=Z_MQK, silu_hi=Z_MQK + 2)
    return pl.pallas_call(
        kern,
        out_shape=jax.ShapeDtypeStruct((B, Ta, Z_COLS), F32),
        grid=(B, Z_NBLK),
        in_specs=[pl.BlockSpec((1, Ta, D), lambda b, n: (b, 0, 0)),
                  pl.BlockSpec((D, ZB), lambda b, n: (0, n)),
                  pl.BlockSpec((1, 8, ZB), lambda b, n: (n, 0, 0))],
        out_specs=pl.BlockSpec((1, Ta, ZB), lambda b, n: (b, 0, n)),
        compiler_params=_cparams("parallel", "arbitrary"),
        name="in_proj",
    )(hmix3, w_re, cw)


def _bwd_chunk(i, n_ctx, n_all):
    return jnp.where(i < n_ctx, n_ctx - 1 - i, n_all - 1 - (i - n_ctx))


def _s5_kernel(uf_ref, ub_ref, lre_ref, lim_ref, ldt_ref, wbre_ref, wbim_ref, wcre_ref, wcim_ref,
               yf_ref, yb_ref, wb_scr, coef_scr, st_scr, rel_scr, x_scr, *, lc, nb, scan_unroll):
    i = pl.program_id(0)
    gn = lre_ref.shape[-1]

    @pl.when(i == 0)
    def _():
        for d in range(2):
            dt = jnp.exp(ldt_ref[d])
            lre = lre_ref[d]
            lim = lim_ref[d]
            mag = jnp.exp(lre * dt)
            ar = mag * jnp.cos(lim * dt)
            ai = mag * jnp.sin(lim * dt)
            den = lre * lre + lim * lim
            fr = ((ar - 1.0) * lre + ai * lim) / den
            fi = (ai * lre - (ar - 1.0) * lim) / den
            coef_scr[d, 0:nb, :] = jnp.broadcast_to(ar, (nb, gn))
            coef_scr[d, nb:2 * nb, :] = jnp.broadcast_to(ai, (nb, gn))
            wre = wbre_ref[d]
            wim = wbim_ref[d]
            wb_scr[d, :, 0:gn] = (wre * fr - wim * fi).astype(BF16)
            wb_scr[d, :, gn:2 * gn] = (wim * fr + wre * fi).astype(BF16)
        st_scr[...] = jnp.zeros_like(st_scr)

    half = 128
    u_refs = (uf_ref, ub_ref)
    y_refs = (yf_ref, yb_ref)

    for d in range(2):
        for b in range(nb):
            for s in range(2):
                rel_scr[d, s, pl.ds(b, lc, stride=nb), :] = u_refs[d][b, :, s * half:(s + 1) * half]
    for d in range(2):
        u_tm = jnp.concatenate([rel_scr[d, 0], rel_scr[d, 1]], axis=1).astype(BF16)
        x_scr[d] = _nn(u_tm, wb_scr[d])

    def body(t, carry):
        out = []
        for d in range(2):
            sr, si = carry[2 * d], carry[2 * d + 1]
            ar = coef_scr[d, 0:nb, :]
            ai = coef_scr[d, nb:2 * nb, :]
            tt = t if d == 0 else lc - 1 - t
            r0 = pl.multiple_of(tt * nb, nb)
            xr = x_scr[d, pl.ds(r0, nb), 0:gn]
            xi = x_scr[d, pl.ds(r0, nb), gn:2 * gn]
            nsr = ar * sr - ai * si + xr
            nsi = ar * si + ai * sr + xi
            x_scr[d, pl.ds(r0, nb), 0:gn] = nsr
            x_scr[d, pl.ds(r0, nb), gn:2 * gn] = nsi
            out += [nsr, nsi]
        return tuple(out)

    init = (st_scr[0, 0:nb, :], st_scr[0, nb:2 * nb, :], st_scr[1, 0:nb, :], st_scr[1, nb:2 * nb, :])
    fin = lax.fori_loop(0, lc, body, init, unroll=scan_unroll)
    for d in range(2):
        st_scr[d, 0:nb, :] = fin[2 * d]
        st_scr[d, nb:2 * nb, :] = fin[2 * d + 1]
        y = (_nn(x_scr[d, :, 0:gn].astype(BF16), wcre_ref[d])
             - _nn(x_scr[d, :, gn:2 * gn].astype(BF16), wcim_ref[d]))
        rel_scr[d, 0] = y[:, 0:half]
        rel_scr[d, 1] = y[:, half:2 * half]
    for d in range(2):
        for b in range(nb):
            for s in range(2):
                c0 = b * 2 * half + s * half
                y_refs[d][:, c0:c0 + half] = rel_scr[d, s, pl.ds(b, lc, stride=nb), :]


def _s5_call(z, lre, lim, ldt, wbre, wbim, wcre, wcim, *, n_ctx, n_all):
    B, Ta, _ = z.shape
    W = GROUP_W
    lc = CHUNK
    gn = lre.shape[-1]
    blk = lc * B
    kern = functools.partial(_s5_kernel, lc=lc, nb=B, scan_unroll=True)
    full = lambda a: pl.BlockSpec(a.shape, lambda i: (0,) * a.ndim)
    out = jax.ShapeDtypeStruct((Ta, B * W), F32)
    return pl.pallas_call(
        kern,
        out_shape=(out, out),
        grid=(n_all,),
        in_specs=[pl.BlockSpec((B, lc, W), lambda i: (0, i, Z_S5)),
                  pl.BlockSpec((B, lc, W), lambda i: (0, _bwd_chunk(i, n_ctx, n_all), Z_S5)),
                  full(lre), full(lim), full(ldt), full(wbre), full(wbim), full(wcre), full(wcim)],
        out_specs=(pl.BlockSpec((lc, B * W), lambda i: (i, 0)),
                   pl.BlockSpec((lc, B * W), lambda i: (_bwd_chunk(i, n_ctx, n_all), 0))),
        scratch_shapes=[pltpu.VMEM((2, W, 2 * gn), BF16),
                        pltpu.VMEM((2, 2 * B, gn), F32),
                        pltpu.VMEM((2, 2 * B, gn), F32),
                        pltpu.VMEM((2, 2, blk, 128), F32),
                        pltpu.VMEM((2, blk, 2 * gn), F32)],
        compiler_params=_cparams("arbitrary"),
        name="s5_scan",
    )(z, z, lre, lim, ldt, wbre, wbim, wcre, wcim)


def _mla_proj_kernel(z_ref, qn_ref, kvn_ref, wq_ref, wqr_ref, wk_ref, wv_ref, cos_ref, sin_ref,
                     q_ref, k_ref, v_ref, *, scale):
    z = z_ref[0]
    cq = z[:, 0:256]
    ckv = z[:, 256:384]
    kr = z[:, 384:512]
    krr = z[:, 512:640]
    cos = cos_ref[...]
    sin = sin_ref[...]
    cqb = _rms(cq, qn_ref[...]).astype(BF16)
    ckvb = _rms(ckv, kvn_ref[...]).astype(BF16)
    q = _nn(cqb, wq_ref[...])
    qr = _nn(cqb, wqr_ref[...])
    kn = _nn(ckvb, wk_ref[...])
    krp = kr * cos + krr * sin
    for h in range(N_HEADS):
        sl = slice(h * 128, (h + 1) * 128)
        q_ref[0, h] = ((q[:, sl] * cos + qr[:, sl] * sin) * scale).astype(BF16)
        k_ref[0, h] = (kn[:, sl] + krp).astype(BF16)
    v_ref[0] = _nn(ckvb, wv_ref[...]).astype(BF16)


def _mla_proj_call(z, qn, kvn, wq, wqr, wk, wv, cos_t, sin_t, *, scale):
    B, Ta, _ = z.shape
    tm = _pick_tile(Ta, 544)
    kern = functools.partial(_mla_proj_kernel, scale=scale)
    full = lambda a: pl.BlockSpec(a.shape, lambda b, t: (0,) * a.ndim)
    return pl.pallas_call(
        kern,
        out_shape=(jax.ShapeDtypeStruct((B, N_HEADS, Ta, 128), BF16),
                   jax.ShapeDtypeStruct((B, N_HEADS, Ta, 128), BF16),
                   jax.ShapeDtypeStruct((B, Ta, GROUP_W), BF16)),
        grid=(B, Ta // tm),
        in_specs=[pl.BlockSpec((1, tm, 3 * ZB), lambda b, t: (b, t, 0)),
                  full(qn), full(kvn), full(wq), full(wqr), full(wk), full(wv),
                  pl.BlockSpec((tm, 128), lambda b, t: (t, 0)),
                  pl.BlockSpec((tm, 128), lambda b, t: (t, 0))],
        out_specs=(pl.BlockSpec((1, N_HEADS, tm, 128), lambda b, t: (b, 0, t, 0)),
                   pl.BlockSpec((1, N_HEADS, tm, 128), lambda b, t: (b, 0, t, 0)),
                   pl.BlockSpec((1, tm, GROUP_W), lambda b, t: (b, t, 0))),
        compiler_params=_cparams("parallel", "parallel"),
        name="mla_proj",
    )(z, qn, kvn, wq, wqr, wk, wv, cos_t, sin_t)


def _attn_kernel(q_ref, k_ref, v_ref, o_ref, *, tc, ta, n_ctx_tiles, key_chunk):
    i = pl.program_id(2)
    lane = _iota((1, 128), 1)

    @pl.when(i < n_ctx_tiles)
    def _():
        outs = []
        for j in range(2):
            s = _nt(q_ref[0, j], k_ref[0, j, 0:tc, :])
            p = jnp.exp2(s - jnp.max(s, axis=-1, keepdims=True))
            l = jnp.sum(p, axis=-1, keepdims=True)
            outs.append(_nn(p.astype(BF16), v_ref[0, 0:tc, :]) * (1.0 / l))
        o_ref[0] = jnp.where(lane < HEAD_DIM, outs[0], outs[1])

    @pl.when(i >= n_ctx_tiles)
    def _():
        bounds = [0, tc] + list(range(tc + key_chunk, ta + 1, key_chunk))
        units = [(c, j) for c in range(len(bounds) - 1) for j in range(2)]
        qs = [q_ref[0, 0], q_ref[0, 1]]
        score = lambda c, j: _nt(qs[j], k_ref[0, j, bounds[c]:bounds[c + 1], :])
        m = [None, None]
        l = [None, None]
        acc = [None, None]

        def weighted_values(pend):
            c, j, pb, alpha = pend
            pv = _nn(pb, v_ref[0, bounds[c]:bounds[c + 1], :])
            acc[j] = pv if alpha is None else alpha * acc[j] + pv

        pending = None
        s_next = score(*units[0])
        for idx, (c, j) in enumerate(units):
            s = s_next
            if idx + 1 < len(units):
                s_next = score(*units[idx + 1])
            mc = jnp.max(s, axis=-1, keepdims=True)
            if c == 0:
                alpha = None
                m[j] = mc
                p = jnp.exp2(s - mc)
                l[j] = jnp.sum(p, axis=-1, keepdims=True)
            else:
                m_new = jnp.maximum(m[j], mc)
                alpha = jnp.exp2(m[j] - m_new)
                p = jnp.exp2(s - m_new)
                l[j] = alpha * l[j] + jnp.sum(p, axis=-1, keepdims=True)
                m[j] = m_new
            if pending is not None:
                weighted_values(pending)
            pending = (c, j, p.astype(BF16), alpha)
        weighted_values(pending)
        o_ref[0] = jnp.where(lane < HEAD_DIM, acc[0] * (1.0 / l[0]), acc[1] * (1.0 / l[1]))


def _attn_call(q, k, v, *, tc):
    B, H, Ta, _ = q.shape
    tq = _pick_tile(math.gcd(tc, Ta), 256)
    kern = functools.partial(_attn_kernel, tc=tc, ta=Ta, n_ctx_tiles=tc // tq,
                             key_chunk=_pick_tile(Ta - tc, 1024, 128))
    return pl.pallas_call(
        kern,
        out_shape=jax.ShapeDtypeStruct((B, Ta, GROUP_W), F32),
        grid=(B, H // 2, Ta // tq),
        in_specs=[pl.BlockSpec((1, 2, tq, 128), lambda b, h, i: (b, h, i, 0)),
                  pl.BlockSpec((1, 2, Ta, 128), lambda b, h, i: (b, h, 0, 0)),
                  pl.BlockSpec((1, Ta, 128), lambda b, h, i: (b, 0, h))],
        out_specs=pl.BlockSpec((1, tq, 128), lambda b, h, i: (b, i, h)),
        compiler_params=_cparams("parallel", "parallel", "arbitrary"),
        name="mla_attn",
    )(q, k, v)


def _chunk_masks(d):
    L = CHUNK
    row = _iota((L, GROUP_W), 0)
    s_idx = _iota((L, GROUP_W), 1) % L
    tr = _iota((L, L), 0)
    tcol = _iota((L, L), 1)
    if d == 0:
        return (tcol <= tr).astype(F32), s_idx < row, s_idx <= row, s_idx == row
    return (tcol >= tr).astype(F32), s_idx > row, s_idx >= row, s_idx == row


def _rwkv_kernel(rf_ref, rb_ref, lf_ref, lb_ref, wup_ref, aup_ref, gup_ref, w0_ref, a0_ref,
                 kk_ref, ka_ref, rk_ref, yf_ref, yb_ref, g_ref, bon_ref, s_scr, *, cps):
    i = pl.program_id(1)
    L = CHUNK
    W = GROUP_W

    @pl.when(i == 0)
    def _():
        s_scr[...] = jnp.zeros_like(s_scr)

    hms = _head_masks()
    bd = _block_diag_mask()
    rs = lambda x: _row_stack(x, hms)
    bd_b = bd > 0.5
    bdiag = lambda x: jnp.where(bd_b, jnp.concatenate([x.astype(BF16)] * N_HEADS, axis=0), 0.0)
    eye_w = (_iota((W, W), 0) == _iota((W, W), 1)).astype(F32)
    rows = cps * L
    grp_r = _iota((rows, rows), 0)
    grp_c = _iota((rows, rows), 1)
    same_chunk = (grp_r // L) == (grp_c // L)

    per_dir = []
    for d, (r_ref, l_ref) in enumerate(((rf_ref, lf_ref), (rb_ref, lb_ref))):
        rkv = r_ref[0]
        lora = l_ref[0]
        r = rkv[:, 0:W]
        k = rkv[:, W:2 * W]
        v = rkv[:, 2 * W:3 * W]
        _, strict, incl, eye = _chunk_masks(d)
        tri = (same_chunk & ((grp_c <= grp_r) if d == 0 else (grp_c >= grp_r))).astype(BF16)

        lw = -math.exp(-0.5) * _sigmoid(w0_ref[d] + _mm1(_nn, jnp.tanh(lora), wup_ref[d]))
        a = _sigmoid(a0_ref[d] + _mm1(_nn, lora, aup_ref[d]))
        kkv = k * kk_ref[...]
        kkn = kkv * lax.rsqrt(jnp.maximum(_mm1(_nn, kkv * kkv, bd), 1e-24))
        keff = k * (1.0 + (a - 1.0) * ka_ref[...])
        kka = kkn * a
        lw_hi, lw_lo = _split2(lw)
        cum2 = _nn(tri, jnp.concatenate([lw_hi, lw_lo], axis=1))
        cum = cum2[:, 0:W] + cum2[:, W:2 * W]
        e_dn = jnp.exp(-cum)
        per_dir.append(dict(v=v, lw=lw, cum=cum, kka=kka, keff=keff, strict=strict, incl=incl, eye=eye,
                            al=-kkn * jnp.exp(cum - lw), rt=r * jnp.exp(cum), bh=kka * e_dn, kh=keff * e_dn))
        if d == 0:
            g_ref[0] = _mm1(_nn, _sigmoid(lora), gup_ref[...])
            bon_ref[0] = _mm1(_nn, r * k * rk_ref[...], bd) * v

    chains = []
    for c in range(cps):
        for d in range(2):
            pd = per_dir[d]
            sl = slice(c * L, (c + 1) * L)
            tot = jnp.sum(pd["lw"][sl], axis=0, keepdims=True)
            e_tc = jnp.exp(tot - pd["cum"][sl])
            ch = dict(d=d, c=c, al=pd["al"][sl], rt=pd["rt"][sl], v=pd["v"][sl], tot=tot,
                      bt=pd["kka"][sl] * e_tc, kt=pd["keff"][sl] * e_tc)
            ch["rsv"] = rs(ch["v"])
            a_all = _mm1(_nt, jnp.concatenate([ch["al"], ch["rt"]], axis=0),
                         jnp.concatenate([rs(pd["bh"][sl]), rs(pd["kh"][sl])], axis=0))
            ch["a_ab"] = jnp.where(pd["strict"], a_all[0:L, 0:W], 0.0)
            ch["a_ak"] = jnp.where(pd["strict"], a_all[0:L, W:2 * W], 0.0)
            ch["a_rb"] = jnp.where(pd["incl"], a_all[L:2 * L, 0:W], 0.0)
            ch["a_rk"] = jnp.where(pd["incl"], a_all[L:2 * L, W:2 * W], 0.0)
            ch["p"] = jnp.where(pd["eye"], 1.0, 0.0) + ch["a_ab"]
            chains.append(ch)

    for ch in chains:
        ch["sq"] = _mmn(_nn, ch["a_ab"], bdiag(ch["a_ab"]))
        ch["zk"] = _mm1(_nn, ch["a_ak"], ch["rsv"])
        ch["y0k"] = _mm1(_nn, ch["a_rk"], ch["rsv"])
    n_sq = int(math.log2(L)) - 1
    for it in range(n_sq):
        for ch in chains:
            if it < n_sq - 1:
                ps = _mmn(_nn, jnp.concatenate([ch["p"], ch["sq"]], axis=0), bdiag(ch["sq"]))
                ch["p"] = ch["p"] + ps[0:L]
                ch["sq"] = ps[L:2 * L]
            else:
                ch["p"] = ch["p"] + _mmn(_nn, ch["p"], bdiag(ch["sq"]))
    for ch in chains:
        pu = _mm1(_nn, ch["p"], jnp.concatenate([rs(ch["al"]), rs(ch["zk"])], axis=1))
        ch["w"], ch["uk"] = pu[:, 0:W], pu[:, W:2 * W]
    for ch in chains:
        gy = _mm1(_nn, ch["a_rb"], jnp.concatenate([rs(ch["w"]), rs(ch["uk"])], axis=1))
        ch["g"] = ch["rt"] + gy[:, 0:W]
        ch["y0"] = gy[:, W:2 * W] + ch["y0k"]
        ch["decay"] = jnp.exp(ch["tot"])
        ch["pm"] = bd * _mm1(_tn, ch["w"], ch["bt"])
        ch["q0"] = bd * _mm1(_tn, jnp.concatenate([ch["uk"], ch["v"]], axis=0),
                             jnp.concatenate([ch["bt"], ch["kt"]], axis=0))

    by_key = {(ch["d"], ch["c"]): ch for ch in chains}
    st = [s_scr[0], s_scr[1]]
    ys = [[None] * cps, [None] * cps]
    for step in range(cps):
        for d in range(2):
            c = step if d == 0 else cps - 1 - step
            ch = by_key[(d, c)]
            ys[d][c] = _mm1(_nt, ch["g"], st[d]) + ch["y0"]
            st[d] = st[d] * ch["decay"] + (_mm1(_nn, st[d], ch["pm"]) + ch["q0"])
    for d, y_ref in enumerate((yf_ref, yb_ref)):
        s_scr[d] = st[d]
        y_ref[0] = jnp.concatenate(ys[d], axis=0)


def _chunks_per_step(n_ctx, n_all):
    for cps in (4, 2, 1):
        if n_ctx % cps == 0 and (n_all - n_ctx) % cps == 0:
            return cps


def _rwkv_call(z, wup, aup, gup, w0, a0, kk, ka, rk, *, n_ctx, n_all):
    B, Ta, _ = z.shape
    W = GROUP_W
    cps = _chunks_per_step(n_ctx, n_all)
    rows = cps * CHUNK
    gc, ga = n_ctx // cps, n_all // cps
    full = lambda a: pl.BlockSpec(a.shape, lambda b, i: (0,) * a.ndim)
    fwd = lambda blk: (lambda b, i: (b, i, blk))
    bwd = lambda blk: (lambda b, i: (b, _bwd_chunk(i, gc, ga), blk))
    out = jax.ShapeDtypeStruct((B, Ta, W), F32)
    return pl.pallas_call(
        functools.partial(_rwkv_kernel, cps=cps),
        out_shape=(out, out, out, out),
        grid=(B, ga),
        in_specs=[pl.BlockSpec((1, rows, 3 * W), fwd(Z_RKV // 3)),
                  pl.BlockSpec((1, rows, 3 * W), bwd(Z_RKV // 3)),
                  pl.BlockSpec((1, rows, W), fwd(Z_LORA)),
                  pl.BlockSpec((1, rows, W), bwd(Z_LORA)),
                  full(wup), full(aup), full(gup), full(w0), full(a0), full(kk), full(ka), full(rk)],
        out_specs=(pl.BlockSpec((1, rows, W), fwd(0)), pl.BlockSpec((1, rows, W), bwd(0)),
                   pl.BlockSpec((1, rows, W), fwd(0)), pl.BlockSpec((1, rows, W), fwd(0))),
        scratch_shapes=[pltpu.VMEM((2, W, W), F32)],
        compiler_params=_cparams("parallel", "arbitrary"),
        name="rwkv7",
    )(z, z, z, z, wup, aup, gup, w0, a0, kk, ka, rk)


def _mlstm_kernel(qf_ref, qb_ref, vf_ref, vb_ref, gf_ref, gb_ref, gbias_ref,
                  yf_ref, yb_ref, c_scr, n_scr, m_scr, *, cps):
    i = pl.program_id(1)
    L = CHUNK
    W = GROUP_W

    @pl.when(i == 0)
    def _():
        c_scr[...] = jnp.zeros_like(c_scr)
        n_scr[...] = jnp.zeros_like(n_scr)
        m_scr[...] = jnp.zeros_like(m_scr)

    hms = _head_masks()
    bd = _block_diag_mask()
    rs = lambda x: _row_stack(x, hms)
    ci = _iota((W, W), 0)
    cj = _iota((W, W), 1) // HEAD_DIM
    neg_inf = -jnp.inf
    rows = cps * L
    grp_r = _iota((rows, rows), 0)
    grp_c = _iota((rows, rows), 1)
    same_chunk = (grp_r // L) == (grp_c // L)

    def sum2(x, e):
        hi, lo = _split2(x)
        eb = e.astype(BF16)
        return _nn(hi, eb) + _nn(lo, eb)

    per_dir = []
    for d, (q_ref, v_ref, g_ref) in enumerate(((qf_ref, vf_ref, gf_ref), (qb_ref, vb_ref, gb_ref))):
        qk = q_ref[0]
        g = g_ref[0] + gbias_ref[...]
        _, strict, incl, eye = _chunk_masks(d)
        tri = (same_chunk & ((grp_c <= grp_r) if d == 0 else (grp_c >= grp_r))).astype(BF16)
        expand = jnp.concatenate([(ci == d * 2 * N_HEADS + cj).astype(F32),
                                  (ci == d * 2 * N_HEADS + N_HEADS + cj).astype(F32)], axis=1)
        gates = sum2(g, expand)
        li = gates[:, 0:W]
        lf = -_softplus(-gates[:, W:2 * W])
        lf_hi, lf_lo = _split2(lf)
        b2 = _nn(tri, jnp.concatenate([lf_hi, lf_lo], axis=1))
        per_dir.append(dict(q=qk[:, 0:W], k=qk[:, W:2 * W] * (HEAD_DIM ** -0.5), v=v_ref[0][:, 0:W],
                            li=li, bcol=b2[:, 0:W] + b2[:, W:2 * W], incl=incl, eye=eye))

    chains = []
    for c in range(cps):
        for d in range(2):
            pd = per_dir[d]
            sl = slice(c * L, (c + 1) * L)
            q, k, v, li, bcol = pd["q"][sl], pd["k"][sl], pd["v"][sl], pd["li"][sl], pd["bcol"][sl]
            brow = jnp.sum(jnp.where(pd["eye"], bcol, 0.0), axis=0, keepdims=True)
            lirow = jnp.sum(jnp.where(pd["eye"], li, 0.0), axis=0, keepdims=True)
            logd = jnp.where(pd["incl"], bcol - brow + lirow, neg_inf)
            mx = jnp.zeros((L, W), F32)
            for hm in hms:
                mh = jnp.max(jnp.where(hm, logd, neg_inf), axis=1, keepdims=True)
                mx = jnp.where(hm, mh, mx)
            blast = bcol[L - 1:L, :] if d == 0 else bcol[0:1, :]
            lwc = blast - bcol + li
            mlw = jnp.max(lwc, axis=0, keepdims=True)
            kw = k * jnp.exp(lwc - mlw)
            chains.append(dict(d=d, c=c, q=q, v=v, bcol=bcol, mx=mx, blast=blast, mlw=mlw, kw=kw,
                               dexp=jnp.exp(logd - mx), rsk=rs(k), rsv=rs(v),
                               nu0=jnp.sum(kw, axis=0, keepdims=True)))
    for ch in chains:
        ch["sp"] = _mm1(_nt, ch["q"], ch["rsk"]) * ch["dexp"]
        ch["cu0"] = bd * _mm1(_tn, ch["v"], ch["kw"])
    for ch in chains:
        ch["num0"] = _mm1(_nn, ch["sp"], ch["rsv"])
        ch["den0"] = _mm1(_nn, ch["sp"], bd)

    by_key = {(ch["d"], ch["c"]): ch for ch in chains}
    c_st = [c_scr[0], c_scr[1]]
    n_row = [n_scr[0, 0:1, :], n_scr[1, 0:1, :]]
    m_row = [m_scr[0, 0:1, :], m_scr[1, 0:1, :]]
    ys = [[None] * cps, [None] * cps]
    for step in range(cps):
        for d in range(2):
            c = step if d == 0 else cps - 1 - step
            ch = by_key[(d, c)]
            inter = ch["bcol"] + m_row[d]
            mt = jnp.maximum(inter, ch["mx"])
            f_in = jnp.exp(ch["mx"] - mt)
            w_int = jnp.exp(inter - mt)
            num = f_in * ch["num0"] + w_int * _mm1(_nt, ch["q"], c_st[d])
            den = f_in * ch["den0"] + w_int * _mm1(_nn, ch["q"] * n_row[d], bd)
            ys[d][c] = num / jnp.maximum(jnp.abs(den), jnp.exp(-mt))
            m_new = jnp.maximum(ch["blast"] + m_row[d], ch["mlw"])
            sc = jnp.exp(ch["blast"] + m_row[d] - m_new)
            e2 = jnp.exp(ch["mlw"] - m_new)
            c_st[d] = sc * c_st[d] + e2 * ch["cu0"]
            n_row[d] = sc * n_row[d] + e2 * ch["nu0"]
            m_row[d] = m_new
    for d, y_ref in enumerate((yf_ref, yb_ref)):
        c_scr[d] = c_st[d]
        n_scr[d, 0:1, :] = n_row[d]
        m_scr[d, 0:1, :] = m_row[d]
        y_ref[0] = jnp.concatenate(ys[d], axis=0)


def _mlstm_call(z, gbias, *, n_ctx, n_all):
    B, Ta, _ = z.shape
    W = GROUP_W
    cps = _chunks_per_step(n_ctx, n_all)
    L = cps * CHUNK
    n_ctx, n_all = n_ctx // cps, n_all // cps
    fwd = lambda blk: (lambda b, i: (b, i, blk))
    bwd = lambda blk: (lambda b, i: (b, _bwd_chunk(i, n_ctx, n_all), blk))
    out = jax.ShapeDtypeStruct((B, Ta, W), F32)
    return pl.pallas_call(
        functools.partial(_mlstm_kernel, cps=cps),
        out_shape=(out, out),
        grid=(B, n_all),
        in_specs=[pl.BlockSpec((1, L, 2 * W), fwd(Z_MQK // 2)),
                  pl.BlockSpec((1, L, 2 * W), bwd(Z_MQK // 2)),
                  pl.BlockSpec((1, L, 2 * W), fwd(Z_MVO // 2)),
                  pl.BlockSpec((1, L, 2 * W), bwd(Z_MVO // 2)),
                  pl.BlockSpec((1, L, W), fwd(Z_GATE)),
                  pl.BlockSpec((1, L, W), bwd(Z_GATE)),
                  pl.BlockSpec(gbias.shape, lambda b, i: (0, 0))],
        out_specs=(pl.BlockSpec((1, L, W), fwd(0)), pl.BlockSpec((1, L, W), bwd(0))),
        scratch_shapes=[pltpu.VMEM((2, W, W), F32), pltpu.VMEM((2, 8, W), F32), pltpu.VMEM((2, 8, W), F32)],
        compiler_params=_cparams("parallel", "arbitrary"),
        name="mlstm",
    )(z, z, z, z, z, z, gbias)


def _head_norm(y, bd, eps):
    bdb = bd.astype(BF16)

    def head_mean(x):
        hi, lo = _split2(x)
        return (_nn(hi, bdb) + _nn(lo, bdb)) * (1.0 / HEAD_DIM)

    yc = y - head_mean(y)
    return yc * lax.rsqrt(head_mean(yc * yc) + eps)


def _outproj_kernel(x_ref, mods_ref, s5f_ref, s5b_ref, u_ref, at_ref, rf_ref, rb_ref, rg_ref, rbon_ref,
                    mf_ref, mb_ref, vo_ref, s5d_ref, wglu_ref, bglu_ref, lnw_ref, lnb_ref, mnw_ref,
                    gpost_ref, wout_ref, o_ref, *, tm, tc, n_batch):
    b = pl.program_id(0)
    t = pl.program_id(1)
    d = x_ref.shape[-1]
    W = GROUP_W
    is_ctx = (t * tm + _iota((tm, 1), 0)) < tc
    gate = jnp.where(is_ctx, mods_ref[pl.ds(n_batch, 1), pl.ds(5 * d, d)],
                     mods_ref[pl.ds(b, 1), pl.ds(5 * d, d)])
    bd = _block_diag_mask()

    y = s5f_ref[...] + s5b_ref[...] + s5d_ref[...] * u_ref[0]
    zg = 0.5 * y * (1.0 + jnp.tanh(math.sqrt(2.0 / math.pi) * (y + 0.044715 * (y * y * y))))
    s5o = zg * _sigmoid(_mm1(_nn, zg, wglu_ref[...]) + bglu_ref[...])

    yr = _head_norm(rf_ref[0] + rb_ref[0], bd, RWKV_GN_EPS)
    rwo = (yr * lnw_ref[...] + lnb_ref[...] + rbon_ref[0]) * rg_ref[0]

    ym = _head_norm(mf_ref[0] + mb_ref[0], bd, NORM_EPS)
    mlo = ym * mnw_ref[...] * _sigmoid(vo_ref[0][:, W:2 * W])

    cat = jnp.concatenate([s5o, at_ref[0], rwo, mlo], axis=1).astype(BF16)
    yx = _nn(cat, wout_ref[...])
    o_ref[0] = x_ref[0] + gate * _rms(yx, gpost_ref[...])


def _outproj_call(xa, mods, s5f, s5b, z, attn, rf, rb, rg, rbon, mf, mb,
                  s5d, wglu, bglu, lnw, lnb, mnw, gpost, wout, *, tc):
    B, Ta, D = xa.shape
    W = GROUP_W
    tm = _pick_tile(Ta, 544)
    kern = functools.partial(_outproj_kernel, tm=tm, tc=tc, n_batch=B)
    full = lambda a: pl.BlockSpec(a.shape, lambda b, t: (0,) * a.ndim)
    tok = pl.BlockSpec((1, tm, W), lambda b, t: (b, t, 0))
    tmaj = pl.BlockSpec((tm, W), lambda b, t: (t, b))
    return pl.pallas_call(
        kern,
        out_shape=jax.ShapeDtypeStruct((B, Ta, D), F32),
        grid=(B, Ta // tm),
        in_specs=[pl.BlockSpec((1, tm, D), lambda b, t: (b, t, 0)), full(mods),
                  tmaj, tmaj, pl.BlockSpec((1, tm, W), lambda b, t: (b, t, Z_S5)),
                  tok, tok, tok, tok, tok, tok, tok,
                  pl.BlockSpec((1, tm, 2 * W), lambda b, t: (b, t, Z_MVO // 2)),
                  full(s5d), full(wglu), full(bglu), full(lnw), full(lnb), full(mnw), full(gpost), full(wout)],
        out_specs=pl.BlockSpec((1, tm, D), lambda b, t: (b, t, 0)),
        compiler_params=_cparams("parallel", "parallel"),
        name="mix_out",
    )(xa, mods, s5f, s5b, z, attn, rf, rb, rg, rbon, mf, mb, z, s5d, wglu, bglu, lnw, lnb, mnw, gpost, wout)


def _rope_rotate_cols(w):
    h = ROPE_AXIS // 2
    return jnp.concatenate([-w[..., h:2 * h], w[..., 0:h], -w[..., 3 * h:4 * h], w[..., 2 * h:3 * h]], axis=-1)


def _inproj_relayout(w_in):
    L, D, _ = w_in.shape
    o_s5, o_mla, o_rw, o_ml = 0, 256, 672, 1568
    seg = lambda a, n: w_in[:, :, a:a + n]
    zer = lambda n: jnp.zeros((L, D, n), w_in.dtype)
    k_rope = seg(o_mla + 384, ROPE_DIM)
    parts = [seg(o_mla, 256), seg(o_mla + 256, 128),
             zer(64), k_rope, zer(32),
             zer(64), _rope_rotate_cols(k_rope), zer(32), zer(128),
             seg(o_rw, 768), seg(o_s5, 256),
             seg(o_rw + 768, 128), zer(128),
             seg(o_ml, 512), seg(o_ml + 512, 512), seg(o_ml + 1024, 16), zer(ZB - 16)]
    out = jnp.concatenate(parts, axis=2)
    assert out.shape[2] == Z_COLS
    return out


def _rope_tables(T, tc):
    rows = T // GRID_W
    r_idx, c_idx = jnp.meshgrid(jnp.arange(rows), jnp.arange(GRID_W), indexing='ij')
    inv_freq = 1.0 / (ROPE_BASE ** (jnp.arange(0, ROPE_AXIS, 2, dtype=F32) / ROPE_AXIS))
    ang_r = r_idx.reshape(-1, 1).astype(F32) * inv_freq
    ang_c = c_idx.reshape(-1, 1).astype(F32) * inv_freq
    ang = jnp.concatenate([ang_r, ang_r, ang_c, ang_c], axis=-1)
    cos = jnp.concatenate([jnp.ones((tc, ROPE_DIM), F32), jnp.cos(ang)], axis=0)
    sin = jnp.concatenate([jnp.zeros((tc, ROPE_DIM), F32), jnp.sin(ang)], axis=0)
    ta = T + tc
    cos_t = jnp.concatenate([jnp.ones((ta, 64), F32), cos, jnp.zeros((ta, 32), F32)], axis=1)
    sin_t = jnp.concatenate([jnp.zeros((ta, 64), F32), sin, jnp.zeros((ta, 32), F32)], axis=1)
    return cos_t, sin_t


def _pad_rows(w, r0, total):
    pad = [(0, 0)] * (w.ndim - 2) + [(r0, total - r0 - w.shape[-2]), (0, 0)]
    return jnp.pad(w, pad)


def kernel(x, c, ctx, c_ctx, w_ada, b_ada, norm_pre, norm_post, ffn_w_gate, ffn_w_up, ffn_w_down, w_in, w_out, s5_lam_re, s5_lam_im, s5_log_dt, s5_b_re, s5_b_im, s5_c_re, s5_c_im, s5_d, s5_w_glu, s5_b_glu, mla_q_norm, mla_kv_norm, mla_w_uq, mla_w_ukv, rwkv_conv_w, rwkv_conv_b, rwkv_w0, rwkv_w_up, rwkv_a0, rwkv_a_up, rwkv_g_up, rwkv_k_k, rwkv_k_a, rwkv_r_k, rwkv_ln_w, rwkv_ln_b, mlstm_conv_w, mlstm_conv_b, mlstm_gate_b, mlstm_norm):
    B, T, D = x.shape
    Tc = ctx.shape[1]
    Ta = T + Tc
    L = w_ada.shape[0]
    W = GROUP_W
    assert T % CHUNK == 0 and Tc % CHUNK == 0 and B % 8 == 0 and B <= 8
    n_ctx, n_all = Tc // CHUNK, Ta // CHUNK

    rows = 16
    cvec = jnp.concatenate([c, c_ctx[None, :], jnp.zeros((rows - B - 1, D), F32)], axis=0)
    mods_all = _ada_call(cvec, w_ada, b_ada)

    w_in_re = _inproj_relayout(w_in).astype(BF16)
    cw = jnp.zeros((L, Z_NBLK, 8, ZB), F32)
    rc = jnp.concatenate([rwkv_conv_w, rwkv_conv_b[:, None, :]], axis=1).reshape(L, 4, 3, ZB).transpose(0, 2, 1, 3)
    mc = jnp.concatenate([mlstm_conv_w, mlstm_conv_b[:, None, :]], axis=1).reshape(L, 4, 2, ZB).transpose(0, 2, 1, 3)
    cw = cw.at[:, Z_RKV:Z_RKV + 3, 0:4].set(rc).at[:, Z_MQK:Z_MQK + 2, 0:4].set(mc)

    wg = ffn_w_gate.astype(BF16)
    wu = ffn_w_up.astype(BF16)
    wd = ffn_w_down.astype(BF16)
    wout = w_out.astype(BF16)

    G = s5_lam_re.shape[2]
    N = s5_lam_re.shape[3]
    eye_g = jnp.eye(G, dtype=F32)
    lre = s5_lam_re.reshape(L, 2, 1, G * N)
    lim = s5_lam_im.reshape(L, 2, 1, G * N)
    ldt = jnp.repeat(s5_log_dt, N, axis=-1).reshape(L, 2, 1, G * N)
    wbre = jnp.einsum('ldgnp,gh->ldgphn', s5_b_re, eye_g).reshape(L, 2, G * S5_P, G * N)
    wbim = jnp.einsum('ldgnp,gh->ldgphn', s5_b_im, eye_g).reshape(L, 2, G * S5_P, G * N)
    wcre = jnp.einsum('ldgpn,gh->ldgnhp', s5_c_re, eye_g).reshape(L, 2, G * N, G * S5_P).astype(BF16)
    wcim = jnp.einsum('ldgpn,gh->ldgnhp', s5_c_im, eye_g).reshape(L, 2, G * N, G * S5_P).astype(BF16)

    nope = HEAD_DIM
    qd = nope + ROPE_DIM
    wq4 = mla_w_uq.reshape(L, -1, N_HEADS, qd)
    wq = jnp.pad(wq4, ((0, 0), (0, 0), (0, 0), (0, 128 - qd))).reshape(L, -1, N_HEADS * 128).astype(BF16)
    wq_rot = _rope_rotate_cols(wq4[..., nope:])
    wqr = jnp.pad(wq_rot, ((0, 0), (0, 0), (0, 0), (nope, 128 - qd))).reshape(L, -1, N_HEADS * 128).astype(BF16)
    wkv4 = mla_w_ukv.reshape(L, -1, N_HEADS, 2 * HEAD_DIM)
    wk = jnp.pad(wkv4[..., :HEAD_DIM], ((0, 0), (0, 0), (0, 0), (0, 64))).reshape(L, -1, N_HEADS * 128).astype(BF16)
    wv = wkv4[..., HEAD_DIM:].reshape(L, -1, W).astype(BF16)
    cos_t, sin_t = _rope_tables(T, Tc)
    scale = float(qd) ** -0.5 * math.log2(math.e)

    wup = _pad_rows(rwkv_w_up, 0, W)
    aup = _pad_rows(rwkv_a_up, 32, W)
    gup = _pad_rows(rwkv_g_up, 64, W)
    gbias = jnp.pad(mlstm_gate_b, ((0, 0), (0, W - mlstm_gate_b.shape[1])))

    xa = jnp.concatenate([ctx, x], axis=1)
    r1 = lambda a: a.reshape(1, -1)

    for l in range(L):
        mods = mods_all[l]
        xa2, hmix = _ffn_call(xa.reshape(B * Ta, D), mods, r1(norm_pre[l, 0]), r1(norm_post[l, 0]),
                              r1(norm_pre[l, 1]), wg[l, 0], wu[l, 0], wd[l, 0],
                              ta=Ta, tc=Tc, n_batch=B, koff=0, emit_hmix=True)
        xa = xa2.reshape(B, Ta, D)
        z = _inproj_call(hmix.reshape(B, Ta, D), w_in_re[l], cw[l], tc=Tc)

        s5f, s5b = _s5_call(z, lre[l], lim[l], ldt[l], wbre[l], wbim[l], wcre[l], wcim[l],
                            n_ctx=n_ctx, n_all=n_all)

        q, k, v = _mla_proj_call(z, r1(mla_q_norm[l]), r1(mla_kv_norm[l]), wq[l], wqr[l], wk[l], wv[l],
                                 cos_t, sin_t, scale=scale)
        attn = _attn_call(q, k, v, tc=Tc)

        rf, rb, rg, rbon = _rwkv_call(z, wup[l], aup[l], gup[l], rwkv_w0[l][:, None, :], rwkv_a0[l][:, None, :],
                                      r1(rwkv_k_k[l]), r1(rwkv_k_a[l]), r1(rwkv_r_k[l]), n_ctx=n_ctx, n_all=n_all)
        mf, mb = _mlstm_call(z, gbias[l:l + 1], n_ctx=n_ctx, n_all=n_all)

        xa = _outproj_call(xa, mods, s5f, s5b, z, attn, rf, rb, rg, rbon,
                           mf, mb, r1(s5_d[l]), s5_w_glu[l].astype(BF16), r1(s5_b_glu[l]), r1(rwkv_ln_w[l]),
                           r1(rwkv_ln_b[l]), r1(mlstm_norm[l]), r1(norm_post[l, 1]), wout[l], tc=Tc)

        xa2, _ = _ffn_call(xa.reshape(B * Ta, D), mods, r1(norm_pre[l, 2]), r1(norm_post[l, 2]),
                           r1(norm_pre[l, 1]), wg[l, 1], wu[l, 1], wd[l, 1],
                           ta=Ta, tc=Tc, n_batch=B, koff=6, emit_hmix=False)
        xa = xa2.reshape(B, Ta, D)

    return xa[:, Tc:, :]
```

```python
import functools
import math

import numpy as np
import jax
import jax.numpy as jnp
from jax import lax
from jax.experimental import pallas as pl
from jax.experimental.pallas import tpu as pltpu

F32 = jnp.float32
BF16 = jnp.bfloat16

GROUP_W = 256
HEAD_DIM = 64
N_HEADS = GROUP_W // HEAD_DIM
CHUNK = 64
N_MOD = 9
NORM_EPS = 1e-6
RWKV_GN_EPS = HEAD_DIM * 1e-5
GRID_W = 64
ROPE_BASE = 10000.0
ROPE_DIM = 32
ROPE_AXIS = 16
S5_P = 16
S5_STATE = 64
MACARON = 0.5
VMEM_LIMIT_BYTES = 56 * 1024 * 1024

ZB = 256
Z_MLA, Z_RKV, Z_S5, Z_LORA, Z_MQK, Z_MVO, Z_GATE = 0, 3, 6, 7, 8, 10, 12
Z_NBLK = 13
Z_COLS = Z_NBLK * ZB


def _nn(a, b):
    return lax.dot_general(a, b, (((1,), (0,)), ((), ())), preferred_element_type=F32)


def _nt(a, b):
    return lax.dot_general(a, b, (((1,), (1,)), ((), ())), preferred_element_type=F32)


def _tn(a, b):
    return lax.dot_general(a, b, (((0,), (0,)), ((), ())), preferred_element_type=F32)


def _split2(x):
    hi = x.astype(BF16)
    lo = (x - hi.astype(F32)).astype(BF16)
    return hi, lo


def _split3(x):
    p1 = x.astype(BF16)
    r = x - p1.astype(F32)
    p2 = r.astype(BF16)
    p3 = (r - p2.astype(F32)).astype(BF16)
    return p1, p2, p3


def _mm3(dotf, a, b):
    ah, al = _split2(a)
    bh, bl = _split2(b)
    return dotf(ah, bh) + (dotf(ah, bl) + dotf(al, bh))


def _mm1(dotf, a, b):
    return dotf(a.astype(BF16), b.astype(BF16))


_mmn = _mm1


def _mm_exact_rhs(a, e):
    eb = e.astype(BF16)
    p1, p2, p3 = _split3(a)
    return _nn(p1, eb) + (_nn(p2, eb) + _nn(p3, eb))


def _mm_exact_lhs(e, a):
    eb = e.astype(BF16)
    p1, p2, p3 = _split3(a)
    return _nn(eb, p1) + (_nn(eb, p2) + _nn(eb, p3))


def _rms(x, g):
    return x * lax.rsqrt(jnp.mean(x * x, axis=-1, keepdims=True) + NORM_EPS) * g


def _sigmoid(x):
    return 0.5 * jnp.tanh(0.5 * x) + 0.5


def _group_tri_masks(tri_scr, rows):
    r = _iota((rows, rows), 0)
    c = _iota((rows, rows), 1)
    same = (r // CHUNK) == (c // CHUNK)
    tri_scr[0] = (same & (c <= r)).astype(BF16)
    tri_scr[1] = (same & (c >= r)).astype(BF16)


def _softplus(x):
    return jnp.maximum(x, 0.0) + jnp.log(1.0 + jnp.exp(-jnp.abs(x)))


def _iota(shape, dim):
    return lax.broadcasted_iota(jnp.int32, shape, dim)


def _head_masks():
    lane = _iota((1, GROUP_W), 1)
    return [lane // HEAD_DIM == h for h in range(N_HEADS)]


def _block_diag_mask():
    r = _iota((GROUP_W, GROUP_W), 0) // HEAD_DIM
    c = _iota((GROUP_W, GROUP_W), 1) // HEAD_DIM
    return (r == c).astype(F32)


def _row_stack(x, hms):
    xb = x.astype(BF16)
    return jnp.concatenate([jnp.where(m, xb, 0.0) for m in hms], axis=0)


def _pick_tile(n, target, mult=16):
    best = None
    for t in range(mult, min(n, target) + 1, mult):
        if n % t == 0:
            best = t
    if best is None:
        raise ValueError(f"no tile for {n}")
    return best


def _cparams(*sem):
    return pltpu.CompilerParams(dimension_semantics=sem, vmem_limit_bytes=VMEM_LIMIT_BYTES)


def _ada_kernel(c_ref, w_ref, b_ref, o_ref):
    c = c_ref[...]
    s = c * _sigmoid(c)
    o_ref[0] = _mm3(_nn, s, w_ref[0]) + b_ref[0]


def _ada_call(cvec, w_ada, b_ada):
    L, D, N = w_ada.shape
    R = cvec.shape[0]
    tn = _pick_tile(N, 1152, 128)
    return pl.pallas_call(
        _ada_kernel,
        out_shape=jax.ShapeDtypeStruct((L, R, N), F32),
        grid=(L, N // tn),
        in_specs=[pl.BlockSpec((R, D), lambda l, j: (0, 0)),
                  pl.BlockSpec((1, D, tn), lambda l, j: (l, 0, j)),
                  pl.BlockSpec((1, 1, tn), lambda l, j: (l, 0, j))],
        out_specs=pl.BlockSpec((1, R, tn), lambda l, j: (l, 0, j)),
        compiler_params=_cparams("parallel", "parallel"),
        name="ada_mod",
    )(cvec, w_ada, b_ada.reshape(L, 1, N))


FFN_COLS = 256


def _ffn_kernel(xn_ref, xp_ref, mods_ref, gpre_ref, gpost_ref, gmix_ref, wg_ref, wu_ref, wd_ref,
                o_ref, hmix_ref, h_scr, acc_scr, *, tm, tiles_per_batch, tc, n_batch, koff, emit_hmix,
                n_tiles, f_split):
    i = pl.program_id(0)
    f = pl.program_id(1)
    d = xn_ref.shape[-1]
    fd = wg_ref.shape[1]

    def mod_of(tile):
        b = tile // tiles_per_batch
        t0 = (tile % tiles_per_batch) * tm
        is_ctx = (t0 + _iota((tm, 1), 0)) < tc

        def mod(fn):
            row = lambda r: (lambda k: mods_ref[pl.ds(r, 1), pl.ds(k * d, d)])
            return jnp.where(is_ctx, fn(row(n_batch)), fn(row(b)))
        return mod

    unit = lambda x: x * lax.rsqrt(jnp.mean(x * x, axis=-1, keepdims=True) + NORM_EPS)

    def pre_norm(tile, x):
        mod = mod_of(tile)
        gain = mod(lambda m: gpre_ref[...] * (1.0 + m(koff + 1)))
        return (unit(x) * gain + mod(lambda m: m(koff))).astype(BF16)

    def finish(tile, acc, x):
        mod = mod_of(tile)
        xn = x + unit(acc) * mod(lambda m: (MACARON * m(koff + 2)) * gpost_ref[...])
        o_ref[...] = xn
        if emit_hmix:
            gain = mod(lambda m: gmix_ref[...] * (1.0 + m(4)))
            hmix_ref[...] = (unit(xn) * gain + mod(lambda m: m(3))).astype(BF16)
        else:
            hmix_ref[...] = jnp.zeros_like(hmix_ref)

    def hidden_cols(hb, lo, hi):
        acc = None
        for c0 in range(lo, hi, FFN_COLS):
            g = _nn(hb, wg_ref[:, c0:c0 + FFN_COLS])
            u = _nn(hb, wu_ref[:, c0:c0 + FFN_COLS])
            t = _nn((g * _sigmoid(g) * u).astype(BF16), wd_ref[c0:c0 + FFN_COLS, :])
            acc = t if acc is None else acc + t
        return acc

    @pl.when(jnp.logical_and(i == 0, f == 0))
    def _():
        h_scr[0] = pre_norm(0, xn_ref[...])
        acc_scr[1] = jnp.zeros((tm, d), F32)

    for s in range(2):
        tile = 2 * i + s

        @pl.when(jnp.logical_and(tile < n_tiles, f == 2 * s))
        def _(s=s, tile=tile):
            finish(jnp.maximum(tile - 1, 0), acc_scr[1 - s], xp_ref[...])
            acc_scr[s] = hidden_cols(h_scr[s], 0, f_split)

        @pl.when(jnp.logical_and(tile < n_tiles, f == 2 * s + 1))
        def _(s=s, tile=tile):
            h_scr[1 - s] = pre_norm(jnp.minimum(tile + 1, n_tiles - 1), xn_ref[...])
            acc_scr[s] += hidden_cols(h_scr[s], f_split, fd)

    @pl.when(jnp.logical_and(2 * i == n_tiles, f == 0))
    def _():
        finish(n_tiles - 1, acc_scr[1], xp_ref[...])


def _ffn_call(xa2, mods, gpre, gpost, gmix, wg, wu, wd, *, ta, tc, n_batch, koff, emit_hmix):
    M, D = xa2.shape
    Fd = wg.shape[1]
    assert Fd % FFN_COLS == 0
    tm = _pick_tile(ta, 544)
    n_tiles = M // tm
    assert n_tiles % 2 == 0
    f_split = ((Fd // FFN_COLS + 1) // 2) * FFN_COLS
    kern = functools.partial(_ffn_kernel, tm=tm, tiles_per_batch=ta // tm, tc=tc, n_batch=n_batch,
                             koff=koff, emit_hmix=emit_hmix, n_tiles=n_tiles, f_split=f_split)
    hm_rows = tm if emit_hmix else 16
    last = n_tiles - 1
    resident = lambda a: pl.BlockSpec(a.shape, lambda i, f: (0, 0), pipeline_mode=pl.Buffered(1))
    vec = pl.BlockSpec((1, D), lambda i, f: (0, 0))
    return pl.pallas_call(
        kern,
        out_shape=(jax.ShapeDtypeStruct((M, D), F32),
                   jax.ShapeDtypeStruct((M if emit_hmix else 16 * n_tiles, D), BF16)),
        grid=(n_tiles // 2 + 1, 4),
        in_specs=[pl.BlockSpec((tm, D), lambda i, f: (jnp.minimum(2 * i + (f + 1) // 2, last), 0)),
                  pl.BlockSpec((tm, D), lambda i, f: (jnp.clip(2 * i - 1 + f // 2, 0, last), 0)),
                  pl.BlockSpec(mods.shape, lambda i, f: (0, 0)),
                  vec, vec, vec, resident(wg), resident(wu), resident(wd)],
        out_specs=(pl.BlockSpec((tm, D), lambda i, f: (jnp.clip(2 * i - 1 + f // 2, 0, last), 0)),
                   pl.BlockSpec((hm_rows, D), lambda i, f: (jnp.clip(2 * i - 1 + f // 2, 0, last), 0))),
        scratch_shapes=[pltpu.VMEM((2, tm, D), BF16), pltpu.VMEM((2, tm, D), F32)],
        compiler_params=_cparams("arbitrary", "arbitrary"),
        name="half_ffn",
    )(xa2, xa2, mods, gpre, gpost, gmix, wg, wu, wd)


def _inproj_kernel(h_ref, w_ref, cw_ref, o_ref, *, tc, conv_lo, conv_hi, silu_lo, silu_hi):
    nb = pl.program_id(1)
    z = _nn(h_ref[0], w_ref[...])
    ta = z.shape[0]
    is_conv = ((nb >= conv_lo[0]) & (nb < conv_hi[0])) | ((nb >= conv_lo[1]) & (nb < conv_hi[1]))
    is_silu = (nb >= silu_lo) & (nb < silu_hi)

    @pl.when(jnp.logical_not(is_conv))
    def _():
        o_ref[0] = z

    def conv():
        row = _iota((ta, 1), 0)
        zp = jnp.where((row == 0) | (row == tc), 0.0, pltpu.roll(z, 1, 0))
        zn = jnp.where((row == tc - 1) | (row == ta - 1), 0.0, pltpu.roll(z, ta - 1, 0))
        cw = cw_ref[0]
        return cw[3:4] + zp * cw[0:1] + z * cw[1:2] + zn * cw[2:3]

    @pl.when(is_conv & jnp.logical_not(is_silu))
    def _():
        o_ref[0] = conv()

    @pl.when(is_conv & is_silu)
    def _():
        y = conv()
        o_ref[0] = y * _sigmoid(y)


def _inproj_call(hmix3, w_re, cw, *, tc):
    B, Ta, D = hmix3.shape
    kern = functools.partial(_inproj_kernel, tc=tc, conv_lo=(Z_RKV, Z_MQK), conv_hi=(Z_RKV + 3, Z_MQK + 2),
                             silu_lo=Z_MQK, silu_hi=Z_MQK + 2)
    return pl.pallas_call(
        kern,
        out_shape=jax.ShapeDtypeStruct((B, Ta, Z_COLS), F32),
        grid=(B, Z_NBLK),
        in_specs=[pl.BlockSpec((1, Ta, D), lambda b, n: (b, 0, 0)),
                  pl.BlockSpec((D, ZB), lambda b, n: (0, n)),
                  pl.BlockSpec((1, 8, ZB), lambda b, n: (n, 0, 0))],
        out_specs=pl.BlockSpec((1, Ta, ZB), lambda b, n: (b, 0, n)),
        compiler_params=_cparams("parallel", "arbitrary"),
        name="in_proj",
    )(hmix3, w_re, cw)


def _bwd_chunk(i, n_ctx, n_all):
    return jnp.where(i < n_ctx, n_ctx - 1 - i, n_all - 1 - (i - n_ctx))


def _s5_kernel(uf_ref, ub_ref, lre_ref, lim_ref, ldt_ref, wbre_ref, wbim_ref, wcre_ref, wcim_ref,
               yf_ref, yb_ref, wb_scr, coef_scr, st_scr, rel_scr, x_scr, *, lc, nb, scan_unroll):
    i = pl.program_id(0)
    gn = lre_ref.shape[-1]

    @pl.when(i == 0)
    def _():
        for d in range(2):
            dt = jnp.exp(ldt_ref[d])
            lre = lre_ref[d]
            lim = lim_ref[d]
            mag = jnp.exp(lre * dt)
            ar = mag * jnp.cos(lim * dt)
            ai = mag * jnp.sin(lim * dt)
            den = lre * lre + lim * lim
            fr = ((ar - 1.0) * lre + ai * lim) / den
            fi = (ai * lre - (ar - 1.0) * lim) / den
            coef_scr[d, 0:nb, :] = jnp.broadcast_to(ar, (nb, gn))
            coef_scr[d, nb:2 * nb, :] = jnp.broadcast_to(ai, (nb, gn))
            wre = wbre_ref[d]
            wim = wbim_ref[d]
            wb_scr[d, :, 0:gn] = (wre * fr - wim * fi).astype(BF16)
            wb_scr[d, :, gn:2 * gn] = (wim * fr + wre * fi).astype(BF16)
        st_scr[...] = jnp.zeros_like(st_scr)

    half = 128
    u_refs = (uf_ref, ub_ref)
    y_refs = (yf_ref, yb_ref)

    for d in range(2):
        for b in range(nb):
            for s in range(2):
                rel_scr[d, s, pl.ds(b, lc, stride=nb), :] = u_refs[d][b, :, s * half:(s + 1) * half]
    for d in range(2):
        u_tm = jnp.concatenate([rel_scr[d, 0], rel_scr[d, 1]], axis=1).astype(BF16)
        x_scr[d] = _nn(u_tm, wb_scr[d])

    def body(t, carry):
        out = []
        for d in range(2):
            sr, si = carry[2 * d], carry[2 * d + 1]
            ar = coef_scr[d, 0:nb, :]
            ai = coef_scr[d, nb:2 * nb, :]
            tt = t if d == 0 else lc - 1 - t
            r0 = pl.multiple_of(tt * nb, nb)
            xr = x_scr[d, pl.ds(r0, nb), 0:gn]
            xi = x_scr[d, pl.ds(r0, nb), gn:2 * gn]
            nsr = ar * sr - ai * si + xr
            nsi = ar * si + ai * sr + xi
            x_scr[d, pl.ds(r0, nb), 0:gn] = nsr
            x_scr[d, pl.ds(r0, nb), gn:2 * gn] = nsi
            out += [nsr, nsi]
        return tuple(out)

    init = (st_scr[0, 0:nb, :], st_scr[0, nb:2 * nb, :], st_scr[1, 0:nb, :], st_scr[1, nb:2 * nb, :])
    fin = lax.fori_loop(0, lc, body, init, unroll=scan_unroll)
    for d in range(2):
        st_scr[d, 0:nb, :] = fin[2 * d]
        st_scr[d, nb:2 * nb, :] = fin[2 * d + 1]
        y = (_nn(x_scr[d, :, 0:gn].astype(BF16), wcre_ref[d])
             - _nn(x_scr[d, :, gn:2 * gn].astype(BF16), wcim_ref[d]))
        rel_scr[d, 0] = y[:, 0:half]
        rel_scr[d, 1] = y[:, half:2 * half]
    for d in range(2):
        for b in range(nb):
            for s in range(2):
                c0 = b * 2 * half + s * half
                y_refs[d][:, c0:c0 + half] = rel_scr[d, s, pl.ds(b, lc, stride=nb), :]


def _s5_call(z, lre, lim, ldt, wbre, wbim, wcre, wcim, *, n_ctx, n_all):
    B, Ta, _ = z.shape
    W = GROUP_W
    lc = CHUNK
    gn = lre.shape[-1]
    blk = lc * B
    kern = functools.partial(_s5_kernel, lc=lc, nb=B, scan_unroll=True)
    full = lambda a: pl.BlockSpec(a.shape, lambda i: (0,) * a.ndim)
    out = jax.ShapeDtypeStruct((Ta, B * W), F32)
    return pl.pallas_call(
        kern,
        out_shape=(out, out),
        grid=(n_all,),
        in_specs=[pl.BlockSpec((B, lc, W), lambda i: (0, i, Z_S5)),
                  pl.BlockSpec((B, lc, W), lambda i: (0, _bwd_chunk(i, n_ctx, n_all), Z_S5)),
                  full(lre), full(lim), full(ldt), full(wbre), full(wbim), full(wcre), full(wcim)],
        out_specs=(pl.BlockSpec((lc, B * W), lambda i: (i, 0)),
                   pl.BlockSpec((lc, B * W), lambda i: (_bwd_chunk(i, n_ctx, n_all), 0))),
        scratch_shapes=[pltpu.VMEM((2, W, 2 * gn), BF16),
                        pltpu.VMEM((2, 2 * B, gn), F32),
                        pltpu.VMEM((2, 2 * B, gn), F32),
                        pltpu.VMEM((2, 2, blk, 128), F32),
                        pltpu.VMEM((2, blk, 2 * gn), F32)],
        compiler_params=_cparams("arbitrary"),
        name="s5_scan",
    )(z, z, lre, lim, ldt, wbre, wbim, wcre, wcim)


def _mla_proj_kernel(z_ref, qn_ref, kvn_ref, wq_ref, wqr_ref, wk_ref, wv_ref, cos_ref, sin_ref,
                     q_ref, k_ref, v_ref, *, scale):
    z = z_ref[0]
    cq = z[:, 0:256]
    ckv = z[:, 256:384]
    kr = z[:, 384:512]
    krr = z[:, 512:640]
    cos = cos_ref[...]
    sin = sin_ref[...]
    cqb = _rms(cq, qn_ref[...]).astype(BF16)
    ckvb = _rms(ckv, kvn_ref[...]).astype(BF16)
    q = _nn(cqb, wq_ref[...])
    qr = _nn(cqb, wqr_ref[...])
    kn = _nn(ckvb, wk_ref[...])
    krp = kr * cos + krr * sin
    for h in range(N_HEADS):
        sl = slice(h * 128, (h + 1) * 128)
        q_ref[0, h] = ((q[:, sl] * cos + qr[:, sl] * sin) * scale).astype(BF16)
        k_ref[0, h] = (kn[:, sl] + krp).astype(BF16)
    ones_pad = ((_iota((1, N_HEADS * 128), 1) % 128) >= HEAD_DIM).astype(F32)
    vv = _nn(ckvb, wv_ref[...]) + ones_pad
    for h in range(N_HEADS):
        v_ref[0, h] = vv[:, h * 128:(h + 1) * 128].astype(BF16)


def _mla_proj_call(z, qn, kvn, wq, wqr, wk, wv, cos_t, sin_t, *, scale):
    B, Ta, _ = z.shape
    tm = _pick_tile(Ta, 544)
    kern = functools.partial(_mla_proj_kernel, scale=scale)
    full = lambda a: pl.BlockSpec(a.shape, lambda b, t: (0,) * a.ndim)
    return pl.pallas_call(
        kern,
        out_shape=(jax.ShapeDtypeStruct((B, N_HEADS, Ta, 128), BF16),
                   jax.ShapeDtypeStruct((B, N_HEADS, Ta, 128), BF16),
                   jax.ShapeDtypeStruct((B, N_HEADS, Ta, 128), BF16)),
        grid=(B, Ta // tm),
        in_specs=[pl.BlockSpec((1, tm, 3 * ZB), lambda b, t: (b, t, 0)),
                  full(qn), full(kvn), full(wq), full(wqr), full(wk), full(wv),
                  pl.BlockSpec((tm, 128), lambda b, t: (t, 0)),
                  pl.BlockSpec((tm, 128), lambda b, t: (t, 0))],
        out_specs=(pl.BlockSpec((1, N_HEADS, tm, 128), lambda b, t: (b, 0, t, 0)),
                   pl.BlockSpec((1, N_HEADS, tm, 128), lambda b, t: (b, 0, t, 0)),
                   pl.BlockSpec((1, N_HEADS, tm, 128), lambda b, t: (b, 0, t, 0))),
        compiler_params=_cparams("parallel", "parallel"),
        name="mla_proj",
    )(z, qn, kvn, wq, wqr, wk, wv, cos_t, sin_t)


def _attn_kernel(q_ref, k_ref, v_ref, o_ref, *, tc, ta, n_ctx_tiles, key_chunk):
    i = pl.program_id(2)
    lane = _iota((1, 128), 1)

    def write_out(acc):
        outs = [a * (1.0 / pltpu.roll(a, HEAD_DIM, 1)) for a in acc]
        o_ref[0] = jnp.where(lane < HEAD_DIM, outs[0], pltpu.roll(outs[1], HEAD_DIM, 1))

    @pl.when(i < n_ctx_tiles)
    def _():
        acc = []
        for j in range(2):
            s = _nt(q_ref[0, j], k_ref[0, j, 0:tc, :])
            p = jnp.exp2(s - jnp.max(s, axis=-1, keepdims=True))
            acc.append(_nn(p.astype(BF16), v_ref[0, j, 0:tc, :]))
        write_out(acc)

    @pl.when(i >= n_ctx_tiles)
    def _():
        bounds = [0, tc] + list(range(tc + key_chunk, ta + 1, key_chunk))
        units = [(c, j) for c in range(len(bounds) - 1) for j in range(2)]
        qs = [q_ref[0, 0], q_ref[0, 1]]
        score = lambda c, j: _nt(qs[j], k_ref[0, j, bounds[c]:bounds[c + 1], :])
        m = [None, None]
        acc = [None, None]

        def weighted_values(pend):
            c, j, pb, alpha = pend
            pv = _nn(pb, v_ref[0, j, bounds[c]:bounds[c + 1], :])
            acc[j] = pv if alpha is None else alpha * acc[j] + pv

        pending = None
        s_next = score(*units[0])
        for idx, (c, j) in enumerate(units):
            s = s_next
            if idx + 1 < len(units):
                s_next = score(*units[idx + 1])
            mc = jnp.max(s, axis=-1, keepdims=True)
            if c == 0:
                alpha = None
                m[j] = mc
                p = jnp.exp2(s - mc)
            else:
                m_new = jnp.maximum(m[j], mc)
                alpha = jnp.exp2(m[j] - m_new)
                p = jnp.exp2(s - m_new)
                m[j] = m_new
            if pending is not None:
                weighted_values(pending)
            pending = (c, j, p.astype(BF16), alpha)
        weighted_values(pending)
        write_out(acc)


def _attn_call(q, k, v, *, tc):
    B, H, Ta, _ = q.shape
    tq = _pick_tile(math.gcd(tc, Ta), 256)
    kern = functools.partial(_attn_kernel, tc=tc, ta=Ta, n_ctx_tiles=tc // tq,
                             key_chunk=_pick_tile(Ta - tc, 1024, 128))
    return pl.pallas_call(
        kern,
        out_shape=jax.ShapeDtypeStruct((B, Ta, GROUP_W), F32),
        grid=(B, H // 2, Ta // tq),
        in_specs=[pl.BlockSpec((1, 2, tq, 128), lambda b, h, i: (b, h, i, 0)),
                  pl.BlockSpec((1, 2, Ta, 128), lambda b, h, i: (b, h, 0, 0)),
                  pl.BlockSpec((1, 2, Ta, 128), lambda b, h, i: (b, h, 0, 0))],
        out_specs=pl.BlockSpec((1, tq, 128), lambda b, h, i: (b, i, h)),
        compiler_params=_cparams("parallel", "parallel", "arbitrary"),
        name="mla_attn",
    )(q, k, v)


def _chunk_masks(d):
    L = CHUNK
    row = _iota((L, GROUP_W), 0)
    s_idx = _iota((L, GROUP_W), 1) % L
    tr = _iota((L, L), 0)
    tcol = _iota((L, L), 1)
    if d == 0:
        return (tcol <= tr).astype(F32), s_idx < row, s_idx <= row, s_idx == row
    return (tcol >= tr).astype(F32), s_idx > row, s_idx >= row, s_idx == row


def _rwkv_kernel(rf_ref, rb_ref, lf_ref, lb_ref, wup_ref, aup_ref, gup_ref, w0_ref, a0_ref,
                 kk_ref, ka_ref, rk_ref, yf_ref, yb_ref, g_ref, bon_ref, s_scr, tri_scr, bdb_scr, *, cps):
    i = pl.program_id(1)
    L = CHUNK
    W = GROUP_W

    @pl.when(i == 0)
    def _():
        s_scr[...] = jnp.zeros_like(s_scr)
        _group_tri_masks(tri_scr, cps * L)
        bdb_scr[...] = _block_diag_mask().astype(BF16)

    hms = _head_masks()
    bd = _block_diag_mask()
    bdb = bdb_scr[...]
    rs = lambda x: _row_stack(x, hms)
    bd_b = bd > 0.5
    bdiag = lambda x: jnp.where(bd_b, jnp.concatenate([x.astype(BF16)] * N_HEADS, axis=0), 0.0)

    per_dir = []
    for d, (r_ref, l_ref) in enumerate(((rf_ref, lf_ref), (rb_ref, lb_ref))):
        rkv = r_ref[0]
        lora = l_ref[0]
        r = rkv[:, 0:W]
        k = rkv[:, W:2 * W]
        v = rkv[:, 2 * W:3 * W]
        _, strict, incl, eye = _chunk_masks(d)
        tri = tri_scr[d]

        lw = -math.exp(-0.5) * _sigmoid(w0_ref[d] + _mm1(_nn, jnp.tanh(lora), wup_ref[d]))
        a = _sigmoid(a0_ref[d] + _mm1(_nn, lora, aup_ref[d]))
        kkv = k * kk_ref[...]
        kkn = kkv * lax.rsqrt(jnp.maximum(_mm1(_nn, kkv * kkv, bdb), 1e-24))
        keff = k * (1.0 + (a - 1.0) * ka_ref[...])
        kka = kkn * a
        lw_hi, lw_lo = _split2(lw)
        cum2 = _nn(tri, jnp.concatenate([lw_hi, lw_lo], axis=1))
        cum = cum2[:, 0:W] + cum2[:, W:2 * W]
        e_dn = jnp.exp(-cum)
        per_dir.append(dict(v=v, lw=lw, cum=cum, kka=kka, keff=keff, strict=strict, incl=incl, eye=eye,
                            al=-kkn * jnp.exp(cum - lw), rt=r * jnp.exp(cum), bh=kka * e_dn, kh=keff * e_dn))
        if d == 0:
            g_ref[0] = _mm1(_nn, _sigmoid(lora), gup_ref[...])
            bon_ref[0] = _mm1(_nn, r * k * rk_ref[...], bdb) * v

    chains = []
    for c in range(cps):
        for d in range(2):
            pd = per_dir[d]
            sl = slice(c * L, (c + 1) * L)
            tot = jnp.sum(pd["lw"][sl], axis=0, keepdims=True)
            e_tc = jnp.exp(tot - pd["cum"][sl])
            ch = dict(d=d, c=c, al=pd["al"][sl], rt=pd["rt"][sl], v=pd["v"][sl], tot=tot,
                      bt=pd["kka"][sl] * e_tc, kt=pd["keff"][sl] * e_tc)
            ch["rsv"] = rs(ch["v"])
            a_all = _mm1(_nt, jnp.concatenate([ch["al"], ch["rt"]], axis=0),
                         jnp.concatenate([rs(pd["bh"][sl]), rs(pd["kh"][sl])], axis=0))
            ch["a_ab"] = jnp.where(pd["strict"], a_all[0:L, 0:W], 0.0)
            ch["a_ak"] = jnp.where(pd["strict"], a_all[0:L, W:2 * W], 0.0)
            ch["a_rb"] = jnp.where(pd["incl"], a_all[L:2 * L, 0:W], 0.0)
            ch["a_rk"] = jnp.where(pd["incl"], a_all[L:2 * L, W:2 * W], 0.0)
            ch["p"] = jnp.where(pd["eye"], 1.0, 0.0) + ch["a_ab"]
            chains.append(ch)

    for ch in chains:
        ch["sq"] = _mmn(_nn, ch["a_ab"], bdiag(ch["a_ab"]))
        ch["zk"] = _mm1(_nn, ch["a_ak"], ch["rsv"])
        ch["y0k"] = _mm1(_nn, ch["a_rk"], ch["rsv"])
    n_sq = int(math.log2(L)) - 1
    for it in range(n_sq):
        for ch in chains:
            if it < n_sq - 1:
                ps = _mmn(_nn, jnp.concatenate([ch["p"], ch["sq"]], axis=0), bdiag(ch["sq"]))
                ch["p"] = ch["p"] + ps[0:L]
                ch["sq"] = ps[L:2 * L]
            else:
                ch["p"] = ch["p"] + _mmn(_nn, ch["p"], bdiag(ch["sq"]))
    for ch in chains:
        pu = _mm1(_nn, ch["p"], jnp.concatenate([rs(ch["al"]), rs(ch["zk"])], axis=1))
        ch["w"], ch["uk"] = pu[:, 0:W], pu[:, W:2 * W]
    for ch in chains:
        gy = _mm1(_nn, ch["a_rb"], jnp.concatenate([rs(ch["w"]), rs(ch["uk"])], axis=1))
        ch["g"] = ch["rt"] + gy[:, 0:W]
        ch["y0"] = gy[:, W:2 * W] + ch["y0k"]
        ch["decay"] = jnp.exp(ch["tot"])
        ch["pm"] = bd * _mm1(_tn, ch["w"], ch["bt"])
        ch["q0"] = bd * _mm1(_tn, jnp.concatenate([ch["uk"], ch["v"]], axis=0),
                             jnp.concatenate([ch["bt"], ch["kt"]], axis=0))

    by_key = {(ch["d"], ch["c"]): ch for ch in chains}
    st = [s_scr[0], s_scr[1]]
    ys = [[None] * cps, [None] * cps]
    for step in range(cps):
        for d in range(2):
            c = step if d == 0 else cps - 1 - step
            ch = by_key[(d, c)]
            ys[d][c] = _mm1(_nt, ch["g"], st[d]) + ch["y0"]
            st[d] = st[d] * ch["decay"] + (_mm1(_nn, st[d], ch["pm"]) + ch["q0"])
    for d, y_ref in enumerate((yf_ref, yb_ref)):
        s_scr[d] = st[d]
        y_ref[0] = jnp.concatenate(ys[d], axis=0)


def _chunks_per_step(n_ctx, n_all):
    for cps in (4, 2, 1):
        if n_ctx % cps == 0 and (n_all - n_ctx) % cps == 0:
            return cps


def _rwkv_call(z, wup, aup, gup, w0, a0, kk, ka, rk, *, n_ctx, n_all):
    B, Ta, _ = z.shape
    W = GROUP_W
    cps = _chunks_per_step(n_ctx, n_all)
    rows = cps * CHUNK
    gc, ga = n_ctx // cps, n_all // cps
    full = lambda a: pl.BlockSpec(a.shape, lambda b, i: (0,) * a.ndim)
    fwd = lambda blk: (lambda b, i: (b, i, blk))
    bwd = lambda blk: (lambda b, i: (b, _bwd_chunk(i, gc, ga), blk))
    out = jax.ShapeDtypeStruct((B, Ta, W), F32)
    return pl.pallas_call(
        functools.partial(_rwkv_kernel, cps=cps),
        out_shape=(out, out, out, out),
        grid=(B, ga),
        in_specs=[pl.BlockSpec((1, rows, 3 * W), fwd(Z_RKV // 3)),
                  pl.BlockSpec((1, rows, 3 * W), bwd(Z_RKV // 3)),
                  pl.BlockSpec((1, rows, W), fwd(Z_LORA)),
                  pl.BlockSpec((1, rows, W), bwd(Z_LORA)),
                  full(wup), full(aup), full(gup), full(w0), full(a0), full(kk), full(ka), full(rk)],
        out_specs=(pl.BlockSpec((1, rows, W), fwd(0)), pl.BlockSpec((1, rows, W), bwd(0)),
                   pl.BlockSpec((1, rows, W), fwd(0)), pl.BlockSpec((1, rows, W), fwd(0))),
        scratch_shapes=[pltpu.VMEM((2, W, W), F32), pltpu.VMEM((2, rows, rows), BF16), pltpu.VMEM((W, W), BF16)],
        compiler_params=_cparams("parallel", "arbitrary"),
        name="rwkv7",
    )(z, z, z, z, wup, aup, gup, w0, a0, kk, ka, rk)


def _mlstm_kernel(qf_ref, qb_ref, vf_ref, vb_ref, gf_ref, gb_ref, gbias_ref,
                  yf_ref, yb_ref, c_scr, n_scr, m_scr, tri_scr, exp_scr, bdb_scr, *, cps):
    i = pl.program_id(1)
    L = CHUNK
    W = GROUP_W

    @pl.when(i == 0)
    def _():
        c_scr[...] = jnp.zeros_like(c_scr)
        n_scr[...] = jnp.zeros_like(n_scr)
        m_scr[...] = jnp.zeros_like(m_scr)
        _group_tri_masks(tri_scr, cps * L)
        bdb_scr[...] = _block_diag_mask().astype(BF16)
        ci = _iota((W, W), 0)
        cj = _iota((W, W), 1) // HEAD_DIM
        for d in range(2):
            exp_scr[d, :, 0:W] = (ci == d * 2 * N_HEADS + cj).astype(BF16)
            exp_scr[d, :, W:2 * W] = (ci == d * 2 * N_HEADS + N_HEADS + cj).astype(BF16)

    hms = _head_masks()
    bd = _block_diag_mask()
    bdb = bdb_scr[...]
    rs = lambda x: _row_stack(x, hms)
    neg_inf = -jnp.inf

    def sum2(x, eb):
        hi, lo = _split2(x)
        return _nn(hi, eb) + _nn(lo, eb)

    per_dir = []
    for d, (q_ref, v_ref, g_ref) in enumerate(((qf_ref, vf_ref, gf_ref), (qb_ref, vb_ref, gb_ref))):
        qk = q_ref[0]
        g = g_ref[0] + gbias_ref[...]
        _, strict, incl, eye = _chunk_masks(d)
        tri = tri_scr[d]
        gates = sum2(g, exp_scr[d])
        li = gates[:, 0:W]
        lf = -_softplus(-gates[:, W:2 * W])
        lf_hi, lf_lo = _split2(lf)
        b2 = _nn(tri, jnp.concatenate([lf_hi, lf_lo], axis=1))
        per_dir.append(dict(q=qk[:, 0:W], k=qk[:, W:2 * W] * (HEAD_DIM ** -0.5), v=v_ref[0][:, 0:W],
                            li=li, bcol=b2[:, 0:W] + b2[:, W:2 * W], incl=incl, eye=eye))

    chains = []
    for c in range(cps):
        for d in range(2):
            pd = per_dir[d]
            sl = slice(c * L, (c + 1) * L)
            q, k, v, li, bcol = pd["q"][sl], pd["k"][sl], pd["v"][sl], pd["li"][sl], pd["bcol"][sl]
            brow = jnp.sum(jnp.where(pd["eye"], bcol, 0.0), axis=0, keepdims=True)
            lirow = jnp.sum(jnp.where(pd["eye"], li, 0.0), axis=0, keepdims=True)
            logd = jnp.where(pd["incl"], bcol - brow + lirow, neg_inf)
            mx = jnp.zeros((L, W), F32)
            for hm in hms:
                mh = jnp.max(jnp.where(hm, logd, neg_inf), axis=1, keepdims=True)
                mx = jnp.where(hm, mh, mx)
            blast = bcol[L - 1:L, :] if d == 0 else bcol[0:1, :]
            lwc = blast - bcol + li
            mlw = jnp.max(lwc, axis=0, keepdims=True)
            kw = k * jnp.exp(lwc - mlw)
            chains.append(dict(d=d, c=c, q=q, v=v, bcol=bcol, mx=mx, blast=blast, mlw=mlw, kw=kw,
                               dexp=jnp.exp(logd - mx), rsk=rs(k), rsv=rs(v),
                               nu0=jnp.sum(kw, axis=0, keepdims=True)))
    for ch in chains:
        ch["sp"] = _mm1(_nt, ch["q"], ch["rsk"]) * ch["dexp"]
        ch["cu0"] = bd * _mm1(_tn, ch["v"], ch["kw"])
    for ch in chains:
        ch["num0"] = _mm1(_nn, ch["sp"], ch["rsv"])
        ch["den0"] = _mm1(_nn, ch["sp"], bdb)

    by_key = {(ch["d"], ch["c"]): ch for ch in chains}
    c_st = [c_scr[0], c_scr[1]]
    n_row = [n_scr[0, 0:1, :], n_scr[1, 0:1, :]]
    m_row = [m_scr[0, 0:1, :], m_scr[1, 0:1, :]]
    ys = [[None] * cps, [None] * cps]
    for step in range(cps):
        for d in range(2):
            c = step if d == 0 else cps - 1 - step
            ch = by_key[(d, c)]
            inter = ch["bcol"] + m_row[d]
            mt = jnp.maximum(inter, ch["mx"])
            f_in = jnp.exp(ch["mx"] - mt)
            w_int = jnp.exp(inter - mt)
            num = f_in * ch["num0"] + w_int * _mm1(_nt, ch["q"], c_st[d])
            den = f_in * ch["den0"] + w_int * _mm1(_nn, ch["q"] * n_row[d], bdb)
            ys[d][c] = num / jnp.maximum(jnp.abs(den), jnp.exp(-mt))
            m_new = jnp.maximum(ch["blast"] + m_row[d], ch["mlw"])
            sc = jnp.exp(ch["blast"] + m_row[d] - m_new)
            e2 = jnp.exp(ch["mlw"] - m_new)
            c_st[d] = sc * c_st[d] + e2 * ch["cu0"]
            n_row[d] = sc * n_row[d] + e2 * ch["nu0"]
            m_row[d] = m_new
    for d, y_ref in enumerate((yf_ref, yb_ref)):
        c_scr[d] = c_st[d]
        n_scr[d, 0:1, :] = n_row[d]
        m_scr[d, 0:1, :] = m_row[d]
        y_ref[0] = jnp.concatenate(ys[d], axis=0)


def _mlstm_call(z, gbias, *, n_ctx, n_all):
    B, Ta, _ = z.shape
    W = GROUP_W
    cps = _chunks_per_step(n_ctx, n_all)
    L = cps * CHUNK
    n_ctx, n_all = n_ctx // cps, n_all // cps
    fwd = lambda blk: (lambda b, i: (b, i, blk))
    bwd = lambda blk: (lambda b, i: (b, _bwd_chunk(i, n_ctx, n_all), blk))
    out = jax.ShapeDtypeStruct((B, Ta, W), F32)
    return pl.pallas_call(
        functools.partial(_mlstm_kernel, cps=cps),
        out_shape=(out, out),
        grid=(B, n_all),
        in_specs=[pl.BlockSpec((1, L, 2 * W), fwd(Z_MQK // 2)),
                  pl.BlockSpec((1, L, 2 * W), bwd(Z_MQK // 2)),
                  pl.BlockSpec((1, L, 2 * W), fwd(Z_MVO // 2)),
                  pl.BlockSpec((1, L, 2 * W), bwd(Z_MVO // 2)),
                  pl.BlockSpec((1, L, W), fwd(Z_GATE)),
                  pl.BlockSpec((1, L, W), bwd(Z_GATE)),
                  pl.BlockSpec(gbias.shape, lambda b, i: (0, 0))],
        out_specs=(pl.BlockSpec((1, L, W), fwd(0)), pl.BlockSpec((1, L, W), bwd(0))),
        scratch_shapes=[pltpu.VMEM((2, W, W), F32), pltpu.VMEM((2, 8, W), F32), pltpu.VMEM((2, 8, W), F32),
                        pltpu.VMEM((2, L, L), BF16), pltpu.VMEM((2, W, 2 * W), BF16), pltpu.VMEM((W, W), BF16)],
        compiler_params=_cparams("parallel", "arbitrary"),
        name="mlstm",
    )(z, z, z, z, z, z, gbias)


def _head_norm(y, bd, eps):
    bdb = bd.astype(BF16)

    def head_mean(x):
        hi, lo = _split2(x)
        return (_nn(hi, bdb) + _nn(lo, bdb)) * (1.0 / HEAD_DIM)

    yc = y - head_mean(y)
    return yc * lax.rsqrt(head_mean(yc * yc) + eps)


def _outproj_kernel(x_ref, mods_ref, s5f_ref, s5b_ref, u_ref, at_ref, rf_ref, rb_ref, rg_ref, rbon_ref,
                    mf_ref, mb_ref, vo_ref, s5d_ref, wglu_ref, bglu_ref, lnw_ref, lnb_ref, mnw_ref,
                    gpost_ref, wout_ref, o_ref, *, tm, tc, n_batch):
    b = pl.program_id(0)
    t = pl.program_id(1)
    d = x_ref.shape[-1]
    W = GROUP_W
    is_ctx = (t * tm + _iota((tm, 1), 0)) < tc
    gate = jnp.where(is_ctx, mods_ref[pl.ds(n_batch, 1), pl.ds(5 * d, d)],
                     mods_ref[pl.ds(b, 1), pl.ds(5 * d, d)])
    bd = _block_diag_mask()

    y = s5f_ref[...] + s5b_ref[...] + s5d_ref[...] * u_ref[0]
    zg = 0.5 * y * (1.0 + jnp.tanh(math.sqrt(2.0 / math.pi) * (y + 0.044715 * (y * y * y))))
    s5o = zg * _sigmoid(_mm1(_nn, zg, wglu_ref[...]) + bglu_ref[...])

    yr = _head_norm(rf_ref[0] + rb_ref[0], bd, RWKV_GN_EPS)
    rwo = (yr * lnw_ref[...] + lnb_ref[...] + rbon_ref[0]) * rg_ref[0]

    ym = _head_norm(mf_ref[0] + mb_ref[0], bd, NORM_EPS)
    mlo = ym * mnw_ref[...] * _sigmoid(vo_ref[0][:, W:2 * W])

    cat = jnp.concatenate([s5o, at_ref[0], rwo, mlo], axis=1).astype(BF16)
    yx = _nn(cat, wout_ref[...])
    o_ref[0] = x_ref[0] + gate * _rms(yx, gpost_ref[...])


def _outproj_call(xa, mods, s5f, s5b, z, attn, rf, rb, rg, rbon, mf, mb,
                  s5d, wglu, bglu, lnw, lnb, mnw, gpost, wout, *, tc):
    B, Ta, D = xa.shape
    W = GROUP_W
    tm = _pick_tile(Ta, 544)
    kern = functools.partial(_outproj_kernel, tm=tm, tc=tc, n_batch=B)
    full = lambda a: pl.BlockSpec(a.shape, lambda b, t: (0,) * a.ndim)
    tok = pl.BlockSpec((1, tm, W), lambda b, t: (b, t, 0))
    tmaj = pl.BlockSpec((tm, W), lambda b, t: (t, b))
    return pl.pallas_call(
        kern,
        out_shape=jax.ShapeDtypeStruct((B, Ta, D), F32),
        grid=(B, Ta // tm),
        in_specs=[pl.BlockSpec((1, tm, D), lambda b, t: (b, t, 0)), full(mods),
                  tmaj, tmaj, pl.BlockSpec((1, tm, W), lambda b, t: (b, t, Z_S5)),
                  tok, tok, tok, tok, tok, tok, tok,
                  pl.BlockSpec((1, tm, 2 * W), lambda b, t: (b, t, Z_MVO // 2)),
                  full(s5d), full(wglu), full(bglu), full(lnw), full(lnb), full(mnw), full(gpost), full(wout)],
        out_specs=pl.BlockSpec((1, tm, D), lambda b, t: (b, t, 0)),
        compiler_params=_cparams("parallel", "parallel"),
        name="mix_out",
    )(xa, mods, s5f, s5b, z, attn, rf, rb, rg, rbon, mf, mb, z, s5d, wglu, bglu, lnw, lnb, mnw, gpost, wout)


def _rope_rotate_cols(w):
    h = ROPE_AXIS // 2
    return jnp.concatenate([-w[..., h:2 * h], w[..., 0:h], -w[..., 3 * h:4 * h], w[..., 2 * h:3 * h]], axis=-1)


def _inproj_relayout(w_in):
    L, D, _ = w_in.shape
    o_s5, o_mla, o_rw, o_ml = 0, 256, 672, 1568
    seg = lambda a, n: w_in[:, :, a:a + n]
    zer = lambda n: jnp.zeros((L, D, n), w_in.dtype)
    k_rope = seg(o_mla + 384, ROPE_DIM)
    parts = [seg(o_mla, 256), seg(o_mla + 256, 128),
             zer(64), k_rope, zer(32),
             zer(64), _rope_rotate_cols(k_rope), zer(32), zer(128),
             seg(o_rw, 768), seg(o_s5, 256),
             seg(o_rw + 768, 128), zer(128),
             seg(o_ml, 512), seg(o_ml + 512, 512), seg(o_ml + 1024, 16), zer(ZB - 16)]
    out = jnp.concatenate(parts, axis=2)
    assert out.shape[2] == Z_COLS
    return out


def _rope_tables(T, tc):
    rows = T // GRID_W
    r_idx, c_idx = jnp.meshgrid(jnp.arange(rows), jnp.arange(GRID_W), indexing='ij')
    inv_freq = 1.0 / (ROPE_BASE ** (jnp.arange(0, ROPE_AXIS, 2, dtype=F32) / ROPE_AXIS))
    ang_r = r_idx.reshape(-1, 1).astype(F32) * inv_freq
    ang_c = c_idx.reshape(-1, 1).astype(F32) * inv_freq
    ang = jnp.concatenate([ang_r, ang_r, ang_c, ang_c], axis=-1)
    cos = jnp.concatenate([jnp.ones((tc, ROPE_DIM), F32), jnp.cos(ang)], axis=0)
    sin = jnp.concatenate([jnp.zeros((tc, ROPE_DIM), F32), jnp.sin(ang)], axis=0)
    ta = T + tc
    cos_t = jnp.concatenate([jnp.ones((ta, 64), F32), cos, jnp.zeros((ta, 32), F32)], axis=1)
    sin_t = jnp.concatenate([jnp.zeros((ta, 64), F32), sin, jnp.zeros((ta, 32), F32)], axis=1)
    return cos_t, sin_t


def _pad_rows(w, r0, total):
    pad = [(0, 0)] * (w.ndim - 2) + [(r0, total - r0 - w.shape[-2]), (0, 0)]
    return jnp.pad(w, pad)


def kernel(x, c, ctx, c_ctx, w_ada, b_ada, norm_pre, norm_post, ffn_w_gate, ffn_w_up, ffn_w_down, w_in, w_out, s5_lam_re, s5_lam_im, s5_log_dt, s5_b_re, s5_b_im, s5_c_re, s5_c_im, s5_d, s5_w_glu, s5_b_glu, mla_q_norm, mla_kv_norm, mla_w_uq, mla_w_ukv, rwkv_conv_w, rwkv_conv_b, rwkv_w0, rwkv_w_up, rwkv_a0, rwkv_a_up, rwkv_g_up, rwkv_k_k, rwkv_k_a, rwkv_r_k, rwkv_ln_w, rwkv_ln_b, mlstm_conv_w, mlstm_conv_b, mlstm_gate_b, mlstm_norm):
    B, T, D = x.shape
    Tc = ctx.shape[1]
    Ta = T + Tc
    L = w_ada.shape[0]
    W = GROUP_W
    assert T % CHUNK == 0 and Tc % CHUNK == 0 and B % 8 == 0 and B <= 8
    n_ctx, n_all = Tc // CHUNK, Ta // CHUNK

    rows = 16
    cvec = jnp.concatenate([c, c_ctx[None, :], jnp.zeros((rows - B - 1, D), F32)], axis=0)
    mods_all = _ada_call(cvec, w_ada, b_ada)

    w_in_re = _inproj_relayout(w_in).astype(BF16)
    cw = jnp.zeros((L, Z_NBLK, 8, ZB), F32)
    rc = jnp.concatenate([rwkv_conv_w, rwkv_conv_b[:, None, :]], axis=1).reshape(L, 4, 3, ZB).transpose(0, 2, 1, 3)
    mc = jnp.concatenate([mlstm_conv_w, mlstm_conv_b[:, None, :]], axis=1).reshape(L, 4, 2, ZB).transpose(0, 2, 1, 3)
    cw = cw.at[:, Z_RKV:Z_RKV + 3, 0:4].set(rc).at[:, Z_MQK:Z_MQK + 2, 0:4].set(mc)

    wg = ffn_w_gate.astype(BF16)
    wu = ffn_w_up.astype(BF16)
    wd = ffn_w_down.astype(BF16)
    wout = w_out.astype(BF16)

    G = s5_lam_re.shape[2]
    N = s5_lam_re.shape[3]
    eye_g = jnp.eye(G, dtype=F32)
    lre = s5_lam_re.reshape(L, 2, 1, G * N)
    lim = s5_lam_im.reshape(L, 2, 1, G * N)
    ldt = jnp.repeat(s5_log_dt, N, axis=-1).reshape(L, 2, 1, G * N)
    wbre = jnp.einsum('ldgnp,gh->ldgphn', s5_b_re, eye_g).reshape(L, 2, G * S5_P, G * N)
    wbim = jnp.einsum('ldgnp,gh->ldgphn', s5_b_im, eye_g).reshape(L, 2, G * S5_P, G * N)
    wcre = jnp.einsum('ldgpn,gh->ldgnhp', s5_c_re, eye_g).reshape(L, 2, G * N, G * S5_P).astype(BF16)
    wcim = jnp.einsum('ldgpn,gh->ldgnhp', s5_c_im, eye_g).reshape(L, 2, G * N, G * S5_P).astype(BF16)

    nope = HEAD_DIM
    qd = nope + ROPE_DIM
    wq4 = mla_w_uq.reshape(L, -1, N_HEADS, qd)
    wq = jnp.pad(wq4, ((0, 0), (0, 0), (0, 0), (0, 128 - qd))).reshape(L, -1, N_HEADS * 128).astype(BF16)
    wq_rot = _rope_rotate_cols(wq4[..., nope:])
    wqr = jnp.pad(wq_rot, ((0, 0), (0, 0), (0, 0), (nope, 128 - qd))).reshape(L, -1, N_HEADS * 128).astype(BF16)
    wkv4 = mla_w_ukv.reshape(L, -1, N_HEADS, 2 * HEAD_DIM)
    wk = jnp.pad(wkv4[..., :HEAD_DIM], ((0, 0), (0, 0), (0, 0), (0, 64))).reshape(L, -1, N_HEADS * 128).astype(BF16)
    wv = jnp.pad(wkv4[..., HEAD_DIM:], ((0, 0), (0, 0), (0, 0), (0, 64))).reshape(L, -1, N_HEADS * 128).astype(BF16)
    cos_t, sin_t = _rope_tables(T, Tc)
    scale = float(qd) ** -0.5 * math.log2(math.e)

    wup = _pad_rows(rwkv_w_up, 0, W)
    aup = _pad_rows(rwkv_a_up, 32, W)
    gup = _pad_rows(rwkv_g_up, 64, W)
    gbias = jnp.pad(mlstm_gate_b, ((0, 0), (0, W - mlstm_gate_b.shape[1])))

    xa = jnp.concatenate([ctx, x], axis=1)
    r1 = lambda a: a.reshape(1, -1)

    for l in range(L):
        mods = mods_all[l]
        xa2, hmix = _ffn_call(xa.reshape(B * Ta, D), mods, r1(norm_pre[l, 0]), r1(norm_post[l, 0]),
                              r1(norm_pre[l, 1]), wg[l, 0], wu[l, 0], wd[l, 0],
                              ta=Ta, tc=Tc, n_batch=B, koff=0, emit_hmix=True)
        xa = xa2.reshape(B, Ta, D)
        z = _inproj_call(hmix.reshape(B, Ta, D), w_in_re[l], cw[l], tc=Tc)

        s5f, s5b = _s5_call(z, lre[l], lim[l], ldt[l], wbre[l], wbim[l], wcre[l], wcim[l],
                            n_ctx=n_ctx, n_all=n_all)

        q, k, v = _mla_proj_call(z, r1(mla_q_norm[l]), r1(mla_kv_norm[l]), wq[l], wqr[l], wk[l], wv[l],
                                 cos_t, sin_t, scale=scale)
        attn = _attn_call(q, k, v, tc=Tc)

        rf, rb, rg, rbon = _rwkv_call(z, wup[l], aup[l], gup[l], rwkv_w0[l][:, None, :], rwkv_a0[l][:, None, :],
                                      r1(rwkv_k_k[l]), r1(rwkv_k_a[l]), r1(rwkv_r_k[l]), n_ctx=n_ctx, n_all=n_all)
        mf, mb = _mlstm_call(z, gbias[l:l + 1], n_ctx=n_ctx, n_all=n_all)

        xa = _outproj_call(xa, mods, s5f, s5b, z, attn, rf, rb, rg, rbon,
                           mf, mb, r1(s5_d[l]), s5_w_glu[l].astype(BF16), r1(s5_b_glu[l]), r1(rwkv_ln_w[l]),
                           r1(rwkv_ln_b[l]), r1(mlstm_norm[l]), r1(norm_post[l, 1]), wout[l], tc=Tc)

        xa2, _ = _ffn_call(xa.reshape(B * Ta, D), mods, r1(norm_pre[l, 2]), r1(norm_post[l, 2]),
                           r1(norm_pre[l, 1]), wg[l, 1], wu[l, 1], wd[l, 1],
                           ta=Ta, tc=Tc, n_batch=B, koff=6, emit_hmix=False)
        xa = xa2.reshape(B, Ta, D)

    return xa[:, Tc:, :]
```

```python
import functools
import math

import numpy as np
import jax
import jax.numpy as jnp
from jax import lax
from jax.experimental import pallas as pl
from jax.experimental.pallas import tpu as pltpu

F32 = jnp.float32
BF16 = jnp.bfloat16

GROUP_W = 256
HEAD_DIM = 64
N_HEADS = GROUP_W // HEAD_DIM
CHUNK = 64
N_MOD = 9
NORM_EPS = 1e-6
RWKV_GN_EPS = HEAD_DIM * 1e-5
GRID_W = 64
ROPE_BASE = 10000.0
ROPE_DIM = 32
ROPE_AXIS = 16
S5_P = 16
S5_STATE = 64
MACARON = 0.5
VMEM_LIMIT_BYTES = 56 * 1024 * 1024

ZB = 256
Z_MLA, Z_RKV, Z_S5, Z_LORA, Z_MQK, Z_MVO, Z_GATE = 0, 3, 6, 7, 8, 10, 12
Z_NBLK = 13
Z_COLS = Z_NBLK * ZB


def _nn(a, b):
    return lax.dot_general(a, b, (((1,), (0,)), ((), ())), preferred_element_type=F32)


def _nt(a, b):
    return lax.dot_general(a, b, (((1,), (1,)), ((), ())), preferred_element_type=F32)


def _tn(a, b):
    return lax.dot_general(a, b, (((0,), (0,)), ((), ())), preferred_element_type=F32)


def _split2(x):
    hi = x.astype(BF16)
    lo = (x - hi.astype(F32)).astype(BF16)
    return hi, lo


def _split3(x):
    p1 = x.astype(BF16)
    r = x - p1.astype(F32)
    p2 = r.astype(BF16)
    p3 = (r - p2.astype(F32)).astype(BF16)
    return p1, p2, p3


def _mm3(dotf, a, b):
    ah, al = _split2(a)
    bh, bl = _split2(b)
    return dotf(ah, bh) + (dotf(ah, bl) + dotf(al, bh))


def _mm1(dotf, a, b):
    return dotf(a.astype(BF16), b.astype(BF16))


_mmn = _mm1


def _mm_exact_rhs(a, e):
    eb = e.astype(BF16)
    p1, p2, p3 = _split3(a)
    return _nn(p1, eb) + (_nn(p2, eb) + _nn(p3, eb))


def _mm_exact_lhs(e, a):
    eb = e.astype(BF16)
    p1, p2, p3 = _split3(a)
    return _nn(eb, p1) + (_nn(eb, p2) + _nn(eb, p3))


def _rms(x, g):
    return x * lax.rsqrt(jnp.mean(x * x, axis=-1, keepdims=True) + NORM_EPS) * g


def _sigmoid(x):
    return 0.5 * jnp.tanh(0.5 * x) + 0.5


def _group_tri_masks(tri_scr, rows):
    r = _iota((rows, rows), 0)
    c = _iota((rows, rows), 1)
    same = (r // CHUNK) == (c // CHUNK)
    tri_scr[0] = (same & (c <= r)).astype(BF16)
    tri_scr[1] = (same & (c >= r)).astype(BF16)


def _softplus(x):
    return jnp.maximum(x, 0.0) + jnp.log(1.0 + jnp.exp(-jnp.abs(x)))


def _iota(shape, dim):
    return lax.broadcasted_iota(jnp.int32, shape, dim)


def _head_masks():
    lane = _iota((1, GROUP_W), 1)
    return [lane // HEAD_DIM == h for h in range(N_HEADS)]


def _block_diag_mask():
    r = _iota((GROUP_W, GROUP_W), 0) // HEAD_DIM
    c = _iota((GROUP_W, GROUP_W), 1) // HEAD_DIM
    return (r == c).astype(F32)


def _row_stack(x, hms):
    xb = x.astype(BF16)
    return jnp.concatenate([jnp.where(m, xb, 0.0) for m in hms], axis=0)


def _pick_tile(n, target, mult=16):
    best = None
    for t in range(mult, min(n, target) + 1, mult):
        if n % t == 0:
            best = t
    if best is None:
        raise ValueError(f"no tile for {n}")
    return best


def _cparams(*sem):
    return pltpu.CompilerParams(dimension_semantics=sem, vmem_limit_bytes=VMEM_LIMIT_BYTES)


def _ada_kernel(c_ref, w_ref, b_ref, o_ref):
    c = c_ref[...]
    s = c * _sigmoid(c)
    o_ref[0] = _mm3(_nn, s, w_ref[0]) + b_ref[0]


def _ada_call(cvec, w_ada, b_ada):
    L, D, N = w_ada.shape
    R = cvec.shape[0]
    tn = _pick_tile(N, 1152, 128)
    return pl.pallas_call(
        _ada_kernel,
        out_shape=jax.ShapeDtypeStruct((L, R, N), F32),
        grid=(L, N // tn),
        in_specs=[pl.BlockSpec((R, D), lambda l, j: (0, 0)),
                  pl.BlockSpec((1, D, tn), lambda l, j: (l, 0, j)),
                  pl.BlockSpec((1, 1, tn), lambda l, j: (l, 0, j))],
        out_specs=pl.BlockSpec((1, R, tn), lambda l, j: (l, 0, j)),
        compiler_params=_cparams("parallel", "parallel"),
        name="ada_mod",
    )(cvec, w_ada, b_ada.reshape(L, 1, N))


FFN_COLS = 256


def _ffn_kernel(xn_ref, xp_ref, mods_ref, gpre_ref, gpost_ref, gmix_ref, wg_ref, wu_ref, wd_ref,
                o_ref, hmix_ref, h_scr, acc_scr, *, tm, tiles_per_batch, tc, n_batch, koff, emit_hmix,
                n_tiles, f_split):
    i = pl.program_id(0)
    f = pl.program_id(1)
    d = xn_ref.shape[-1]
    fd = wg_ref.shape[1]

    def mod_of(tile):
        b = tile // tiles_per_batch
        t0 = (tile % tiles_per_batch) * tm
        is_ctx = (t0 + _iota((tm, 1), 0)) < tc

        def mod(fn):
            row = lambda r: (lambda k: mods_ref[pl.ds(r, 1), pl.ds(k * d, d)])
            return jnp.where(is_ctx, fn(row(n_batch)), fn(row(b)))
        return mod

    unit = lambda x: x * lax.rsqrt(jnp.mean(x * x, axis=-1, keepdims=True) + NORM_EPS)

    def pre_norm(tile, x):
        mod = mod_of(tile)
        gain = mod(lambda m: gpre_ref[...] * (1.0 + m(koff + 1)))
        return (unit(x) * gain + mod(lambda m: m(koff))).astype(BF16)

    def finish(tile, acc, x):
        mod = mod_of(tile)
        xn = x + unit(acc) * mod(lambda m: (MACARON * m(koff + 2)) * gpost_ref[...])
        o_ref[...] = xn
        if emit_hmix:
            gain = mod(lambda m: gmix_ref[...] * (1.0 + m(4)))
            hmix_ref[...] = (unit(xn) * gain + mod(lambda m: m(3))).astype(BF16)
        else:
            hmix_ref[...] = jnp.zeros_like(hmix_ref)

    def hidden_cols(hb, lo, hi):
        acc = None
        for c0 in range(lo, hi, FFN_COLS):
            g = _nn(hb, wg_ref[:, c0:c0 + FFN_COLS])
            u = _nn(hb, wu_ref[:, c0:c0 + FFN_COLS])
            t = _nn((g * _sigmoid(g) * u).astype(BF16), wd_ref[c0:c0 + FFN_COLS, :])
            acc = t if acc is None else acc + t
        return acc

    @pl.when(jnp.logical_and(i == 0, f == 0))
    def _():
        h_scr[0] = pre_norm(0, xn_ref[...])
        acc_scr[1] = jnp.zeros((tm, d), F32)

    for s in range(2):
        tile = 2 * i + s

        @pl.when(jnp.logical_and(tile < n_tiles, f == 2 * s))
        def _(s=s, tile=tile):
            finish(jnp.maximum(tile - 1, 0), acc_scr[1 - s], xp_ref[...])
            acc_scr[s] = hidden_cols(h_scr[s], 0, f_split)

        @pl.when(jnp.logical_and(tile < n_tiles, f == 2 * s + 1))
        def _(s=s, tile=tile):
            h_scr[1 - s] = pre_norm(jnp.minimum(tile + 1, n_tiles - 1), xn_ref[...])
            acc_scr[s] += hidden_cols(h_scr[s], f_split, fd)

    @pl.when(jnp.logical_and(2 * i == n_tiles, f == 0))
    def _():
        finish(n_tiles - 1, acc_scr[1], xp_ref[...])


def _ffn_call(xa2, mods, gpre, gpost, gmix, wg, wu, wd, *, ta, tc, n_batch, koff, emit_hmix):
    M, D = xa2.shape
    Fd = wg.shape[1]
    assert Fd % FFN_COLS == 0
    tm = _pick_tile(ta, 544)
    n_tiles = M // tm
    assert n_tiles % 2 == 0
    f_split = ((Fd // FFN_COLS + 1) // 2) * FFN_COLS
    kern = functools.partial(_ffn_kernel, tm=tm, tiles_per_batch=ta // tm, tc=tc, n_batch=n_batch,
                             koff=koff, emit_hmix=emit_hmix, n_tiles=n_tiles, f_split=f_split)
    hm_rows = tm if emit_hmix else 16
    last = n_tiles - 1
    resident = lambda a: pl.BlockSpec(a.shape, lambda i, f: (0, 0), pipeline_mode=pl.Buffered(1))
    vec = pl.BlockSpec((1, D), lambda i, f: (0, 0))
    return pl.pallas_call(
        kern,
        out_shape=(jax.ShapeDtypeStruct((M, D), F32),
                   jax.ShapeDtypeStruct((M if emit_hmix else 16 * n_tiles, D), BF16)),
        grid=(n_tiles // 2 + 1, 4),
        in_specs=[pl.BlockSpec((tm, D), lambda i, f: (jnp.minimum(2 * i + (f + 1) // 2, last), 0)),
                  pl.BlockSpec((tm, D), lambda i, f: (jnp.clip(2 * i - 1 + f // 2, 0, last), 0)),
                  pl.BlockSpec(mods.shape, lambda i, f: (0, 0)),
                  vec, vec, vec, resident(wg), resident(wu), resident(wd)],
        out_specs=(pl.BlockSpec((tm, D), lambda i, f: (jnp.clip(2 * i - 1 + f // 2, 0, last), 0)),
                   pl.BlockSpec((hm_rows, D), lambda i, f: (jnp.clip(2 * i - 1 + f // 2, 0, last), 0))),
        scratch_shapes=[pltpu.VMEM((2, tm, D), BF16), pltpu.VMEM((2, tm, D), F32)],
        compiler_params=_cparams("arbitrary", "arbitrary"),
        name="half_ffn",
    )(xa2, xa2, mods, gpre, gpost, gmix, wg, wu, wd)


def _inproj_kernel(h_ref, w_ref, cw_ref, o_ref, *, tc, conv_lo, conv_hi, silu_lo, silu_hi):
    nb = pl.program_id(1)
    z = _nn(h_ref[0], w_ref[...])
    ta = z.shape[0]
    is_conv = ((nb >= conv_lo[0]) & (nb < conv_hi[0])) | ((nb >= conv_lo[1]) & (nb < conv_hi[1]))
    is_silu = (nb >= silu_lo) & (nb < silu_hi)

    @pl.when(jnp.logical_not(is_conv))
    def _():
        o_ref[0] = z

    def conv():
        row = _iota((ta, 1), 0)
        zp = jnp.where((row == 0) | (row == tc), 0.0, pltpu.roll(z, 1, 0))
        zn = jnp.where((row == tc - 1) | (row == ta - 1), 0.0, pltpu.roll(z, ta - 1, 0))
        cw = cw_ref[0]
        return cw[3:4] + zp * cw[0:1] + z * cw[1:2] + zn * cw[2:3]

    @pl.when(is_conv & jnp.logical_not(is_silu))
    def _():
        o_ref[0] = conv()

    @pl.when(is_conv & is_silu)
    def _():
        y = conv()
        o_ref[0] = y * _sigmoid(y)


def _inproj_call(hmix3, w_re, cw, *, tc):
    B, Ta, D = hmix3.shape
    kern = functools.partial(_inproj_kernel, tc=tc, conv_lo=(Z_RKV, Z_MQK), conv_hi=(Z_RKV + 3, Z_MQK + 2),
                             silu_lo=Z_MQK, silu_hi=Z_MQK + 2)
    return pl.pallas_call(
        kern,
        out_shape=jax.ShapeDtypeStruct((B, Ta, Z_COLS), F32),
        grid=(B, Z_NBLK),
        in_specs=[pl.BlockSpec((1, Ta, D), lambda b, n: (b, 0, 0)),
                  pl.BlockSpec((D, ZB), lambda b, n: (0, n)),
                  pl.BlockSpec((1, 8, ZB), lambda b, n: (n, 0, 0))],
        out_specs=pl.BlockSpec((1, Ta, ZB), lambda b, n: (b, 0, n)),
        compiler_params=_cparams("parallel", "arbitrary"),
        name="in_proj",
    )(hmix3, w_re, cw)


def _bwd_chunk(i, n_ctx, n_all):
    return jnp.where(i < n_ctx, n_ctx - 1 - i, n_all - 1 - (i - n_ctx))


def _s5_kernel(uf_ref, ub_ref, lre_ref, lim_ref, ldt_ref, wbre_ref, wbim_ref, wcre_ref, wcim_ref,
               yf_ref, yb_ref, wb_scr, coef_scr, st_scr, rel_scr, x_scr, *, lc, nb, scan_unroll):
    i = pl.program_id(0)
    gn = lre_ref.shape[-1]

    @pl.when(i == 0)
    def _():
        for d in range(2):
            dt = jnp.exp(ldt_ref[d])
            lre = lre_ref[d]
            lim = lim_ref[d]
            mag = jnp.exp(lre * dt)
            ar = mag * jnp.cos(lim * dt)
            ai = mag * jnp.sin(lim * dt)
            den = lre * lre + lim * lim
            fr = ((ar - 1.0) * lre + ai * lim) / den
            fi = (ai * lre - (ar - 1.0) * lim) / den
            coef_scr[d, 0:nb, :] = jnp.broadcast_to(ar, (nb, gn))
            coef_scr[d, nb:2 * nb, :] = jnp.broadcast_to(ai, (nb, gn))
            wre = wbre_ref[d]
            wim = wbim_ref[d]
            wb_scr[d, :, 0:gn] = (wre * fr - wim * fi).astype(BF16)
            wb_scr[d, :, gn:2 * gn] = (wim * fr + wre * fi).astype(BF16)
        st_scr[...] = jnp.zeros_like(st_scr)

    half = 128
    u_refs = (uf_ref, ub_ref)
    y_refs = (yf_ref, yb_ref)

    for d in range(2):
        for b in range(nb):
            for s in range(2):
                rel_scr[d, s, pl.ds(b, lc, stride=nb), :] = u_refs[d][b, :, s * half:(s + 1) * half]
    for d in range(2):
        u_tm = jnp.concatenate([rel_scr[d, 0], rel_scr[d, 1]], axis=1).astype(BF16)
        x_scr[d] = _nn(u_tm, wb_scr[d])

    def body(t, carry):
        out = []
        for d in range(2):
            sr, si = carry[2 * d], carry[2 * d + 1]
            ar = coef_scr[d, 0:nb, :]
            ai = coef_scr[d, nb:2 * nb, :]
            tt = t if d == 0 else lc - 1 - t
            r0 = pl.multiple_of(tt * nb, nb)
            xr = x_scr[d, pl.ds(r0, nb), 0:gn]
            xi = x_scr[d, pl.ds(r0, nb), gn:2 * gn]
            nsr = ar * sr - ai * si + xr
            nsi = ar * si + ai * sr + xi
            x_scr[d, pl.ds(r0, nb), 0:gn] = nsr
            x_scr[d, pl.ds(r0, nb), gn:2 * gn] = nsi
            out += [nsr, nsi]
        return tuple(out)

    init = (st_scr[0, 0:nb, :], st_scr[0, nb:2 * nb, :], st_scr[1, 0:nb, :], st_scr[1, nb:2 * nb, :])
    fin = lax.fori_loop(0, lc, body, init, unroll=scan_unroll)
    for d in range(2):
        st_scr[d, 0:nb, :] = fin[2 * d]
        st_scr[d, nb:2 * nb, :] = fin[2 * d + 1]
        y = (_nn(x_scr[d, :, 0:gn].astype(BF16), wcre_ref[d])
             - _nn(x_scr[d, :, gn:2 * gn].astype(BF16), wcim_ref[d]))
        rel_scr[d, 0] = y[:, 0:half]
        rel_scr[d, 1] = y[:, half:2 * half]
    for d in range(2):
        for b in range(nb):
            for s in range(2):
                c0 = b * 2 * half + s * half
                y_refs[d][:, c0:c0 + half] = rel_scr[d, s, pl.ds(b, lc, stride=nb), :]


def _s5_call(z, lre, lim, ldt, wbre, wbim, wcre, wcim, *, n_ctx, n_all):
    B, Ta, _ = z.shape
    W = GROUP_W
    lc = CHUNK
    gn = lre.shape[-1]
    blk = lc * B
    kern = functools.partial(_s5_kernel, lc=lc, nb=B, scan_unroll=True)
    full = lambda a: pl.BlockSpec(a.shape, lambda i: (0,) * a.ndim)
    out = jax.ShapeDtypeStruct((Ta, B * W), F32)
    return pl.pallas_call(
        kern,
        out_shape=(out, out),
        grid=(n_all,),
        in_specs=[pl.BlockSpec((B, lc, W), lambda i: (0, i, Z_S5)),
                  pl.BlockSpec((B, lc, W), lambda i: (0, _bwd_chunk(i, n_ctx, n_all), Z_S5)),
                  full(lre), full(lim), full(ldt), full(wbre), full(wbim), full(wcre), full(wcim)],
        out_specs=(pl.BlockSpec((lc, B * W), lambda i: (i, 0)),
                   pl.BlockSpec((lc, B * W), lambda i: (_bwd_chunk(i, n_ctx, n_all), 0))),
        scratch_shapes=[pltpu.VMEM((2, W, 2 * gn), BF16),
                        pltpu.VMEM((2, 2 * B, gn), F32),
                        pltpu.VMEM((2, 2 * B, gn), F32),
                        pltpu.VMEM((2, 2, blk, 128), F32),
                        pltpu.VMEM((2, blk, 2 * gn), F32)],
        compiler_params=_cparams("arbitrary"),
        name="s5_scan",
    )(z, z, lre, lim, ldt, wbre, wbim, wcre, wcim)


def _mla_proj_kernel(z_ref, qn_ref, kvn_ref, wq_ref, wqr_ref, wk_ref, wv_ref, cos_ref, sin_ref,
                     q_ref, k_ref, v_ref, *, scale):
    z = z_ref[0]
    cq = z[:, 0:256]
    ckv = z[:, 256:384]
    kr = z[:, 384:512]
    krr = z[:, 512:640]
    cos = cos_ref[...]
    sin = sin_ref[...]
    cqb = _rms(cq, qn_ref[...]).astype(BF16)
    ckvb = _rms(ckv, kvn_ref[...]).astype(BF16)
    q = _nn(cqb, wq_ref[...])
    qr = _nn(cqb, wqr_ref[...])
    kn = _nn(ckvb, wk_ref[...])
    krp = kr * cos + krr * sin
    for h in range(N_HEADS):
        sl = slice(h * 128, (h + 1) * 128)
        q_ref[0, h] = ((q[:, sl] * cos + qr[:, sl] * sin) * scale).astype(BF16)
        k_ref[0, h] = (kn[:, sl] + krp).astype(BF16)
    ones_pad = ((_iota((1, N_HEADS * 128), 1) % 128) >= HEAD_DIM).astype(F32)
    vv = _nn(ckvb, wv_ref[...]) + ones_pad
    for h in range(N_HEADS):
        v_ref[0, h] = vv[:, h * 128:(h + 1) * 128].astype(BF16)


def _mla_proj_call(z, qn, kvn, wq, wqr, wk, wv, cos_t, sin_t, *, scale):
    B, Ta, _ = z.shape
    tm = _pick_tile(Ta, 544)
    kern = functools.partial(_mla_proj_kernel, scale=scale)
    full = lambda a: pl.BlockSpec(a.shape, lambda b, t: (0,) * a.ndim)
    return pl.pallas_call(
        kern,
        out_shape=(jax.ShapeDtypeStruct((B, N_HEADS, Ta, 128), BF16),
                   jax.ShapeDtypeStruct((B, N_HEADS, Ta, 128), BF16),
                   jax.ShapeDtypeStruct((B, N_HEADS, Ta, 128), BF16)),
        grid=(B, Ta // tm),
        in_specs=[pl.BlockSpec((1, tm, 3 * ZB), lambda b, t: (b, t, 0)),
                  full(qn), full(kvn), full(wq), full(wqr), full(wk), full(wv),
                  pl.BlockSpec((tm, 128), lambda b, t: (t, 0)),
                  pl.BlockSpec((tm, 128), lambda b, t: (t, 0))],
        out_specs=(pl.BlockSpec((1, N_HEADS, tm, 128), lambda b, t: (b, 0, t, 0)),
                   pl.BlockSpec((1, N_HEADS, tm, 128), lambda b, t: (b, 0, t, 0)),
                   pl.BlockSpec((1, N_HEADS, tm, 128), lambda b, t: (b, 0, t, 0))),
        compiler_params=_cparams("parallel", "parallel"),
        name="mla_proj",
    )(z, qn, kvn, wq, wqr, wk, wv, cos_t, sin_t)


def _attn_kernel(q_ref, k_ref, v_ref, o_ref, *, tc, ta, n_ctx_tiles, key_chunk):
    i = pl.program_id(2)
    lane = _iota((1, 128), 1)

    def write_out(acc):
        outs = [a * (1.0 / pltpu.roll(a, HEAD_DIM, 1)) for a in acc]
        o_ref[0] = jnp.where(lane < HEAD_DIM, outs[0], pltpu.roll(outs[1], HEAD_DIM, 1))

    @pl.when(i < n_ctx_tiles)
    def _():
        acc = []
        for j in range(2):
            s = _nt(q_ref[0, j], k_ref[0, j, 0:tc, :])
            p = jnp.exp2(s - jnp.max(s, axis=-1, keepdims=True))
            acc.append(_nn(p.astype(BF16), v_ref[0, j, 0:tc, :]))
        write_out(acc)

    @pl.when(i >= n_ctx_tiles)
    def _():
        bounds = [0, tc] + list(range(tc + key_chunk, ta + 1, key_chunk))
        units = [(c, j) for c in range(len(bounds) - 1) for j in range(2)]
        qs = [q_ref[0, 0], q_ref[0, 1]]
        score = lambda c, j: _nt(qs[j], k_ref[0, j, bounds[c]:bounds[c + 1], :])
        m = [None, None]
        acc = [None, None]

        def weighted_values(pend):
            c, j, pb, alpha = pend
            pv = _nn(pb, v_ref[0, j, bounds[c]:bounds[c + 1], :])
            acc[j] = pv if alpha is None else alpha * acc[j] + pv

        pending = None
        s_next = score(*units[0])
        for idx, (c, j) in enumerate(units):
            s = s_next
            if idx + 1 < len(units):
                s_next = score(*units[idx + 1])
            mc = jnp.max(s, axis=-1, keepdims=True)
            if c == 0:
                alpha = None
                m[j] = mc
                p = jnp.exp2(s - mc)
            else:
                m_new = jnp.maximum(m[j], mc)
                alpha = jnp.exp2(m[j] - m_new)
                p = jnp.exp2(s - m_new)
                m[j] = m_new
            if pending is not None:
                weighted_values(pending)
            pending = (c, j, p.astype(BF16), alpha)
        weighted_values(pending)
        write_out(acc)


def _attn_call(q, k, v, *, tc):
    B, H, Ta, _ = q.shape
    tq = _pick_tile(math.gcd(tc, Ta), 256)
    kern = functools.partial(_attn_kernel, tc=tc, ta=Ta, n_ctx_tiles=tc // tq,
                             key_chunk=_pick_tile(Ta - tc, 1024, 128))
    return pl.pallas_call(
        kern,
        out_shape=jax.ShapeDtypeStruct((B, Ta, GROUP_W), F32),
        grid=(B, H // 2, Ta // tq),
        in_specs=[pl.BlockSpec((1, 2, tq, 128), lambda b, h, i: (b, h, i, 0)),
                  pl.BlockSpec((1, 2, Ta, 128), lambda b, h, i: (b, h, 0, 0)),
                  pl.BlockSpec((1, 2, Ta, 128), lambda b, h, i: (b, h, 0, 0))],
        out_specs=pl.BlockSpec((1, tq, 128), lambda b, h, i: (b, i, h)),
        compiler_params=_cparams("parallel", "parallel", "arbitrary"),
        name="mla_attn",
    )(q, k, v)


def _chunk_masks(d):
    L = CHUNK
    row = _iota((L, GROUP_W), 0)
    s_idx = _iota((L, GROUP_W), 1) % L
    tr = _iota((L, L), 0)
    tcol = _iota((L, L), 1)
    if d == 0:
        return (tcol <= tr).astype(F32), s_idx < row, s_idx <= row, s_idx == row
    return (tcol >= tr).astype(F32), s_idx > row, s_idx >= row, s_idx == row


def _run_stages(*generators):
    live = list(generators)
    while live:
        for g in list(live):
            try:
                next(g)
            except StopIteration:
                live.remove(g)


def _rwkv_kernel(*refs, cps):
    _run_stages(_rwkv_stages(*refs, cps=cps))


def _rwkv_stages(rf_ref, rb_ref, lf_ref, lb_ref, wup_ref, aup_ref, gup_ref, w0_ref, a0_ref,
                 kk_ref, ka_ref, rk_ref, yf_ref, yb_ref, g_ref, bon_ref, s_scr, tri_scr, bdb_scr, *, cps):
    i = pl.program_id(1)
    L = CHUNK
    W = GROUP_W

    @pl.when(i == 0)
    def _():
        s_scr[...] = jnp.zeros_like(s_scr)
        _group_tri_masks(tri_scr, cps * L)
        bdb_scr[...] = _block_diag_mask().astype(BF16)

    hms = _head_masks()
    bd = _block_diag_mask()
    bdb = bdb_scr[...]
    rs = lambda x: _row_stack(x, hms)
    bd_b = bd > 0.5
    bdiag = lambda x: jnp.where(bd_b, jnp.concatenate([x.astype(BF16)] * N_HEADS, axis=0), 0.0)

    per_dir = []
    for d, (r_ref, l_ref) in enumerate(((rf_ref, lf_ref), (rb_ref, lb_ref))):
        rkv = r_ref[0]
        lora = l_ref[0]
        r = rkv[:, 0:W]
        k = rkv[:, W:2 * W]
        v = rkv[:, 2 * W:3 * W]
        _, strict, incl, eye = _chunk_masks(d)
        tri = tri_scr[d]

        lw = -math.exp(-0.5) * _sigmoid(w0_ref[d] + _mm1(_nn, jnp.tanh(lora), wup_ref[d]))
        a = _sigmoid(a0_ref[d] + _mm1(_nn, lora, aup_ref[d]))
        kkv = k * kk_ref[...]
        kkn = kkv * lax.rsqrt(jnp.maximum(_mm1(_nn, kkv * kkv, bdb), 1e-24))
        keff = k * (1.0 + (a - 1.0) * ka_ref[...])
        kka = kkn * a
        lw_hi, lw_lo = _split2(lw)
        cum2 = _nn(tri, jnp.concatenate([lw_hi, lw_lo], axis=1))
        cum = cum2[:, 0:W] + cum2[:, W:2 * W]
        e_dn = jnp.exp(-cum)
        per_dir.append(dict(v=v, lw=lw, cum=cum, kka=kka, keff=keff, strict=strict, incl=incl, eye=eye,
                            al=-kkn * jnp.exp(cum - lw), rt=r * jnp.exp(cum), bh=kka * e_dn, kh=keff * e_dn))
        if d == 0:
            g_ref[0] = _mm1(_nn, _sigmoid(lora), gup_ref[...])
            bon_ref[0] = _mm1(_nn, r * k * rk_ref[...], bdb) * v
        yield

    chains = []
    for c in range(cps):
        for d in range(2):
            pd = per_dir[d]
            sl = slice(c * L, (c + 1) * L)
            tot = jnp.sum(pd["lw"][sl], axis=0, keepdims=True)
            e_tc = jnp.exp(tot - pd["cum"][sl])
            ch = dict(d=d, c=c, al=pd["al"][sl], rt=pd["rt"][sl], v=pd["v"][sl], tot=tot,
                      bt=pd["kka"][sl] * e_tc, kt=pd["keff"][sl] * e_tc)
            ch["rsv"] = rs(ch["v"])
            a_all = _mm1(_nt, jnp.concatenate([ch["al"], ch["rt"]], axis=0),
                         jnp.concatenate([rs(pd["bh"][sl]), rs(pd["kh"][sl])], axis=0))
            ch["a_ab"] = jnp.where(pd["strict"], a_all[0:L, 0:W], 0.0)
            ch["a_ak"] = jnp.where(pd["strict"], a_all[0:L, W:2 * W], 0.0)
            ch["a_rb"] = jnp.where(pd["incl"], a_all[L:2 * L, 0:W], 0.0)
            ch["a_rk"] = jnp.where(pd["incl"], a_all[L:2 * L, W:2 * W], 0.0)
            ch["p"] = jnp.where(pd["eye"], 1.0, 0.0) + ch["a_ab"]
            chains.append(ch)
        yield

    for ch in chains:
        ch["sq"] = _mmn(_nn, ch["a_ab"], bdiag(ch["a_ab"]))
        ch["zk"] = _mm1(_nn, ch["a_ak"], ch["rsv"])
        ch["y0k"] = _mm1(_nn, ch["a_rk"], ch["rsv"])
    yield
    n_sq = int(math.log2(L)) - 1
    for it in range(n_sq):
        for ch in chains:
            if it < n_sq - 1:
                ps = _mmn(_nn, jnp.concatenate([ch["p"], ch["sq"]], axis=0), bdiag(ch["sq"]))
                ch["p"] = ch["p"] + ps[0:L]
                ch["sq"] = ps[L:2 * L]
            else:
                ch["p"] = ch["p"] + _mmn(_nn, ch["p"], bdiag(ch["sq"]))
        yield
    for ch in chains:
        pu = _mm1(_nn, ch["p"], jnp.concatenate([rs(ch["al"]), rs(ch["zk"])], axis=1))
        ch["w"], ch["uk"] = pu[:, 0:W], pu[:, W:2 * W]
    yield
    for ch in chains:
        gy = _mm1(_nn, ch["a_rb"], jnp.concatenate([rs(ch["w"]), rs(ch["uk"])], axis=1))
        ch["g"] = ch["rt"] + gy[:, 0:W]
        ch["y0"] = gy[:, W:2 * W] + ch["y0k"]
        ch["decay"] = jnp.exp(ch["tot"])
        ch["pm"] = bd * _mm1(_tn, ch["w"], ch["bt"])
        ch["q0"] = bd * _mm1(_tn, jnp.concatenate([ch["uk"], ch["v"]], axis=0),
                             jnp.concatenate([ch["bt"], ch["kt"]], axis=0))
    yield

    by_key = {(ch["d"], ch["c"]): ch for ch in chains}
    st = [s_scr[0], s_scr[1]]
    ys = [[None] * cps, [None] * cps]
    for step in range(cps):
        for d in range(2):
            c = step if d == 0 else cps - 1 - step
            ch = by_key[(d, c)]
            ys[d][c] = _mm1(_nt, ch["g"], st[d]) + ch["y0"]
            st[d] = st[d] * ch["decay"] + (_mm1(_nn, st[d], ch["pm"]) + ch["q0"])
        yield
    for d, y_ref in enumerate((yf_ref, yb_ref)):
        s_scr[d] = st[d]
        y_ref[0] = jnp.concatenate(ys[d], axis=0)


def _chunks_per_step(n_ctx, n_all):
    for cps in (4, 2, 1):
        if n_ctx % cps == 0 and (n_all - n_ctx) % cps == 0:
            return cps


def _rwkv_call(z, wup, aup, gup, w0, a0, kk, ka, rk, *, n_ctx, n_all):
    B, Ta, _ = z.shape
    W = GROUP_W
    cps = _chunks_per_step(n_ctx, n_all)
    rows = cps * CHUNK
    gc, ga = n_ctx // cps, n_all // cps
    full = lambda a: pl.BlockSpec(a.shape, lambda b, i: (0,) * a.ndim)
    fwd = lambda blk: (lambda b, i: (b, i, blk))
    bwd = lambda blk: (lambda b, i: (b, _bwd_chunk(i, gc, ga), blk))
    out = jax.ShapeDtypeStruct((B, Ta, W), F32)
    return pl.pallas_call(
        functools.partial(_rwkv_kernel, cps=cps),
        out_shape=(out, out, out, out),
        grid=(B, ga),
        in_specs=[pl.BlockSpec((1, rows, 3 * W), fwd(Z_RKV // 3)),
                  pl.BlockSpec((1, rows, 3 * W), bwd(Z_RKV // 3)),
                  pl.BlockSpec((1, rows, W), fwd(Z_LORA)),
                  pl.BlockSpec((1, rows, W), bwd(Z_LORA)),
                  full(wup), full(aup), full(gup), full(w0), full(a0), full(kk), full(ka), full(rk)],
        out_specs=(pl.BlockSpec((1, rows, W), fwd(0)), pl.BlockSpec((1, rows, W), bwd(0)),
                   pl.BlockSpec((1, rows, W), fwd(0)), pl.BlockSpec((1, rows, W), fwd(0))),
        scratch_shapes=[pltpu.VMEM((2, W, W), F32), pltpu.VMEM((2, rows, rows), BF16), pltpu.VMEM((W, W), BF16)],
        compiler_params=_cparams("parallel", "arbitrary"),
        name="rwkv7",
    )(z, z, z, z, wup, aup, gup, w0, a0, kk, ka, rk)


def _mlstm_kernel(*refs, cps):
    _run_stages(_mlstm_stages(*refs, cps=cps))


def _mlstm_stages(qf_ref, qb_ref, vf_ref, vb_ref, gf_ref, gb_ref, gbias_ref,
                  yf_ref, yb_ref, c_scr, n_scr, m_scr, tri_scr, exp_scr, bdb_scr, *, cps):
    i = pl.program_id(1)
    L = CHUNK
    W = GROUP_W

    @pl.when(i == 0)
    def _():
        c_scr[...] = jnp.zeros_like(c_scr)
        n_scr[...] = jnp.zeros_like(n_scr)
        m_scr[...] = jnp.zeros_like(m_scr)
        _group_tri_masks(tri_scr, cps * L)
        bdb_scr[...] = _block_diag_mask().astype(BF16)
        ci = _iota((W, W), 0)
        cj = _iota((W, W), 1) // HEAD_DIM
        for d in range(2):
            exp_scr[d, :, 0:W] = (ci == d * 2 * N_HEADS + cj).astype(BF16)
            exp_scr[d, :, W:2 * W] = (ci == d * 2 * N_HEADS + N_HEADS + cj).astype(BF16)

    hms = _head_masks()
    bd = _block_diag_mask()
    bdb = bdb_scr[...]
    rs = lambda x: _row_stack(x, hms)
    neg_inf = -jnp.inf

    def sum2(x, eb):
        hi, lo = _split2(x)
        return _nn(hi, eb) + _nn(lo, eb)

    per_dir = []
    for d, (q_ref, v_ref, g_ref) in enumerate(((qf_ref, vf_ref, gf_ref), (qb_ref, vb_ref, gb_ref))):
        qk = q_ref[0]
        g = g_ref[0] + gbias_ref[...]
        _, strict, incl, eye = _chunk_masks(d)
        tri = tri_scr[d]
        gates = sum2(g, exp_scr[d])
        li = gates[:, 0:W]
        lf = -_softplus(-gates[:, W:2 * W])
        lf_hi, lf_lo = _split2(lf)
        b2 = _nn(tri, jnp.concatenate([lf_hi, lf_lo], axis=1))
        per_dir.append(dict(q=qk[:, 0:W], k=qk[:, W:2 * W] * (HEAD_DIM ** -0.5), v=v_ref[0][:, 0:W],
                            li=li, bcol=b2[:, 0:W] + b2[:, W:2 * W], incl=incl, eye=eye))
        yield

    chains = []
    for c in range(cps):
        for d in range(2):
            pd = per_dir[d]
            sl = slice(c * L, (c + 1) * L)
            q, k, v, li, bcol = pd["q"][sl], pd["k"][sl], pd["v"][sl], pd["li"][sl], pd["bcol"][sl]
            brow = jnp.sum(jnp.where(pd["eye"], bcol, 0.0), axis=0, keepdims=True)
            lirow = jnp.sum(jnp.where(pd["eye"], li, 0.0), axis=0, keepdims=True)
            logd = jnp.where(pd["incl"], bcol - brow + lirow, neg_inf)
            mx = jnp.zeros((L, W), F32)
            for hm in hms:
                mh = jnp.max(jnp.where(hm, logd, neg_inf), axis=1, keepdims=True)
                mx = jnp.where(hm, mh, mx)
            blast = bcol[L - 1:L, :] if d == 0 else bcol[0:1, :]
            lwc = blast - bcol + li
            mlw = jnp.max(lwc, axis=0, keepdims=True)
            kw = k * jnp.exp(lwc - mlw)
            chains.append(dict(d=d, c=c, q=q, v=v, bcol=bcol, mx=mx, blast=blast, mlw=mlw, kw=kw,
                               dexp=jnp.exp(logd - mx), rsk=rs(k), rsv=rs(v),
                               nu0=jnp.sum(kw, axis=0, keepdims=True)))
        yield
    for ch in chains:
        ch["sp"] = _mm1(_nt, ch["q"], ch["rsk"]) * ch["dexp"]
        ch["cu0"] = bd * _mm1(_tn, ch["v"], ch["kw"])
    yield
    for ch in chains:
        ch["num0"] = _mm1(_nn, ch["sp"], ch["rsv"])
        ch["den0"] = _mm1(_nn, ch["sp"], bdb)
    yield

    by_key = {(ch["d"], ch["c"]): ch for ch in chains}
    c_st = [c_scr[0], c_scr[1]]
    n_row = [n_scr[0, 0:1, :], n_scr[1, 0:1, :]]
    m_row = [m_scr[0, 0:1, :], m_scr[1, 0:1, :]]
    ys = [[None] * cps, [None] * cps]
    for step in range(cps):
        for d in range(2):
            c = step if d == 0 else cps - 1 - step
            ch = by_key[(d, c)]
            inter = ch["bcol"] + m_row[d]
            mt = jnp.maximum(inter, ch["mx"])
            f_in = jnp.exp(ch["mx"] - mt)
            w_int = jnp.exp(inter - mt)
            num = f_in * ch["num0"] + w_int * _mm1(_nt, ch["q"], c_st[d])
            den = f_in * ch["den0"] + w_int * _mm1(_nn, ch["q"] * n_row[d], bdb)
            ys[d][c] = num / jnp.maximum(jnp.abs(den), jnp.exp(-mt))
            m_new = jnp.maximum(ch["blast"] + m_row[d], ch["mlw"])
            sc = jnp.exp(ch["blast"] + m_row[d] - m_new)
            e2 = jnp.exp(ch["mlw"] - m_new)
            c_st[d] = sc * c_st[d] + e2 * ch["cu0"]
            n_row[d] = sc * n_row[d] + e2 * ch["nu0"]
            m_row[d] = m_new
        yield
    for d, y_ref in enumerate((yf_ref, yb_ref)):
        c_scr[d] = c_st[d]
        n_scr[d, 0:1, :] = n_row[d]
        m_scr[d, 0:1, :] = m_row[d]
        y_ref[0] = jnp.concatenate(ys[d], axis=0)


def _mlstm_call(z, gbias, *, n_ctx, n_all):
    B, Ta, _ = z.shape
    W = GROUP_W
    cps = _chunks_per_step(n_ctx, n_all)
    L = cps * CHUNK
    n_ctx, n_all = n_ctx // cps, n_all // cps
    fwd = lambda blk: (lambda b, i: (b, i, blk))
    bwd = lambda blk: (lambda b, i: (b, _bwd_chunk(i, n_ctx, n_all), blk))
    out = jax.ShapeDtypeStruct((B, Ta, W), F32)
    return pl.pallas_call(
        functools.partial(_mlstm_kernel, cps=cps),
        out_shape=(out, out),
        grid=(B, n_all),
        in_specs=[pl.BlockSpec((1, L, 2 * W), fwd(Z_MQK // 2)),
                  pl.BlockSpec((1, L, 2 * W), bwd(Z_MQK // 2)),
                  pl.BlockSpec((1, L, 2 * W), fwd(Z_MVO // 2)),
                  pl.BlockSpec((1, L, 2 * W), bwd(Z_MVO // 2)),
                  pl.BlockSpec((1, L, W), fwd(Z_GATE)),
                  pl.BlockSpec((1, L, W), bwd(Z_GATE)),
                  pl.BlockSpec(gbias.shape, lambda b, i: (0, 0))],
        out_specs=(pl.BlockSpec((1, L, W), fwd(0)), pl.BlockSpec((1, L, W), bwd(0))),
        scratch_shapes=[pltpu.VMEM((2, W, W), F32), pltpu.VMEM((2, 8, W), F32), pltpu.VMEM((2, 8, W), F32),
                        pltpu.VMEM((2, L, L), BF16), pltpu.VMEM((2, W, 2 * W), BF16), pltpu.VMEM((W, W), BF16)],
        compiler_params=_cparams("parallel", "arbitrary"),
        name="mlstm",
    )(z, z, z, z, z, z, gbias)


def _recur_kernel(*refs, cps):
    r_in, m_in = refs[0:12], refs[12:19]
    r_out, m_out = refs[19:23], refs[23:25]
    r_scr, m_scr = refs[25:28], refs[28:34]
    _run_stages(_rwkv_stages(*r_in, *r_out, *r_scr, cps=cps),
                _mlstm_stages(*m_in, *m_out, *m_scr, cps=cps))


def _recur_call(z, wup, aup, gup, w0, a0, kk, ka, rk, gbias, *, n_ctx, n_all):
    B, Ta, _ = z.shape
    W = GROUP_W
    cps = _chunks_per_step(n_ctx, n_all)
    rows = cps * CHUNK
    gc, ga = n_ctx // cps, n_all // cps
    full = lambda a: pl.BlockSpec(a.shape, lambda b, i: (0,) * a.ndim)
    fwd = lambda blk: (lambda b, i: (b, i, blk))
    bwd = lambda blk: (lambda b, i: (b, _bwd_chunk(i, gc, ga), blk))
    tok = lambda width, imap: pl.BlockSpec((1, rows, width), imap)
    out = jax.ShapeDtypeStruct((B, Ta, W), F32)
    return pl.pallas_call(
        functools.partial(_recur_kernel, cps=cps),
        out_shape=(out,) * 6,
        grid=(B, ga),
        in_specs=[tok(3 * W, fwd(Z_RKV // 3)), tok(3 * W, bwd(Z_RKV // 3)),
                  tok(W, fwd(Z_LORA)), tok(W, bwd(Z_LORA)),
                  full(wup), full(aup), full(gup), full(w0), full(a0), full(kk), full(ka), full(rk),
                  tok(2 * W, fwd(Z_MQK // 2)), tok(2 * W, bwd(Z_MQK // 2)),
                  tok(2 * W, fwd(Z_MVO // 2)), tok(2 * W, bwd(Z_MVO // 2)),
                  tok(W, fwd(Z_GATE)), tok(W, bwd(Z_GATE)), full(gbias)],
        out_specs=(tok(W, fwd(0)), tok(W, bwd(0)), tok(W, fwd(0)), tok(W, fwd(0)),
                   tok(W, fwd(0)), tok(W, bwd(0))),
        scratch_shapes=[pltpu.VMEM((2, W, W), F32), pltpu.VMEM((2, rows, rows), BF16), pltpu.VMEM((W, W), BF16),
                        pltpu.VMEM((2, W, W), F32), pltpu.VMEM((2, 8, W), F32), pltpu.VMEM((2, 8, W), F32),
                        pltpu.VMEM((2, rows, rows), BF16), pltpu.VMEM((2, W, 2 * W), BF16),
                        pltpu.VMEM((W, W), BF16)],
        compiler_params=_cparams("parallel", "arbitrary"),
        name="rwkv_mlstm",
    )(z, z, z, z, wup, aup, gup, w0, a0, kk, ka, rk, z, z, z, z, z, z, gbias)


def _head_norm(y, bd, eps):
    bdb = bd.astype(BF16)

    def head_mean(x):
        hi, lo = _split2(x)
        return (_nn(hi, bdb) + _nn(lo, bdb)) * (1.0 / HEAD_DIM)

    yc = y - head_mean(y)
    return yc * lax.rsqrt(head_mean(yc * yc) + eps)


def _outproj_kernel(x_ref, mods_ref, s5f_ref, s5b_ref, u_ref, at_ref, rf_ref, rb_ref, rg_ref, rbon_ref,
                    mf_ref, mb_ref, vo_ref, s5d_ref, wglu_ref, bglu_ref, lnw_ref, lnb_ref, mnw_ref,
                    gpost_ref, wout_ref, o_ref, *, tm, tc, n_batch):
    b = pl.program_id(0)
    t = pl.program_id(1)
    d = x_ref.shape[-1]
    W = GROUP_W
    is_ctx = (t * tm + _iota((tm, 1), 0)) < tc
    gate = jnp.where(is_ctx, mods_ref[pl.ds(n_batch, 1), pl.ds(5 * d, d)],
                     mods_ref[pl.ds(b, 1), pl.ds(5 * d, d)])
    bd = _block_diag_mask()

    y = s5f_ref[...] + s5b_ref[...] + s5d_ref[...] * u_ref[0]
    zg = 0.5 * y * (1.0 + jnp.tanh(math.sqrt(2.0 / math.pi) * (y + 0.044715 * (y * y * y))))
    s5o = zg * _sigmoid(_mm1(_nn, zg, wglu_ref[...]) + bglu_ref[...])

    yr = _head_norm(rf_ref[0] + rb_ref[0], bd, RWKV_GN_EPS)
    rwo = (yr * lnw_ref[...] + lnb_ref[...] + rbon_ref[0]) * rg_ref[0]

    ym = _head_norm(mf_ref[0] + mb_ref[0], bd, NORM_EPS)
    mlo = ym * mnw_ref[...] * _sigmoid(vo_ref[0][:, W:2 * W])

    cat = jnp.concatenate([s5o, at_ref[0], rwo, mlo], axis=1).astype(BF16)
    yx = _nn(cat, wout_ref[...])
    o_ref[0] = x_ref[0] + gate * _rms(yx, gpost_ref[...])


def _outproj_call(xa, mods, s5f, s5b, z, attn, rf, rb, rg, rbon, mf, mb,
                  s5d, wglu, bglu, lnw, lnb, mnw, gpost, wout, *, tc):
    B, Ta, D = xa.shape
    W = GROUP_W
    tm = _pick_tile(Ta, 544)
    kern = functools.partial(_outproj_kernel, tm=tm, tc=tc, n_batch=B)
    full = lambda a: pl.BlockSpec(a.shape, lambda b, t: (0,) * a.ndim)
    tok = pl.BlockSpec((1, tm, W), lambda b, t: (b, t, 0))
    tmaj = pl.BlockSpec((tm, W), lambda b, t: (t, b))
    return pl.pallas_call(
        kern,
        out_shape=jax.ShapeDtypeStruct((B, Ta, D), F32),
        grid=(B, Ta // tm),
        in_specs=[pl.BlockSpec((1, tm, D), lambda b, t: (b, t, 0)), full(mods),
                  tmaj, tmaj, pl.BlockSpec((1, tm, W), lambda b, t: (b, t, Z_S5)),
                  tok, tok, tok, tok, tok, tok, tok,
                  pl.BlockSpec((1, tm, 2 * W), lambda b, t: (b, t, Z_MVO // 2)),
                  full(s5d), full(wglu), full(bglu), full(lnw), full(lnb), full(mnw), full(gpost), full(wout)],
        out_specs=pl.BlockSpec((1, tm, D), lambda b, t: (b, t, 0)),
        compiler_params=_cparams("parallel", "parallel"),
        name="mix_out",
    )(xa, mods, s5f, s5b, z, attn, rf, rb, rg, rbon, mf, mb, z, s5d, wglu, bglu, lnw, lnb, mnw, gpost, wout)


def _rope_rotate_cols(w):
    h = ROPE_AXIS // 2
    return jnp.concatenate([-w[..., h:2 * h], w[..., 0:h], -w[..., 3 * h:4 * h], w[..., 2 * h:3 * h]], axis=-1)


def _inproj_relayout(w_in):
    L, D, _ = w_in.shape
    o_s5, o_mla, o_rw, o_ml = 0, 256, 672, 1568
    seg = lambda a, n: w_in[:, :, a:a + n]
    zer = lambda n: jnp.zeros((L, D, n), w_in.dtype)
    k_rope = seg(o_mla + 384, ROPE_DIM)
    parts = [seg(o_mla, 256), seg(o_mla + 256, 128),
             zer(64), k_rope, zer(32),
             zer(64), _rope_rotate_cols(k_rope), zer(32), zer(128),
             seg(o_rw, 768), seg(o_s5, 256),
             seg(o_rw + 768, 128), zer(128),
             seg(o_ml, 512), seg(o_ml + 512, 512), seg(o_ml + 1024, 16), zer(ZB - 16)]
    out = jnp.concatenate(parts, axis=2)
    assert out.shape[2] == Z_COLS
    return out


def _rope_tables(T, tc):
    rows = T // GRID_W
    r_idx, c_idx = jnp.meshgrid(jnp.arange(rows), jnp.arange(GRID_W), indexing='ij')
    inv_freq = 1.0 / (ROPE_BASE ** (jnp.arange(0, ROPE_AXIS, 2, dtype=F32) / ROPE_AXIS))
    ang_r = r_idx.reshape(-1, 1).astype(F32) * inv_freq
    ang_c = c_idx.reshape(-1, 1).astype(F32) * inv_freq
    ang = jnp.concatenate([ang_r, ang_r, ang_c, ang_c], axis=-1)
    cos = jnp.concatenate([jnp.ones((tc, ROPE_DIM), F32), jnp.cos(ang)], axis=0)
    sin = jnp.concatenate([jnp.zeros((tc, ROPE_DIM), F32), jnp.sin(ang)], axis=0)
    ta = T + tc
    cos_t = jnp.concatenate([jnp.ones((ta, 64), F32), cos, jnp.zeros((ta, 32), F32)], axis=1)
    sin_t = jnp.concatenate([jnp.zeros((ta, 64), F32), sin, jnp.zeros((ta, 32), F32)], axis=1)
    return cos_t, sin_t


def _pad_rows(w, r0, total):
    pad = [(0, 0)] * (w.ndim - 2) + [(r0, total - r0 - w.shape[-2]), (0, 0)]
    return jnp.pad(w, pad)


def kernel(x, c, ctx, c_ctx, w_ada, b_ada, norm_pre, norm_post, ffn_w_gate, ffn_w_up, ffn_w_down, w_in, w_out, s5_lam_re, s5_lam_im, s5_log_dt, s5_b_re, s5_b_im, s5_c_re, s5_c_im, s5_d, s5_w_glu, s5_b_glu, mla_q_norm, mla_kv_norm, mla_w_uq, mla_w_ukv, rwkv_conv_w, rwkv_conv_b, rwkv_w0, rwkv_w_up, rwkv_a0, rwkv_a_up, rwkv_g_up, rwkv_k_k, rwkv_k_a, rwkv_r_k, rwkv_ln_w, rwkv_ln_b, mlstm_conv_w, mlstm_conv_b, mlstm_gate_b, mlstm_norm):
    B, T, D = x.shape
    Tc = ctx.shape[1]
    Ta = T + Tc
    L = w_ada.shape[0]
    W = GROUP_W
    assert T % CHUNK == 0 and Tc % CHUNK == 0 and B % 8 == 0 and B <= 8
    n_ctx, n_all = Tc // CHUNK, Ta // CHUNK

    rows = 16
    cvec = jnp.concatenate([c, c_ctx[None, :], jnp.zeros((rows - B - 1, D), F32)], axis=0)
    mods_all = _ada_call(cvec, w_ada, b_ada)

    w_in_re = _inproj_relayout(w_in).astype(BF16)
    cw = jnp.zeros((L, Z_NBLK, 8, ZB), F32)
    rc = jnp.concatenate([rwkv_conv_w, rwkv_conv_b[:, None, :]], axis=1).reshape(L, 4, 3, ZB).transpose(0, 2, 1, 3)
    mc = jnp.concatenate([mlstm_conv_w, mlstm_conv_b[:, None, :]], axis=1).reshape(L, 4, 2, ZB).transpose(0, 2, 1, 3)
    cw = cw.at[:, Z_RKV:Z_RKV + 3, 0:4].set(rc).at[:, Z_MQK:Z_MQK + 2, 0:4].set(mc)

    wg = ffn_w_gate.astype(BF16)
    wu = ffn_w_up.astype(BF16)
    wd = ffn_w_down.astype(BF16)
    wout = w_out.astype(BF16)

    G = s5_lam_re.shape[2]
    N = s5_lam_re.shape[3]
    eye_g = jnp.eye(G, dtype=F32)
    lre = s5_lam_re.reshape(L, 2, 1, G * N)
    lim = s5_lam_im.reshape(L, 2, 1, G * N)
    ldt = jnp.repeat(s5_log_dt, N, axis=-1).reshape(L, 2, 1, G * N)
    wbre = jnp.einsum('ldgnp,gh->ldgphn', s5_b_re, eye_g).reshape(L, 2, G * S5_P, G * N)
    wbim = jnp.einsum('ldgnp,gh->ldgphn', s5_b_im, eye_g).reshape(L, 2, G * S5_P, G * N)
    wcre = jnp.einsum('ldgpn,gh->ldgnhp', s5_c_re, eye_g).reshape(L, 2, G * N, G * S5_P).astype(BF16)
    wcim = jnp.einsum('ldgpn,gh->ldgnhp', s5_c_im, eye_g).reshape(L, 2, G * N, G * S5_P).astype(BF16)

    nope = HEAD_DIM
    qd = nope + ROPE_DIM
    wq4 = mla_w_uq.reshape(L, -1, N_HEADS, qd)
    wq = jnp.pad(wq4, ((0, 0), (0, 0), (0, 0), (0, 128 - qd))).reshape(L, -1, N_HEADS * 128).astype(BF16)
    wq_rot = _rope_rotate_cols(wq4[..., nope:])
    wqr = jnp.pad(wq_rot, ((0, 0), (0, 0), (0, 0), (nope, 128 - qd))).reshape(L, -1, N_HEADS * 128).astype(BF16)
    wkv4 = mla_w_ukv.reshape(L, -1, N_HEADS, 2 * HEAD_DIM)
    wk = jnp.pad(wkv4[..., :HEAD_DIM], ((0, 0), (0, 0), (0, 0), (0, 64))).reshape(L, -1, N_HEADS * 128).astype(BF16)
    wv = jnp.pad(wkv4[..., HEAD_DIM:], ((0, 0), (0, 0), (0, 0), (0, 64))).reshape(L, -1, N_HEADS * 128).astype(BF16)
    cos_t, sin_t = _rope_tables(T, Tc)
    scale = float(qd) ** -0.5 * math.log2(math.e)

    wup = _pad_rows(rwkv_w_up, 0, W)
    aup = _pad_rows(rwkv_a_up, 32, W)
    gup = _pad_rows(rwkv_g_up, 64, W)
    gbias = jnp.pad(mlstm_gate_b, ((0, 0), (0, W - mlstm_gate_b.shape[1])))

    xa = jnp.concatenate([ctx, x], axis=1)
    r1 = lambda a: a.reshape(1, -1)

    for l in range(L):
        mods = mods_all[l]
        xa2, hmix = _ffn_call(xa.reshape(B * Ta, D), mods, r1(norm_pre[l, 0]), r1(norm_post[l, 0]),
                              r1(norm_pre[l, 1]), wg[l, 0], wu[l, 0], wd[l, 0],
                              ta=Ta, tc=Tc, n_batch=B, koff=0, emit_hmix=True)
        xa = xa2.reshape(B, Ta, D)
        z = _inproj_call(hmix.reshape(B, Ta, D), w_in_re[l], cw[l], tc=Tc)

        s5f, s5b = _s5_call(z, lre[l], lim[l], ldt[l], wbre[l], wbim[l], wcre[l], wcim[l],
                            n_ctx=n_ctx, n_all=n_all)

        q, k, v = _mla_proj_call(z, r1(mla_q_norm[l]), r1(mla_kv_norm[l]), wq[l], wqr[l], wk[l], wv[l],
                                 cos_t, sin_t, scale=scale)
        attn = _attn_call(q, k, v, tc=Tc)

        rf, rb, rg, rbon, mf, mb = _recur_call(
            z, wup[l], aup[l], gup[l], rwkv_w0[l][:, None, :], rwkv_a0[l][:, None, :],
            r1(rwkv_k_k[l]), r1(rwkv_k_a[l]), r1(rwkv_r_k[l]), gbias[l:l + 1], n_ctx=n_ctx, n_all=n_all)

        xa = _outproj_call(xa, mods, s5f, s5b, z, attn, rf, rb, rg, rbon,
                           mf, mb, r1(s5_d[l]), s5_w_glu[l].astype(BF16), r1(s5_b_glu[l]), r1(rwkv_ln_w[l]),
                           r1(rwkv_ln_b[l]), r1(mlstm_norm[l]), r1(norm_post[l, 1]), wout[l], tc=Tc)

        xa2, _ = _ffn_call(xa.reshape(B * Ta, D), mods, r1(norm_pre[l, 2]), r1(norm_post[l, 2]),
                           r1(norm_pre[l, 1]), wg[l, 1], wu[l, 1], wd[l, 1],
                           ta=Ta, tc=Tc, n_batch=B, koff=6, emit_hmix=False)
        xa = xa2.reshape(B, Ta, D)

    return xa[:, Tc:, :]
```

```python
import functools
import math

import numpy as np
import jax
import jax.numpy as jnp
from jax import lax
from jax.experimental import pallas as pl
from jax.experimental.pallas import tpu as pltpu

F32 = jnp.float32
BF16 = jnp.bfloat16

GROUP_W = 256
HEAD_DIM = 64
N_HEADS = GROUP_W // HEAD_DIM
CHUNK = 64
N_MOD = 9
NORM_EPS = 1e-6
RWKV_GN_EPS = HEAD_DIM * 1e-5
GRID_W = 64
ROPE_BASE = 10000.0
ROPE_DIM = 32
ROPE_AXIS = 16
S5_P = 16
S5_STATE = 64
MACARON = 0.5
VMEM_LIMIT_BYTES = 56 * 1024 * 1024

ZB = 256
Z_MLA, Z_RKV, Z_S5, Z_LORA, Z_MQK, Z_MVO, Z_GATE = 0, 3, 6, 7, 8, 10, 12
Z_NBLK = 13
Z_COLS = Z_NBLK * ZB


def _nn(a, b):
    return lax.dot_general(a, b, (((1,), (0,)), ((), ())), preferred_element_type=F32)


def _nt(a, b):
    return lax.dot_general(a, b, (((1,), (1,)), ((), ())), preferred_element_type=F32)


def _tn(a, b):
    return lax.dot_general(a, b, (((0,), (0,)), ((), ())), preferred_element_type=F32)


def _split2(x):
    hi = x.astype(BF16)
    lo = (x - hi.astype(F32)).astype(BF16)
    return hi, lo


def _split3(x):
    p1 = x.astype(BF16)
    r = x - p1.astype(F32)
    p2 = r.astype(BF16)
    p3 = (r - p2.astype(F32)).astype(BF16)
    return p1, p2, p3


def _mm3(dotf, a, b):
    ah, al = _split2(a)
    bh, bl = _split2(b)
    return dotf(ah, bh) + (dotf(ah, bl) + dotf(al, bh))


def _mm1(dotf, a, b):
    return dotf(a.astype(BF16), b.astype(BF16))


_mmn = _mm1


def _mm_exact_rhs(a, e):
    eb = e.astype(BF16)
    p1, p2, p3 = _split3(a)
    return _nn(p1, eb) + (_nn(p2, eb) + _nn(p3, eb))


def _mm_exact_lhs(e, a):
    eb = e.astype(BF16)
    p1, p2, p3 = _split3(a)
    return _nn(eb, p1) + (_nn(eb, p2) + _nn(eb, p3))


def _rms(x, g):
    return x * lax.rsqrt(jnp.mean(x * x, axis=-1, keepdims=True) + NORM_EPS) * g


def _sigmoid(x):
    return 0.5 * jnp.tanh(0.5 * x) + 0.5


def _group_tri_masks(tri_scr, rows):
    r = _iota((rows, rows), 0)
    c = _iota((rows, rows), 1)
    same = (r // CHUNK) == (c // CHUNK)
    tri_scr[0] = (same & (c <= r)).astype(BF16)
    tri_scr[1] = (same & (c >= r)).astype(BF16)


def _softplus(x):
    return jnp.maximum(x, 0.0) + jnp.log(1.0 + jnp.exp(-jnp.abs(x)))


def _iota(shape, dim):
    return lax.broadcasted_iota(jnp.int32, shape, dim)


def _head_masks():
    lane = _iota((1, GROUP_W), 1)
    return [lane // HEAD_DIM == h for h in range(N_HEADS)]


def _block_diag_mask():
    r = _iota((GROUP_W, GROUP_W), 0) // HEAD_DIM
    c = _iota((GROUP_W, GROUP_W), 1) // HEAD_DIM
    return (r == c).astype(F32)


def _row_stack(x, hms):
    xb = x.astype(BF16)
    return jnp.concatenate([jnp.where(m, xb, 0.0) for m in hms], axis=0)


def _pick_tile(n, target, mult=16):
    best = None
    for t in range(mult, min(n, target) + 1, mult):
        if n % t == 0:
            best = t
    if best is None:
        raise ValueError(f"no tile for {n}")
    return best


def _cparams(*sem):
    return pltpu.CompilerParams(dimension_semantics=sem, vmem_limit_bytes=VMEM_LIMIT_BYTES)


def _ada_kernel(c_ref, w_ref, b_ref, o_ref):
    c = c_ref[...]
    s = c * _sigmoid(c)
    o_ref[0] = _mm3(_nn, s, w_ref[0]) + b_ref[0]


def _ada_call(cvec, w_ada, b_ada):
    L, D, N = w_ada.shape
    R = cvec.shape[0]
    tn = _pick_tile(N, 1152, 128)
    return pl.pallas_call(
        _ada_kernel,
        out_shape=jax.ShapeDtypeStruct((L, R, N), F32),
        grid=(L, N // tn),
        in_specs=[pl.BlockSpec((R, D), lambda l, j: (0, 0)),
                  pl.BlockSpec((1, D, tn), lambda l, j: (l, 0, j)),
                  pl.BlockSpec((1, 1, tn), lambda l, j: (l, 0, j))],
        out_specs=pl.BlockSpec((1, R, tn), lambda l, j: (l, 0, j)),
        compiler_params=_cparams("parallel", "parallel"),
        name="ada_mod",
    )(cvec, w_ada, b_ada.reshape(L, 1, N))


FFN_COLS = 256


def _ffn_kernel(xn_ref, xp_ref, mods_ref, gpre_ref, gpost_ref, gmix_ref, wg_ref, wu_ref, wd_ref,
                o_ref, hmix_ref, h_scr, acc_scr, *, tm, tiles_per_batch, tc, n_batch, koff, emit_hmix,
                n_tiles, f_split):
    i = pl.program_id(0)
    f = pl.program_id(1)
    d = xn_ref.shape[-1]
    fd = wg_ref.shape[1]

    def mod_of(tile):
        b = tile // tiles_per_batch
        t0 = (tile % tiles_per_batch) * tm
        is_ctx = (t0 + _iota((tm, 1), 0)) < tc

        def mod(fn):
            row = lambda r: (lambda k: mods_ref[pl.ds(r, 1), pl.ds(k * d, d)])
            return jnp.where(is_ctx, fn(row(n_batch)), fn(row(b)))
        return mod

    unit = lambda x: x * lax.rsqrt(jnp.mean(x * x, axis=-1, keepdims=True) + NORM_EPS)

    def pre_norm(tile, x):
        mod = mod_of(tile)
        gain = mod(lambda m: gpre_ref[...] * (1.0 + m(koff + 1)))
        return (unit(x) * gain + mod(lambda m: m(koff))).astype(BF16)

    def finish(tile, acc, x):
        mod = mod_of(tile)
        xn = x + unit(acc) * mod(lambda m: (MACARON * m(koff + 2)) * gpost_ref[...])
        o_ref[...] = xn
        if emit_hmix:
            gain = mod(lambda m: gmix_ref[...] * (1.0 + m(4)))
            hmix_ref[...] = (unit(xn) * gain + mod(lambda m: m(3))).astype(BF16)
        else:
            hmix_ref[...] = jnp.zeros_like(hmix_ref)

    def hidden_cols(hb, lo, hi):
        acc = None
        for c0 in range(lo, hi, FFN_COLS):
            g = _nn(hb, wg_ref[:, c0:c0 + FFN_COLS])
            u = _nn(hb, wu_ref[:, c0:c0 + FFN_COLS])
            t = _nn((g * _sigmoid(g) * u).astype(BF16), wd_ref[c0:c0 + FFN_COLS, :])
            acc = t if acc is None else acc + t
        return acc

    @pl.when(jnp.logical_and(i == 0, f == 0))
    def _():
        h_scr[0] = pre_norm(0, xn_ref[...])
        acc_scr[1] = jnp.zeros((tm, d), F32)

    for s in range(2):
        tile = 2 * i + s

        @pl.when(jnp.logical_and(tile < n_tiles, f == 2 * s))
        def _(s=s, tile=tile):
            finish(jnp.maximum(tile - 1, 0), acc_scr[1 - s], xp_ref[...])
            acc_scr[s] = hidden_cols(h_scr[s], 0, f_split)

        @pl.when(jnp.logical_and(tile < n_tiles, f == 2 * s + 1))
        def _(s=s, tile=tile):
            h_scr[1 - s] = pre_norm(jnp.minimum(tile + 1, n_tiles - 1), xn_ref[...])
            acc_scr[s] += hidden_cols(h_scr[s], f_split, fd)

    @pl.when(jnp.logical_and(2 * i == n_tiles, f == 0))
    def _():
        finish(n_tiles - 1, acc_scr[1], xp_ref[...])


def _ffn_call(xa2, mods, gpre, gpost, gmix, wg, wu, wd, *, ta, tc, n_batch, koff, emit_hmix):
    M, D = xa2.shape
    Fd = wg.shape[1]
    assert Fd % FFN_COLS == 0
    tm = _pick_tile(ta, 544)
    n_tiles = M // tm
    assert n_tiles % 2 == 0
    f_split = ((Fd // FFN_COLS + 1) // 2) * FFN_COLS
    kern = functools.partial(_ffn_kernel, tm=tm, tiles_per_batch=ta // tm, tc=tc, n_batch=n_batch,
                             koff=koff, emit_hmix=emit_hmix, n_tiles=n_tiles, f_split=f_split)
    hm_rows = tm if emit_hmix else 16
    last = n_tiles - 1
    resident = lambda a: pl.BlockSpec(a.shape, lambda i, f: (0, 0), pipeline_mode=pl.Buffered(1))
    vec = pl.BlockSpec((1, D), lambda i, f: (0, 0))
    return pl.pallas_call(
        kern,
        out_shape=(jax.ShapeDtypeStruct((M, D), F32),
                   jax.ShapeDtypeStruct((M if emit_hmix else 16 * n_tiles, D), BF16)),
        grid=(n_tiles // 2 + 1, 4),
        in_specs=[pl.BlockSpec((tm, D), lambda i, f: (jnp.minimum(2 * i + (f + 1) // 2, last), 0)),
                  pl.BlockSpec((tm, D), lambda i, f: (jnp.clip(2 * i - 1 + f // 2, 0, last), 0)),
                  pl.BlockSpec(mods.shape, lambda i, f: (0, 0)),
                  vec, vec, vec, resident(wg), resident(wu), resident(wd)],
        out_specs=(pl.BlockSpec((tm, D), lambda i, f: (jnp.clip(2 * i - 1 + f // 2, 0, last), 0)),
                   pl.BlockSpec((hm_rows, D), lambda i, f: (jnp.clip(2 * i - 1 + f // 2, 0, last), 0))),
        scratch_shapes=[pltpu.VMEM((2, tm, D), BF16), pltpu.VMEM((2, tm, D), F32)],
        compiler_params=_cparams("arbitrary", "arbitrary"),
        name="half_ffn",
    )(xa2, xa2, mods, gpre, gpost, gmix, wg, wu, wd)


def _inproj_kernel(h_ref, w_ref, cw_ref, o_ref, *, tc, conv_lo, conv_hi, silu_lo, silu_hi):
    nb = pl.program_id(1)
    z = _nn(h_ref[0], w_ref[...])
    ta = z.shape[0]
    is_conv = ((nb >= conv_lo[0]) & (nb < conv_hi[0])) | ((nb >= conv_lo[1]) & (nb < conv_hi[1]))
    is_silu = (nb >= silu_lo) & (nb < silu_hi)

    @pl.when(jnp.logical_not(is_conv))
    def _():
        o_ref[0] = z

    def conv():
        row = _iota((ta, 1), 0)
        zp = jnp.where((row == 0) | (row == tc), 0.0, pltpu.roll(z, 1, 0))
        zn = jnp.where((row == tc - 1) | (row == ta - 1), 0.0, pltpu.roll(z, ta - 1, 0))
        cw = cw_ref[0]
        return cw[3:4] + zp * cw[0:1] + z * cw[1:2] + zn * cw[2:3]

    @pl.when(is_conv & jnp.logical_not(is_silu))
    def _():
        o_ref[0] = conv()

    @pl.when(is_conv & is_silu)
    def _():
        y = conv()
        o_ref[0] = y * _sigmoid(y)


def _inproj_call(hmix3, w_re, cw, *, tc):
    B, Ta, D = hmix3.shape
    kern = functools.partial(_inproj_kernel, tc=tc, conv_lo=(Z_RKV, Z_MQK), conv_hi=(Z_RKV + 3, Z_MQK + 2),
                             silu_lo=Z_MQK, silu_hi=Z_MQK + 2)
    return pl.pallas_call(
        kern,
        out_shape=jax.ShapeDtypeStruct((B, Ta, Z_COLS), F32),
        grid=(B, Z_NBLK),
        in_specs=[pl.BlockSpec((1, Ta, D), lambda b, n: (b, 0, 0)),
                  pl.BlockSpec((D, ZB), lambda b, n: (0, n)),
                  pl.BlockSpec((1, 8, ZB), lambda b, n: (n, 0, 0))],
        out_specs=pl.BlockSpec((1, Ta, ZB), lambda b, n: (b, 0, n)),
        compiler_params=_cparams("parallel", "arbitrary"),
        name="in_proj",
    )(hmix3, w_re, cw)


def _bwd_chunk(i, n_ctx, n_all):
    return jnp.where(i < n_ctx, n_ctx - 1 - i, n_all - 1 - (i - n_ctx))


def _s5_kernel(uf_ref, ub_ref, lre_ref, lim_ref, ldt_ref, wbre_ref, wbim_ref, wcre_ref, wcim_ref,
               yf_ref, yb_ref, wb_scr, coef_scr, st_scr, rel_scr, x_scr, *, lc, nb, scan_unroll):
    i = pl.program_id(0)
    gn = lre_ref.shape[-1]

    @pl.when(i == 0)
    def _():
        for d in range(2):
            dt = jnp.exp(ldt_ref[d])
            lre = lre_ref[d]
            lim = lim_ref[d]
            mag = jnp.exp(lre * dt)
            ar = mag * jnp.cos(lim * dt)
            ai = mag * jnp.sin(lim * dt)
            den = lre * lre + lim * lim
            fr = ((ar - 1.0) * lre + ai * lim) / den
            fi = (ai * lre - (ar - 1.0) * lim) / den
            coef_scr[d, 0:nb, :] = jnp.broadcast_to(ar, (nb, gn))
            coef_scr[d, nb:2 * nb, :] = jnp.broadcast_to(ai, (nb, gn))
            wre = wbre_ref[d]
            wim = wbim_ref[d]
            wb_scr[d, :, 0:gn] = (wre * fr - wim * fi).astype(BF16)
            wb_scr[d, :, gn:2 * gn] = (wim * fr + wre * fi).astype(BF16)
        st_scr[...] = jnp.zeros_like(st_scr)

    half = 128
    u_refs = (uf_ref, ub_ref)
    y_refs = (yf_ref, yb_ref)

    for d in range(2):
        for b in range(nb):
            for s in range(2):
                rel_scr[d, s, pl.ds(b, lc, stride=nb), :] = u_refs[d][b, :, s * half:(s + 1) * half]
    for d in range(2):
        u_tm = jnp.concatenate([rel_scr[d, 0], rel_scr[d, 1]], axis=1).astype(BF16)
        x_scr[d] = _nn(u_tm, wb_scr[d])

    def body(t, carry):
        out = []
        for d in range(2):
            sr, si = carry[2 * d], carry[2 * d + 1]
            ar = coef_scr[d, 0:nb, :]
            ai = coef_scr[d, nb:2 * nb, :]
            tt = t if d == 0 else lc - 1 - t
            r0 = pl.multiple_of(tt * nb, nb)
            xr = x_scr[d, pl.ds(r0, nb), 0:gn]
            xi = x_scr[d, pl.ds(r0, nb), gn:2 * gn]
            nsr = ar * sr - ai * si + xr
            nsi = ar * si + ai * sr + xi
            x_scr[d, pl.ds(r0, nb), 0:gn] = nsr
            x_scr[d, pl.ds(r0, nb), gn:2 * gn] = nsi
            out += [nsr, nsi]
        return tuple(out)

    init = (st_scr[0, 0:nb, :], st_scr[0, nb:2 * nb, :], st_scr[1, 0:nb, :], st_scr[1, nb:2 * nb, :])
    fin = lax.fori_loop(0, lc, body, init, unroll=scan_unroll)
    for d in range(2):
        st_scr[d, 0:nb, :] = fin[2 * d]
        st_scr[d, nb:2 * nb, :] = fin[2 * d + 1]
        y = (_nn(x_scr[d, :, 0:gn].astype(BF16), wcre_ref[d])
             - _nn(x_scr[d, :, gn:2 * gn].astype(BF16), wcim_ref[d]))
        rel_scr[d, 0] = y[:, 0:half]
        rel_scr[d, 1] = y[:, half:2 * half]
    for d in range(2):
        for b in range(nb):
            for s in range(2):
                c0 = b * 2 * half + s * half
                y_refs[d][:, c0:c0 + half] = rel_scr[d, s, pl.ds(b, lc, stride=nb), :]


def _s5_call(z, lre, lim, ldt, wbre, wbim, wcre, wcim, *, n_ctx, n_all):
    B, Ta, _ = z.shape
    W = GROUP_W
    lc = CHUNK
    gn = lre.shape[-1]
    blk = lc * B
    kern = functools.partial(_s5_kernel, lc=lc, nb=B, scan_unroll=True)
    full = lambda a: pl.BlockSpec(a.shape, lambda i: (0,) * a.ndim)
    out = jax.ShapeDtypeStruct((Ta, B * W), F32)
    return pl.pallas_call(
        kern,
        out_shape=(out, out),
        grid=(n_all,),
        in_specs=[pl.BlockSpec((B, lc, W), lambda i: (0, i, Z_S5)),
                  pl.BlockSpec((B, lc, W), lambda i: (0, _bwd_chunk(i, n_ctx, n_all), Z_S5)),
                  full(lre), full(lim), full(ldt), full(wbre), full(wbim), full(wcre), full(wcim)],
        out_specs=(pl.BlockSpec((lc, B * W), lambda i: (i, 0)),
                   pl.BlockSpec((lc, B * W), lambda i: (_bwd_chunk(i, n_ctx, n_all), 0))),
        scratch_shapes=[pltpu.VMEM((2, W, 2 * gn), BF16),
                        pltpu.VMEM((2, 2 * B, gn), F32),
                        pltpu.VMEM((2, 2 * B, gn), F32),
                        pltpu.VMEM((2, 2, blk, 128), F32),
                        pltpu.VMEM((2, blk, 2 * gn), F32)],
        compiler_params=_cparams("arbitrary"),
        name="s5_scan",
    )(z, z, lre, lim, ldt, wbre, wbim, wcre, wcim)


def _mla_proj_kernel(z_ref, qn_ref, kvn_ref, wq_ref, wqr_ref, wk_ref, wv_ref, cos_ref, sin_ref,
                     q_ref, k_ref, v_ref, *, scale):
    z = z_ref[0]
    cq = z[:, 0:256]
    ckv = z[:, 256:384]
    kr = z[:, 384:512]
    krr = z[:, 512:640]
    cos = cos_ref[...]
    sin = sin_ref[...]
    cqb = _rms(cq, qn_ref[...]).astype(BF16)
    ckvb = _rms(ckv, kvn_ref[...]).astype(BF16)
    q = _nn(cqb, wq_ref[...])
    qr = _nn(cqb, wqr_ref[...])
    kn = _nn(ckvb, wk_ref[...])
    krp = kr * cos + krr * sin
    for h in range(N_HEADS):
        sl = slice(h * 128, (h + 1) * 128)
        q_ref[0, h] = ((q[:, sl] * cos + qr[:, sl] * sin) * scale).astype(BF16)
        k_ref[0, h] = (kn[:, sl] + krp).astype(BF16)
    ones_pad = ((_iota((1, N_HEADS * 128), 1) % 128) >= HEAD_DIM).astype(F32)
    vv = _nn(ckvb, wv_ref[...]) + ones_pad
    for h in range(N_HEADS):
        v_ref[0, h] = vv[:, h * 128:(h + 1) * 128].astype(BF16)


def _mla_proj_call(z, qn, kvn, wq, wqr, wk, wv, cos_t, sin_t, *, scale):
    B, Ta, _ = z.shape
    tm = _pick_tile(Ta, 544)
    kern = functools.partial(_mla_proj_kernel, scale=scale)
    full = lambda a: pl.BlockSpec(a.shape, lambda b, t: (0,) * a.ndim)
    return pl.pallas_call(
        kern,
        out_shape=(jax.ShapeDtypeStruct((B, N_HEADS, Ta, 128), BF16),
                   jax.ShapeDtypeStruct((B, N_HEADS, Ta, 128), BF16),
                   jax.ShapeDtypeStruct((B, N_HEADS, Ta, 128), BF16)),
        grid=(B, Ta // tm),
        in_specs=[pl.BlockSpec((1, tm, 3 * ZB), lambda b, t: (b, t, 0)),
                  full(qn), full(kvn), full(wq), full(wqr), full(wk), full(wv),
                  pl.BlockSpec((tm, 128), lambda b, t: (t, 0)),
                  pl.BlockSpec((tm, 128), lambda b, t: (t, 0))],
        out_specs=(pl.BlockSpec((1, N_HEADS, tm, 128), lambda b, t: (b, 0, t, 0)),
                   pl.BlockSpec((1, N_HEADS, tm, 128), lambda b, t: (b, 0, t, 0)),
                   pl.BlockSpec((1, N_HEADS, tm, 128), lambda b, t: (b, 0, t, 0))),
        compiler_params=_cparams("parallel", "parallel"),
        name="mla_proj",
    )(z, qn, kvn, wq, wqr, wk, wv, cos_t, sin_t)


def _attn_kernel(q_ref, k_ref, v_ref, o_ref, *, tc, ta, n_ctx_tiles, key_chunk):
    i = pl.program_id(1)
    lane = _iota((1, 128), 1)

    def write_out(acc):
        outs = [a * (1.0 / pltpu.roll(a, HEAD_DIM, 1)) for a in acc]
        for hp in range(N_HEADS // 2):
            o_ref[0, :, hp * 128:(hp + 1) * 128] = jnp.where(lane < HEAD_DIM, outs[2 * hp],
                                                             pltpu.roll(outs[2 * hp + 1], HEAD_DIM, 1))

    @pl.when(i < n_ctx_tiles)
    def _():
        acc = []
        for j in range(N_HEADS):
            s = _nt(q_ref[0, j], k_ref[0, j, 0:tc, :])
            p = jnp.exp2(s - jnp.max(s, axis=-1, keepdims=True))
            acc.append(_nn(p.astype(BF16), v_ref[0, j, 0:tc, :]))
        write_out(acc)

    @pl.when(i >= n_ctx_tiles)
    def _():
        bounds = [0, tc] + list(range(tc + key_chunk, ta + 1, key_chunk))
        units = [(c, j) for c in range(len(bounds) - 1) for j in range(N_HEADS)]
        qs = [q_ref[0, j] for j in range(N_HEADS)]
        score = lambda c, j: _nt(qs[j], k_ref[0, j, bounds[c]:bounds[c + 1], :])
        m = [None] * N_HEADS
        acc = [None] * N_HEADS

        def weighted_values(pend):
            c, j, pb, alpha = pend
            pv = _nn(pb, v_ref[0, j, bounds[c]:bounds[c + 1], :])
            acc[j] = pv if alpha is None else alpha * acc[j] + pv

        pending = None
        s_next = score(*units[0])
        for idx, (c, j) in enumerate(units):
            s = s_next
            if idx + 1 < len(units):
                s_next = score(*units[idx + 1])
            mc = jnp.max(s, axis=-1, keepdims=True)
            if c == 0:
                alpha = None
                m[j] = mc
                p = jnp.exp2(s - mc)
            else:
                m_new = jnp.maximum(m[j], mc)
                alpha = jnp.exp2(m[j] - m_new)
                p = jnp.exp2(s - m_new)
                m[j] = m_new
            if pending is not None:
                weighted_values(pending)
            pending = (c, j, p.astype(BF16), alpha)
        weighted_values(pending)
        write_out(acc)


def _attn_call(q, k, v, *, tc):
    B, H, Ta, _ = q.shape
    tq = _pick_tile(math.gcd(tc, Ta), 256)
    kern = functools.partial(_attn_kernel, tc=tc, ta=Ta, n_ctx_tiles=tc // tq,
                             key_chunk=_pick_tile(Ta - tc, 1024, 128))
    return pl.pallas_call(
        kern,
        out_shape=jax.ShapeDtypeStruct((B, Ta, GROUP_W), F32),
        grid=(B, Ta // tq),
        in_specs=[pl.BlockSpec((1, H, tq, 128), lambda b, i: (b, 0, i, 0)),
                  pl.BlockSpec((1, H, Ta, 128), lambda b, i: (b, 0, 0, 0)),
                  pl.BlockSpec((1, H, Ta, 128), lambda b, i: (b, 0, 0, 0))],
        out_specs=pl.BlockSpec((1, tq, GROUP_W), lambda b, i: (b, i, 0)),
        compiler_params=_cparams("parallel", "arbitrary"),
        name="mla_attn",
    )(q, k, v)


def _chunk_masks(d):
    L = CHUNK
    row = _iota((L, GROUP_W), 0)
    s_idx = _iota((L, GROUP_W), 1) % L
    tr = _iota((L, L), 0)
    tcol = _iota((L, L), 1)
    if d == 0:
        return (tcol <= tr).astype(F32), s_idx < row, s_idx <= row, s_idx == row
    return (tcol >= tr).astype(F32), s_idx > row, s_idx >= row, s_idx == row


def _run_stages(*generators):
    live = list(generators)
    while live:
        for g in list(live):
            try:
                next(g)
            except StopIteration:
                live.remove(g)


def _rwkv_kernel(*refs, cps):
    _run_stages(_rwkv_stages(*refs, cps=cps))


def _rwkv_stages(rf_ref, rb_ref, lf_ref, lb_ref, wup_ref, aup_ref, gup_ref, w0_ref, a0_ref,
                 kk_ref, ka_ref, rk_ref, yf_ref, yb_ref, g_ref, bon_ref, s_scr, tri_scr, bdb_scr, *, cps):
    i = pl.program_id(1)
    L = CHUNK
    W = GROUP_W

    @pl.when(i == 0)
    def _():
        s_scr[...] = jnp.zeros_like(s_scr)
        _group_tri_masks(tri_scr, cps * L)
        bdb_scr[...] = _block_diag_mask().astype(BF16)

    hms = _head_masks()
    bd = _block_diag_mask()
    bdb = bdb_scr[...]
    rs = lambda x: _row_stack(x, hms)
    bd_b = bd > 0.5
    bdiag = lambda x: jnp.where(bd_b, jnp.concatenate([x.astype(BF16)] * N_HEADS, axis=0), 0.0)

    per_dir = []
    for d, (r_ref, l_ref) in enumerate(((rf_ref, lf_ref), (rb_ref, lb_ref))):
        rkv = r_ref[0]
        lora = l_ref[0]
        r = rkv[:, 0:W]
        k = rkv[:, W:2 * W]
        v = rkv[:, 2 * W:3 * W]
        _, strict, incl, eye = _chunk_masks(d)
        tri = tri_scr[d]

        lw = -math.exp(-0.5) * _sigmoid(w0_ref[d] + _mm1(_nn, jnp.tanh(lora), wup_ref[d]))
        a = _sigmoid(a0_ref[d] + _mm1(_nn, lora, aup_ref[d]))
        kkv = k * kk_ref[...]
        kkn = kkv * lax.rsqrt(jnp.maximum(_mm1(_nn, kkv * kkv, bdb), 1e-24))
        keff = k * (1.0 + (a - 1.0) * ka_ref[...])
        kka = kkn * a
        lw_hi, lw_lo = _split2(lw)
        cum2 = _nn(tri, jnp.concatenate([lw_hi, lw_lo], axis=1))
        cum = cum2[:, 0:W] + cum2[:, W:2 * W]
        e_dn = jnp.exp(-cum)
        per_dir.append(dict(v=v, lw=lw, cum=cum, kka=kka, keff=keff, strict=strict, incl=incl, eye=eye,
                            al=-kkn * jnp.exp(cum - lw), rt=r * jnp.exp(cum), bh=kka * e_dn, kh=keff * e_dn))
        if d == 0:
            g_ref[0] = _mm1(_nn, _sigmoid(lora), gup_ref[...])
            bon_ref[0] = _mm1(_nn, r * k * rk_ref[...], bdb) * v
        yield

    chains = []
    for c in range(cps):
        for d in range(2):
            pd = per_dir[d]
            sl = slice(c * L, (c + 1) * L)
            tot = jnp.sum(pd["lw"][sl], axis=0, keepdims=True)
            e_tc = jnp.exp(tot - pd["cum"][sl])
            ch = dict(d=d, c=c, al=pd["al"][sl], rt=pd["rt"][sl], v=pd["v"][sl], tot=tot,
                      bt=pd["kka"][sl] * e_tc, kt=pd["keff"][sl] * e_tc)
            ch["rsv"] = rs(ch["v"])
            a_all = _mm1(_nt, jnp.concatenate([ch["al"], ch["rt"]], axis=0),
                         jnp.concatenate([rs(pd["bh"][sl]), rs(pd["kh"][sl])], axis=0))
            ch["a_ab"] = jnp.where(pd["strict"], a_all[0:L, 0:W], 0.0)
            ch["a_ak"] = jnp.where(pd["strict"], a_all[0:L, W:2 * W], 0.0)
            ch["a_rb"] = jnp.where(pd["incl"], a_all[L:2 * L, 0:W], 0.0)
            ch["a_rk"] = jnp.where(pd["incl"], a_all[L:2 * L, W:2 * W], 0.0)
            ch["p"] = jnp.where(pd["eye"], 1.0, 0.0) + ch["a_ab"]
            chains.append(ch)
        yield

    for ch in chains:
        ch["sq"] = _mmn(_nn, ch["a_ab"], bdiag(ch["a_ab"]))
        ch["zk"] = _mm1(_nn, ch["a_ak"], ch["rsv"])
        ch["y0k"] = _mm1(_nn, ch["a_rk"], ch["rsv"])
    yield
    n_sq = int(math.log2(L)) - 1
    for it in range(n_sq):
        for ch in chains:
            if it < n_sq - 1:
                ps = _mmn(_nn, jnp.concatenate([ch["p"], ch["sq"]], axis=0), bdiag(ch["sq"]))
                ch["p"] = ch["p"] + ps[0:L]
                ch["sq"] = ps[L:2 * L]
            else:
                ch["p"] = ch["p"] + _mmn(_nn, ch["p"], bdiag(ch["sq"]))
        yield
    for ch in chains:
        pu = _mm1(_nn, ch["p"], jnp.concatenate([rs(ch["al"]), rs(ch["zk"])], axis=1))
        ch["w"], ch["uk"] = pu[:, 0:W], pu[:, W:2 * W]
    yield
    for ch in chains:
        gy = _mm1(_nn, ch["a_rb"], jnp.concatenate([rs(ch["w"]), rs(ch["uk"])], axis=1))
        ch["g"] = ch["rt"] + gy[:, 0:W]
        ch["y0"] = gy[:, W:2 * W] + ch["y0k"]
        ch["decay"] = jnp.exp(ch["tot"])
        ch["pm"] = bd * _mm1(_tn, ch["w"], ch["bt"])
        ch["q0"] = bd * _mm1(_tn, jnp.concatenate([ch["uk"], ch["v"]], axis=0),
                             jnp.concatenate([ch["bt"], ch["kt"]], axis=0))
    yield

    by_key = {(ch["d"], ch["c"]): ch for ch in chains}
    st = [s_scr[0], s_scr[1]]
    ys = [[None] * cps, [None] * cps]
    for step in range(cps):
        for d in range(2):
            c = step if d == 0 else cps - 1 - step
            ch = by_key[(d, c)]
            ys[d][c] = _mm1(_nt, ch["g"], st[d]) + ch["y0"]
            st[d] = st[d] * ch["decay"] + (_mm1(_nn, st[d], ch["pm"]) + ch["q0"])
        yield
    for d, y_ref in enumerate((yf_ref, yb_ref)):
        s_scr[d] = st[d]
        y_ref[0] = jnp.concatenate(ys[d], axis=0)


def _chunks_per_step(n_ctx, n_all):
    for cps in (4, 2, 1):
        if n_ctx % cps == 0 and (n_all - n_ctx) % cps == 0:
            return cps


def _rwkv_call(z, wup, aup, gup, w0, a0, kk, ka, rk, *, n_ctx, n_all):
    B, Ta, _ = z.shape
    W = GROUP_W
    cps = _chunks_per_step(n_ctx, n_all)
    rows = cps * CHUNK
    gc, ga = n_ctx // cps, n_all // cps
    full = lambda a: pl.BlockSpec(a.shape, lambda b, i: (0,) * a.ndim)
    fwd = lambda blk: (lambda b, i: (b, i, blk))
    bwd = lambda blk: (lambda b, i: (b, _bwd_chunk(i, gc, ga), blk))
    out = jax.ShapeDtypeStruct((B, Ta, W), F32)
    return pl.pallas_call(
        functools.partial(_rwkv_kernel, cps=cps),
        out_shape=(out, out, out, out),
        grid=(B, ga),
        in_specs=[pl.BlockSpec((1, rows, 3 * W), fwd(Z_RKV // 3)),
                  pl.BlockSpec((1, rows, 3 * W), bwd(Z_RKV // 3)),
                  pl.BlockSpec((1, rows, W), fwd(Z_LORA)),
                  pl.BlockSpec((1, rows, W), bwd(Z_LORA)),
                  full(wup), full(aup), full(gup), full(w0), full(a0), full(kk), full(ka), full(rk)],
        out_specs=(pl.BlockSpec((1, rows, W), fwd(0)), pl.BlockSpec((1, rows, W), bwd(0)),
                   pl.BlockSpec((1, rows, W), fwd(0)), pl.BlockSpec((1, rows, W), fwd(0))),
        scratch_shapes=[pltpu.VMEM((2, W, W), F32), pltpu.VMEM((2, rows, rows), BF16), pltpu.VMEM((W, W), BF16)],
        compiler_params=_cparams("parallel", "arbitrary"),
        name="rwkv7",
    )(z, z, z, z, wup, aup, gup, w0, a0, kk, ka, rk)


def _mlstm_kernel(*refs, cps):
    _run_stages(_mlstm_stages(*refs, cps=cps))


def _mlstm_stages(qf_ref, qb_ref, vf_ref, vb_ref, gf_ref, gb_ref, gbias_ref,
                  yf_ref, yb_ref, c_scr, n_scr, m_scr, tri_scr, exp_scr, bdb_scr, *, cps):
    i = pl.program_id(1)
    L = CHUNK
    W = GROUP_W

    @pl.when(i == 0)
    def _():
        c_scr[...] = jnp.zeros_like(c_scr)
        n_scr[...] = jnp.zeros_like(n_scr)
        m_scr[...] = jnp.zeros_like(m_scr)
        _group_tri_masks(tri_scr, cps * L)
        bdb_scr[...] = _block_diag_mask().astype(BF16)
        ci = _iota((W, W), 0)
        cj = _iota((W, W), 1) // HEAD_DIM
        for d in range(2):
            exp_scr[d, :, 0:W] = (ci == d * 2 * N_HEADS + cj).astype(BF16)
            exp_scr[d, :, W:2 * W] = (ci == d * 2 * N_HEADS + N_HEADS + cj).astype(BF16)

    hms = _head_masks()
    bd = _block_diag_mask()
    bdb = bdb_scr[...]
    rs = lambda x: _row_stack(x, hms)
    neg_inf = -jnp.inf

    def sum2(x, eb):
        hi, lo = _split2(x)
        return _nn(hi, eb) + _nn(lo, eb)

    per_dir = []
    for d, (q_ref, v_ref, g_ref) in enumerate(((qf_ref, vf_ref, gf_ref), (qb_ref, vb_ref, gb_ref))):
        qk = q_ref[0]
        g = g_ref[0] + gbias_ref[...]
        _, strict, incl, eye = _chunk_masks(d)
        tri = tri_scr[d]
        gates = sum2(g, exp_scr[d])
        li = gates[:, 0:W]
        lf = -_softplus(-gates[:, W:2 * W])
        lf_hi, lf_lo = _split2(lf)
        b2 = _nn(tri, jnp.concatenate([lf_hi, lf_lo], axis=1))
        per_dir.append(dict(q=qk[:, 0:W], k=qk[:, W:2 * W] * (HEAD_DIM ** -0.5), v=v_ref[0][:, 0:W],
                            li=li, bcol=b2[:, 0:W] + b2[:, W:2 * W], incl=incl, eye=eye))
        yield

    chains = []
    for c in range(cps):
        for d in range(2):
            pd = per_dir[d]
            sl = slice(c * L, (c + 1) * L)
            q, k, v, li, bcol = pd["q"][sl], pd["k"][sl], pd["v"][sl], pd["li"][sl], pd["bcol"][sl]
            brow = jnp.sum(jnp.where(pd["eye"], bcol, 0.0), axis=0, keepdims=True)
            lirow = jnp.sum(jnp.where(pd["eye"], li, 0.0), axis=0, keepdims=True)
            logd = jnp.where(pd["incl"], bcol - brow + lirow, neg_inf)
            mx = jnp.zeros((L, W), F32)
            for hm in hms:
                mh = jnp.max(jnp.where(hm, logd, neg_inf), axis=1, keepdims=True)
                mx = jnp.where(hm, mh, mx)
            blast = bcol[L - 1:L, :] if d == 0 else bcol[0:1, :]
            lwc = blast - bcol + li
            mlw = jnp.max(lwc, axis=0, keepdims=True)
            kw = k * jnp.exp(lwc - mlw)
            chains.append(dict(d=d, c=c, q=q, v=v, bcol=bcol, mx=mx, blast=blast, mlw=mlw, kw=kw,
                               dexp=jnp.exp(logd - mx), rsk=rs(k), rsv=rs(v),
                               nu0=jnp.sum(kw, axis=0, keepdims=True)))
        yield
    for ch in chains:
        ch["sp"] = _mm1(_nt, ch["q"], ch["rsk"]) * ch["dexp"]
        ch["cu0"] = bd * _mm1(_tn, ch["v"], ch["kw"])
    yield
    for ch in chains:
        ch["num0"] = _mm1(_nn, ch["sp"], ch["rsv"])
        ch["den0"] = _mm1(_nn, ch["sp"], bdb)
    yield

    by_key = {(ch["d"], ch["c"]): ch for ch in chains}
    c_st = [c_scr[0], c_scr[1]]
    n_row = [n_scr[0, 0:1, :], n_scr[1, 0:1, :]]
    m_row = [m_scr[0, 0:1, :], m_scr[1, 0:1, :]]
    ys = [[None] * cps, [None] * cps]
    for step in range(cps):
        for d in range(2):
            c = step if d == 0 else cps - 1 - step
            ch = by_key[(d, c)]
            inter = ch["bcol"] + m_row[d]
            mt = jnp.maximum(inter, ch["mx"])
            f_in = jnp.exp(ch["mx"] - mt)
            w_int = jnp.exp(inter - mt)
            num = f_in * ch["num0"] + w_int * _mm1(_nt, ch["q"], c_st[d])
            den = f_in * ch["den0"] + w_int * _mm1(_nn, ch["q"] * n_row[d], bdb)
            ys[d][c] = num / jnp.maximum(jnp.abs(den), jnp.exp(-mt))
            m_new = jnp.maximum(ch["blast"] + m_row[d], ch["mlw"])
            sc = jnp.exp(ch["blast"] + m_row[d] - m_new)
            e2 = jnp.exp(ch["mlw"] - m_new)
            c_st[d] = sc * c_st[d] + e2 * ch["cu0"]
            n_row[d] = sc * n_row[d] + e2 * ch["nu0"]
            m_row[d] = m_new
        yield
    for d, y_ref in enumerate((yf_ref, yb_ref)):
        c_scr[d] = c_st[d]
        n_scr[d, 0:1, :] = n_row[d]
        m_scr[d, 0:1, :] = m_row[d]
        y_ref[0] = jnp.concatenate(ys[d], axis=0)


def _mlstm_call(z, gbias, *, n_ctx, n_all):
    B, Ta, _ = z.shape
    W = GROUP_W
    cps = _chunks_per_step(n_ctx, n_all)
    L = cps * CHUNK
    n_ctx, n_all = n_ctx // cps, n_all // cps
    fwd = lambda blk: (lambda b, i: (b, i, blk))
    bwd = lambda blk: (lambda b, i: (b, _bwd_chunk(i, n_ctx, n_all), blk))
    out = jax.ShapeDtypeStruct((B, Ta, W), F32)
    return pl.pallas_call(
        functools.partial(_mlstm_kernel, cps=cps),
        out_shape=(out, out),
        grid=(B, n_all),
        in_specs=[pl.BlockSpec((1, L, 2 * W), fwd(Z_MQK // 2)),
                  pl.BlockSpec((1, L, 2 * W), bwd(Z_MQK // 2)),
                  pl.BlockSpec((1, L, 2 * W), fwd(Z_MVO // 2)),
                  pl.BlockSpec((1, L, 2 * W), bwd(Z_MVO // 2)),
                  pl.BlockSpec((1, L, W), fwd(Z_GATE)),
                  pl.BlockSpec((1, L, W), bwd(Z_GATE)),
                  pl.BlockSpec(gbias.shape, lambda b, i: (0, 0))],
        out_specs=(pl.BlockSpec((1, L, W), fwd(0)), pl.BlockSpec((1, L, W), bwd(0))),
        scratch_shapes=[pltpu.VMEM((2, W, W), F32), pltpu.VMEM((2, 8, W), F32), pltpu.VMEM((2, 8, W), F32),
                        pltpu.VMEM((2, L, L), BF16), pltpu.VMEM((2, W, 2 * W), BF16), pltpu.VMEM((W, W), BF16)],
        compiler_params=_cparams("parallel", "arbitrary"),
        name="mlstm",
    )(z, z, z, z, z, z, gbias)


def _recur_kernel(*refs, cps):
    r_in, m_in = refs[0:12], refs[12:19]
    r_out, m_out = refs[19:23], refs[23:25]
    r_scr, m_scr = refs[25:28], refs[28:34]
    _run_stages(_rwkv_stages(*r_in, *r_out, *r_scr, cps=cps),
                _mlstm_stages(*m_in, *m_out, *m_scr, cps=cps))


def _recur_call(z, wup, aup, gup, w0, a0, kk, ka, rk, gbias, *, n_ctx, n_all):
    B, Ta, _ = z.shape
    W = GROUP_W
    cps = _chunks_per_step(n_ctx, n_all)
    rows = cps * CHUNK
    gc, ga = n_ctx // cps, n_all // cps
    full = lambda a: pl.BlockSpec(a.shape, lambda b, i: (0,) * a.ndim)
    fwd = lambda blk: (lambda b, i: (b, i, blk))
    bwd = lambda blk: (lambda b, i: (b, _bwd_chunk(i, gc, ga), blk))
    tok = lambda width, imap: pl.BlockSpec((1, rows, width), imap)
    out = jax.ShapeDtypeStruct((B, Ta, W), F32)
    return pl.pallas_call(
        functools.partial(_recur_kernel, cps=cps),
        out_shape=(out,) * 6,
        grid=(B, ga),
        in_specs=[tok(3 * W, fwd(Z_RKV // 3)), tok(3 * W, bwd(Z_RKV // 3)),
                  tok(W, fwd(Z_LORA)), tok(W, bwd(Z_LORA)),
                  full(wup), full(aup), full(gup), full(w0), full(a0), full(kk), full(ka), full(rk),
                  tok(2 * W, fwd(Z_MQK // 2)), tok(2 * W, bwd(Z_MQK // 2)),
                  tok(2 * W, fwd(Z_MVO // 2)), tok(2 * W, bwd(Z_MVO // 2)),
                  tok(W, fwd(Z_GATE)), tok(W, bwd(Z_GATE)), full(gbias)],
        out_specs=(tok(W, fwd(0)), tok(W, bwd(0)), tok(W, fwd(0)), tok(W, fwd(0)),
                   tok(W, fwd(0)), tok(W, bwd(0))),
        scratch_shapes=[pltpu.VMEM((2, W, W), F32), pltpu.VMEM((2, rows, rows), BF16), pltpu.VMEM((W, W), BF16),
                        pltpu.VMEM((2, W, W), F32), pltpu.VMEM((2, 8, W), F32), pltpu.VMEM((2, 8, W), F32),
                        pltpu.VMEM((2, rows, rows), BF16), pltpu.VMEM((2, W, 2 * W), BF16),
                        pltpu.VMEM((W, W), BF16)],
        compiler_params=_cparams("parallel", "arbitrary"),
        name="rwkv_mlstm",
    )(z, z, z, z, wup, aup, gup, w0, a0, kk, ka, rk, z, z, z, z, z, z, gbias)


def _head_norm(y, bd, eps):
    bdb = bd.astype(BF16)

    def head_mean(x):
        hi, lo = _split2(x)
        return (_nn(hi, bdb) + _nn(lo, bdb)) * (1.0 / HEAD_DIM)

    yc = y - head_mean(y)
    return yc * lax.rsqrt(head_mean(yc * yc) + eps)


def _outproj_kernel(x_ref, mods_ref, s5f_ref, s5b_ref, u_ref, at_ref, rf_ref, rb_ref, rg_ref, rbon_ref,
                    mf_ref, mb_ref, vo_ref, s5d_ref, wglu_ref, bglu_ref, lnw_ref, lnb_ref, mnw_ref,
                    gpost_ref, wout_ref, o_ref, *, tm, tc, n_batch):
    b = pl.program_id(0)
    t = pl.program_id(1)
    d = x_ref.shape[-1]
    W = GROUP_W
    is_ctx = (t * tm + _iota((tm, 1), 0)) < tc
    gate = jnp.where(is_ctx, mods_ref[pl.ds(n_batch, 1), pl.ds(5 * d, d)],
                     mods_ref[pl.ds(b, 1), pl.ds(5 * d, d)])
    bd = _block_diag_mask()

    y = s5f_ref[...] + s5b_ref[...] + s5d_ref[...] * u_ref[0]
    zg = 0.5 * y * (1.0 + jnp.tanh(math.sqrt(2.0 / math.pi) * (y + 0.044715 * (y * y * y))))
    s5o = zg * _sigmoid(_mm1(_nn, zg, wglu_ref[...]) + bglu_ref[...])

    yr = _head_norm(rf_ref[0] + rb_ref[0], bd, RWKV_GN_EPS)
    rwo = (yr * lnw_ref[...] + lnb_ref[...] + rbon_ref[0]) * rg_ref[0]

    ym = _head_norm(mf_ref[0] + mb_ref[0], bd, NORM_EPS)
    mlo = ym * mnw_ref[...] * _sigmoid(vo_ref[0][:, W:2 * W])

    cat = jnp.concatenate([s5o, at_ref[0], rwo, mlo], axis=1).astype(BF16)
    yx = _nn(cat, wout_ref[...])
    o_ref[0] = x_ref[0] + gate * _rms(yx, gpost_ref[...])


def _outproj_call(xa, mods, s5f, s5b, z, attn, rf, rb, rg, rbon, mf, mb,
                  s5d, wglu, bglu, lnw, lnb, mnw, gpost, wout, *, tc):
    B, Ta, D = xa.shape
    W = GROUP_W
    tm = _pick_tile(Ta, 544)
    kern = functools.partial(_outproj_kernel, tm=tm, tc=tc, n_batch=B)
    full = lambda a: pl.BlockSpec(a.shape, lambda b, t: (0,) * a.ndim)
    tok = pl.BlockSpec((1, tm, W), lambda b, t: (b, t, 0))
    tmaj = pl.BlockSpec((tm, W), lambda b, t: (t, b))
    return pl.pallas_call(
        kern,
        out_shape=jax.ShapeDtypeStruct((B, Ta, D), F32),
        grid=(B, Ta // tm),
        in_specs=[pl.BlockSpec((1, tm, D), lambda b, t: (b, t, 0)), full(mods),
                  tmaj, tmaj, pl.BlockSpec((1, tm, W), lambda b, t: (b, t, Z_S5)),
                  tok, tok, tok, tok, tok, tok, tok,
                  pl.BlockSpec((1, tm, 2 * W), lambda b, t: (b, t, Z_MVO // 2)),
                  full(s5d), full(wglu), full(bglu), full(lnw), full(lnb), full(mnw), full(gpost), full(wout)],
        out_specs=pl.BlockSpec((1, tm, D), lambda b, t: (b, t, 0)),
        compiler_params=_cparams("parallel", "parallel"),
        name="mix_out",
    )(xa, mods, s5f, s5b, z, attn, rf, rb, rg, rbon, mf, mb, z, s5d, wglu, bglu, lnw, lnb, mnw, gpost, wout)


def _rope_rotate_cols(w):
    h = ROPE_AXIS // 2
    return jnp.concatenate([-w[..., h:2 * h], w[..., 0:h], -w[..., 3 * h:4 * h], w[..., 2 * h:3 * h]], axis=-1)


def _inproj_relayout(w_in):
    L, D, _ = w_in.shape
    o_s5, o_mla, o_rw, o_ml = 0, 256, 672, 1568
    seg = lambda a, n: w_in[:, :, a:a + n]
    zer = lambda n: jnp.zeros((L, D, n), w_in.dtype)
    k_rope = seg(o_mla + 384, ROPE_DIM)
    parts = [seg(o_mla, 256), seg(o_mla + 256, 128),
             zer(64), k_rope, zer(32),
             zer(64), _rope_rotate_cols(k_rope), zer(32), zer(128),
             seg(o_rw, 768), seg(o_s5, 256),
             seg(o_rw + 768, 128), zer(128),
             seg(o_ml, 512), seg(o_ml + 512, 512), seg(o_ml + 1024, 16), zer(ZB - 16)]
    out = jnp.concatenate(parts, axis=2)
    assert out.shape[2] == Z_COLS
    return out


def _rope_tables(T, tc):
    rows = T // GRID_W
    r_idx, c_idx = jnp.meshgrid(jnp.arange(rows), jnp.arange(GRID_W), indexing='ij')
    inv_freq = 1.0 / (ROPE_BASE ** (jnp.arange(0, ROPE_AXIS, 2, dtype=F32) / ROPE_AXIS))
    ang_r = r_idx.reshape(-1, 1).astype(F32) * inv_freq
    ang_c = c_idx.reshape(-1, 1).astype(F32) * inv_freq
    ang = jnp.concatenate([ang_r, ang_r, ang_c, ang_c], axis=-1)
    cos = jnp.concatenate([jnp.ones((tc, ROPE_DIM), F32), jnp.cos(ang)], axis=0)
    sin = jnp.concatenate([jnp.zeros((tc, ROPE_DIM), F32), jnp.sin(ang)], axis=0)
    ta = T + tc
    cos_t = jnp.concatenate([jnp.ones((ta, 64), F32), cos, jnp.zeros((ta, 32), F32)], axis=1)
    sin_t = jnp.concatenate([jnp.zeros((ta, 64), F32), sin, jnp.zeros((ta, 32), F32)], axis=1)
    return cos_t, sin_t


def _pad_rows(w, r0, total):
    pad = [(0, 0)] * (w.ndim - 2) + [(r0, total - r0 - w.shape[-2]), (0, 0)]
    return jnp.pad(w, pad)


def kernel(x, c, ctx, c_ctx, w_ada, b_ada, norm_pre, norm_post, ffn_w_gate, ffn_w_up, ffn_w_down, w_in, w_out, s5_lam_re, s5_lam_im, s5_log_dt, s5_b_re, s5_b_im, s5_c_re, s5_c_im, s5_d, s5_w_glu, s5_b_glu, mla_q_norm, mla_kv_norm, mla_w_uq, mla_w_ukv, rwkv_conv_w, rwkv_conv_b, rwkv_w0, rwkv_w_up, rwkv_a0, rwkv_a_up, rwkv_g_up, rwkv_k_k, rwkv_k_a, rwkv_r_k, rwkv_ln_w, rwkv_ln_b, mlstm_conv_w, mlstm_conv_b, mlstm_gate_b, mlstm_norm):
    B, T, D = x.shape
    Tc = ctx.shape[1]
    Ta = T + Tc
    L = w_ada.shape[0]
    W = GROUP_W
    assert T % CHUNK == 0 and Tc % CHUNK == 0 and B % 8 == 0 and B <= 8
    n_ctx, n_all = Tc // CHUNK, Ta // CHUNK

    rows = 16
    cvec = jnp.concatenate([c, c_ctx[None, :], jnp.zeros((rows - B - 1, D), F32)], axis=0)
    mods_all = _ada_call(cvec, w_ada, b_ada)

    w_in_re = _inproj_relayout(w_in).astype(BF16)
    cw = jnp.zeros((L, Z_NBLK, 8, ZB), F32)
    rc = jnp.concatenate([rwkv_conv_w, rwkv_conv_b[:, None, :]], axis=1).reshape(L, 4, 3, ZB).transpose(0, 2, 1, 3)
    mc = jnp.concatenate([mlstm_conv_w, mlstm_conv_b[:, None, :]], axis=1).reshape(L, 4, 2, ZB).transpose(0, 2, 1, 3)
    cw = cw.at[:, Z_RKV:Z_RKV + 3, 0:4].set(rc).at[:, Z_MQK:Z_MQK + 2, 0:4].set(mc)

    wg = ffn_w_gate.astype(BF16)
    wu = ffn_w_up.astype(BF16)
    wd = ffn_w_down.astype(BF16)
    wout = w_out.astype(BF16)

    G = s5_lam_re.shape[2]
    N = s5_lam_re.shape[3]
    eye_g = jnp.eye(G, dtype=F32)
    lre = s5_lam_re.reshape(L, 2, 1, G * N)
    lim = s5_lam_im.reshape(L, 2, 1, G * N)
    ldt = jnp.repeat(s5_log_dt, N, axis=-1).reshape(L, 2, 1, G * N)
    wbre = jnp.einsum('ldgnp,gh->ldgphn', s5_b_re, eye_g).reshape(L, 2, G * S5_P, G * N)
    wbim = jnp.einsum('ldgnp,gh->ldgphn', s5_b_im, eye_g).reshape(L, 2, G * S5_P, G * N)
    wcre = jnp.einsum('ldgpn,gh->ldgnhp', s5_c_re, eye_g).reshape(L, 2, G * N, G * S5_P).astype(BF16)
    wcim = jnp.einsum('ldgpn,gh->ldgnhp', s5_c_im, eye_g).reshape(L, 2, G * N, G * S5_P).astype(BF16)

    nope = HEAD_DIM
    qd = nope + ROPE_DIM
    wq4 = mla_w_uq.reshape(L, -1, N_HEADS, qd)
    wq = jnp.pad(wq4, ((0, 0), (0, 0), (0, 0), (0, 128 - qd))).reshape(L, -1, N_HEADS * 128).astype(BF16)
    wq_rot = _rope_rotate_cols(wq4[..., nope:])
    wqr = jnp.pad(wq_rot, ((0, 0), (0, 0), (0, 0), (nope, 128 - qd))).reshape(L, -1, N_HEADS * 128).astype(BF16)
    wkv4 = mla_w_ukv.reshape(L, -1, N_HEADS, 2 * HEAD_DIM)
    wk = jnp.pad(wkv4[..., :HEAD_DIM], ((0, 0), (0, 0), (0, 0), (0, 64))).reshape(L, -1, N_HEADS * 128).astype(BF16)
    wv = jnp.pad(wkv4[..., HEAD_DIM:], ((0, 0), (0, 0), (0, 0), (0, 64))).reshape(L, -1, N_HEADS * 128).astype(BF16)
    cos_t, sin_t = _rope_tables(T, Tc)
    scale = float(qd) ** -0.5 * math.log2(math.e)

    wup = _pad_rows(rwkv_w_up, 0, W)
    aup = _pad_rows(rwkv_a_up, 32, W)
    gup = _pad_rows(rwkv_g_up, 64, W)
    gbias = jnp.pad(mlstm_gate_b, ((0, 0), (0, W - mlstm_gate_b.shape[1])))

    xa = jnp.concatenate([ctx, x], axis=1)
    r1 = lambda a: a.reshape(1, -1)

    for l in range(L):
        mods = mods_all[l]
        xa2, hmix = _ffn_call(xa.reshape(B * Ta, D), mods, r1(norm_pre[l, 0]), r1(norm_post[l, 0]),
                              r1(norm_pre[l, 1]), wg[l, 0], wu[l, 0], wd[l, 0],
                              ta=Ta, tc=Tc, n_batch=B, koff=0, emit_hmix=True)
        xa = xa2.reshape(B, Ta, D)
        z = _inproj_call(hmix.reshape(B, Ta, D), w_in_re[l], cw[l], tc=Tc)

        s5f, s5b = _s5_call(z, lre[l], lim[l], ldt[l], wbre[l], wbim[l], wcre[l], wcim[l],
                            n_ctx=n_ctx, n_all=n_all)

        q, k, v = _mla_proj_call(z, r1(mla_q_norm[l]), r1(mla_kv_norm[l]), wq[l], wqr[l], wk[l], wv[l],
                                 cos_t, sin_t, scale=scale)
        attn = _attn_call(q, k, v, tc=Tc)

        rf, rb, rg, rbon, mf, mb = _recur_call(
            z, wup[l], aup[l], gup[l], rwkv_w0[l][:, None, :], rwkv_a0[l][:, None, :],
            r1(rwkv_k_k[l]), r1(rwkv_k_a[l]), r1(rwkv_r_k[l]), gbias[l:l + 1], n_ctx=n_ctx, n_all=n_all)

        xa = _outproj_call(xa, mods, s5f, s5b, z, attn, rf, rb, rg, rbon,
                           mf, mb, r1(s5_d[l]), s5_w_glu[l].astype(BF16), r1(s5_b_glu[l]), r1(rwkv_ln_w[l]),
                           r1(rwkv_ln_b[l]), r1(mlstm_norm[l]), r1(norm_post[l, 1]), wout[l], tc=Tc)

        xa2, _ = _ffn_call(xa.reshape(B * Ta, D), mods, r1(norm_pre[l, 2]), r1(norm_post[l, 2]),
                           r1(norm_pre[l, 1]), wg[l, 1], wu[l, 1], wd[l, 1],
                           ta=Ta, tc=Tc, n_batch=B, koff=6, emit_hmix=False)
        xa = xa2.reshape(B, Ta, D)

    return xa[:, Tc:, :]
```

```python
import functools
import math

import numpy as np
import jax
import jax.numpy as jnp
from jax import lax
from jax.experimental import pallas as pl
from jax.experimental.pallas import tpu as pltpu

F32 = jnp.float32
BF16 = jnp.bfloat16

GROUP_W = 256
HEAD_DIM = 64
N_HEADS = GROUP_W // HEAD_DIM
CHUNK = 64
S5_CHUNK = 128
N_MOD = 9
NORM_EPS = 1e-6
RWKV_GN_EPS = HEAD_DIM * 1e-5
GRID_W = 64
ROPE_BASE = 10000.0
ROPE_DIM = 32
ROPE_AXIS = 16
S5_P = 16
S5_STATE = 64
MACARON = 0.5
VMEM_LIMIT_BYTES = 56 * 1024 * 1024

ZB = 256
Z_MLA, Z_RKV, Z_S5, Z_LORA, Z_MQK, Z_MVO = 0, 3, 6, 7, 8, 10
Z_GATE = 2
GATE_LANE0 = 128
Z_NBLK = 12
Z_COLS = Z_NBLK * ZB


def _nn(a, b):
    return lax.dot_general(a, b, (((1,), (0,)), ((), ())), preferred_element_type=F32)


def _nt(a, b):
    return lax.dot_general(a, b, (((1,), (1,)), ((), ())), preferred_element_type=F32)


def _tn(a, b):
    return lax.dot_general(a, b, (((0,), (0,)), ((), ())), preferred_element_type=F32)


def _split2(x):
    hi = x.astype(BF16)
    lo = (x - hi.astype(F32)).astype(BF16)
    return hi, lo


def _split3(x):
    p1 = x.astype(BF16)
    r = x - p1.astype(F32)
    p2 = r.astype(BF16)
    p3 = (r - p2.astype(F32)).astype(BF16)
    return p1, p2, p3


def _mm3(dotf, a, b):
    ah, al = _split2(a)
    bh, bl = _split2(b)
    return dotf(ah, bh) + (dotf(ah, bl) + dotf(al, bh))


def _mm1(dotf, a, b):
    return dotf(a.astype(BF16), b.astype(BF16))


_mmn = _mm1


def _mm_exact_rhs(a, e):
    eb = e.astype(BF16)
    p1, p2, p3 = _split3(a)
    return _nn(p1, eb) + (_nn(p2, eb) + _nn(p3, eb))


def _mm_exact_lhs(e, a):
    eb = e.astype(BF16)
    p1, p2, p3 = _split3(a)
    return _nn(eb, p1) + (_nn(eb, p2) + _nn(eb, p3))


def _rms(x, g):
    return x * lax.rsqrt(jnp.mean(x * x, axis=-1, keepdims=True) + NORM_EPS) * g


def _sigmoid(x):
    return 0.5 * jnp.tanh(0.5 * x) + 0.5


def _group_tri_masks(tri_scr, rows):
    r = _iota((rows, rows), 0)
    c = _iota((rows, rows), 1)
    same = (r // CHUNK) == (c // CHUNK)
    tri_scr[0] = (same & (c <= r)).astype(BF16)
    tri_scr[1] = (same & (c >= r)).astype(BF16)


def _softplus(x):
    return jnp.maximum(x, 0.0) + jnp.log(1.0 + jnp.exp(-jnp.abs(x)))


def _iota(shape, dim):
    return lax.broadcasted_iota(jnp.int32, shape, dim)


def _head_masks():
    lane = _iota((1, GROUP_W), 1)
    return [lane // HEAD_DIM == h for h in range(N_HEADS)]


def _block_diag_mask():
    r = _iota((GROUP_W, GROUP_W), 0) // HEAD_DIM
    c = _iota((GROUP_W, GROUP_W), 1) // HEAD_DIM
    return (r == c).astype(F32)


def _row_stack(x, hms):
    xb = x.astype(BF16)
    return jnp.concatenate([jnp.where(m, xb, 0.0) for m in hms], axis=0)


def _pick_tile(n, target, mult=16):
    best = None
    for t in range(mult, min(n, target) + 1, mult):
        if n % t == 0:
            best = t
    if best is None:
        raise ValueError(f"no tile for {n}")
    return best


def _cparams(*sem):
    return pltpu.CompilerParams(dimension_semantics=sem, vmem_limit_bytes=VMEM_LIMIT_BYTES)


def _ada_kernel(c_ref, w_ref, b_ref, o_ref):
    c = c_ref[...]
    s = c * _sigmoid(c)
    o_ref[0] = _mm3(_nn, s, w_ref[0]) + b_ref[0]


def _ada_call(cvec, w_ada, b_ada):
    L, D, N = w_ada.shape
    R = cvec.shape[0]
    tn = _pick_tile(N, 1152, 128)
    return pl.pallas_call(
        _ada_kernel,
        out_shape=jax.ShapeDtypeStruct((L, R, N), F32),
        grid=(L, N // tn),
        in_specs=[pl.BlockSpec((R, D), lambda l, j: (0, 0)),
                  pl.BlockSpec((1, D, tn), lambda l, j: (l, 0, j)),
                  pl.BlockSpec((1, 1, tn), lambda l, j: (l, 0, j))],
        out_specs=pl.BlockSpec((1, R, tn), lambda l, j: (l, 0, j)),
        compiler_params=_cparams("parallel", "parallel"),
        name="ada_mod",
    )(cvec, w_ada, b_ada.reshape(L, 1, N))


FFN_COLS = 256


def _ffn_kernel(xn_ref, xp_ref, mods_ref, gpre_ref, gpost_ref, gmix_ref, wg_ref, wu_ref, wd_ref,
                o_ref, hmix_ref, h_scr, acc_scr, *, tm, tiles_per_batch, tc, n_batch, koff, emit_hmix,
                n_tiles, f_split):
    i = pl.program_id(0)
    f = pl.program_id(1)
    d = xn_ref.shape[-1]
    fd = wg_ref.shape[1]

    def mod_of(tile):
        b = tile // tiles_per_batch
        t0 = (tile % tiles_per_batch) * tm
        is_ctx = (t0 + _iota((tm, 1), 0)) < tc

        def mod(fn):
            row = lambda r: (lambda k: mods_ref[pl.ds(r, 1), pl.ds(k * d, d)])
            return jnp.where(is_ctx, fn(row(n_batch)), fn(row(b)))
        return mod

    unit = lambda x: x * lax.rsqrt(jnp.mean(x * x, axis=-1, keepdims=True) + NORM_EPS)

    def pre_norm(tile, x):
        mod = mod_of(tile)
        gain = mod(lambda m: gpre_ref[...] * (1.0 + m(koff + 1)))
        return (unit(x) * gain + mod(lambda m: m(koff))).astype(BF16)

    def finish(tile, acc, x):
        mod = mod_of(tile)
        xn = x + unit(acc) * mod(lambda m: (MACARON * m(koff + 2)) * gpost_ref[...])
        o_ref[...] = xn
        if emit_hmix:
            gain = mod(lambda m: gmix_ref[...] * (1.0 + m(4)))
            hmix_ref[...] = (unit(xn) * gain + mod(lambda m: m(3))).astype(BF16)
        else:
            hmix_ref[...] = jnp.zeros_like(hmix_ref)

    def hidden_cols(hb, lo, hi):
        acc = None
        for c0 in range(lo, hi, FFN_COLS):
            g = _nn(hb, wg_ref[:, c0:c0 + FFN_COLS])
            u = _nn(hb, wu_ref[:, c0:c0 + FFN_COLS])
            t = _nn((g * _sigmoid(g) * u).astype(BF16), wd_ref[c0:c0 + FFN_COLS, :])
            acc = t if acc is None else acc + t
        return acc

    @pl.when(jnp.logical_and(i == 0, f == 0))
    def _():
        h_scr[0] = pre_norm(0, xn_ref[...])
        acc_scr[1] = jnp.zeros((tm, d), F32)

    for s in range(2):
        tile = 2 * i + s

        @pl.when(jnp.logical_and(tile < n_tiles, f == 2 * s))
        def _(s=s, tile=tile):
            finish(jnp.maximum(tile - 1, 0), acc_scr[1 - s], xp_ref[...])
            acc_scr[s] = hidden_cols(h_scr[s], 0, f_split)

        @pl.when(jnp.logical_and(tile < n_tiles, f == 2 * s + 1))
        def _(s=s, tile=tile):
            h_scr[1 - s] = pre_norm(jnp.minimum(tile + 1, n_tiles - 1), xn_ref[...])
            acc_scr[s] += hidden_cols(h_scr[s], f_split, fd)

    @pl.when(jnp.logical_and(2 * i == n_tiles, f == 0))
    def _():
        finish(n_tiles - 1, acc_scr[1], xp_ref[...])


def _ffn_call(xa2, mods, gpre, gpost, gmix, wg, wu, wd, *, ta, tc, n_batch, koff, emit_hmix):
    M, D = xa2.shape
    Fd = wg.shape[1]
    assert Fd % FFN_COLS == 0
    tm = _pick_tile(ta, 544)
    n_tiles = M // tm
    assert n_tiles % 2 == 0
    f_split = ((Fd // FFN_COLS + 1) // 2) * FFN_COLS
    kern = functools.partial(_ffn_kernel, tm=tm, tiles_per_batch=ta // tm, tc=tc, n_batch=n_batch,
                             koff=koff, emit_hmix=emit_hmix, n_tiles=n_tiles, f_split=f_split)
    hm_rows = tm if emit_hmix else 16
    last = n_tiles - 1
    resident = lambda a: pl.BlockSpec(a.shape, lambda i, f: (0, 0), pipeline_mode=pl.Buffered(1))
    vec = pl.BlockSpec((1, D), lambda i, f: (0, 0))
    return pl.pallas_call(
        kern,
        out_shape=(jax.ShapeDtypeStruct((M, D), F32),
                   jax.ShapeDtypeStruct((M if emit_hmix else 16 * n_tiles, D), BF16)),
        grid=(n_tiles // 2 + 1, 4),
        in_specs=[pl.BlockSpec((tm, D), lambda i, f: (jnp.minimum(2 * i + (f + 1) // 2, last), 0)),
                  pl.BlockSpec((tm, D), lambda i, f: (jnp.clip(2 * i - 1 + f // 2, 0, last), 0)),
                  pl.BlockSpec(mods.shape, lambda i, f: (0, 0)),
                  vec, vec, vec, resident(wg), resident(wu), resident(wd)],
        out_specs=(pl.BlockSpec((tm, D), lambda i, f: (jnp.clip(2 * i - 1 + f // 2, 0, last), 0)),
                   pl.BlockSpec((hm_rows, D), lambda i, f: (jnp.clip(2 * i - 1 + f // 2, 0, last), 0))),
        scratch_shapes=[pltpu.VMEM((2, tm, D), BF16), pltpu.VMEM((2, tm, D), F32)],
        compiler_params=_cparams("arbitrary", "arbitrary"),
        name="half_ffn",
    )(xa2, xa2, mods, gpre, gpost, gmix, wg, wu, wd)


def _inproj_kernel(h_ref, w_ref, cw_ref, o_ref, *, tc, conv_lo, conv_hi, silu_lo, silu_hi):
    nb = pl.program_id(1)
    ta = h_ref.shape[1]
    is_conv = ((nb >= conv_lo[0]) & (nb < conv_hi[0])) | ((nb >= conv_lo[1]) & (nb < conv_hi[1]))
    is_silu = (nb >= silu_lo) & (nb < silu_hi)

    @pl.when(jnp.logical_not(is_conv))
    def _():
        o_ref[0] = _nn(h_ref[0], w_ref[...])

    def conv():
        z = _nn(h_ref[0], w_ref[...])
        row = _iota((ta, 1), 0)
        zp = jnp.where((row == 0) | (row == tc), 0.0, pltpu.roll(z, 1, 0))
        zn = jnp.where((row == tc - 1) | (row == ta - 1), 0.0, pltpu.roll(z, ta - 1, 0))
        cw = cw_ref[0]
        return cw[3:4] + zp * cw[0:1] + z * cw[1:2] + zn * cw[2:3]

    @pl.when(is_conv & jnp.logical_not(is_silu))
    def _():
        o_ref[0] = conv()

    @pl.when(is_conv & is_silu)
    def _():
        y = conv()
        o_ref[0] = y * _sigmoid(y)


def _inproj_call(hmix3, w_re, cw, *, tc):
    B, Ta, D = hmix3.shape
    kern = functools.partial(_inproj_kernel, tc=tc, conv_lo=(Z_RKV, Z_MQK), conv_hi=(Z_RKV + 3, Z_MQK + 2),
                             silu_lo=Z_MQK, silu_hi=Z_MQK + 2)
    return pl.pallas_call(
        kern,
        out_shape=jax.ShapeDtypeStruct((B, Ta, Z_COLS), F32),
        grid=(B, Z_NBLK),
        in_specs=[pl.BlockSpec((1, Ta, D), lambda b, n: (b, 0, 0)),
                  pl.BlockSpec((D, ZB), lambda b, n: (0, n)),
                  pl.BlockSpec((1, 8, ZB), lambda b, n: (n, 0, 0))],
        out_specs=pl.BlockSpec((1, Ta, ZB), lambda b, n: (b, 0, n)),
        compiler_params=_cparams("parallel", "arbitrary"),
        name="in_proj",
    )(hmix3, w_re, cw)


def _bwd_chunk(i, n_ctx, n_all):
    return jnp.where(i < n_ctx, n_ctx - 1 - i, n_all - 1 - (i - n_ctx))


def _s5_kernel(uf_ref, ub_ref, lre_ref, lim_ref, ldt_ref, wbre_ref, wbim_ref, wcre_ref, wcim_ref,
               yf_ref, yb_ref, wb_scr, coef_scr, st_scr, rel_scr, x_scr, *, lc, nb, scan_unroll):
    i = pl.program_id(0)
    gn = lre_ref.shape[-1]

    @pl.when(i == 0)
    def _():
        for d in range(2):
            dt = jnp.exp(ldt_ref[d])
            lre = lre_ref[d]
            lim = lim_ref[d]
            mag = jnp.exp(lre * dt)
            ar = mag * jnp.cos(lim * dt)
            ai = mag * jnp.sin(lim * dt)
            den = lre * lre + lim * lim
            fr = ((ar - 1.0) * lre + ai * lim) / den
            fi = (ai * lre - (ar - 1.0) * lim) / den
            coef_scr[d, 0:nb, :] = jnp.broadcast_to(ar, (nb, gn))
            coef_scr[d, nb:2 * nb, :] = jnp.broadcast_to(ai, (nb, gn))
            wre = wbre_ref[d]
            wim = wbim_ref[d]
            wb_scr[d, :, 0:gn] = (wre * fr - wim * fi).astype(BF16)
            wb_scr[d, :, gn:2 * gn] = (wim * fr + wre * fi).astype(BF16)
        st_scr[...] = jnp.zeros_like(st_scr)

    half = 128
    u_refs = (uf_ref, ub_ref)
    y_refs = (yf_ref, yb_ref)

    for d in range(2):
        for b in range(nb):
            for s in range(2):
                rel_scr[d, s, pl.ds(b, lc, stride=nb), :] = u_refs[d][b, :, s * half:(s + 1) * half]
    for d in range(2):
        u_tm = jnp.concatenate([rel_scr[d, 0], rel_scr[d, 1]], axis=1).astype(BF16)
        x_scr[d] = _nn(u_tm, wb_scr[d])

    def body(t, carry):
        out = []
        for d in range(2):
            sr, si = carry[2 * d], carry[2 * d + 1]
            ar = coef_scr[d, 0:nb, :]
            ai = coef_scr[d, nb:2 * nb, :]
            tt = t if d == 0 else lc - 1 - t
            r0 = pl.multiple_of(tt * nb, nb)
            xr = x_scr[d, pl.ds(r0, nb), 0:gn]
            xi = x_scr[d, pl.ds(r0, nb), gn:2 * gn]
            nsr = ar * sr - ai * si + xr
            nsi = ar * si + ai * sr + xi
            x_scr[d, pl.ds(r0, nb), 0:gn] = nsr
            x_scr[d, pl.ds(r0, nb), gn:2 * gn] = nsi
            out += [nsr, nsi]
        return tuple(out)

    init = (st_scr[0, 0:nb, :], st_scr[0, nb:2 * nb, :], st_scr[1, 0:nb, :], st_scr[1, nb:2 * nb, :])
    fin = lax.fori_loop(0, lc, body, init, unroll=scan_unroll)
    for d in range(2):
        st_scr[d, 0:nb, :] = fin[2 * d]
        st_scr[d, nb:2 * nb, :] = fin[2 * d + 1]
        y = (_nn(x_scr[d, :, 0:gn].astype(BF16), wcre_ref[d])
             - _nn(x_scr[d, :, gn:2 * gn].astype(BF16), wcim_ref[d]))
        rel_scr[d, 0] = y[:, 0:half]
        rel_scr[d, 1] = y[:, half:2 * half]
    for d in range(2):
        for b in range(nb):
            for s in range(2):
                c0 = b * 2 * half + s * half
                y_refs[d][:, c0:c0 + half] = rel_scr[d, s, pl.ds(b, lc, stride=nb), :]


def _s5_call(z, lre, lim, ldt, wbre, wbim, wcre, wcim, *, n_ctx, n_all):
    B, Ta, _ = z.shape
    W = GROUP_W
    lc = S5_CHUNK
    assert (n_ctx * CHUNK) % lc == 0 and (n_all * CHUNK) % lc == 0
    n_ctx, n_all = n_ctx * CHUNK // lc, n_all * CHUNK // lc
    gn = lre.shape[-1]
    blk = lc * B
    kern = functools.partial(_s5_kernel, lc=lc, nb=B, scan_unroll=True)
    full = lambda a: pl.BlockSpec(a.shape, lambda i: (0,) * a.ndim)
    out = jax.ShapeDtypeStruct((Ta, B * W), F32)
    return pl.pallas_call(
        kern,
        out_shape=(out, out),
        grid=(n_all,),
        in_specs=[pl.BlockSpec((B, lc, W), lambda i: (0, i, Z_S5)),
                  pl.BlockSpec((B, lc, W), lambda i: (0, _bwd_chunk(i, n_ctx, n_all), Z_S5)),
                  full(lre), full(lim), full(ldt), full(wbre), full(wbim), full(wcre), full(wcim)],
        out_specs=(pl.BlockSpec((lc, B * W), lambda i: (i, 0)),
                   pl.BlockSpec((lc, B * W), lambda i: (_bwd_chunk(i, n_ctx, n_all), 0))),
        scratch_shapes=[pltpu.VMEM((2, W, 2 * gn), BF16),
                        pltpu.VMEM((2, 2 * B, gn), F32),
                        pltpu.VMEM((2, 2 * B, gn), F32),
                        pltpu.VMEM((2, 2, blk, 128), F32),
                        pltpu.VMEM((2, blk, 2 * gn), F32)],
        compiler_params=_cparams("arbitrary"),
        name="s5_scan",
    )(z, z, lre, lim, ldt, wbre, wbim, wcre, wcim)


def _mla_proj_kernel(z_ref, qn_ref, kvn_ref, wq_ref, wqr_ref, wk_ref, wv_ref, cos_ref, sin_ref,
                     q_ref, k_ref, v_ref, *, scale):
    z = z_ref[0]
    cq = z[:, 0:256]
    ckv = z[:, 256:384]
    kr = z[:, 384:512]
    krr = z[:, 512:640]
    cos = cos_ref[...]
    sin = sin_ref[...]
    cqb = _rms(cq, qn_ref[...]).astype(BF16)
    ckvb = _rms(ckv, kvn_ref[...]).astype(BF16)
    q = _nn(cqb, wq_ref[...])
    qr = _nn(cqb, wqr_ref[...])
    kn = _nn(ckvb, wk_ref[...])
    krp = kr * cos + krr * sin
    for h in range(N_HEADS):
        sl = slice(h * 128, (h + 1) * 128)
        q_ref[0, h] = ((q[:, sl] * cos + qr[:, sl] * sin) * scale).astype(BF16)
        k_ref[0, h] = (kn[:, sl] + krp).astype(BF16)
    ones_pad = ((_iota((1, N_HEADS * 128), 1) % 128) >= HEAD_DIM).astype(F32)
    vv = _nn(ckvb, wv_ref[...]) + ones_pad
    for h in range(N_HEADS):
        v_ref[0, h] = vv[:, h * 128:(h + 1) * 128].astype(BF16)


def _mla_proj_call(z, qn, kvn, wq, wqr, wk, wv, cos_t, sin_t, *, scale):
    B, Ta, _ = z.shape
    tm = _pick_tile(Ta, 544)
    kern = functools.partial(_mla_proj_kernel, scale=scale)
    full = lambda a: pl.BlockSpec(a.shape, lambda b, t: (0,) * a.ndim)
    return pl.pallas_call(
        kern,
        out_shape=(jax.ShapeDtypeStruct((B, N_HEADS, Ta, 128), BF16),
                   jax.ShapeDtypeStruct((B, N_HEADS, Ta, 128), BF16),
                   jax.ShapeDtypeStruct((B, N_HEADS, Ta, 128), BF16)),
        grid=(B, Ta // tm),
        in_specs=[pl.BlockSpec((1, tm, 3 * ZB), lambda b, t: (b, t, 0)),
                  full(qn), full(kvn), full(wq), full(wqr), full(wk), full(wv),
                  pl.BlockSpec((tm, 128), lambda b, t: (t, 0)),
                  pl.BlockSpec((tm, 128), lambda b, t: (t, 0))],
        out_specs=(pl.BlockSpec((1, N_HEADS, tm, 128), lambda b, t: (b, 0, t, 0)),
                   pl.BlockSpec((1, N_HEADS, tm, 128), lambda b, t: (b, 0, t, 0)),
                   pl.BlockSpec((1, N_HEADS, tm, 128), lambda b, t: (b, 0, t, 0))),
        compiler_params=_cparams("parallel", "parallel"),
        name="mla_proj",
    )(z, qn, kvn, wq, wqr, wk, wv, cos_t, sin_t)


def _attn_kernel(q_ref, k_ref, v_ref, o_ref, *, tc, ta, n_ctx_tiles, key_chunk):
    i = pl.program_id(1)
    lane = _iota((1, 128), 1)

    def write_out(acc):
        outs = [a * (1.0 / pltpu.roll(a, HEAD_DIM, 1)) for a in acc]
        for hp in range(N_HEADS // 2):
            o_ref[0, :, hp * 128:(hp + 1) * 128] = jnp.where(lane < HEAD_DIM, outs[2 * hp],
                                                             pltpu.roll(outs[2 * hp + 1], HEAD_DIM, 1))

    @pl.when(i < n_ctx_tiles)
    def _():
        acc = []
        for j in range(N_HEADS):
            s = _nt(q_ref[0, j], k_ref[0, j, 0:tc, :])
            p = jnp.exp2(s - jnp.max(s, axis=-1, keepdims=True))
            acc.append(_nn(p.astype(BF16), v_ref[0, j, 0:tc, :]))
        write_out(acc)

    @pl.when(i >= n_ctx_tiles)
    def _():
        bounds = [0, tc] + list(range(tc + key_chunk, ta + 1, key_chunk))
        units = [(c, j) for c in range(len(bounds) - 1) for j in range(N_HEADS)]
        qs = [q_ref[0, j] for j in range(N_HEADS)]
        score = lambda c, j: _nt(qs[j], k_ref[0, j, bounds[c]:bounds[c + 1], :])
        m = [None] * N_HEADS
        acc = [None] * N_HEADS

        def weighted_values(pend):
            c, j, pb, alpha = pend
            pv = _nn(pb, v_ref[0, j, bounds[c]:bounds[c + 1], :])
            acc[j] = pv if alpha is None else alpha * acc[j] + pv

        pending = None
        s_next = score(*units[0])
        for idx, (c, j) in enumerate(units):
            s = s_next
            if idx + 1 < len(units):
                s_next = score(*units[idx + 1])
            mc = jnp.max(s, axis=-1, keepdims=True)
            if c == 0:
                alpha = None
                m[j] = mc
                p = jnp.exp2(s - mc)
            else:
                m_new = jnp.maximum(m[j], mc)
                alpha = jnp.exp2(m[j] - m_new)
                p = jnp.exp2(s - m_new)
                m[j] = m_new
            if pending is not None:
                weighted_values(pending)
            pending = (c, j, p.astype(BF16), alpha)
        weighted_values(pending)
        write_out(acc)


def _attn_call(q, k, v, *, tc):
    B, H, Ta, _ = q.shape
    tq = _pick_tile(math.gcd(tc, Ta), 256)
    kern = functools.partial(_attn_kernel, tc=tc, ta=Ta, n_ctx_tiles=tc // tq,
                             key_chunk=_pick_tile(Ta - tc, 1024, 128))
    return pl.pallas_call(
        kern,
        out_shape=jax.ShapeDtypeStruct((B, Ta, GROUP_W), F32),
        grid=(B, Ta // tq),
        in_specs=[pl.BlockSpec((1, H, tq, 128), lambda b, i: (b, 0, i, 0)),
                  pl.BlockSpec((1, H, Ta, 128), lambda b, i: (b, 0, 0, 0)),
                  pl.BlockSpec((1, H, Ta, 128), lambda b, i: (b, 0, 0, 0))],
        out_specs=pl.BlockSpec((1, tq, GROUP_W), lambda b, i: (b, i, 0)),
        compiler_params=_cparams("parallel", "arbitrary"),
        name="mla_attn",
    )(q, k, v)


def _chunk_masks(d):
    L = CHUNK
    row = _iota((L, GROUP_W), 0)
    s_idx = _iota((L, GROUP_W), 1) % L
    tr = _iota((L, L), 0)
    tcol = _iota((L, L), 1)
    if d == 0:
        return (tcol <= tr).astype(F32), s_idx < row, s_idx <= row, s_idx == row
    return (tcol >= tr).astype(F32), s_idx > row, s_idx >= row, s_idx == row


def _run_stages(*generators):
    live = list(generators)
    while live:
        for g in list(live):
            try:
                next(g)
            except StopIteration:
                live.remove(g)


def _rwkv_kernel(*refs, cps):
    _run_stages(_rwkv_stages(*refs, cps=cps))


def _rwkv_stages(rf_ref, rb_ref, lf_ref, lb_ref, wup_ref, aup_ref, gup_ref, w0_ref, a0_ref,
                 kk_ref, ka_ref, rk_ref, yf_ref, yb_ref, g_ref, bon_ref, s_scr, tri_scr, bdb_scr, *, cps):
    i = pl.program_id(1)
    L = CHUNK
    W = GROUP_W

    @pl.when(i == 0)
    def _():
        s_scr[...] = jnp.zeros_like(s_scr)
        _group_tri_masks(tri_scr, cps * L)
        bdb_scr[...] = _block_diag_mask().astype(BF16)

    hms = _head_masks()
    bd = _block_diag_mask()
    bdb = bdb_scr[...]
    rs = lambda x: _row_stack(x, hms)
    bd_b = bd > 0.5
    bdiag = lambda x: jnp.where(bd_b, jnp.concatenate([x.astype(BF16)] * N_HEADS, axis=0), 0.0)

    per_dir = []
    for d, (r_ref, l_ref) in enumerate(((rf_ref, lf_ref), (rb_ref, lb_ref))):
        rkv = r_ref[0]
        lora = l_ref[0]
        r = rkv[:, 0:W]
        k = rkv[:, W:2 * W]
        v = rkv[:, 2 * W:3 * W]
        _, strict, incl, eye = _chunk_masks(d)
        tri = tri_scr[d]

        lw = -math.exp(-0.5) * _sigmoid(w0_ref[d] + _mm1(_nn, jnp.tanh(lora), wup_ref[d]))
        a = _sigmoid(a0_ref[d] + _mm1(_nn, lora, aup_ref[d]))
        kkv = k * kk_ref[...]
        kkn = kkv * lax.rsqrt(jnp.maximum(_mm1(_nn, kkv * kkv, bdb), 1e-24))
        keff = k * (1.0 + (a - 1.0) * ka_ref[...])
        kka = kkn * a
        lw_hi, lw_lo = _split2(lw)
        cum2 = _nn(tri, jnp.concatenate([lw_hi, lw_lo], axis=1))
        cum = cum2[:, 0:W] + cum2[:, W:2 * W]
        e_dn = jnp.exp(-cum)
        per_dir.append(dict(v=v, lw=lw, cum=cum, kka=kka, keff=keff, strict=strict, incl=incl, eye=eye,
                            al=-kkn * jnp.exp(cum - lw), rt=r * jnp.exp(cum), bh=kka * e_dn, kh=keff * e_dn))
        if d == 0:
            g_ref[0] = _mm1(_nn, _sigmoid(lora), gup_ref[...])
            bon_ref[0] = _mm1(_nn, r * k * rk_ref[...], bdb) * v
        yield

    chains = []
    for c in range(cps):
        for d in range(2):
            pd = per_dir[d]
            sl = slice(c * L, (c + 1) * L)
            tot = jnp.sum(pd["lw"][sl], axis=0, keepdims=True)
            e_tc = jnp.exp(tot - pd["cum"][sl])
            ch = dict(d=d, c=c, al=pd["al"][sl], rt=pd["rt"][sl], v=pd["v"][sl], tot=tot,
                      bt=pd["kka"][sl] * e_tc, kt=pd["keff"][sl] * e_tc)
            ch["rsv"] = rs(ch["v"])
            a_all = _mm1(_nt, jnp.concatenate([ch["al"], ch["rt"]], axis=0),
                         jnp.concatenate([rs(pd["bh"][sl]), rs(pd["kh"][sl])], axis=0))
            ch["a_ab"] = jnp.where(pd["strict"], a_all[0:L, 0:W], 0.0)
            ch["a_ak"] = jnp.where(pd["strict"], a_all[0:L, W:2 * W], 0.0)
            ch["a_rb"] = jnp.where(pd["incl"], a_all[L:2 * L, 0:W], 0.0)
            ch["a_rk"] = jnp.where(pd["incl"], a_all[L:2 * L, W:2 * W], 0.0)
            ch["p"] = jnp.where(pd["eye"], 1.0, 0.0) + ch["a_ab"]
            chains.append(ch)
        yield

    for ch in chains:
        ch["sq"] = _mmn(_nn, ch["a_ab"], bdiag(ch["a_ab"]))
        ch["zk"] = _mm1(_nn, ch["a_ak"], ch["rsv"])
        ch["y0k"] = _mm1(_nn, ch["a_rk"], ch["rsv"])
    yield
    n_sq = int(math.log2(L)) - 1
    for it in range(n_sq):
        for ch in chains:
            if it < n_sq - 1:
                ps = _mmn(_nn, jnp.concatenate([ch["p"], ch["sq"]], axis=0), bdiag(ch["sq"]))
                ch["p"] = ch["p"] + ps[0:L]
                ch["sq"] = ps[L:2 * L]
            else:
                ch["p"] = ch["p"] + _mmn(_nn, ch["p"], bdiag(ch["sq"]))
        yield
    for ch in chains:
        pu = _mm1(_nn, ch["p"], jnp.concatenate([rs(ch["al"]), rs(ch["zk"])], axis=1))
        ch["w"], ch["uk"] = pu[:, 0:W], pu[:, W:2 * W]
    yield
    for ch in chains:
        gy = _mm1(_nn, ch["a_rb"], jnp.concatenate([rs(ch["w"]), rs(ch["uk"])], axis=1))
        ch["g"] = ch["rt"] + gy[:, 0:W]
        ch["y0"] = gy[:, W:2 * W] + ch["y0k"]
        ch["decay"] = jnp.exp(ch["tot"])
        ch["pm"] = bd * _mm1(_tn, ch["w"], ch["bt"])
        ch["q0"] = bd * _mm1(_tn, jnp.concatenate([ch["uk"], ch["v"]], axis=0),
                             jnp.concatenate([ch["bt"], ch["kt"]], axis=0))
    yield

    by_key = {(ch["d"], ch["c"]): ch for ch in chains}
    st = [s_scr[0], s_scr[1]]
    ys = [[None] * cps, [None] * cps]
    for step in range(cps):
        for d in range(2):
            c = step if d == 0 else cps - 1 - step
            ch = by_key[(d, c)]
            ys[d][c] = _mm1(_nt, ch["g"], st[d]) + ch["y0"]
            st[d] = st[d] * ch["decay"] + (_mm1(_nn, st[d], ch["pm"]) + ch["q0"])
        yield
    for d, y_ref in enumerate((yf_ref, yb_ref)):
        s_scr[d] = st[d]
        y_ref[0] = jnp.concatenate(ys[d], axis=0)


def _chunks_per_step(n_ctx, n_all):
    for cps in (4, 2, 1):
        if n_ctx % cps == 0 and (n_all - n_ctx) % cps == 0:
            return cps


def _rwkv_call(z, wup, aup, gup, w0, a0, kk, ka, rk, *, n_ctx, n_all):
    B, Ta, _ = z.shape
    W = GROUP_W
    cps = _chunks_per_step(n_ctx, n_all)
    rows = cps * CHUNK
    gc, ga = n_ctx // cps, n_all // cps
    full = lambda a: pl.BlockSpec(a.shape, lambda b, i: (0,) * a.ndim)
    fwd = lambda blk: (lambda b, i: (b, i, blk))
    bwd = lambda blk: (lambda b, i: (b, _bwd_chunk(i, gc, ga), blk))
    out = jax.ShapeDtypeStruct((B, Ta, W), F32)
    return pl.pallas_call(
        functools.partial(_rwkv_kernel, cps=cps),
        out_shape=(out, out, out, out),
        grid=(B, ga),
        in_specs=[pl.BlockSpec((1, rows, 3 * W), fwd(Z_RKV // 3)),
                  pl.BlockSpec((1, rows, 3 * W), bwd(Z_RKV // 3)),
                  pl.BlockSpec((1, rows, W), fwd(Z_LORA)),
                  pl.BlockSpec((1, rows, W), bwd(Z_LORA)),
                  full(wup), full(aup), full(gup), full(w0), full(a0), full(kk), full(ka), full(rk)],
        out_specs=(pl.BlockSpec((1, rows, W), fwd(0)), pl.BlockSpec((1, rows, W), bwd(0)),
                   pl.BlockSpec((1, rows, W), fwd(0)), pl.BlockSpec((1, rows, W), fwd(0))),
        scratch_shapes=[pltpu.VMEM((2, W, W), F32), pltpu.VMEM((2, rows, rows), BF16), pltpu.VMEM((W, W), BF16)],
        compiler_params=_cparams("parallel", "arbitrary"),
        name="rwkv7",
    )(z, z, z, z, wup, aup, gup, w0, a0, kk, ka, rk)


def _mlstm_kernel(*refs, cps):
    _run_stages(_mlstm_stages(*refs, cps=cps))


def _mlstm_stages(qf_ref, qb_ref, vf_ref, vb_ref, gf_ref, gb_ref, gbias_ref,
                  yf_ref, yb_ref, c_scr, n_scr, m_scr, tri_scr, exp_scr, bdb_scr, *, cps):
    i = pl.program_id(1)
    L = CHUNK
    W = GROUP_W

    @pl.when(i == 0)
    def _():
        c_scr[...] = jnp.zeros_like(c_scr)
        n_scr[...] = jnp.zeros_like(n_scr)
        m_scr[...] = jnp.zeros_like(m_scr)
        _group_tri_masks(tri_scr, cps * L)
        bdb_scr[...] = _block_diag_mask().astype(BF16)
        ci = _iota((W, W), 0)
        cj = _iota((W, W), 1) // HEAD_DIM
        for d in range(2):
            col = GATE_LANE0 + d * 2 * N_HEADS
            exp_scr[d, :, 0:W] = (ci == col + cj).astype(BF16)
            exp_scr[d, :, W:2 * W] = (ci == col + N_HEADS + cj).astype(BF16)

    hms = _head_masks()
    bd = _block_diag_mask()
    bdb = bdb_scr[...]
    rs = lambda x: _row_stack(x, hms)
    neg_inf = -jnp.inf

    def sum2(x, eb):
        hi, lo = _split2(x)
        return _nn(hi, eb) + _nn(lo, eb)

    per_dir = []
    for d, (q_ref, v_ref, g_ref) in enumerate(((qf_ref, vf_ref, gf_ref), (qb_ref, vb_ref, gb_ref))):
        qk = q_ref[0]
        g = g_ref[0] + gbias_ref[...]
        _, strict, incl, eye = _chunk_masks(d)
        tri = tri_scr[d]
        gates = sum2(g, exp_scr[d])
        li = gates[:, 0:W]
        lf = -_softplus(-gates[:, W:2 * W])
        lf_hi, lf_lo = _split2(lf)
        b2 = _nn(tri, jnp.concatenate([lf_hi, lf_lo], axis=1))
        per_dir.append(dict(q=qk[:, 0:W], k=qk[:, W:2 * W] * (HEAD_DIM ** -0.5), v=v_ref[0][:, 0:W],
                            li=li, bcol=b2[:, 0:W] + b2[:, W:2 * W], incl=incl, eye=eye))
        yield

    chains = []
    for c in range(cps):
        for d in range(2):
            pd = per_dir[d]
            sl = slice(c * L, (c + 1) * L)
            q, k, v, li, bcol = pd["q"][sl], pd["k"][sl], pd["v"][sl], pd["li"][sl], pd["bcol"][sl]
            brow = jnp.sum(jnp.where(pd["eye"], bcol, 0.0), axis=0, keepdims=True)
            lirow = jnp.sum(jnp.where(pd["eye"], li, 0.0), axis=0, keepdims=True)
            logd = jnp.where(pd["incl"], bcol - brow + lirow, neg_inf)
            mx = jnp.zeros((L, W), F32)
            for hm in hms:
                mh = jnp.max(jnp.where(hm, logd, neg_inf), axis=1, keepdims=True)
                mx = jnp.where(hm, mh, mx)
            blast = bcol[L - 1:L, :] if d == 0 else bcol[0:1, :]
            lwc = blast - bcol + li
            mlw = jnp.max(lwc, axis=0, keepdims=True)
            kw = k * jnp.exp(lwc - mlw)
            chains.append(dict(d=d, c=c, q=q, v=v, bcol=bcol, mx=mx, blast=blast, mlw=mlw, kw=kw,
                               dexp=jnp.exp(logd - mx), rsk=rs(k), rsv=rs(v),
                               nu0=jnp.sum(kw, axis=0, keepdims=True)))
        yield
    for ch in chains:
        ch["sp"] = _mm1(_nt, ch["q"], ch["rsk"]) * ch["dexp"]
        ch["cu0"] = bd * _mm1(_tn, ch["v"], ch["kw"])
    yield
    for ch in chains:
        ch["num0"] = _mm1(_nn, ch["sp"], ch["rsv"])
        ch["den0"] = _mm1(_nn, ch["sp"], bdb)
    yield

    by_key = {(ch["d"], ch["c"]): ch for ch in chains}
    c_st = [c_scr[0], c_scr[1]]
    n_row = [n_scr[0, 0:1, :], n_scr[1, 0:1, :]]
    m_row = [m_scr[0, 0:1, :], m_scr[1, 0:1, :]]
    ys = [[None] * cps, [None] * cps]
    for step in range(cps):
        for d in range(2):
            c = step if d == 0 else cps - 1 - step
            ch = by_key[(d, c)]
            inter = ch["bcol"] + m_row[d]
            mt = jnp.maximum(inter, ch["mx"])
            f_in = jnp.exp(ch["mx"] - mt)
            w_int = jnp.exp(inter - mt)
            num = f_in * ch["num0"] + w_int * _mm1(_nt, ch["q"], c_st[d])
            den = f_in * ch["den0"] + w_int * _mm1(_nn, ch["q"] * n_row[d], bdb)
            ys[d][c] = num / jnp.maximum(jnp.abs(den), jnp.exp(-mt))
            m_new = jnp.maximum(ch["blast"] + m_row[d], ch["mlw"])
            sc = jnp.exp(ch["blast"] + m_row[d] - m_new)
            e2 = jnp.exp(ch["mlw"] - m_new)
            c_st[d] = sc * c_st[d] + e2 * ch["cu0"]
            n_row[d] = sc * n_row[d] + e2 * ch["nu0"]
            m_row[d] = m_new
        yield
    for d, y_ref in enumerate((yf_ref, yb_ref)):
        c_scr[d] = c_st[d]
        n_scr[d, 0:1, :] = n_row[d]
        m_scr[d, 0:1, :] = m_row[d]
        y_ref[0] = jnp.concatenate(ys[d], axis=0)


def _mlstm_call(z, gbias, *, n_ctx, n_all):
    B, Ta, _ = z.shape
    W = GROUP_W
    cps = _chunks_per_step(n_ctx, n_all)
    L = cps * CHUNK
    n_ctx, n_all = n_ctx // cps, n_all // cps
    fwd = lambda blk: (lambda b, i: (b, i, blk))
    bwd = lambda blk: (lambda b, i: (b, _bwd_chunk(i, n_ctx, n_all), blk))
    out = jax.ShapeDtypeStruct((B, Ta, W), F32)
    return pl.pallas_call(
        functools.partial(_mlstm_kernel, cps=cps),
        out_shape=(out, out),
        grid=(B, n_all),
        in_specs=[pl.BlockSpec((1, L, 2 * W), fwd(Z_MQK // 2)),
                  pl.BlockSpec((1, L, 2 * W), bwd(Z_MQK // 2)),
                  pl.BlockSpec((1, L, 2 * W), fwd(Z_MVO // 2)),
                  pl.BlockSpec((1, L, 2 * W), bwd(Z_MVO // 2)),
                  pl.BlockSpec((1, L, W), fwd(Z_GATE)),
                  pl.BlockSpec((1, L, W), bwd(Z_GATE)),
                  pl.BlockSpec(gbias.shape, lambda b, i: (0, 0))],
        out_specs=(pl.BlockSpec((1, L, W), fwd(0)), pl.BlockSpec((1, L, W), bwd(0))),
        scratch_shapes=[pltpu.VMEM((2, W, W), F32), pltpu.VMEM((2, 8, W), F32), pltpu.VMEM((2, 8, W), F32),
                        pltpu.VMEM((2, L, L), BF16), pltpu.VMEM((2, W, 2 * W), BF16), pltpu.VMEM((W, W), BF16)],
        compiler_params=_cparams("parallel", "arbitrary"),
        name="mlstm",
    )(z, z, z, z, z, z, gbias)


def _recur_kernel(*refs, cps):
    r_in, m_in = refs[0:12], refs[12:19]
    r_out, m_out = refs[19:23], refs[23:25]
    r_scr, m_scr = refs[25:28], refs[28:34]
    _run_stages(_rwkv_stages(*r_in, *r_out, *r_scr, cps=cps),
                _mlstm_stages(*m_in, *m_out, *m_scr, cps=cps))


def _recur_call(z, wup, aup, gup, w0, a0, kk, ka, rk, gbias, *, n_ctx, n_all):
    B, Ta, _ = z.shape
    W = GROUP_W
    cps = _chunks_per_step(n_ctx, n_all)
    rows = cps * CHUNK
    gc, ga = n_ctx // cps, n_all // cps
    full = lambda a: pl.BlockSpec(a.shape, lambda b, i: (0,) * a.ndim)
    fwd = lambda blk: (lambda b, i: (b, i, blk))
    bwd = lambda blk: (lambda b, i: (b, _bwd_chunk(i, gc, ga), blk))
    tok = lambda width, imap: pl.BlockSpec((1, rows, width), imap)
    out = jax.ShapeDtypeStruct((B, Ta, W), F32)
    return pl.pallas_call(
        functools.partial(_recur_kernel, cps=cps),
        out_shape=(out,) * 6,
        grid=(B, ga),
        in_specs=[tok(3 * W, fwd(Z_RKV // 3)), tok(3 * W, bwd(Z_RKV // 3)),
                  tok(W, fwd(Z_LORA)), tok(W, bwd(Z_LORA)),
                  full(wup), full(aup), full(gup), full(w0), full(a0), full(kk), full(ka), full(rk),
                  tok(2 * W, fwd(Z_MQK // 2)), tok(2 * W, bwd(Z_MQK // 2)),
                  tok(2 * W, fwd(Z_MVO // 2)), tok(2 * W, bwd(Z_MVO // 2)),
                  tok(W, fwd(Z_GATE)), tok(W, bwd(Z_GATE)), full(gbias)],
        out_specs=(tok(W, fwd(0)), tok(W, bwd(0)), tok(W, fwd(0)), tok(W, fwd(0)),
                   tok(W, fwd(0)), tok(W, bwd(0))),
        scratch_shapes=[pltpu.VMEM((2, W, W), F32), pltpu.VMEM((2, rows, rows), BF16), pltpu.VMEM((W, W), BF16),
                        pltpu.VMEM((2, W, W), F32), pltpu.VMEM((2, 8, W), F32), pltpu.VMEM((2, 8, W), F32),
                        pltpu.VMEM((2, rows, rows), BF16), pltpu.VMEM((2, W, 2 * W), BF16),
                        pltpu.VMEM((W, W), BF16)],
        compiler_params=_cparams("parallel", "arbitrary"),
        name="rwkv_mlstm",
    )(z, z, z, z, wup, aup, gup, w0, a0, kk, ka, rk, z, z, z, z, z, z, gbias)


def _head_norm(y, bd, eps):
    bdb = bd.astype(BF16)

    def head_mean(x):
        hi, lo = _split2(x)
        return (_nn(hi, bdb) + _nn(lo, bdb)) * (1.0 / HEAD_DIM)

    yc = y - head_mean(y)
    return yc * lax.rsqrt(head_mean(yc * yc) + eps)


def _outproj_kernel(x_ref, mods_ref, s5f_ref, s5b_ref, u_ref, at_ref, rf_ref, rb_ref, rg_ref, rbon_ref,
                    mf_ref, mb_ref, vo_ref, s5d_ref, wglu_ref, bglu_ref, lnw_ref, lnb_ref, mnw_ref,
                    gpost_ref, wout_ref, o_ref, *, tm, tc, n_batch):
    b = pl.program_id(0)
    t = pl.program_id(1)
    d = x_ref.shape[-1]
    W = GROUP_W
    is_ctx = (t * tm + _iota((tm, 1), 0)) < tc
    gate = jnp.where(is_ctx, mods_ref[pl.ds(n_batch, 1), pl.ds(5 * d, d)],
                     mods_ref[pl.ds(b, 1), pl.ds(5 * d, d)])
    bd = _block_diag_mask()

    y = s5f_ref[...] + s5b_ref[...] + s5d_ref[...] * u_ref[0]
    zg = 0.5 * y * (1.0 + jnp.tanh(math.sqrt(2.0 / math.pi) * (y + 0.044715 * (y * y * y))))
    s5o = zg * _sigmoid(_mm1(_nn, zg, wglu_ref[...]) + bglu_ref[...])

    yr = _head_norm(rf_ref[0] + rb_ref[0], bd, RWKV_GN_EPS)
    rwo = (yr * lnw_ref[...] + lnb_ref[...] + rbon_ref[0]) * rg_ref[0]

    ym = _head_norm(mf_ref[0] + mb_ref[0], bd, NORM_EPS)
    mlo = ym * mnw_ref[...] * _sigmoid(vo_ref[0][:, W:2 * W])

    cat = jnp.concatenate([s5o, at_ref[0], rwo, mlo], axis=1).astype(BF16)
    yx = _nn(cat, wout_ref[...])
    o_ref[0] = x_ref[0] + gate * _rms(yx, gpost_ref[...])


def _outproj_call(xa, mods, s5f, s5b, z, attn, rf, rb, rg, rbon, mf, mb,
                  s5d, wglu, bglu, lnw, lnb, mnw, gpost, wout, *, tc):
    B, Ta, D = xa.shape
    W = GROUP_W
    tm = _pick_tile(Ta, 544)
    kern = functools.partial(_outproj_kernel, tm=tm, tc=tc, n_batch=B)
    full = lambda a: pl.BlockSpec(a.shape, lambda b, t: (0,) * a.ndim)
    tok = pl.BlockSpec((1, tm, W), lambda b, t: (b, t, 0))
    tmaj = pl.BlockSpec((tm, W), lambda b, t: (t, b))
    return pl.pallas_call(
        kern,
        out_shape=jax.ShapeDtypeStruct((B, Ta, D), F32),
        grid=(B, Ta // tm),
        in_specs=[pl.BlockSpec((1, tm, D), lambda b, t: (b, t, 0)), full(mods),
                  tmaj, tmaj, pl.BlockSpec((1, tm, W), lambda b, t: (b, t, Z_S5)),
                  tok, tok, tok, tok, tok, tok, tok,
                  pl.BlockSpec((1, tm, 2 * W), lambda b, t: (b, t, Z_MVO // 2)),
                  full(s5d), full(wglu), full(bglu), full(lnw), full(lnb), full(mnw), full(gpost), full(wout)],
        out_specs=pl.BlockSpec((1, tm, D), lambda b, t: (b, t, 0)),
        compiler_params=_cparams("parallel", "parallel"),
        name="mix_out",
    )(xa, mods, s5f, s5b, z, attn, rf, rb, rg, rbon, mf, mb, z, s5d, wglu, bglu, lnw, lnb, mnw, gpost, wout)


def _rope_rotate_cols(w):
    h = ROPE_AXIS // 2
    return jnp.concatenate([-w[..., h:2 * h], w[..., 0:h], -w[..., 3 * h:4 * h], w[..., 2 * h:3 * h]], axis=-1)


def _inproj_relayout(w_in):
    L, D, _ = w_in.shape
    o_s5, o_mla, o_rw, o_ml = 0, 256, 672, 1568
    seg = lambda a, n: w_in[:, :, a:a + n]
    zer = lambda n: jnp.zeros((L, D, n), w_in.dtype)
    k_rope = seg(o_mla + 384, ROPE_DIM)
    parts = [seg(o_mla, 256), seg(o_mla + 256, 128),
             zer(64), k_rope, zer(32),
             zer(64), _rope_rotate_cols(k_rope), zer(32), seg(o_ml + 1024, 16), zer(128 - 16),
             seg(o_rw, 768), seg(o_s5, 256),
             seg(o_rw + 768, 128), zer(128),
             seg(o_ml, 512), seg(o_ml + 512, 512)]
    out = jnp.concatenate(parts, axis=2)
    assert out.shape[2] == Z_COLS
    return out


def _rope_tables(T, tc):
    rows = T // GRID_W
    r_idx, c_idx = jnp.meshgrid(jnp.arange(rows), jnp.arange(GRID_W), indexing='ij')
    inv_freq = 1.0 / (ROPE_BASE ** (jnp.arange(0, ROPE_AXIS, 2, dtype=F32) / ROPE_AXIS))
    ang_r = r_idx.reshape(-1, 1).astype(F32) * inv_freq
    ang_c = c_idx.reshape(-1, 1).astype(F32) * inv_freq
    ang = jnp.concatenate([ang_r, ang_r, ang_c, ang_c], axis=-1)
    cos = jnp.concatenate([jnp.ones((tc, ROPE_DIM), F32), jnp.cos(ang)], axis=0)
    sin = jnp.concatenate([jnp.zeros((tc, ROPE_DIM), F32), jnp.sin(ang)], axis=0)
    ta = T + tc
    cos_t = jnp.concatenate([jnp.ones((ta, 64), F32), cos, jnp.zeros((ta, 32), F32)], axis=1)
    sin_t = jnp.concatenate([jnp.zeros((ta, 64), F32), sin, jnp.zeros((ta, 32), F32)], axis=1)
    return cos_t, sin_t


def _pad_rows(w, r0, total):
    pad = [(0, 0)] * (w.ndim - 2) + [(r0, total - r0 - w.shape[-2]), (0, 0)]
    return jnp.pad(w, pad)


def kernel(x, c, ctx, c_ctx, w_ada, b_ada, norm_pre, norm_post, ffn_w_gate, ffn_w_up, ffn_w_down, w_in, w_out, s5_lam_re, s5_lam_im, s5_log_dt, s5_b_re, s5_b_im, s5_c_re, s5_c_im, s5_d, s5_w_glu, s5_b_glu, mla_q_norm, mla_kv_norm, mla_w_uq, mla_w_ukv, rwkv_conv_w, rwkv_conv_b, rwkv_w0, rwkv_w_up, rwkv_a0, rwkv_a_up, rwkv_g_up, rwkv_k_k, rwkv_k_a, rwkv_r_k, rwkv_ln_w, rwkv_ln_b, mlstm_conv_w, mlstm_conv_b, mlstm_gate_b, mlstm_norm):
    B, T, D = x.shape
    Tc = ctx.shape[1]
    Ta = T + Tc
    L = w_ada.shape[0]
    W = GROUP_W
    assert T % CHUNK == 0 and Tc % CHUNK == 0 and B % 8 == 0 and B <= 8
    n_ctx, n_all = Tc // CHUNK, Ta // CHUNK

    rows = 16
    cvec = jnp.concatenate([c, c_ctx[None, :], jnp.zeros((rows - B - 1, D), F32)], axis=0)
    mods_all = _ada_call(cvec, w_ada, b_ada)

    w_in_re = _inproj_relayout(w_in).astype(BF16)
    cw = jnp.zeros((L, Z_NBLK, 8, ZB), F32)
    rc = jnp.concatenate([rwkv_conv_w, rwkv_conv_b[:, None, :]], axis=1).reshape(L, 4, 3, ZB).transpose(0, 2, 1, 3)
    mc = jnp.concatenate([mlstm_conv_w, mlstm_conv_b[:, None, :]], axis=1).reshape(L, 4, 2, ZB).transpose(0, 2, 1, 3)
    cw = cw.at[:, Z_RKV:Z_RKV + 3, 0:4].set(rc).at[:, Z_MQK:Z_MQK + 2, 0:4].set(mc)

    wg = ffn_w_gate.astype(BF16)
    wu = ffn_w_up.astype(BF16)
    wd = ffn_w_down.astype(BF16)
    wout = w_out.astype(BF16)

    G = s5_lam_re.shape[2]
    N = s5_lam_re.shape[3]
    eye_g = jnp.eye(G, dtype=F32)
    lre = s5_lam_re.reshape(L, 2, 1, G * N)
    lim = s5_lam_im.reshape(L, 2, 1, G * N)
    ldt = jnp.repeat(s5_log_dt, N, axis=-1).reshape(L, 2, 1, G * N)
    wbre = jnp.einsum('ldgnp,gh->ldgphn', s5_b_re, eye_g).reshape(L, 2, G * S5_P, G * N)
    wbim = jnp.einsum('ldgnp,gh->ldgphn', s5_b_im, eye_g).reshape(L, 2, G * S5_P, G * N)
    wcre = jnp.einsum('ldgpn,gh->ldgnhp', s5_c_re, eye_g).reshape(L, 2, G * N, G * S5_P).astype(BF16)
    wcim = jnp.einsum('ldgpn,gh->ldgnhp', s5_c_im, eye_g).reshape(L, 2, G * N, G * S5_P).astype(BF16)

    nope = HEAD_DIM
    qd = nope + ROPE_DIM
    wq4 = mla_w_uq.reshape(L, -1, N_HEADS, qd)
    wq = jnp.pad(wq4, ((0, 0), (0, 0), (0, 0), (0, 128 - qd))).reshape(L, -1, N_HEADS * 128).astype(BF16)
    wq_rot = _rope_rotate_cols(wq4[..., nope:])
    wqr = jnp.pad(wq_rot, ((0, 0), (0, 0), (0, 0), (nope, 128 - qd))).reshape(L, -1, N_HEADS * 128).astype(BF16)
    wkv4 = mla_w_ukv.reshape(L, -1, N_HEADS, 2 * HEAD_DIM)
    wk = jnp.pad(wkv4[..., :HEAD_DIM], ((0, 0), (0, 0), (0, 0), (0, 64))).reshape(L, -1, N_HEADS * 128).astype(BF16)
    wv = jnp.pad(wkv4[..., HEAD_DIM:], ((0, 0), (0, 0), (0, 0), (0, 64))).reshape(L, -1, N_HEADS * 128).astype(BF16)
    cos_t, sin_t = _rope_tables(T, Tc)
    scale = float(qd) ** -0.5 * math.log2(math.e)

    wup = _pad_rows(rwkv_w_up, 0, W)
    aup = _pad_rows(rwkv_a_up, 32, W)
    gup = _pad_rows(rwkv_g_up, 64, W)
    gbias = jnp.pad(mlstm_gate_b, ((0, 0), (GATE_LANE0, W - GATE_LANE0 - mlstm_gate_b.shape[1])))

    xa = jnp.concatenate([ctx, x], axis=1)
    r1 = lambda a: a.reshape(1, -1)

    for l in range(L):
        mods = mods_all[l]
        xa2, hmix = _ffn_call(xa.reshape(B * Ta, D), mods, r1(norm_pre[l, 0]), r1(norm_post[l, 0]),
                              r1(norm_pre[l, 1]), wg[l, 0], wu[l, 0], wd[l, 0],
                              ta=Ta, tc=Tc, n_batch=B, koff=0, emit_hmix=True)
        xa = xa2.reshape(B, Ta, D)
        z = _inproj_call(hmix.reshape(B, Ta, D), w_in_re[l], cw[l], tc=Tc)

        s5f, s5b = _s5_call(z, lre[l], lim[l], ldt[l], wbre[l], wbim[l], wcre[l], wcim[l],
                            n_ctx=n_ctx, n_all=n_all)

        q, k, v = _mla_proj_call(z, r1(mla_q_norm[l]), r1(mla_kv_norm[l]), wq[l], wqr[l], wk[l], wv[l],
                                 cos_t, sin_t, scale=scale)
        attn = _attn_call(q, k, v, tc=Tc)

        rf, rb, rg, rbon, mf, mb = _recur_call(
            z, wup[l], aup[l], gup[l], rwkv_w0[l][:, None, :], rwkv_a0[l][:, None, :],
            r1(rwkv_k_k[l]), r1(rwkv_k_a[l]), r1(rwkv_r_k[l]), gbias[l:l + 1], n_ctx=n_ctx, n_all=n_all)

        xa = _outproj_call(xa, mods, s5f, s5b, z, attn, rf, rb, rg, rbon,
                           mf, mb, r1(s5_d[l]), s5_w_glu[l].astype(BF16), r1(s5_b_glu[l]), r1(rwkv_ln_w[l]),
                           r1(rwkv_ln_b[l]), r1(mlstm_norm[l]), r1(norm_post[l, 1]), wout[l], tc=Tc)

        xa2, _ = _ffn_call(xa.reshape(B * Ta, D), mods, r1(norm_pre[l, 2]), r1(norm_post[l, 2]),
                           r1(norm_pre[l, 1]), wg[l, 1], wu[l, 1], wd[l, 1],
                           ta=Ta, tc=Tc, n_batch=B, koff=6, emit_hmix=False)
        xa = xa2.reshape(B, Ta, D)

    return xa[:, Tc:, :]
```

```python
import functools
import math

import numpy as np
import jax
import jax.numpy as jnp
from jax import lax
from jax.experimental import pallas as pl
from jax.experimental.pallas import tpu as pltpu

F32 = jnp.float32
BF16 = jnp.bfloat16

GROUP_W = 256
HEAD_DIM = 64
N_HEADS = GROUP_W // HEAD_DIM
CHUNK = 64
S5_CHUNK = 128
N_MOD = 9
NORM_EPS = 1e-6
RWKV_GN_EPS = HEAD_DIM * 1e-5
GRID_W = 64
ROPE_BASE = 10000.0
ROPE_DIM = 32
ROPE_AXIS = 16
S5_P = 16
S5_STATE = 64
MACARON = 0.5
VMEM_LIMIT_BYTES = 56 * 1024 * 1024

ZB = 256
Z_MLA, Z_RKV, Z_S5, Z_LORA, Z_MQK, Z_MVO = 0, 3, 6, 7, 8, 10
Z_GATE = 2
GATE_LANE0 = 128
Z_NBLK = 12
Z_COLS = Z_NBLK * ZB


def _nn(a, b):
    return lax.dot_general(a, b, (((1,), (0,)), ((), ())), preferred_element_type=F32)


def _nt(a, b):
    return lax.dot_general(a, b, (((1,), (1,)), ((), ())), preferred_element_type=F32)


def _tn(a, b):
    return lax.dot_general(a, b, (((0,), (0,)), ((), ())), preferred_element_type=F32)


def _split2(x):
    hi = x.astype(BF16)
    lo = (x - hi.astype(F32)).astype(BF16)
    return hi, lo


def _split3(x):
    p1 = x.astype(BF16)
    r = x - p1.astype(F32)
    p2 = r.astype(BF16)
    p3 = (r - p2.astype(F32)).astype(BF16)
    return p1, p2, p3


def _mm3(dotf, a, b):
    ah, al = _split2(a)
    bh, bl = _split2(b)
    return dotf(ah, bh) + (dotf(ah, bl) + dotf(al, bh))


def _mm1(dotf, a, b):
    return dotf(a.astype(BF16), b.astype(BF16))


_mmn = _mm1


def _mm_exact_rhs(a, e):
    eb = e.astype(BF16)
    p1, p2, p3 = _split3(a)
    return _nn(p1, eb) + (_nn(p2, eb) + _nn(p3, eb))


def _mm_exact_lhs(e, a):
    eb = e.astype(BF16)
    p1, p2, p3 = _split3(a)
    return _nn(eb, p1) + (_nn(eb, p2) + _nn(eb, p3))


def _rms(x, g):
    return x * lax.rsqrt(jnp.mean(x * x, axis=-1, keepdims=True) + NORM_EPS) * g


def _sigmoid(x):
    return 0.5 * jnp.tanh(0.5 * x) + 0.5


def _group_tri_masks(tri_scr, rows):
    r = _iota((rows, rows), 0)
    c = _iota((rows, rows), 1)
    same = (r // CHUNK) == (c // CHUNK)
    tri_scr[0] = (same & (c <= r)).astype(BF16)
    tri_scr[1] = (same & (c >= r)).astype(BF16)


def _softplus(x):
    return jnp.maximum(x, 0.0) + jnp.log(1.0 + jnp.exp(-jnp.abs(x)))


def _iota(shape, dim):
    return lax.broadcasted_iota(jnp.int32, shape, dim)


def _head_masks():
    lane = _iota((1, GROUP_W), 1)
    return [lane // HEAD_DIM == h for h in range(N_HEADS)]


def _block_diag_mask():
    r = _iota((GROUP_W, GROUP_W), 0) // HEAD_DIM
    c = _iota((GROUP_W, GROUP_W), 1) // HEAD_DIM
    return (r == c).astype(F32)


def _row_stack(x, hms):
    xb = x.astype(BF16)
    return jnp.concatenate([jnp.where(m, xb, 0.0) for m in hms], axis=0)


def _pick_tile(n, target, mult=16):
    best = None
    for t in range(mult, min(n, target) + 1, mult):
        if n % t == 0:
            best = t
    if best is None:
        raise ValueError(f"no tile for {n}")
    return best


def _cparams(*sem):
    return pltpu.CompilerParams(dimension_semantics=sem, vmem_limit_bytes=VMEM_LIMIT_BYTES)


def _ada_kernel(c_ref, w_ref, b_ref, o_ref):
    c = c_ref[...]
    s = c * _sigmoid(c)
    o_ref[0] = _mm3(_nn, s, w_ref[0]) + b_ref[0]


def _ada_call(cvec, w_ada, b_ada):
    L, D, N = w_ada.shape
    R = cvec.shape[0]
    tn = _pick_tile(N, 1152, 128)
    return pl.pallas_call(
        _ada_kernel,
        out_shape=jax.ShapeDtypeStruct((L, R, N), F32),
        grid=(L, N // tn),
        in_specs=[pl.BlockSpec((R, D), lambda l, j: (0, 0)),
                  pl.BlockSpec((1, D, tn), lambda l, j: (l, 0, j)),
                  pl.BlockSpec((1, 1, tn), lambda l, j: (l, 0, j))],
        out_specs=pl.BlockSpec((1, R, tn), lambda l, j: (l, 0, j)),
        compiler_params=_cparams("parallel", "parallel"),
        name="ada_mod",
    )(cvec, w_ada, b_ada.reshape(L, 1, N))


FFN_COLS = 256


def _ffn_kernel(xn_ref, xp_ref, mods_ref, gpre_ref, gpost_ref, gmix_ref, wg_ref, wu_ref, wd_ref,
                o_ref, hmix_ref, h_scr, acc_scr, *, tm, tiles_per_batch, tc, n_batch, koff, emit_hmix,
                n_tiles):
    i = pl.program_id(0)
    f = pl.program_id(1)
    d = xn_ref.shape[-1]
    fd = wg_ref.shape[1]

    def mod_of(tile):
        b = tile // tiles_per_batch
        t0 = (tile % tiles_per_batch) * tm
        is_ctx = (t0 + _iota((tm, 1), 0)) < tc

        def mod(fn):
            row = lambda r: (lambda k: mods_ref[pl.ds(r, 1), pl.ds(k * d, d)])
            return jnp.where(is_ctx, fn(row(n_batch)), fn(row(b)))
        return mod

    unit = lambda x: x * lax.rsqrt(jnp.mean(x * x, axis=-1, keepdims=True) + NORM_EPS)

    def pre_norm(tile, x):
        mod = mod_of(tile)
        gain = mod(lambda m: gpre_ref[...] * (1.0 + m(koff + 1)))
        return (unit(x) * gain + mod(lambda m: m(koff))).astype(BF16)

    def finish(tile, acc, x):
        mod = mod_of(tile)
        xn = x + unit(acc) * mod(lambda m: (MACARON * m(koff + 2)) * gpost_ref[...])
        o_ref[...] = xn
        if emit_hmix:
            gain = mod(lambda m: gmix_ref[...] * (1.0 + m(4)))
            hmix_ref[...] = (unit(xn) * gain + mod(lambda m: m(3))).astype(BF16)
        else:
            hmix_ref[...] = jnp.zeros_like(hmix_ref)

    def hidden_cols(hb, lo, hi):
        acc = None
        for c0 in range(lo, hi, FFN_COLS):
            g = _nn(hb, wg_ref[:, c0:c0 + FFN_COLS])
            u = _nn(hb, wu_ref[:, c0:c0 + FFN_COLS])
            t = _nn((g * _sigmoid(g) * u).astype(BF16), wd_ref[c0:c0 + FFN_COLS, :])
            acc = t if acc is None else acc + t
        return acc

    @pl.when(jnp.logical_and(i == 0, f == 0))
    def _():
        h_scr[0] = pre_norm(0, xp_ref[...])
        acc_scr[1] = jnp.zeros((tm, d), F32)

    for s in range(2):
        tile = 2 * i + s

        @pl.when(jnp.logical_and(tile < n_tiles, f == s))
        def _(s=s, tile=tile):
            finish(jnp.maximum(tile - 1, 0), acc_scr[1 - s], xp_ref[...])
            h_scr[1 - s] = pre_norm(jnp.minimum(tile + 1, n_tiles - 1), xn_ref[...])
            acc_scr[s] = hidden_cols(h_scr[s], 0, fd)

    @pl.when(jnp.logical_and(2 * i == n_tiles, f == 0))
    def _():
        finish(n_tiles - 1, acc_scr[1], xp_ref[...])


def _ffn_call(xa2, mods, gpre, gpost, gmix, wg, wu, wd, *, ta, tc, n_batch, koff, emit_hmix):
    M, D = xa2.shape
    Fd = wg.shape[1]
    assert Fd % FFN_COLS == 0
    tm = _pick_tile(ta, 544)
    n_tiles = M // tm
    assert n_tiles % 2 == 0
    kern = functools.partial(_ffn_kernel, tm=tm, tiles_per_batch=ta // tm, tc=tc, n_batch=n_batch,
                             koff=koff, emit_hmix=emit_hmix, n_tiles=n_tiles)
    hm_rows = tm if emit_hmix else 16
    last = n_tiles - 1
    resident = lambda a: pl.BlockSpec(a.shape, lambda i, f: (0, 0), pipeline_mode=pl.Buffered(1))
    vec = pl.BlockSpec((1, D), lambda i, f: (0, 0))
    return pl.pallas_call(
        kern,
        out_shape=(jax.ShapeDtypeStruct((M, D), F32),
                   jax.ShapeDtypeStruct((M if emit_hmix else 16 * n_tiles, D), BF16)),
        grid=(n_tiles // 2 + 1, 2),
        in_specs=[pl.BlockSpec((tm, D), lambda i, f: (jnp.minimum(2 * i + f + 1, last), 0)),
                  pl.BlockSpec((tm, D), lambda i, f: (jnp.clip(2 * i + f - 1, 0, last), 0)),
                  pl.BlockSpec(mods.shape, lambda i, f: (0, 0)),
                  vec, vec, vec, resident(wg), resident(wu), resident(wd)],
        out_specs=(pl.BlockSpec((tm, D), lambda i, f: (jnp.clip(2 * i + f - 1, 0, last), 0)),
                   pl.BlockSpec((hm_rows, D), lambda i, f: (jnp.clip(2 * i + f - 1, 0, last), 0))),
        scratch_shapes=[pltpu.VMEM((2, tm, D), BF16), pltpu.VMEM((2, tm, D), F32)],
        compiler_params=_cparams("arbitrary", "arbitrary"),
        name="half_ffn",
    )(xa2, xa2, mods, gpre, gpost, gmix, wg, wu, wd)


def _inproj_kernel(h_ref, w_ref, cw_ref, o_ref, *, tc, conv_lo, conv_hi, silu_lo, silu_hi):
    nb = pl.program_id(1)
    ta = h_ref.shape[1]
    is_conv = ((nb >= conv_lo[0]) & (nb < conv_hi[0])) | ((nb >= conv_lo[1]) & (nb < conv_hi[1]))
    is_silu = (nb >= silu_lo) & (nb < silu_hi)

    @pl.when(jnp.logical_not(is_conv))
    def _():
        o_ref[0] = _nn(h_ref[0], w_ref[...])

    def conv():
        z = _nn(h_ref[0], w_ref[...])
        row = _iota((ta, 1), 0)
        zp = jnp.where((row == 0) | (row == tc), 0.0, pltpu.roll(z, 1, 0))
        zn = jnp.where((row == tc - 1) | (row == ta - 1), 0.0, pltpu.roll(z, ta - 1, 0))
        cw = cw_ref[0]
        return cw[3:4] + zp * cw[0:1] + z * cw[1:2] + zn * cw[2:3]

    @pl.when(is_conv & jnp.logical_not(is_silu))
    def _():
        o_ref[0] = conv()

    @pl.when(is_conv & is_silu)
    def _():
        y = conv()
        o_ref[0] = y * _sigmoid(y)


def _inproj_call(hmix3, w_re, cw, *, tc):
    B, Ta, D = hmix3.shape
    kern = functools.partial(_inproj_kernel, tc=tc, conv_lo=(Z_RKV, Z_MQK), conv_hi=(Z_RKV + 3, Z_MQK + 2),
                             silu_lo=Z_MQK, silu_hi=Z_MQK + 2)
    return pl.pallas_call(
        kern,
        out_shape=jax.ShapeDtypeStruct((B, Ta, Z_COLS), F32),
        grid=(B, Z_NBLK),
        in_specs=[pl.BlockSpec((1, Ta, D), lambda b, n: (b, 0, 0)),
                  pl.BlockSpec((D, ZB), lambda b, n: (0, n)),
                  pl.BlockSpec((1, 8, ZB), lambda b, n: (n, 0, 0))],
        out_specs=pl.BlockSpec((1, Ta, ZB), lambda b, n: (b, 0, n)),
        compiler_params=_cparams("parallel", "arbitrary"),
        name="in_proj",
    )(hmix3, w_re, cw)


def _bwd_chunk(i, n_ctx, n_all):
    return jnp.where(i < n_ctx, n_ctx - 1 - i, n_all - 1 - (i - n_ctx))


def _s5_kernel(uf_ref, ub_ref, lre_ref, lim_ref, ldt_ref, wbre_ref, wbim_ref, wcre_ref, wcim_ref,
               yf_ref, yb_ref, wb_scr, coef_scr, st_scr, rel_scr, x_scr, *, lc, nb, scan_unroll):
    i = pl.program_id(0)
    gn = lre_ref.shape[-1]

    @pl.when(i == 0)
    def _():
        for d in range(2):
            dt = jnp.exp(ldt_ref[d])
            lre = lre_ref[d]
            lim = lim_ref[d]
            mag = jnp.exp(lre * dt)
            ar = mag * jnp.cos(lim * dt)
            ai = mag * jnp.sin(lim * dt)
            den = lre * lre + lim * lim
            fr = ((ar - 1.0) * lre + ai * lim) / den
            fi = (ai * lre - (ar - 1.0) * lim) / den
            coef_scr[d, 0:nb, :] = jnp.broadcast_to(ar, (nb, gn))
            coef_scr[d, nb:2 * nb, :] = jnp.broadcast_to(ai, (nb, gn))
            wre = wbre_ref[d]
            wim = wbim_ref[d]
            wb_scr[d, :, 0:gn] = (wre * fr - wim * fi).astype(BF16)
            wb_scr[d, :, gn:2 * gn] = (wim * fr + wre * fi).astype(BF16)
        st_scr[...] = jnp.zeros_like(st_scr)

    half = 128
    u_refs = (uf_ref, ub_ref)
    y_refs = (yf_ref, yb_ref)

    for d in range(2):
        for b in range(nb):
            for s in range(2):
                rel_scr[d, s, pl.ds(b, lc, stride=nb), :] = u_refs[d][b, :, s * half:(s + 1) * half]
    for d in range(2):
        u_tm = jnp.concatenate([rel_scr[d, 0], rel_scr[d, 1]], axis=1).astype(BF16)
        x_scr[d] = _nn(u_tm, wb_scr[d])

    for d in range(2):
        def body(t, carry, d=d):
            sr, si = carry
            ar = coef_scr[d, 0:nb, :]
            ai = coef_scr[d, nb:2 * nb, :]
            tt = t if d == 0 else lc - 1 - t
            r0 = pl.multiple_of(tt * nb, nb)
            xr = x_scr[d, pl.ds(r0, nb), 0:gn]
            xi = x_scr[d, pl.ds(r0, nb), gn:2 * gn]
            nsr = ar * sr - ai * si + xr
            nsi = ar * si + ai * sr + xi
            x_scr[d, pl.ds(r0, nb), 0:gn] = nsr
            x_scr[d, pl.ds(r0, nb), gn:2 * gn] = nsi
            return nsr, nsi

        sr, si = lax.fori_loop(0, lc, body, (st_scr[d, 0:nb, :], st_scr[d, nb:2 * nb, :]), unroll=scan_unroll)
        st_scr[d, 0:nb, :] = sr
        st_scr[d, nb:2 * nb, :] = si
        y = (_nn(x_scr[d, :, 0:gn].astype(BF16), wcre_ref[d])
             - _nn(x_scr[d, :, gn:2 * gn].astype(BF16), wcim_ref[d]))
        rel_scr[d, 0] = y[:, 0:half]
        rel_scr[d, 1] = y[:, half:2 * half]
    for d in range(2):
        for b in range(nb):
            for s in range(2):
                c0 = b * 2 * half + s * half
                y_refs[d][:, c0:c0 + half] = rel_scr[d, s, pl.ds(b, lc, stride=nb), :]


def _s5_call(z, lre, lim, ldt, wbre, wbim, wcre, wcim, *, n_ctx, n_all):
    B, Ta, _ = z.shape
    W = GROUP_W
    lc = S5_CHUNK
    assert (n_ctx * CHUNK) % lc == 0 and (n_all * CHUNK) % lc == 0
    n_ctx, n_all = n_ctx * CHUNK // lc, n_all * CHUNK // lc
    gn = lre.shape[-1]
    blk = lc * B
    kern = functools.partial(_s5_kernel, lc=lc, nb=B, scan_unroll=True)
    full = lambda a: pl.BlockSpec(a.shape, lambda i: (0,) * a.ndim)
    out = jax.ShapeDtypeStruct((Ta, B * W), F32)
    return pl.pallas_call(
        kern,
        out_shape=(out, out),
        grid=(n_all,),
        in_specs=[pl.BlockSpec((B, lc, W), lambda i: (0, i, Z_S5)),
                  pl.BlockSpec((B, lc, W), lambda i: (0, _bwd_chunk(i, n_ctx, n_all), Z_S5)),
                  full(lre), full(lim), full(ldt), full(wbre), full(wbim), full(wcre), full(wcim)],
        out_specs=(pl.BlockSpec((lc, B * W), lambda i: (i, 0)),
                   pl.BlockSpec((lc, B * W), lambda i: (_bwd_chunk(i, n_ctx, n_all), 0))),
        scratch_shapes=[pltpu.VMEM((2, W, 2 * gn), BF16),
                        pltpu.VMEM((2, 2 * B, gn), F32),
                        pltpu.VMEM((2, 2 * B, gn), F32),
                        pltpu.VMEM((2, 2, blk, 128), F32),
                        pltpu.VMEM((2, blk, 2 * gn), F32)],
        compiler_params=_cparams("arbitrary"),
        name="s5_scan",
    )(z, z, lre, lim, ldt, wbre, wbim, wcre, wcim)


def _mla_proj_kernel(z_ref, qn_ref, kvn_ref, wq_ref, wqr_ref, wk_ref, wv_ref, cos_ref, sin_ref,
                     q_ref, k_ref, v_ref, *, scale):
    z = z_ref[0]
    cq = z[:, 0:256]
    ckv = z[:, 256:384]
    kr = z[:, 384:512]
    krr = z[:, 512:640]
    cos = cos_ref[...]
    sin = sin_ref[...]
    cqb = _rms(cq, qn_ref[...]).astype(BF16)
    ckvb = _rms(ckv, kvn_ref[...]).astype(BF16)
    q = _nn(cqb, wq_ref[...])
    qr = _nn(cqb, wqr_ref[...])
    kn = _nn(ckvb, wk_ref[...])
    krp = kr * cos + krr * sin
    for h in range(N_HEADS):
        sl = slice(h * 128, (h + 1) * 128)
        q_ref[0, h] = ((q[:, sl] * cos + qr[:, sl] * sin) * scale).astype(BF16)
        k_ref[0, h] = (kn[:, sl] + krp).astype(BF16)
    ones_pad = ((_iota((1, N_HEADS * 128), 1) % 128) >= HEAD_DIM).astype(F32)
    vv = _nn(ckvb, wv_ref[...]) + ones_pad
    for h in range(N_HEADS):
        v_ref[0, h] = vv[:, h * 128:(h + 1) * 128].astype(BF16)


def _mla_proj_call(z, qn, kvn, wq, wqr, wk, wv, cos_t, sin_t, *, scale):
    B, Ta, _ = z.shape
    tm = _pick_tile(Ta, 544)
    kern = functools.partial(_mla_proj_kernel, scale=scale)
    full = lambda a: pl.BlockSpec(a.shape, lambda b, t: (0,) * a.ndim)
    return pl.pallas_call(
        kern,
        out_shape=(jax.ShapeDtypeStruct((B, N_HEADS, Ta, 128), BF16),
                   jax.ShapeDtypeStruct((B, N_HEADS, Ta, 128), BF16),
                   jax.ShapeDtypeStruct((B, N_HEADS, Ta, 128), BF16)),
        grid=(B, Ta // tm),
        in_specs=[pl.BlockSpec((1, tm, 3 * ZB), lambda b, t: (b, t, 0)),
                  full(qn), full(kvn), full(wq), full(wqr), full(wk), full(wv),
                  pl.BlockSpec((tm, 128), lambda b, t: (t, 0)),
                  pl.BlockSpec((tm, 128), lambda b, t: (t, 0))],
        out_specs=(pl.BlockSpec((1, N_HEADS, tm, 128), lambda b, t: (b, 0, t, 0)),
                   pl.BlockSpec((1, N_HEADS, tm, 128), lambda b, t: (b, 0, t, 0)),
                   pl.BlockSpec((1, N_HEADS, tm, 128), lambda b, t: (b, 0, t, 0))),
        compiler_params=_cparams("parallel", "parallel"),
        name="mla_proj",
    )(z, qn, kvn, wq, wqr, wk, wv, cos_t, sin_t)


def _attn_kernel(q_ref, k_ref, v_ref, o_ref, *, tc, ta, n_ctx_tiles, key_chunk):
    i = pl.program_id(1)
    lane = _iota((1, 128), 1)

    def write_out(acc):
        outs = [a * (1.0 / pltpu.roll(a, HEAD_DIM, 1)) for a in acc]
        for hp in range(N_HEADS // 2):
            o_ref[0, :, hp * 128:(hp + 1) * 128] = jnp.where(lane < HEAD_DIM, outs[2 * hp],
                                                             pltpu.roll(outs[2 * hp + 1], HEAD_DIM, 1))

    @pl.when(i < n_ctx_tiles)
    def _():
        acc = []
        for j in range(N_HEADS):
            s = _nt(q_ref[0, j], k_ref[0, j, 0:tc, :])
            p = jnp.exp2(s - jnp.max(s, axis=-1, keepdims=True))
            acc.append(_nn(p.astype(BF16), v_ref[0, j, 0:tc, :]))
        write_out(acc)

    @pl.when(i >= n_ctx_tiles)
    def _():
        bounds = [0, tc] + list(range(tc + key_chunk, ta + 1, key_chunk))
        units = [(c, j) for c in range(len(bounds) - 1) for j in range(N_HEADS)]
        qs = [q_ref[0, j] for j in range(N_HEADS)]
        score = lambda c, j: _nt(qs[j], k_ref[0, j, bounds[c]:bounds[c + 1], :])
        m = [None] * N_HEADS
        acc = [None] * N_HEADS

        def weighted_values(pend):
            c, j, pb, alpha = pend
            pv = _nn(pb, v_ref[0, j, bounds[c]:bounds[c + 1], :])
            acc[j] = pv if alpha is None else alpha * acc[j] + pv

        pending = None
        s_next = score(*units[0])
        for idx, (c, j) in enumerate(units):
            s = s_next
            if idx + 1 < len(units):
                s_next = score(*units[idx + 1])
            mc = jnp.max(s, axis=-1, keepdims=True)
            if c == 0:
                alpha = None
                m[j] = mc
                p = jnp.exp2(s - mc)
            else:
                m_new = jnp.maximum(m[j], mc)
                alpha = jnp.exp2(m[j] - m_new)
                p = jnp.exp2(s - m_new)
                m[j] = m_new
            if pending is not None:
                weighted_values(pending)
            pending = (c, j, p.astype(BF16), alpha)
        weighted_values(pending)
        write_out(acc)


def _attn_call(q, k, v, *, tc):
    B, H, Ta, _ = q.shape
    tq = _pick_tile(math.gcd(tc, Ta), 256)
    kern = functools.partial(_attn_kernel, tc=tc, ta=Ta, n_ctx_tiles=tc // tq,
                             key_chunk=_pick_tile(Ta - tc, 1024, 128))
    return pl.pallas_call(
        kern,
        out_shape=jax.ShapeDtypeStruct((B, Ta, GROUP_W), F32),
        grid=(B, Ta // tq),
        in_specs=[pl.BlockSpec((1, H, tq, 128), lambda b, i: (b, 0, i, 0)),
                  pl.BlockSpec((1, H, Ta, 128), lambda b, i: (b, 0, 0, 0)),
                  pl.BlockSpec((1, H, Ta, 128), lambda b, i: (b, 0, 0, 0))],
        out_specs=pl.BlockSpec((1, tq, GROUP_W), lambda b, i: (b, i, 0)),
        compiler_params=_cparams("parallel", "arbitrary"),
        name="mla_attn",
    )(q, k, v)


def _chunk_masks(d):
    L = CHUNK
    row = _iota((L, GROUP_W), 0)
    s_idx = _iota((L, GROUP_W), 1) % L
    tr = _iota((L, L), 0)
    tcol = _iota((L, L), 1)
    if d == 0:
        return (tcol <= tr).astype(F32), s_idx < row, s_idx <= row, s_idx == row
    return (tcol >= tr).astype(F32), s_idx > row, s_idx >= row, s_idx == row


def _run_stages(*generators):
    live = list(generators)
    while live:
        for g in list(live):
            try:
                next(g)
            except StopIteration:
                live.remove(g)


def _rwkv_kernel(*refs, cps):
    _run_stages(_rwkv_stages(*refs, cps=cps))


def _rwkv_stages(rf_ref, rb_ref, lf_ref, lb_ref, wup_ref, aup_ref, gup_ref, w0_ref, a0_ref,
                 kk_ref, ka_ref, rk_ref, yf_ref, yb_ref, g_ref, bon_ref, s_scr, tri_scr, bdb_scr, *, cps):
    i = pl.program_id(1)
    L = CHUNK
    W = GROUP_W

    @pl.when(i == 0)
    def _():
        s_scr[...] = jnp.zeros_like(s_scr)
        _group_tri_masks(tri_scr, cps * L)
        bdb_scr[...] = _block_diag_mask().astype(BF16)

    hms = _head_masks()
    bd = _block_diag_mask()
    bdb = bdb_scr[...]
    rs = lambda x: _row_stack(x, hms)
    bd_b = bd > 0.5
    bdiag = lambda x: jnp.where(bd_b, jnp.concatenate([x.astype(BF16)] * N_HEADS, axis=0), 0.0)

    per_dir = []
    for d, (r_ref, l_ref) in enumerate(((rf_ref, lf_ref), (rb_ref, lb_ref))):
        rkv = r_ref[0]
        lora = l_ref[0]
        r = rkv[:, 0:W]
        k = rkv[:, W:2 * W]
        v = rkv[:, 2 * W:3 * W]
        _, strict, incl, eye = _chunk_masks(d)
        tri = tri_scr[d]

        lw = -math.exp(-0.5) * _sigmoid(w0_ref[d] + _mm1(_nn, jnp.tanh(lora), wup_ref[d]))
        a = _sigmoid(a0_ref[d] + _mm1(_nn, lora, aup_ref[d]))
        kkv = k * kk_ref[...]
        kkn = kkv * lax.rsqrt(jnp.maximum(_mm1(_nn, kkv * kkv, bdb), 1e-24))
        keff = k * (1.0 + (a - 1.0) * ka_ref[...])
        kka = kkn * a
        lw_hi, lw_lo = _split2(lw)
        cum2 = _nn(tri, jnp.concatenate([lw_hi, lw_lo], axis=1))
        cum = cum2[:, 0:W] + cum2[:, W:2 * W]
        e_dn = jnp.exp(-cum)
        per_dir.append(dict(v=v, lw=lw, cum=cum, kka=kka, keff=keff, strict=strict, incl=incl, eye=eye,
                            al=-kkn * jnp.exp(cum - lw), rt=r * jnp.exp(cum), bh=kka * e_dn, kh=keff * e_dn))
        if d == 0:
            g_ref[0] = _mm1(_nn, _sigmoid(lora), gup_ref[...])
            bon_ref[0] = _mm1(_nn, r * k * rk_ref[...], bdb) * v
        yield

    chains = []
    for c in range(cps):
        for d in range(2):
            pd = per_dir[d]
            sl = slice(c * L, (c + 1) * L)
            tot = jnp.sum(pd["lw"][sl], axis=0, keepdims=True)
            e_tc = jnp.exp(tot - pd["cum"][sl])
            ch = dict(d=d, c=c, al=pd["al"][sl], rt=pd["rt"][sl], v=pd["v"][sl], tot=tot,
                      bt=pd["kka"][sl] * e_tc, kt=pd["keff"][sl] * e_tc)
            ch["rsv"] = rs(ch["v"])
            a_all = _mm1(_nt, jnp.concatenate([ch["al"], ch["rt"]], axis=0),
                         jnp.concatenate([rs(pd["bh"][sl]), rs(pd["kh"][sl])], axis=0))
            ch["a_ab"] = jnp.where(pd["strict"], a_all[0:L, 0:W], 0.0)
            ch["a_ak"] = jnp.where(pd["strict"], a_all[0:L, W:2 * W], 0.0)
            ch["a_rb"] = jnp.where(pd["incl"], a_all[L:2 * L, 0:W], 0.0)
            ch["a_rk"] = jnp.where(pd["incl"], a_all[L:2 * L, W:2 * W], 0.0)
            ch["p"] = jnp.where(pd["eye"], 1.0, 0.0) + ch["a_ab"]
            chains.append(ch)
        yield

    for ch in chains:
        ch["sq"] = _mmn(_nn, ch["a_ab"], bdiag(ch["a_ab"]))
        ch["zk"] = _mm1(_nn, ch["a_ak"], ch["rsv"])
        ch["y0k"] = _mm1(_nn, ch["a_rk"], ch["rsv"])
    yield
    n_sq = int(math.log2(L)) - 1
    for it in range(n_sq):
        for ch in chains:
            if it < n_sq - 1:
                ps = _mmn(_nn, jnp.concatenate([ch["p"], ch["sq"]], axis=0), bdiag(ch["sq"]))
                ch["p"] = ch["p"] + ps[0:L]
                ch["sq"] = ps[L:2 * L]
            else:
                ch["p"] = ch["p"] + _mmn(_nn, ch["p"], bdiag(ch["sq"]))
        yield
    for ch in chains:
        pu = _mm1(_nn, ch["p"], jnp.concatenate([rs(ch["al"]), rs(ch["zk"])], axis=1))
        ch["w"], ch["uk"] = pu[:, 0:W], pu[:, W:2 * W]
    yield
    for ch in chains:
        gy = _mm1(_nn, ch["a_rb"], jnp.concatenate([rs(ch["w"]), rs(ch["uk"])], axis=1))
        ch["g"] = ch["rt"] + gy[:, 0:W]
        ch["y0"] = gy[:, W:2 * W] + ch["y0k"]
        ch["decay"] = jnp.exp(ch["tot"])
        ch["pm"] = bd * _mm1(_tn, ch["w"], ch["bt"])
        ch["q0"] = bd * _mm1(_tn, jnp.concatenate([ch["uk"], ch["v"]], axis=0),
                             jnp.concatenate([ch["bt"], ch["kt"]], axis=0))
    yield

    by_key = {(ch["d"], ch["c"]): ch for ch in chains}
    st = [s_scr[0], s_scr[1]]
    ys = [[None] * cps, [None] * cps]
    for step in range(cps):
        for d in range(2):
            c = step if d == 0 else cps - 1 - step
            ch = by_key[(d, c)]
            ys[d][c] = _mm1(_nt, ch["g"], st[d]) + ch["y0"]
            st[d] = st[d] * ch["decay"] + (_mm1(_nn, st[d], ch["pm"]) + ch["q0"])
        yield
    for d, y_ref in enumerate((yf_ref, yb_ref)):
        s_scr[d] = st[d]
        y_ref[0] = jnp.concatenate(ys[d], axis=0)


def _chunks_per_step(n_ctx, n_all):
    for cps in (4, 2, 1):
        if n_ctx % cps == 0 and (n_all - n_ctx) % cps == 0:
            return cps


def _rwkv_call(z, wup, aup, gup, w0, a0, kk, ka, rk, *, n_ctx, n_all):
    B, Ta, _ = z.shape
    W = GROUP_W
    cps = _chunks_per_step(n_ctx, n_all)
    rows = cps * CHUNK
    gc, ga = n_ctx // cps, n_all // cps
    full = lambda a: pl.BlockSpec(a.shape, lambda b, i: (0,) * a.ndim)
    fwd = lambda blk: (lambda b, i: (b, i, blk))
    bwd = lambda blk: (lambda b, i: (b, _bwd_chunk(i, gc, ga), blk))
    out = jax.ShapeDtypeStruct((B, Ta, W), F32)
    return pl.pallas_call(
        functools.partial(_rwkv_kernel, cps=cps),
        out_shape=(out, out, out, out),
        grid=(B, ga),
        in_specs=[pl.BlockSpec((1, rows, 3 * W), fwd(Z_RKV // 3)),
                  pl.BlockSpec((1, rows, 3 * W), bwd(Z_RKV // 3)),
                  pl.BlockSpec((1, rows, W), fwd(Z_LORA)),
                  pl.BlockSpec((1, rows, W), bwd(Z_LORA)),
                  full(wup), full(aup), full(gup), full(w0), full(a0), full(kk), full(ka), full(rk)],
        out_specs=(pl.BlockSpec((1, rows, W), fwd(0)), pl.BlockSpec((1, rows, W), bwd(0)),
                   pl.BlockSpec((1, rows, W), fwd(0)), pl.BlockSpec((1, rows, W), fwd(0))),
        scratch_shapes=[pltpu.VMEM((2, W, W), F32), pltpu.VMEM((2, rows, rows), BF16), pltpu.VMEM((W, W), BF16)],
        compiler_params=_cparams("parallel", "arbitrary"),
        name="rwkv7",
    )(z, z, z, z, wup, aup, gup, w0, a0, kk, ka, rk)


def _mlstm_kernel(*refs, cps):
    _run_stages(_mlstm_stages(*refs, cps=cps))


def _mlstm_stages(qf_ref, qb_ref, vf_ref, vb_ref, gf_ref, gb_ref, gbias_ref,
                  yf_ref, yb_ref, c_scr, n_scr, m_scr, tri_scr, exp_scr, bdb_scr, *, cps):
    i = pl.program_id(1)
    L = CHUNK
    W = GROUP_W

    @pl.when(i == 0)
    def _():
        c_scr[...] = jnp.zeros_like(c_scr)
        n_scr[...] = jnp.zeros_like(n_scr)
        m_scr[...] = jnp.zeros_like(m_scr)
        _group_tri_masks(tri_scr, cps * L)
        bdb_scr[...] = _block_diag_mask().astype(BF16)
        ci = _iota((W, W), 0)
        cj = _iota((W, W), 1) // HEAD_DIM
        for d in range(2):
            col = GATE_LANE0 + d * 2 * N_HEADS
            exp_scr[d, :, 0:W] = (ci == col + cj).astype(BF16)
            exp_scr[d, :, W:2 * W] = (ci == col + N_HEADS + cj).astype(BF16)

    hms = _head_masks()
    bd = _block_diag_mask()
    bdb = bdb_scr[...]
    rs = lambda x: _row_stack(x, hms)
    neg_inf = -jnp.inf

    def sum2(x, eb):
        hi, lo = _split2(x)
        return _nn(hi, eb) + _nn(lo, eb)

    per_dir = []
    for d, (q_ref, v_ref, g_ref) in enumerate(((qf_ref, vf_ref, gf_ref), (qb_ref, vb_ref, gb_ref))):
        qk = q_ref[0]
        g = g_ref[0] + gbias_ref[...]
        _, strict, incl, eye = _chunk_masks(d)
        tri = tri_scr[d]
        gates = sum2(g, exp_scr[d])
        li = gates[:, 0:W]
        lf = -_softplus(-gates[:, W:2 * W])
        lf_hi, lf_lo = _split2(lf)
        b2 = _nn(tri, jnp.concatenate([lf_hi, lf_lo], axis=1))
        per_dir.append(dict(q=qk[:, 0:W], k=qk[:, W:2 * W] * (HEAD_DIM ** -0.5), v=v_ref[0][:, 0:W],
                            li=li, bcol=b2[:, 0:W] + b2[:, W:2 * W], incl=incl, eye=eye))
        yield

    chains = []
    for c in range(cps):
        for d in range(2):
            pd = per_dir[d]
            sl = slice(c * L, (c + 1) * L)
            q, k, v, li, bcol = pd["q"][sl], pd["k"][sl], pd["v"][sl], pd["li"][sl], pd["bcol"][sl]
            brow = jnp.sum(jnp.where(pd["eye"], bcol, 0.0), axis=0, keepdims=True)
            lirow = jnp.sum(jnp.where(pd["eye"], li, 0.0), axis=0, keepdims=True)
            logd = jnp.where(pd["incl"], bcol - brow + lirow, neg_inf)
            mx = jnp.zeros((L, W), F32)
            for hm in hms:
                mh = jnp.max(jnp.where(hm, logd, neg_inf), axis=1, keepdims=True)
                mx = jnp.where(hm, mh, mx)
            blast = bcol[L - 1:L, :] if d == 0 else bcol[0:1, :]
            lwc = blast - bcol + li
            mlw = jnp.max(lwc, axis=0, keepdims=True)
            kw = k * jnp.exp(lwc - mlw)
            chains.append(dict(d=d, c=c, q=q, v=v, bcol=bcol, mx=mx, blast=blast, mlw=mlw, kw=kw,
                               dexp=jnp.exp(logd - mx), rsk=rs(k), rsv=rs(v),
                               nu0=jnp.sum(kw, axis=0, keepdims=True)))
        yield
    for ch in chains:
        ch["sp"] = _mm1(_nt, ch["q"], ch["rsk"]) * ch["dexp"]
        ch["cu0"] = bd * _mm1(_tn, ch["v"], ch["kw"])
    yield
    for ch in chains:
        ch["num0"] = _mm1(_nn, ch["sp"], ch["rsv"])
        ch["den0"] = _mm1(_nn, ch["sp"], bdb)
    yield

    by_key = {(ch["d"], ch["c"]): ch for ch in chains}
    c_st = [c_scr[0], c_scr[1]]
    n_row = [n_scr[0, 0:1, :], n_scr[1, 0:1, :]]
    m_row = [m_scr[0, 0:1, :], m_scr[1, 0:1, :]]
    ys = [[None] * cps, [None] * cps]
    for step in range(cps):
        for d in range(2):
            c = step if d == 0 else cps - 1 - step
            ch = by_key[(d, c)]
            inter = ch["bcol"] + m_row[d]
            mt = jnp.maximum(inter, ch["mx"])
            f_in = jnp.exp(ch["mx"] - mt)
            w_int = jnp.exp(inter - mt)
            num = f_in * ch["num0"] + w_int * _mm1(_nt, ch["q"], c_st[d])
            den = f_in * ch["den0"] + w_int * _mm1(_nn, ch["q"] * n_row[d], bdb)
            ys[d][c] = num / jnp.maximum(jnp.abs(den), jnp.exp(-mt))
            m_new = jnp.maximum(ch["blast"] + m_row[d], ch["mlw"])
            sc = jnp.exp(ch["blast"] + m_row[d] - m_new)
            e2 = jnp.exp(ch["mlw"] - m_new)
            c_st[d] = sc * c_st[d] + e2 * ch["cu0"]
            n_row[d] = sc * n_row[d] + e2 * ch["nu0"]
            m_row[d] = m_new
        yield
    for d, y_ref in enumerate((yf_ref, yb_ref)):
        c_scr[d] = c_st[d]
        n_scr[d, 0:1, :] = n_row[d]
        m_scr[d, 0:1, :] = m_row[d]
        y_ref[0] = jnp.concatenate(ys[d], axis=0)


def _mlstm_call(z, gbias, *, n_ctx, n_all):
    B, Ta, _ = z.shape
    W = GROUP_W
    cps = _chunks_per_step(n_ctx, n_all)
    L = cps * CHUNK
    n_ctx, n_all = n_ctx // cps, n_all // cps
    fwd = lambda blk: (lambda b, i: (b, i, blk))
    bwd = lambda blk: (lambda b, i: (b, _bwd_chunk(i, n_ctx, n_all), blk))
    out = jax.ShapeDtypeStruct((B, Ta, W), F32)
    return pl.pallas_call(
        functools.partial(_mlstm_kernel, cps=cps),
        out_shape=(out, out),
        grid=(B, n_all),
        in_specs=[pl.BlockSpec((1, L, 2 * W), fwd(Z_MQK // 2)),
                  pl.BlockSpec((1, L, 2 * W), bwd(Z_MQK // 2)),
                  pl.BlockSpec((1, L, 2 * W), fwd(Z_MVO // 2)),
                  pl.BlockSpec((1, L, 2 * W), bwd(Z_MVO // 2)),
                  pl.BlockSpec((1, L, W), fwd(Z_GATE)),
                  pl.BlockSpec((1, L, W), bwd(Z_GATE)),
                  pl.BlockSpec(gbias.shape, lambda b, i: (0, 0))],
        out_specs=(pl.BlockSpec((1, L, W), fwd(0)), pl.BlockSpec((1, L, W), bwd(0))),
        scratch_shapes=[pltpu.VMEM((2, W, W), F32), pltpu.VMEM((2, 8, W), F32), pltpu.VMEM((2, 8, W), F32),
                        pltpu.VMEM((2, L, L), BF16), pltpu.VMEM((2, W, 2 * W), BF16), pltpu.VMEM((W, W), BF16)],
        compiler_params=_cparams("parallel", "arbitrary"),
        name="mlstm",
    )(z, z, z, z, z, z, gbias)


def _recur_kernel(*refs, cps):
    r_in, m_in = refs[0:12], refs[12:19]
    r_out, m_out = refs[19:23], refs[23:25]
    r_scr, m_scr = refs[25:28], refs[28:34]
    _run_stages(_rwkv_stages(*r_in, *r_out, *r_scr, cps=cps),
                _mlstm_stages(*m_in, *m_out, *m_scr, cps=cps))


def _recur_call(z, wup, aup, gup, w0, a0, kk, ka, rk, gbias, *, n_ctx, n_all):
    B, Ta, _ = z.shape
    W = GROUP_W
    cps = _chunks_per_step(n_ctx, n_all)
    rows = cps * CHUNK
    gc, ga = n_ctx // cps, n_all // cps
    full = lambda a: pl.BlockSpec(a.shape, lambda b, i: (0,) * a.ndim)
    fwd = lambda blk: (lambda b, i: (b, i, blk))
    bwd = lambda blk: (lambda b, i: (b, _bwd_chunk(i, gc, ga), blk))
    tok = lambda width, imap: pl.BlockSpec((1, rows, width), imap)
    out = jax.ShapeDtypeStruct((B, Ta, W), F32)
    return pl.pallas_call(
        functools.partial(_recur_kernel, cps=cps),
        out_shape=(out,) * 6,
        grid=(B, ga),
        in_specs=[tok(3 * W, fwd(Z_RKV // 3)), tok(3 * W, bwd(Z_RKV // 3)),
                  tok(W, fwd(Z_LORA)), tok(W, bwd(Z_LORA)),
                  full(wup), full(aup), full(gup), full(w0), full(a0), full(kk), full(ka), full(rk),
                  tok(2 * W, fwd(Z_MQK // 2)), tok(2 * W, bwd(Z_MQK // 2)),
                  tok(2 * W, fwd(Z_MVO // 2)), tok(2 * W, bwd(Z_MVO // 2)),
                  tok(W, fwd(Z_GATE)), tok(W, bwd(Z_GATE)), full(gbias)],
        out_specs=(tok(W, fwd(0)), tok(W, bwd(0)), tok(W, fwd(0)), tok(W, fwd(0)),
                   tok(W, fwd(0)), tok(W, bwd(0))),
        scratch_shapes=[pltpu.VMEM((2, W, W), F32), pltpu.VMEM((2, rows, rows), BF16), pltpu.VMEM((W, W), BF16),
                        pltpu.VMEM((2, W, W), F32), pltpu.VMEM((2, 8, W), F32), pltpu.VMEM((2, 8, W), F32),
                        pltpu.VMEM((2, rows, rows), BF16), pltpu.VMEM((2, W, 2 * W), BF16),
                        pltpu.VMEM((W, W), BF16)],
        compiler_params=_cparams("parallel", "arbitrary"),
        name="rwkv_mlstm",
    )(z, z, z, z, wup, aup, gup, w0, a0, kk, ka, rk, z, z, z, z, z, z, gbias)


def _head_norm(y, bd, eps):
    bdb = bd.astype(BF16)

    def head_mean(x):
        hi, lo = _split2(x)
        return (_nn(hi, bdb) + _nn(lo, bdb)) * (1.0 / HEAD_DIM)

    yc = y - head_mean(y)
    return yc * lax.rsqrt(head_mean(yc * yc) + eps)


def _outproj_kernel(x_ref, mods_ref, s5f_ref, s5b_ref, u_ref, at_ref, rf_ref, rb_ref, rg_ref, rbon_ref,
                    mf_ref, mb_ref, vo_ref, s5d_ref, wglu_ref, bglu_ref, lnw_ref, lnb_ref, mnw_ref,
                    gpost_ref, wout_ref, o_ref, *, tm, tc, n_batch):
    b = pl.program_id(0)
    t = pl.program_id(1)
    d = x_ref.shape[-1]
    W = GROUP_W
    is_ctx = (t * tm + _iota((tm, 1), 0)) < tc
    gate = jnp.where(is_ctx, mods_ref[pl.ds(n_batch, 1), pl.ds(5 * d, d)],
                     mods_ref[pl.ds(b, 1), pl.ds(5 * d, d)])
    bd = _block_diag_mask()

    y = s5f_ref[...] + s5b_ref[...] + s5d_ref[...] * u_ref[0]
    zg = 0.5 * y * (1.0 + jnp.tanh(math.sqrt(2.0 / math.pi) * (y + 0.044715 * (y * y * y))))
    s5o = zg * _sigmoid(_mm1(_nn, zg, wglu_ref[...]) + bglu_ref[...])

    yr = _head_norm(rf_ref[0] + rb_ref[0], bd, RWKV_GN_EPS)
    rwo = (yr * lnw_ref[...] + lnb_ref[...] + rbon_ref[0]) * rg_ref[0]

    ym = _head_norm(mf_ref[0] + mb_ref[0], bd, NORM_EPS)
    mlo = ym * mnw_ref[...] * _sigmoid(vo_ref[0][:, W:2 * W])

    cat = jnp.concatenate([s5o, at_ref[0], rwo, mlo], axis=1).astype(BF16)
    yx = _nn(cat, wout_ref[...])
    o_ref[0] = x_ref[0] + gate * _rms(yx, gpost_ref[...])


def _outproj_call(xa, mods, s5f, s5b, z, attn, rf, rb, rg, rbon, mf, mb,
                  s5d, wglu, bglu, lnw, lnb, mnw, gpost, wout, *, tc):
    B, Ta, D = xa.shape
    W = GROUP_W
    tm = _pick_tile(Ta, 544)
    kern = functools.partial(_outproj_kernel, tm=tm, tc=tc, n_batch=B)
    full = lambda a: pl.BlockSpec(a.shape, lambda b, t: (0,) * a.ndim)
    tok = pl.BlockSpec((1, tm, W), lambda b, t: (b, t, 0))
    tmaj = pl.BlockSpec((tm, W), lambda b, t: (t, b))
    return pl.pallas_call(
        kern,
        out_shape=jax.ShapeDtypeStruct((B, Ta, D), F32),
        grid=(B, Ta // tm),
        in_specs=[pl.BlockSpec((1, tm, D), lambda b, t: (b, t, 0)), full(mods),
                  tmaj, tmaj, pl.BlockSpec((1, tm, W), lambda b, t: (b, t, Z_S5)),
                  tok, tok, tok, tok, tok, tok, tok,
                  pl.BlockSpec((1, tm, 2 * W), lambda b, t: (b, t, Z_MVO // 2)),
                  full(s5d), full(wglu), full(bglu), full(lnw), full(lnb), full(mnw), full(gpost), full(wout)],
        out_specs=pl.BlockSpec((1, tm, D), lambda b, t: (b, t, 0)),
        compiler_params=_cparams("parallel", "parallel"),
        name="mix_out",
    )(xa, mods, s5f, s5b, z, attn, rf, rb, rg, rbon, mf, mb, z, s5d, wglu, bglu, lnw, lnb, mnw, gpost, wout)


def _rope_rotate_cols(w):
    h = ROPE_AXIS // 2
    return jnp.concatenate([-w[..., h:2 * h], w[..., 0:h], -w[..., 3 * h:4 * h], w[..., 2 * h:3 * h]], axis=-1)


def _inproj_relayout(w_in):
    L, D, _ = w_in.shape
    o_s5, o_mla, o_rw, o_ml = 0, 256, 672, 1568
    seg = lambda a, n: w_in[:, :, a:a + n]
    zer = lambda n: jnp.zeros((L, D, n), w_in.dtype)
    k_rope = seg(o_mla + 384, ROPE_DIM)
    parts = [seg(o_mla, 256), seg(o_mla + 256, 128),
             zer(64), k_rope, zer(32),
             zer(64), _rope_rotate_cols(k_rope), zer(32), seg(o_ml + 1024, 16), zer(128 - 16),
             seg(o_rw, 768), seg(o_s5, 256),
             seg(o_rw + 768, 128), zer(128),
             seg(o_ml, 512), seg(o_ml + 512, 512)]
    out = jnp.concatenate(parts, axis=2)
    assert out.shape[2] == Z_COLS
    return out


def _rope_tables(T, tc):
    rows = T // GRID_W
    r_idx, c_idx = jnp.meshgrid(jnp.arange(rows), jnp.arange(GRID_W), indexing='ij')
    inv_freq = 1.0 / (ROPE_BASE ** (jnp.arange(0, ROPE_AXIS, 2, dtype=F32) / ROPE_AXIS))
    ang_r = r_idx.reshape(-1, 1).astype(F32) * inv_freq
    ang_c = c_idx.reshape(-1, 1).astype(F32) * inv_freq
    ang = jnp.concatenate([ang_r, ang_r, ang_c, ang_c], axis=-1)
    cos = jnp.concatenate([jnp.ones((tc, ROPE_DIM), F32), jnp.cos(ang)], axis=0)
    sin = jnp.concatenate([jnp.zeros((tc, ROPE_DIM), F32), jnp.sin(ang)], axis=0)
    ta = T + tc
    cos_t = jnp.concatenate([jnp.ones((ta, 64), F32), cos, jnp.zeros((ta, 32), F32)], axis=1)
    sin_t = jnp.concatenate([jnp.zeros((ta, 64), F32), sin, jnp.zeros((ta, 32), F32)], axis=1)
    return cos_t, sin_t


def _pad_rows(w, r0, total):
    pad = [(0, 0)] * (w.ndim - 2) + [(r0, total - r0 - w.shape[-2]), (0, 0)]
    return jnp.pad(w, pad)


def kernel(x, c, ctx, c_ctx, w_ada, b_ada, norm_pre, norm_post, ffn_w_gate, ffn_w_up, ffn_w_down, w_in, w_out, s5_lam_re, s5_lam_im, s5_log_dt, s5_b_re, s5_b_im, s5_c_re, s5_c_im, s5_d, s5_w_glu, s5_b_glu, mla_q_norm, mla_kv_norm, mla_w_uq, mla_w_ukv, rwkv_conv_w, rwkv_conv_b, rwkv_w0, rwkv_w_up, rwkv_a0, rwkv_a_up, rwkv_g_up, rwkv_k_k, rwkv_k_a, rwkv_r_k, rwkv_ln_w, rwkv_ln_b, mlstm_conv_w, mlstm_conv_b, mlstm_gate_b, mlstm_norm):
    B, T, D = x.shape
    Tc = ctx.shape[1]
    Ta = T + Tc
    L = w_ada.shape[0]
    W = GROUP_W
    assert T % CHUNK == 0 and Tc % CHUNK == 0 and B % 8 == 0 and B <= 8
    n_ctx, n_all = Tc // CHUNK, Ta // CHUNK

    rows = 16
    cvec = jnp.concatenate([c, c_ctx[None, :], jnp.zeros((rows - B - 1, D), F32)], axis=0)
    mods_all = _ada_call(cvec, w_ada, b_ada)

    w_in_re = _inproj_relayout(w_in).astype(BF16)
    cw = jnp.zeros((L, Z_NBLK, 8, ZB), F32)
    rc = jnp.concatenate([rwkv_conv_w, rwkv_conv_b[:, None, :]], axis=1).reshape(L, 4, 3, ZB).transpose(0, 2, 1, 3)
    mc = jnp.concatenate([mlstm_conv_w, mlstm_conv_b[:, None, :]], axis=1).reshape(L, 4, 2, ZB).transpose(0, 2, 1, 3)
    cw = cw.at[:, Z_RKV:Z_RKV + 3, 0:4].set(rc).at[:, Z_MQK:Z_MQK + 2, 0:4].set(mc)

    wg = ffn_w_gate.astype(BF16)
    wu = ffn_w_up.astype(BF16)
    wd = ffn_w_down.astype(BF16)
    wout = w_out.astype(BF16)

    G = s5_lam_re.shape[2]
    N = s5_lam_re.shape[3]
    eye_g = jnp.eye(G, dtype=F32)
    lre = s5_lam_re.reshape(L, 2, 1, G * N)
    lim = s5_lam_im.reshape(L, 2, 1, G * N)
    ldt = jnp.repeat(s5_log_dt, N, axis=-1).reshape(L, 2, 1, G * N)
    wbre = jnp.einsum('ldgnp,gh->ldgphn', s5_b_re, eye_g).reshape(L, 2, G * S5_P, G * N)
    wbim = jnp.einsum('ldgnp,gh->ldgphn', s5_b_im, eye_g).reshape(L, 2, G * S5_P, G * N)
    wcre = jnp.einsum('ldgpn,gh->ldgnhp', s5_c_re, eye_g).reshape(L, 2, G * N, G * S5_P).astype(BF16)
    wcim = jnp.einsum('ldgpn,gh->ldgnhp', s5_c_im, eye_g).reshape(L, 2, G * N, G * S5_P).astype(BF16)

    nope = HEAD_DIM
    qd = nope + ROPE_DIM
    wq4 = mla_w_uq.reshape(L, -1, N_HEADS, qd)
    wq = jnp.pad(wq4, ((0, 0), (0, 0), (0, 0), (0, 128 - qd))).reshape(L, -1, N_HEADS * 128).astype(BF16)
    wq_rot = _rope_rotate_cols(wq4[..., nope:])
    wqr = jnp.pad(wq_rot, ((0, 0), (0, 0), (0, 0), (nope, 128 - qd))).reshape(L, -1, N_HEADS * 128).astype(BF16)
    wkv4 = mla_w_ukv.reshape(L, -1, N_HEADS, 2 * HEAD_DIM)
    wk = jnp.pad(wkv4[..., :HEAD_DIM], ((0, 0), (0, 0), (0, 0), (0, 64))).reshape(L, -1, N_HEADS * 128).astype(BF16)
    wv = jnp.pad(wkv4[..., HEAD_DIM:], ((0, 0), (0, 0), (0, 0), (0, 64))).reshape(L, -1, N_HEADS * 128).astype(BF16)
    cos_t, sin_t = _rope_tables(T, Tc)
    scale = float(qd) ** -0.5 * math.log2(math.e)

    wup = _pad_rows(rwkv_w_up, 0, W)
    aup = _pad_rows(rwkv_a_up, 32, W)
    gup = _pad_rows(rwkv_g_up, 64, W)
    gbias = jnp.pad(mlstm_gate_b, ((0, 0), (GATE_LANE0, W - GATE_LANE0 - mlstm_gate_b.shape[1])))

    xa = jnp.concatenate([ctx, x], axis=1)
    r1 = lambda a: a.reshape(1, -1)

    for l in range(L):
        mods = mods_all[l]
        xa2, hmix = _ffn_call(xa.reshape(B * Ta, D), mods, r1(norm_pre[l, 0]), r1(norm_post[l, 0]),
                              r1(norm_pre[l, 1]), wg[l, 0], wu[l, 0], wd[l, 0],
                              ta=Ta, tc=Tc, n_batch=B, koff=0, emit_hmix=True)
        xa = xa2.reshape(B, Ta, D)
        z = _inproj_call(hmix.reshape(B, Ta, D), w_in_re[l], cw[l], tc=Tc)

        s5f, s5b = _s5_call(z, lre[l], lim[l], ldt[l], wbre[l], wbim[l], wcre[l], wcim[l],
                            n_ctx=n_ctx, n_all=n_all)

        q, k, v = _mla_proj_call(z, r1(mla_q_norm[l]), r1(mla_kv_norm[l]), wq[l], wqr[l], wk[l], wv[l],
                                 cos_t, sin_t, scale=scale)
        attn = _attn_call(q, k, v, tc=Tc)

        rf, rb, rg, rbon, mf, mb = _recur_call(
            z, wup[l], aup[l], gup[l], rwkv_w0[l][:, None, :], rwkv_a0[l][:, None, :],
            r1(rwkv_k_k[l]), r1(rwkv_k_a[l]), r1(rwkv_r_k[l]), gbias[l:l + 1], n_ctx=n_ctx, n_all=n_all)

        xa = _outproj_call(xa, mods, s5f, s5b, z, attn, rf, rb, rg, rbon,
                           mf, mb, r1(s5_d[l]), s5_w_glu[l].astype(BF16), r1(s5_b_glu[l]), r1(rwkv_ln_w[l]),
                           r1(rwkv_ln_b[l]), r1(mlstm_norm[l]), r1(norm_post[l, 1]), wout[l], tc=Tc)

        xa2, _ = _ffn_call(xa.reshape(B * Ta, D), mods, r1(norm_pre[l, 2]), r1(norm_post[l, 2]),
                           r1(norm_pre[l, 1]), wg[l, 1], wu[l, 1], wd[l, 1],
                           ta=Ta, tc=Tc, n_batch=B, koff=6, emit_hmix=False)
        xa = xa2.reshape(B, Ta, D)

    return xa[:, Tc:, :]
```

```python
import functools
import math

import numpy as np
import jax
import jax.numpy as jnp
from jax import lax
from jax.experimental import pallas as pl
from jax.experimental.pallas import tpu as pltpu

F32 = jnp.float32
BF16 = jnp.bfloat16

GROUP_W = 256
HEAD_DIM = 64
N_HEADS = GROUP_W // HEAD_DIM
CHUNK = 64
S5_CHUNK = 128
N_MOD = 9
NORM_EPS = 1e-6
RWKV_GN_EPS = HEAD_DIM * 1e-5
GRID_W = 64
ROPE_BASE = 10000.0
ROPE_DIM = 32
ROPE_AXIS = 16
S5_P = 16
S5_STATE = 64
MACARON = 0.5
VMEM_LIMIT_BYTES = 56 * 1024 * 1024

ZB = 256
Z_MLA, Z_RKV, Z_S5, Z_LORA, Z_MQK, Z_MVO = 0, 3, 6, 7, 8, 10
Z_GATE = 2
GATE_LANE0 = 128
Z_NBLK = 12
Z_COLS = Z_NBLK * ZB


def _nn(a, b):
    return lax.dot_general(a, b, (((1,), (0,)), ((), ())), preferred_element_type=F32)


def _nt(a, b):
    return lax.dot_general(a, b, (((1,), (1,)), ((), ())), preferred_element_type=F32)


def _tn(a, b):
    return lax.dot_general(a, b, (((0,), (0,)), ((), ())), preferred_element_type=F32)


def _split2(x):
    hi = x.astype(BF16)
    lo = (x - hi.astype(F32)).astype(BF16)
    return hi, lo


def _split3(x):
    p1 = x.astype(BF16)
    r = x - p1.astype(F32)
    p2 = r.astype(BF16)
    p3 = (r - p2.astype(F32)).astype(BF16)
    return p1, p2, p3


def _mm3(dotf, a, b):
    ah, al = _split2(a)
    bh, bl = _split2(b)
    return dotf(ah, bh) + (dotf(ah, bl) + dotf(al, bh))


def _mm1(dotf, a, b):
    return dotf(a.astype(BF16), b.astype(BF16))


_mmn = _mm1


def _mm_exact_rhs(a, e):
    eb = e.astype(BF16)
    p1, p2, p3 = _split3(a)
    return _nn(p1, eb) + (_nn(p2, eb) + _nn(p3, eb))


def _mm_exact_lhs(e, a):
    eb = e.astype(BF16)
    p1, p2, p3 = _split3(a)
    return _nn(eb, p1) + (_nn(eb, p2) + _nn(eb, p3))


def _rms(x, g):
    return x * lax.rsqrt(jnp.mean(x * x, axis=-1, keepdims=True) + NORM_EPS) * g


def _sigmoid(x):
    return 0.5 * jnp.tanh(0.5 * x) + 0.5


def _group_tri_masks(tri_scr, rows):
    r = _iota((rows, rows), 0)
    c = _iota((rows, rows), 1)
    same = (r // CHUNK) == (c // CHUNK)
    tri_scr[0] = (same & (c <= r)).astype(BF16)
    tri_scr[1] = (same & (c >= r)).astype(BF16)


def _softplus(x):
    return jnp.maximum(x, 0.0) + jnp.log(1.0 + jnp.exp(-jnp.abs(x)))


def _iota(shape, dim):
    return lax.broadcasted_iota(jnp.int32, shape, dim)


def _head_masks():
    lane = _iota((1, GROUP_W), 1)
    return [lane // HEAD_DIM == h for h in range(N_HEADS)]


def _block_diag_mask():
    r = _iota((GROUP_W, GROUP_W), 0) // HEAD_DIM
    c = _iota((GROUP_W, GROUP_W), 1) // HEAD_DIM
    return (r == c).astype(F32)


def _row_stack(x, hms):
    xb = x.astype(BF16)
    return jnp.concatenate([jnp.where(m, xb, 0.0) for m in hms], axis=0)


def _pick_tile(n, target, mult=16):
    best = None
    for t in range(mult, min(n, target) + 1, mult):
        if n % t == 0:
            best = t
    if best is None:
        raise ValueError(f"no tile for {n}")
    return best


def _cparams(*sem):
    return pltpu.CompilerParams(dimension_semantics=sem, vmem_limit_bytes=VMEM_LIMIT_BYTES)


def _ada_kernel(c_ref, w_ref, b_ref, o_ref):
    c = c_ref[...]
    s = c * _sigmoid(c)
    o_ref[0] = _mm3(_nn, s, w_ref[0]) + b_ref[0]


def _ada_call(cvec, w_ada, b_ada):
    L, D, N = w_ada.shape
    R = cvec.shape[0]
    tn = _pick_tile(N, 1152, 128)
    return pl.pallas_call(
        _ada_kernel,
        out_shape=jax.ShapeDtypeStruct((L, R, N), F32),
        grid=(L, N // tn),
        in_specs=[pl.BlockSpec((R, D), lambda l, j: (0, 0)),
                  pl.BlockSpec((1, D, tn), lambda l, j: (l, 0, j)),
                  pl.BlockSpec((1, 1, tn), lambda l, j: (l, 0, j))],
        out_specs=pl.BlockSpec((1, R, tn), lambda l, j: (l, 0, j)),
        compiler_params=_cparams("parallel", "parallel"),
        name="ada_mod",
    )(cvec, w_ada, b_ada.reshape(L, 1, N))


FFN_COLS = 256


def _ffn_kernel(xn_ref, xp_ref, mods_ref, gpre_ref, gpost_ref, gmix_ref, wg_ref, wu_ref, wd_ref,
                o_ref, hmix_ref, h_scr, acc_scr, *, tm, tiles_per_batch, tc, n_batch, koff, emit_hmix,
                n_tiles):
    i = pl.program_id(0)
    f = pl.program_id(1)
    d = xn_ref.shape[-1]
    fd = wg_ref.shape[1]

    def mod_of(tile):
        b = tile // tiles_per_batch
        t0 = (tile % tiles_per_batch) * tm
        is_ctx = (t0 + _iota((tm, 1), 0)) < tc

        def mod(fn):
            row = lambda r: (lambda k: mods_ref[pl.ds(r, 1), pl.ds(k * d, d)])
            return jnp.where(is_ctx, fn(row(n_batch)), fn(row(b)))
        return mod

    unit = lambda x: x * lax.rsqrt(jnp.mean(x * x, axis=-1, keepdims=True) + NORM_EPS)

    def pre_norm(tile, x):
        mod = mod_of(tile)
        gain = mod(lambda m: gpre_ref[...] * (1.0 + m(koff + 1)))
        return (unit(x) * gain + mod(lambda m: m(koff))).astype(BF16)

    def finish(tile, acc, x):
        mod = mod_of(tile)
        xn = x + unit(acc) * mod(lambda m: (MACARON * m(koff + 2)) * gpost_ref[...])
        o_ref[...] = xn
        if emit_hmix:
            gain = mod(lambda m: gmix_ref[...] * (1.0 + m(4)))
            hmix_ref[...] = (unit(xn) * gain + mod(lambda m: m(3))).astype(BF16)
        else:
            hmix_ref[...] = jnp.zeros_like(hmix_ref)

    def hidden_cols(hb, lo, hi):
        acc = None
        for c0 in range(lo, hi, FFN_COLS):
            g = _nn(hb, wg_ref[:, c0:c0 + FFN_COLS])
            u = _nn(hb, wu_ref[:, c0:c0 + FFN_COLS])
            t = _nn((g * _sigmoid(g) * u).astype(BF16), wd_ref[c0:c0 + FFN_COLS, :])
            acc = t if acc is None else acc + t
        return acc

    @pl.when(jnp.logical_and(i == 0, f == 0))
    def _():
        h_scr[0] = pre_norm(0, xp_ref[...])
        acc_scr[1] = jnp.zeros((tm, d), F32)

    for s in range(2):
        tile = 2 * i + s

        @pl.when(jnp.logical_and(tile < n_tiles, f == s))
        def _(s=s, tile=tile):
            finish(jnp.maximum(tile - 1, 0), acc_scr[1 - s], xp_ref[...])
            h_scr[1 - s] = pre_norm(jnp.minimum(tile + 1, n_tiles - 1), xn_ref[...])
            acc_scr[s] = hidden_cols(h_scr[s], 0, fd)

    @pl.when(jnp.logical_and(2 * i == n_tiles, f == 0))
    def _():
        finish(n_tiles - 1, acc_scr[1], xp_ref[...])


def _ffn_call(xa2, mods, gpre, gpost, gmix, wg, wu, wd, *, ta, tc, n_batch, koff, emit_hmix):
    M, D = xa2.shape
    Fd = wg.shape[1]
    assert Fd % FFN_COLS == 0
    tm = _pick_tile(ta, 544)
    n_tiles = M // tm
    assert n_tiles % 2 == 0
    kern = functools.partial(_ffn_kernel, tm=tm, tiles_per_batch=ta // tm, tc=tc, n_batch=n_batch,
                             koff=koff, emit_hmix=emit_hmix, n_tiles=n_tiles)
    hm_rows = tm if emit_hmix else 16
    last = n_tiles - 1
    resident = lambda a: pl.BlockSpec(a.shape, lambda i, f: (0, 0), pipeline_mode=pl.Buffered(1))
    vec = pl.BlockSpec((1, D), lambda i, f: (0, 0))
    return pl.pallas_call(
        kern,
        out_shape=(jax.ShapeDtypeStruct((M, D), F32),
                   jax.ShapeDtypeStruct((M if emit_hmix else 16 * n_tiles, D), BF16)),
        grid=(n_tiles // 2 + 1, 2),
        in_specs=[pl.BlockSpec((tm, D), lambda i, f: (jnp.minimum(2 * i + f + 1, last), 0)),
                  pl.BlockSpec((tm, D), lambda i, f: (jnp.clip(2 * i + f - 1, 0, last), 0)),
                  pl.BlockSpec(mods.shape, lambda i, f: (0, 0)),
                  vec, vec, vec, resident(wg), resident(wu), resident(wd)],
        out_specs=(pl.BlockSpec((tm, D), lambda i, f: (jnp.clip(2 * i + f - 1, 0, last), 0)),
                   pl.BlockSpec((hm_rows, D), lambda i, f: (jnp.clip(2 * i + f - 1, 0, last), 0))),
        scratch_shapes=[pltpu.VMEM((2, tm, D), BF16), pltpu.VMEM((2, tm, D), F32)],
        compiler_params=_cparams("arbitrary", "arbitrary"),
        name="half_ffn",
    )(xa2, xa2, mods, gpre, gpost, gmix, wg, wu, wd)


def _inproj_kernel(h_ref, w_ref, cw_ref, o_ref, *, tc, conv_lo, conv_hi, silu_lo, silu_hi):
    nb = pl.program_id(1)
    ta = h_ref.shape[1]
    is_conv = ((nb >= conv_lo[0]) & (nb < conv_hi[0])) | ((nb >= conv_lo[1]) & (nb < conv_hi[1]))
    is_silu = (nb >= silu_lo) & (nb < silu_hi)

    @pl.when(jnp.logical_not(is_conv))
    def _():
        o_ref[0] = _nn(h_ref[0], w_ref[...])

    def conv():
        z = _nn(h_ref[0], w_ref[...])
        row = _iota((ta, 1), 0)
        zp = jnp.where((row == 0) | (row == tc), 0.0, pltpu.roll(z, 1, 0))
        zn = jnp.where((row == tc - 1) | (row == ta - 1), 0.0, pltpu.roll(z, ta - 1, 0))
        cw = cw_ref[0]
        return cw[3:4] + zp * cw[0:1] + z * cw[1:2] + zn * cw[2:3]

    @pl.when(is_conv & jnp.logical_not(is_silu))
    def _():
        o_ref[0] = conv()

    @pl.when(is_conv & is_silu)
    def _():
        y = conv()
        o_ref[0] = y * _sigmoid(y)


def _inproj_call(hmix3, w_re, cw, *, tc):
    B, Ta, D = hmix3.shape
    kern = functools.partial(_inproj_kernel, tc=tc, conv_lo=(Z_RKV, Z_MQK), conv_hi=(Z_RKV + 3, Z_MQK + 2),
                             silu_lo=Z_MQK, silu_hi=Z_MQK + 2)
    return pl.pallas_call(
        kern,
        out_shape=jax.ShapeDtypeStruct((B, Ta, Z_COLS), F32),
        grid=(B, Z_NBLK),
        in_specs=[pl.BlockSpec((1, Ta, D), lambda b, n: (b, 0, 0)),
                  pl.BlockSpec((D, ZB), lambda b, n: (0, n)),
                  pl.BlockSpec((1, 8, ZB), lambda b, n: (n, 0, 0))],
        out_specs=pl.BlockSpec((1, Ta, ZB), lambda b, n: (b, 0, n)),
        compiler_params=_cparams("parallel", "arbitrary"),
        name="in_proj",
    )(hmix3, w_re, cw)


def _bwd_chunk(i, n_ctx, n_all):
    return jnp.where(i < n_ctx, n_ctx - 1 - i, n_all - 1 - (i - n_ctx))


def _s5_kernel(uf_ref, ub_ref, lre_ref, lim_ref, ldt_ref, wbre_ref, wbim_ref, wcre_ref, wcim_ref,
               yf_ref, yb_ref, wb_scr, coef_scr, st_scr, rel_scr, x_scr, *, lc, nb, scan_unroll):
    i = pl.program_id(0)
    gn = lre_ref.shape[-1]

    @pl.when(i == 0)
    def _():
        for d in range(2):
            dt = jnp.exp(ldt_ref[d])
            lre = lre_ref[d]
            lim = lim_ref[d]
            mag = jnp.exp(lre * dt)
            ar = mag * jnp.cos(lim * dt)
            ai = mag * jnp.sin(lim * dt)
            den = lre * lre + lim * lim
            fr = ((ar - 1.0) * lre + ai * lim) / den
            fi = (ai * lre - (ar - 1.0) * lim) / den
            coef_scr[d, 0:nb, :] = jnp.broadcast_to(ar, (nb, gn))
            coef_scr[d, nb:2 * nb, :] = jnp.broadcast_to(ai, (nb, gn))
            wre = wbre_ref[d]
            wim = wbim_ref[d]
            wb_scr[d, :, 0:gn] = (wre * fr - wim * fi).astype(BF16)
            wb_scr[d, :, gn:2 * gn] = (wim * fr + wre * fi).astype(BF16)
        st_scr[...] = jnp.zeros_like(st_scr)

    half = 128
    u_refs = (uf_ref, ub_ref)
    y_refs = (yf_ref, yb_ref)

    for d in range(2):
        for b in range(nb):
            for s in range(2):
                rel_scr[d, s, pl.ds(b, lc, stride=nb), :] = u_refs[d][b, :, s * half:(s + 1) * half]
    for d in range(2):
        u_tm = jnp.concatenate([rel_scr[d, 0], rel_scr[d, 1]], axis=1).astype(BF16)
        x_scr[d] = _nn(u_tm, wb_scr[d])

    for d in range(2):
        def body(t, carry, d=d):
            sr, si = carry
            ar = coef_scr[d, 0:nb, :]
            ai = coef_scr[d, nb:2 * nb, :]
            tt = t if d == 0 else lc - 1 - t
            r0 = pl.multiple_of(tt * nb, nb)
            xr = x_scr[d, pl.ds(r0, nb), 0:gn]
            xi = x_scr[d, pl.ds(r0, nb), gn:2 * gn]
            nsr = ar * sr - ai * si + xr
            nsi = ar * si + ai * sr + xi
            x_scr[d, pl.ds(r0, nb), 0:gn] = nsr
            x_scr[d, pl.ds(r0, nb), gn:2 * gn] = nsi
            return nsr, nsi

        sr, si = lax.fori_loop(0, lc, body, (st_scr[d, 0:nb, :], st_scr[d, nb:2 * nb, :]), unroll=scan_unroll)
        st_scr[d, 0:nb, :] = sr
        st_scr[d, nb:2 * nb, :] = si
        y = (_nn(x_scr[d, :, 0:gn].astype(BF16), wcre_ref[d])
             - _nn(x_scr[d, :, gn:2 * gn].astype(BF16), wcim_ref[d]))
        rel_scr[d, 0] = y[:, 0:half]
        rel_scr[d, 1] = y[:, half:2 * half]
    for d in range(2):
        for b in range(nb):
            for s in range(2):
                c0 = b * 2 * half + s * half
                y_refs[d][:, c0:c0 + half] = rel_scr[d, s, pl.ds(b, lc, stride=nb), :]


def _s5_call(z, lre, lim, ldt, wbre, wbim, wcre, wcim, *, n_ctx, n_all):
    B, Ta, _ = z.shape
    W = GROUP_W
    lc = S5_CHUNK
    assert (n_ctx * CHUNK) % lc == 0 and (n_all * CHUNK) % lc == 0
    n_ctx, n_all = n_ctx * CHUNK // lc, n_all * CHUNK // lc
    gn = lre.shape[-1]
    blk = lc * B
    kern = functools.partial(_s5_kernel, lc=lc, nb=B, scan_unroll=True)
    full = lambda a: pl.BlockSpec(a.shape, lambda i: (0,) * a.ndim)
    out = jax.ShapeDtypeStruct((Ta, B * W), F32)
    return pl.pallas_call(
        kern,
        out_shape=(out, out),
        grid=(n_all,),
        in_specs=[pl.BlockSpec((B, lc, W), lambda i: (0, i, Z_S5)),
                  pl.BlockSpec((B, lc, W), lambda i: (0, _bwd_chunk(i, n_ctx, n_all), Z_S5)),
                  full(lre), full(lim), full(ldt), full(wbre), full(wbim), full(wcre), full(wcim)],
        out_specs=(pl.BlockSpec((lc, B * W), lambda i: (i, 0)),
                   pl.BlockSpec((lc, B * W), lambda i: (_bwd_chunk(i, n_ctx, n_all), 0))),
        scratch_shapes=[pltpu.VMEM((2, W, 2 * gn), BF16),
                        pltpu.VMEM((2, 2 * B, gn), F32),
                        pltpu.VMEM((2, 2 * B, gn), F32),
                        pltpu.VMEM((2, 2, blk, 128), F32),
                        pltpu.VMEM((2, blk, 2 * gn), F32)],
        compiler_params=_cparams("arbitrary"),
        name="s5_scan",
    )(z, z, lre, lim, ldt, wbre, wbim, wcre, wcim)


def _mla_proj_kernel(z_ref, qn_ref, kvn_ref, wq_ref, wqr_ref, wk_ref, wv_ref, cos_ref, sin_ref,
                     q_ref, k_ref, v_ref, *, scale):
    z = z_ref[0]
    cq = z[:, 0:256]
    ckv = z[:, 256:384]
    kr = z[:, 384:512]
    krr = z[:, 512:640]
    cos = cos_ref[...]
    sin = sin_ref[...]
    cqb = _rms(cq, qn_ref[...]).astype(BF16)
    ckvb = _rms(ckv, kvn_ref[...]).astype(BF16)
    q = _nn(cqb, wq_ref[...])
    qr = _nn(cqb, wqr_ref[...])
    kn = _nn(ckvb, wk_ref[...])
    krp = kr * cos + krr * sin
    for h in range(N_HEADS):
        sl = slice(h * 128, (h + 1) * 128)
        q_ref[0, h] = ((q[:, sl] * cos + qr[:, sl] * sin) * scale).astype(BF16)
        k_ref[0, h] = (kn[:, sl] + krp).astype(BF16)
    ones_pad = ((_iota((1, N_HEADS * 128), 1) % 128) >= HEAD_DIM).astype(F32)
    vv = _nn(ckvb, wv_ref[...]) + ones_pad
    for h in range(N_HEADS):
        v_ref[0, h] = vv[:, h * 128:(h + 1) * 128].astype(BF16)


def _mla_proj_call(z, qn, kvn, wq, wqr, wk, wv, cos_t, sin_t, *, scale):
    B, Ta, _ = z.shape
    tm = _pick_tile(Ta, 544)
    kern = functools.partial(_mla_proj_kernel, scale=scale)
    full = lambda a: pl.BlockSpec(a.shape, lambda b, t: (0,) * a.ndim)
    return pl.pallas_call(
        kern,
        out_shape=(jax.ShapeDtypeStruct((B, N_HEADS, Ta, 128), BF16),
                   jax.ShapeDtypeStruct((B, N_HEADS, Ta, 128), BF16),
                   jax.ShapeDtypeStruct((B, N_HEADS, Ta, 128), BF16)),
        grid=(B, Ta // tm),
        in_specs=[pl.BlockSpec((1, tm, 3 * ZB), lambda b, t: (b, t, 0)),
                  full(qn), full(kvn), full(wq), full(wqr), full(wk), full(wv),
                  pl.BlockSpec((tm, 128), lambda b, t: (t, 0)),
                  pl.BlockSpec((tm, 128), lambda b, t: (t, 0))],
        out_specs=(pl.BlockSpec((1, N_HEADS, tm, 128), lambda b, t: (b, 0, t, 0)),
                   pl.BlockSpec((1, N_HEADS, tm, 128), lambda b, t: (b, 0, t, 0)),
                   pl.BlockSpec((1, N_HEADS, tm, 128), lambda b, t: (b, 0, t, 0))),
        compiler_params=_cparams("parallel", "parallel"),
        name="mla_proj",
    )(z, qn, kvn, wq, wqr, wk, wv, cos_t, sin_t)


def _attn_kernel(q_ref, k_ref, v_ref, o_ref, *, tc, ta, tq, key_chunk):
    i = pl.program_id(1)
    lane = _iota((1, 128), 1)

    def write_out(acc, rows):
        outs = [a * (1.0 / pltpu.roll(a, HEAD_DIM, 1)) for a in acc]
        for hp in range(N_HEADS // 2):
            o_ref[0, 0:rows, hp * 128:(hp + 1) * 128] = jnp.where(
                lane < HEAD_DIM, outs[2 * hp], pltpu.roll(outs[2 * hp + 1], HEAD_DIM, 1))

    def all_keys_pass():
        bounds = [0, tc] + list(range(tc + key_chunk, ta + 1, key_chunk))
        units = [(c, j) for c in range(len(bounds) - 1) for j in range(N_HEADS)]
        qs = [q_ref[0, j] for j in range(N_HEADS)]
        score = lambda c, j: _nt(qs[j], k_ref[0, j, bounds[c]:bounds[c + 1], :])
        m = [None] * N_HEADS
        acc = [None] * N_HEADS

        def weighted_values(pend):
            c, j, pb, alpha = pend
            pv = _nn(pb, v_ref[0, j, bounds[c]:bounds[c + 1], :])
            acc[j] = pv if alpha is None else alpha * acc[j] + pv

        pending = None
        s_next = score(*units[0])
        for idx, (c, j) in enumerate(units):
            s = s_next
            if idx + 1 < len(units):
                s_next = score(*units[idx + 1])
            mc = jnp.max(s, axis=-1, keepdims=True)
            if c == 0:
                alpha = None
                m[j] = mc
                p = jnp.exp2(s - mc)
            else:
                m_new = jnp.maximum(m[j], mc)
                alpha = jnp.exp2(m[j] - m_new)
                p = jnp.exp2(s - m_new)
                m[j] = m_new
            if pending is not None:
                weighted_values(pending)
            pending = (c, j, p.astype(BF16), alpha)
        weighted_values(pending)
        write_out(acc, tq)

    all_keys_pass()

    @pl.when(i == 0)
    def _():
        acc = []
        for j in range(N_HEADS):
            s = _nt(q_ref[0, j, 0:tc, :], k_ref[0, j, 0:tc, :])
            p = jnp.exp2(s - jnp.max(s, axis=-1, keepdims=True))
            acc.append(_nn(p.astype(BF16), v_ref[0, j, 0:tc, :]))
        write_out(acc, tc)


def _attn_call(q, k, v, *, tc):
    B, H, Ta, _ = q.shape
    tq = _pick_tile(Ta, 544)
    assert tc <= tq
    kern = functools.partial(_attn_kernel, tc=tc, ta=Ta, tq=tq,
                             key_chunk=_pick_tile(Ta - tc, 1024, 128))
    return pl.pallas_call(
        kern,
        out_shape=jax.ShapeDtypeStruct((B, Ta, GROUP_W), F32),
        grid=(B, Ta // tq),
        in_specs=[pl.BlockSpec((1, H, tq, 128), lambda b, i: (b, 0, i, 0)),
                  pl.BlockSpec((1, H, Ta, 128), lambda b, i: (b, 0, 0, 0)),
                  pl.BlockSpec((1, H, Ta, 128), lambda b, i: (b, 0, 0, 0))],
        out_specs=pl.BlockSpec((1, tq, GROUP_W), lambda b, i: (b, i, 0)),
        compiler_params=_cparams("parallel", "arbitrary"),
        name="mla_attn",
    )(q, k, v)


def _chunk_masks(d):
    L = CHUNK
    row = _iota((L, GROUP_W), 0)
    s_idx = _iota((L, GROUP_W), 1) % L
    tr = _iota((L, L), 0)
    tcol = _iota((L, L), 1)
    if d == 0:
        return (tcol <= tr).astype(F32), s_idx < row, s_idx <= row, s_idx == row
    return (tcol >= tr).astype(F32), s_idx > row, s_idx >= row, s_idx == row


def _run_stages(*generators):
    live = list(generators)
    while live:
        for g in list(live):
            try:
                next(g)
            except StopIteration:
                live.remove(g)


def _rwkv_kernel(*refs, cps):
    _run_stages(_rwkv_stages(*refs, cps=cps))


def _rwkv_stages(rf_ref, rb_ref, lf_ref, lb_ref, wup_ref, aup_ref, gup_ref, w0_ref, a0_ref,
                 kk_ref, ka_ref, rk_ref, yf_ref, yb_ref, g_ref, bon_ref, s_scr, tri_scr, bdb_scr, *, cps):
    i = pl.program_id(1)
    L = CHUNK
    W = GROUP_W

    @pl.when(i == 0)
    def _():
        s_scr[...] = jnp.zeros_like(s_scr)
        _group_tri_masks(tri_scr, cps * L)
        bdb_scr[...] = _block_diag_mask().astype(BF16)

    hms = _head_masks()
    bd = _block_diag_mask()
    bdb = bdb_scr[...]
    rs = lambda x: _row_stack(x, hms)
    bd_b = bd > 0.5
    bdiag = lambda x: jnp.where(bd_b, jnp.concatenate([x.astype(BF16)] * N_HEADS, axis=0), 0.0)

    per_dir = []
    for d, (r_ref, l_ref) in enumerate(((rf_ref, lf_ref), (rb_ref, lb_ref))):
        rkv = r_ref[0]
        lora = l_ref[0]
        r = rkv[:, 0:W]
        k = rkv[:, W:2 * W]
        v = rkv[:, 2 * W:3 * W]
        _, strict, incl, eye = _chunk_masks(d)
        tri = tri_scr[d]

        lw = -math.exp(-0.5) * _sigmoid(w0_ref[d] + _mm1(_nn, jnp.tanh(lora), wup_ref[d]))
        a = _sigmoid(a0_ref[d] + _mm1(_nn, lora, aup_ref[d]))
        kkv = k * kk_ref[...]
        kkn = kkv * lax.rsqrt(jnp.maximum(_mm1(_nn, kkv * kkv, bdb), 1e-24))
        keff = k * (1.0 + (a - 1.0) * ka_ref[...])
        kka = kkn * a
        lw_hi, lw_lo = _split2(lw)
        cum2 = _nn(tri, jnp.concatenate([lw_hi, lw_lo], axis=1))
        cum = cum2[:, 0:W] + cum2[:, W:2 * W]
        e_dn = jnp.exp(-cum)
        per_dir.append(dict(v=v, lw=lw, cum=cum, kka=kka, keff=keff, strict=strict, incl=incl, eye=eye,
                            al=-kkn * jnp.exp(cum - lw), rt=r * jnp.exp(cum), bh=kka * e_dn, kh=keff * e_dn))
        if d == 0:
            g_ref[0] = _mm1(_nn, _sigmoid(lora), gup_ref[...])
            bon_ref[0] = _mm1(_nn, r * k * rk_ref[...], bdb) * v
        yield

    chains = []
    for c in range(cps):
        for d in range(2):
            pd = per_dir[d]
            sl = slice(c * L, (c + 1) * L)
            tot = jnp.sum(pd["lw"][sl], axis=0, keepdims=True)
            e_tc = jnp.exp(tot - pd["cum"][sl])
            ch = dict(d=d, c=c, al=pd["al"][sl], rt=pd["rt"][sl], v=pd["v"][sl], tot=tot,
                      bt=pd["kka"][sl] * e_tc, kt=pd["keff"][sl] * e_tc)
            ch["rsv"] = rs(ch["v"])
            a_all = _mm1(_nt, jnp.concatenate([ch["al"], ch["rt"]], axis=0),
                         jnp.concatenate([rs(pd["bh"][sl]), rs(pd["kh"][sl])], axis=0))
            ch["a_ab"] = jnp.where(pd["strict"], a_all[0:L, 0:W], 0.0)
            ch["a_ak"] = jnp.where(pd["strict"], a_all[0:L, W:2 * W], 0.0)
            ch["a_rb"] = jnp.where(pd["incl"], a_all[L:2 * L, 0:W], 0.0)
            ch["a_rk"] = jnp.where(pd["incl"], a_all[L:2 * L, W:2 * W], 0.0)
            ch["p"] = jnp.where(pd["eye"], 1.0, 0.0) + ch["a_ab"]
            chains.append(ch)
        yield

    for ch in chains:
        ch["sq"] = _mmn(_nn, ch["a_ab"], bdiag(ch["a_ab"]))
        ch["zk"] = _mm1(_nn, ch["a_ak"], ch["rsv"])
        ch["y0k"] = _mm1(_nn, ch["a_rk"], ch["rsv"])
    yield
    n_sq = int(math.log2(L)) - 1
    for it in range(n_sq):
        for ch in chains:
            if it < n_sq - 1:
                ps = _mmn(_nn, jnp.concatenate([ch["p"], ch["sq"]], axis=0), bdiag(ch["sq"]))
                ch["p"] = ch["p"] + ps[0:L]
                ch["sq"] = ps[L:2 * L]
            else:
                ch["p"] = ch["p"] + _mmn(_nn, ch["p"], bdiag(ch["sq"]))
        yield
    for ch in chains:
        pu = _mm1(_nn, ch["p"], jnp.concatenate([rs(ch["al"]), rs(ch["zk"])], axis=1))
        ch["w"], ch["uk"] = pu[:, 0:W], pu[:, W:2 * W]
    yield
    for ch in chains:
        gy = _mm1(_nn, ch["a_rb"], jnp.concatenate([rs(ch["w"]), rs(ch["uk"])], axis=1))
        ch["g"] = ch["rt"] + gy[:, 0:W]
        ch["y0"] = gy[:, W:2 * W] + ch["y0k"]
        ch["decay"] = jnp.exp(ch["tot"])
        ch["pm"] = bd * _mm1(_tn, ch["w"], ch["bt"])
        ch["q0"] = bd * _mm1(_tn, jnp.concatenate([ch["uk"], ch["v"]], axis=0),
                             jnp.concatenate([ch["bt"], ch["kt"]], axis=0))
    yield

    by_key = {(ch["d"], ch["c"]): ch for ch in chains}
    st = [s_scr[0], s_scr[1]]
    ys = [[None] * cps, [None] * cps]
    for step in range(cps):
        for d in range(2):
            c = step if d == 0 else cps - 1 - step
            ch = by_key[(d, c)]
            ys[d][c] = _mm1(_nt, ch["g"], st[d]) + ch["y0"]
            st[d] = st[d] * ch["decay"] + (_mm1(_nn, st[d], ch["pm"]) + ch["q0"])
        yield
    for d, y_ref in enumerate((yf_ref, yb_ref)):
        s_scr[d] = st[d]
        y_ref[0] = jnp.concatenate(ys[d], axis=0)


def _chunks_per_step(n_ctx, n_all):
    for cps in (4, 2, 1):
        if n_ctx % cps == 0 and (n_all - n_ctx) % cps == 0:
            return cps


def _rwkv_call(z, wup, aup, gup, w0, a0, kk, ka, rk, *, n_ctx, n_all):
    B, Ta, _ = z.shape
    W = GROUP_W
    cps = _chunks_per_step(n_ctx, n_all)
    rows = cps * CHUNK
    gc, ga = n_ctx // cps, n_all // cps
    full = lambda a: pl.BlockSpec(a.shape, lambda b, i: (0,) * a.ndim)
    fwd = lambda blk: (lambda b, i: (b, i, blk))
    bwd = lambda blk: (lambda b, i: (b, _bwd_chunk(i, gc, ga), blk))
    out = jax.ShapeDtypeStruct((B, Ta, W), F32)
    return pl.pallas_call(
        functools.partial(_rwkv_kernel, cps=cps),
        out_shape=(out, out, out, out),
        grid=(B, ga),
        in_specs=[pl.BlockSpec((1, rows, 3 * W), fwd(Z_RKV // 3)),
                  pl.BlockSpec((1, rows, 3 * W), bwd(Z_RKV // 3)),
                  pl.BlockSpec((1, rows, W), fwd(Z_LORA)),
                  pl.BlockSpec((1, rows, W), bwd(Z_LORA)),
                  full(wup), full(aup), full(gup), full(w0), full(a0), full(kk), full(ka), full(rk)],
        out_specs=(pl.BlockSpec((1, rows, W), fwd(0)), pl.BlockSpec((1, rows, W), bwd(0)),
                   pl.BlockSpec((1, rows, W), fwd(0)), pl.BlockSpec((1, rows, W), fwd(0))),
        scratch_shapes=[pltpu.VMEM((2, W, W), F32), pltpu.VMEM((2, rows, rows), BF16), pltpu.VMEM((W, W), BF16)],
        compiler_params=_cparams("parallel", "arbitrary"),
        name="rwkv7",
    )(z, z, z, z, wup, aup, gup, w0, a0, kk, ka, rk)


def _mlstm_kernel(*refs, cps):
    _run_stages(_mlstm_stages(*refs, cps=cps))


def _mlstm_stages(qf_ref, qb_ref, vf_ref, vb_ref, gf_ref, gb_ref, gbias_ref,
                  yf_ref, yb_ref, c_scr, n_scr, m_scr, tri_scr, exp_scr, bdb_scr, *, cps):
    i = pl.program_id(1)
    L = CHUNK
    W = GROUP_W

    @pl.when(i == 0)
    def _():
        c_scr[...] = jnp.zeros_like(c_scr)
        n_scr[...] = jnp.zeros_like(n_scr)
        m_scr[...] = jnp.zeros_like(m_scr)
        _group_tri_masks(tri_scr, cps * L)
        bdb_scr[...] = _block_diag_mask().astype(BF16)
        ci = _iota((W, W), 0)
        cj = _iota((W, W), 1) // HEAD_DIM
        for d in range(2):
            col = GATE_LANE0 + d * 2 * N_HEADS
            exp_scr[d, :, 0:W] = (ci == col + cj).astype(BF16)
            exp_scr[d, :, W:2 * W] = (ci == col + N_HEADS + cj).astype(BF16)

    hms = _head_masks()
    bd = _block_diag_mask()
    bdb = bdb_scr[...]
    rs = lambda x: _row_stack(x, hms)
    neg_inf = -jnp.inf

    def sum2(x, eb):
        hi, lo = _split2(x)
        return _nn(hi, eb) + _nn(lo, eb)

    per_dir = []
    for d, (q_ref, v_ref, g_ref) in enumerate(((qf_ref, vf_ref, gf_ref), (qb_ref, vb_ref, gb_ref))):
        qk = q_ref[0]
        g = g_ref[0] + gbias_ref[...]
        _, strict, incl, eye = _chunk_masks(d)
        tri = tri_scr[d]
        gates = sum2(g, exp_scr[d])
        li = gates[:, 0:W]
        lf = -_softplus(-gates[:, W:2 * W])
        lf_hi, lf_lo = _split2(lf)
        b2 = _nn(tri, jnp.concatenate([lf_hi, lf_lo], axis=1))
        per_dir.append(dict(q=qk[:, 0:W], k=qk[:, W:2 * W] * (HEAD_DIM ** -0.5), v=v_ref[0][:, 0:W],
                            li=li, bcol=b2[:, 0:W] + b2[:, W:2 * W], incl=incl, eye=eye))
        yield

    chains = []
    for c in range(cps):
        for d in range(2):
            pd = per_dir[d]
            sl = slice(c * L, (c + 1) * L)
            q, k, v, li, bcol = pd["q"][sl], pd["k"][sl], pd["v"][sl], pd["li"][sl], pd["bcol"][sl]
            brow = jnp.sum(jnp.where(pd["eye"], bcol, 0.0), axis=0, keepdims=True)
            lirow = jnp.sum(jnp.where(pd["eye"], li, 0.0), axis=0, keepdims=True)
            logd = jnp.where(pd["incl"], bcol - brow + lirow, neg_inf)
            mx = jnp.zeros((L, W), F32)
            for hm in hms:
                mh = jnp.max(jnp.where(hm, logd, neg_inf), axis=1, keepdims=True)
                mx = jnp.where(hm, mh, mx)
            blast = bcol[L - 1:L, :] if d == 0 else bcol[0:1, :]
            lwc = blast - bcol + li
            mlw = jnp.max(lwc, axis=0, keepdims=True)
            kw = k * jnp.exp(lwc - mlw)
            chains.append(dict(d=d, c=c, q=q, v=v, bcol=bcol, mx=mx, blast=blast, mlw=mlw, kw=kw,
                               dexp=jnp.exp(logd - mx), rsk=rs(k), rsv=rs(v),
                               nu0=jnp.sum(kw, axis=0, keepdims=True)))
        yield
    for ch in chains:
        ch["sp"] = _mm1(_nt, ch["q"], ch["rsk"]) * ch["dexp"]
        ch["cu0"] = bd * _mm1(_tn, ch["v"], ch["kw"])
    yield
    for ch in chains:
        ch["num0"] = _mm1(_nn, ch["sp"], ch["rsv"])
        ch["den0"] = _mm1(_nn, ch["sp"], bdb)
    yield

    by_key = {(ch["d"], ch["c"]): ch for ch in chains}
    c_st = [c_scr[0], c_scr[1]]
    n_row = [n_scr[0, 0:1, :], n_scr[1, 0:1, :]]
    m_row = [m_scr[0, 0:1, :], m_scr[1, 0:1, :]]
    ys = [[None] * cps, [None] * cps]
    for step in range(cps):
        for d in range(2):
            c = step if d == 0 else cps - 1 - step
            ch = by_key[(d, c)]
            inter = ch["bcol"] + m_row[d]
            mt = jnp.maximum(inter, ch["mx"])
            f_in = jnp.exp(ch["mx"] - mt)
            w_int = jnp.exp(inter - mt)
            num = f_in * ch["num0"] + w_int * _mm1(_nt, ch["q"], c_st[d])
            den = f_in * ch["den0"] + w_int * _mm1(_nn, ch["q"] * n_row[d], bdb)
            ys[d][c] = num / jnp.maximum(jnp.abs(den), jnp.exp(-mt))
            m_new = jnp.maximum(ch["blast"] + m_row[d], ch["mlw"])
            sc = jnp.exp(ch["blast"] + m_row[d] - m_new)
            e2 = jnp.exp(ch["mlw"] - m_new)
            c_st[d] = sc * c_st[d] + e2 * ch["cu0"]
            n_row[d] = sc * n_row[d] + e2 * ch["nu0"]
            m_row[d] = m_new
        yield
    for d, y_ref in enumerate((yf_ref, yb_ref)):
        c_scr[d] = c_st[d]
        n_scr[d, 0:1, :] = n_row[d]
        m_scr[d, 0:1, :] = m_row[d]
        y_ref[0] = jnp.concatenate(ys[d], axis=0)


def _mlstm_call(z, gbias, *, n_ctx, n_all):
    B, Ta, _ = z.shape
    W = GROUP_W
    cps = _chunks_per_step(n_ctx, n_all)
    L = cps * CHUNK
    n_ctx, n_all = n_ctx // cps, n_all // cps
    fwd = lambda blk: (lambda b, i: (b, i, blk))
    bwd = lambda blk: (lambda b, i: (b, _bwd_chunk(i, n_ctx, n_all), blk))
    out = jax.ShapeDtypeStruct((B, Ta, W), F32)
    return pl.pallas_call(
        functools.partial(_mlstm_kernel, cps=cps),
        out_shape=(out, out),
        grid=(B, n_all),
        in_specs=[pl.BlockSpec((1, L, 2 * W), fwd(Z_MQK // 2)),
                  pl.BlockSpec((1, L, 2 * W), bwd(Z_MQK // 2)),
                  pl.BlockSpec((1, L, 2 * W), fwd(Z_MVO // 2)),
                  pl.BlockSpec((1, L, 2 * W), bwd(Z_MVO // 2)),
                  pl.BlockSpec((1, L, W), fwd(Z_GATE)),
                  pl.BlockSpec((1, L, W), bwd(Z_GATE)),
                  pl.BlockSpec(gbias.shape, lambda b, i: (0, 0))],
        out_specs=(pl.BlockSpec((1, L, W), fwd(0)), pl.BlockSpec((1, L, W), bwd(0))),
        scratch_shapes=[pltpu.VMEM((2, W, W), F32), pltpu.VMEM((2, 8, W), F32), pltpu.VMEM((2, 8, W), F32),
                        pltpu.VMEM((2, L, L), BF16), pltpu.VMEM((2, W, 2 * W), BF16), pltpu.VMEM((W, W), BF16)],
        compiler_params=_cparams("parallel", "arbitrary"),
        name="mlstm",
    )(z, z, z, z, z, z, gbias)


def _recur_kernel(*refs, cps):
    r_in, m_in = refs[0:12], refs[12:19]
    r_out, m_out = refs[19:23], refs[23:25]
    r_scr, m_scr = refs[25:28], refs[28:34]
    _run_stages(_rwkv_stages(*r_in, *r_out, *r_scr, cps=cps),
                _mlstm_stages(*m_in, *m_out, *m_scr, cps=cps))


def _recur_call(z, wup, aup, gup, w0, a0, kk, ka, rk, gbias, *, n_ctx, n_all):
    B, Ta, _ = z.shape
    W = GROUP_W
    cps = _chunks_per_step(n_ctx, n_all)
    rows = cps * CHUNK
    gc, ga = n_ctx // cps, n_all // cps
    full = lambda a: pl.BlockSpec(a.shape, lambda b, i: (0,) * a.ndim)
    fwd = lambda blk: (lambda b, i: (b, i, blk))
    bwd = lambda blk: (lambda b, i: (b, _bwd_chunk(i, gc, ga), blk))
    tok = lambda width, imap: pl.BlockSpec((1, rows, width), imap)
    out = jax.ShapeDtypeStruct((B, Ta, W), F32)
    return pl.pallas_call(
        functools.partial(_recur_kernel, cps=cps),
        out_shape=(out,) * 6,
        grid=(B, ga),
        in_specs=[tok(3 * W, fwd(Z_RKV // 3)), tok(3 * W, bwd(Z_RKV // 3)),
                  tok(W, fwd(Z_LORA)), tok(W, bwd(Z_LORA)),
                  full(wup), full(aup), full(gup), full(w0), full(a0), full(kk), full(ka), full(rk),
                  tok(2 * W, fwd(Z_MQK // 2)), tok(2 * W, bwd(Z_MQK // 2)),
                  tok(2 * W, fwd(Z_MVO // 2)), tok(2 * W, bwd(Z_MVO // 2)),
                  tok(W, fwd(Z_GATE)), tok(W, bwd(Z_GATE)), full(gbias)],
        out_specs=(tok(W, fwd(0)), tok(W, bwd(0)), tok(W, fwd(0)), tok(W, fwd(0)),
                   tok(W, fwd(0)), tok(W, bwd(0))),
        scratch_shapes=[pltpu.VMEM((2, W, W), F32), pltpu.VMEM((2, rows, rows), BF16), pltpu.VMEM((W, W), BF16),
                        pltpu.VMEM((2, W, W), F32), pltpu.VMEM((2, 8, W), F32), pltpu.VMEM((2, 8, W), F32),
                        pltpu.VMEM((2, rows, rows), BF16), pltpu.VMEM((2, W, 2 * W), BF16),
                        pltpu.VMEM((W, W), BF16)],
        compiler_params=_cparams("parallel", "arbitrary"),
        name="rwkv_mlstm",
    )(z, z, z, z, wup, aup, gup, w0, a0, kk, ka, rk, z, z, z, z, z, z, gbias)


def _head_norm(y, bd, eps):
    bdb = bd.astype(BF16)

    def head_mean(x):
        hi, lo = _split2(x)
        return (_nn(hi, bdb) + _nn(lo, bdb)) * (1.0 / HEAD_DIM)

    yc = y - head_mean(y)
    return yc * lax.rsqrt(head_mean(yc * yc) + eps)


def _outproj_kernel(x_ref, mods_ref, s5f_ref, s5b_ref, u_ref, at_ref, rf_ref, rb_ref, rg_ref, rbon_ref,
                    mf_ref, mb_ref, vo_ref, s5d_ref, wglu_ref, bglu_ref, lnw_ref, lnb_ref, mnw_ref,
                    gpost_ref, wout_ref, o_ref, *, tm, tc, n_batch):
    b = pl.program_id(0)
    t = pl.program_id(1)
    d = x_ref.shape[-1]
    W = GROUP_W
    is_ctx = (t * tm + _iota((tm, 1), 0)) < tc
    gate = jnp.where(is_ctx, mods_ref[pl.ds(n_batch, 1), pl.ds(5 * d, d)],
                     mods_ref[pl.ds(b, 1), pl.ds(5 * d, d)])
    bd = _block_diag_mask()

    y = s5f_ref[...] + s5b_ref[...] + s5d_ref[...] * u_ref[0]
    zg = 0.5 * y * (1.0 + jnp.tanh(math.sqrt(2.0 / math.pi) * (y + 0.044715 * (y * y * y))))
    s5o = zg * _sigmoid(_mm1(_nn, zg, wglu_ref[...]) + bglu_ref[...])

    yr = _head_norm(rf_ref[0] + rb_ref[0], bd, RWKV_GN_EPS)
    rwo = (yr * lnw_ref[...] + lnb_ref[...] + rbon_ref[0]) * rg_ref[0]

    ym = _head_norm(mf_ref[0] + mb_ref[0], bd, NORM_EPS)
    mlo = ym * mnw_ref[...] * _sigmoid(vo_ref[0][:, W:2 * W])

    cat = jnp.concatenate([s5o, at_ref[0], rwo, mlo], axis=1).astype(BF16)
    yx = _nn(cat, wout_ref[...])
    o_ref[0] = x_ref[0] + gate * _rms(yx, gpost_ref[...])


def _outproj_call(xa, mods, s5f, s5b, z, attn, rf, rb, rg, rbon, mf, mb,
                  s5d, wglu, bglu, lnw, lnb, mnw, gpost, wout, *, tc):
    B, Ta, D = xa.shape
    W = GROUP_W
    tm = _pick_tile(Ta, 544)
    kern = functools.partial(_outproj_kernel, tm=tm, tc=tc, n_batch=B)
    full = lambda a: pl.BlockSpec(a.shape, lambda b, t: (0,) * a.ndim)
    tok = pl.BlockSpec((1, tm, W), lambda b, t: (b, t, 0))
    tmaj = pl.BlockSpec((tm, W), lambda b, t: (t, b))
    return pl.pallas_call(
        kern,
        out_shape=jax.ShapeDtypeStruct((B, Ta, D), F32),
        grid=(B, Ta // tm),
        in_specs=[pl.BlockSpec((1, tm, D), lambda b, t: (b, t, 0)), full(mods),
                  tmaj, tmaj, pl.BlockSpec((1, tm, W), lambda b, t: (b, t, Z_S5)),
                  tok, tok, tok, tok, tok, tok, tok,
                  pl.BlockSpec((1, tm, 2 * W), lambda b, t: (b, t, Z_MVO // 2)),
                  full(s5d), full(wglu), full(bglu), full(lnw), full(lnb), full(mnw), full(gpost), full(wout)],
        out_specs=pl.BlockSpec((1, tm, D), lambda b, t: (b, t, 0)),
        compiler_params=_cparams("parallel", "parallel"),
        name="mix_out",
    )(xa, mods, s5f, s5b, z, attn, rf, rb, rg, rbon, mf, mb, z, s5d, wglu, bglu, lnw, lnb, mnw, gpost, wout)


def _rope_rotate_cols(w):
    h = ROPE_AXIS // 2
    return jnp.concatenate([-w[..., h:2 * h], w[..., 0:h], -w[..., 3 * h:4 * h], w[..., 2 * h:3 * h]], axis=-1)


def _inproj_relayout(w_in):
    L, D, _ = w_in.shape
    o_s5, o_mla, o_rw, o_ml = 0, 256, 672, 1568
    seg = lambda a, n: w_in[:, :, a:a + n]
    zer = lambda n: jnp.zeros((L, D, n), w_in.dtype)
    k_rope = seg(o_mla + 384, ROPE_DIM)
    parts = [seg(o_mla, 256), seg(o_mla + 256, 128),
             zer(64), k_rope, zer(32),
             zer(64), _rope_rotate_cols(k_rope), zer(32), seg(o_ml + 1024, 16), zer(128 - 16),
             seg(o_rw, 768), seg(o_s5, 256),
             seg(o_rw + 768, 128), zer(128),
             seg(o_ml, 512), seg(o_ml + 512, 512)]
    out = jnp.concatenate(parts, axis=2)
    assert out.shape[2] == Z_COLS
    return out


def _rope_tables(T, tc):
    rows = T // GRID_W
    r_idx, c_idx = jnp.meshgrid(jnp.arange(rows), jnp.arange(GRID_W), indexing='ij')
    inv_freq = 1.0 / (ROPE_BASE ** (jnp.arange(0, ROPE_AXIS, 2, dtype=F32) / ROPE_AXIS))
    ang_r = r_idx.reshape(-1, 1).astype(F32) * inv_freq
    ang_c = c_idx.reshape(-1, 1).astype(F32) * inv_freq
    ang = jnp.concatenate([ang_r, ang_r, ang_c, ang_c], axis=-1)
    cos = jnp.concatenate([jnp.ones((tc, ROPE_DIM), F32), jnp.cos(ang)], axis=0)
    sin = jnp.concatenate([jnp.zeros((tc, ROPE_DIM), F32), jnp.sin(ang)], axis=0)
    ta = T + tc
    cos_t = jnp.concatenate([jnp.ones((ta, 64), F32), cos, jnp.zeros((ta, 32), F32)], axis=1)
    sin_t = jnp.concatenate([jnp.zeros((ta, 64), F32), sin, jnp.zeros((ta, 32), F32)], axis=1)
    return cos_t, sin_t


def _pad_rows(w, r0, total):
    pad = [(0, 0)] * (w.ndim - 2) + [(r0, total - r0 - w.shape[-2]), (0, 0)]
    return jnp.pad(w, pad)


def kernel(x, c, ctx, c_ctx, w_ada, b_ada, norm_pre, norm_post, ffn_w_gate, ffn_w_up, ffn_w_down, w_in, w_out, s5_lam_re, s5_lam_im, s5_log_dt, s5_b_re, s5_b_im, s5_c_re, s5_c_im, s5_d, s5_w_glu, s5_b_glu, mla_q_norm, mla_kv_norm, mla_w_uq, mla_w_ukv, rwkv_conv_w, rwkv_conv_b, rwkv_w0, rwkv_w_up, rwkv_a0, rwkv_a_up, rwkv_g_up, rwkv_k_k, rwkv_k_a, rwkv_r_k, rwkv_ln_w, rwkv_ln_b, mlstm_conv_w, mlstm_conv_b, mlstm_gate_b, mlstm_norm):
    B, T, D = x.shape
    Tc = ctx.shape[1]
    Ta = T + Tc
    L = w_ada.shape[0]
    W = GROUP_W
    assert T % CHUNK == 0 and Tc % CHUNK == 0 and B % 8 == 0 and B <= 8
    n_ctx, n_all = Tc // CHUNK, Ta // CHUNK

    rows = 16
    cvec = jnp.concatenate([c, c_ctx[None, :], jnp.zeros((rows - B - 1, D), F32)], axis=0)
    mods_all = _ada_call(cvec, w_ada, b_ada)

    w_in_re = _inproj_relayout(w_in).astype(BF16)
    cw = jnp.zeros((L, Z_NBLK, 8, ZB), F32)
    rc = jnp.concatenate([rwkv_conv_w, rwkv_conv_b[:, None, :]], axis=1).reshape(L, 4, 3, ZB).transpose(0, 2, 1, 3)
    mc = jnp.concatenate([mlstm_conv_w, mlstm_conv_b[:, None, :]], axis=1).reshape(L, 4, 2, ZB).transpose(0, 2, 1, 3)
    cw = cw.at[:, Z_RKV:Z_RKV + 3, 0:4].set(rc).at[:, Z_MQK:Z_MQK + 2, 0:4].set(mc)

    wg = ffn_w_gate.astype(BF16)
    wu = ffn_w_up.astype(BF16)
    wd = ffn_w_down.astype(BF16)
    wout = w_out.astype(BF16)

    G = s5_lam_re.shape[2]
    N = s5_lam_re.shape[3]
    eye_g = jnp.eye(G, dtype=F32)
    lre = s5_lam_re.reshape(L, 2, 1, G * N)
    lim = s5_lam_im.reshape(L, 2, 1, G * N)
    ldt = jnp.repeat(s5_log_dt, N, axis=-1).reshape(L, 2, 1, G * N)
    wbre = jnp.einsum('ldgnp,gh->ldgphn', s5_b_re, eye_g).reshape(L, 2, G * S5_P, G * N)
    wbim = jnp.einsum('ldgnp,gh->ldgphn', s5_b_im, eye_g).reshape(L, 2, G * S5_P, G * N)
    wcre = jnp.einsum('ldgpn,gh->ldgnhp', s5_c_re, eye_g).reshape(L, 2, G * N, G * S5_P).astype(BF16)
    wcim = jnp.einsum('ldgpn,gh->ldgnhp', s5_c_im, eye_g).reshape(L, 2, G * N, G * S5_P).astype(BF16)

    nope = HEAD_DIM
    qd = nope + ROPE_DIM
    wq4 = mla_w_uq.reshape(L, -1, N_HEADS, qd)
    wq = jnp.pad(wq4, ((0, 0), (0, 0), (0, 0), (0, 128 - qd))).reshape(L, -1, N_HEADS * 128).astype(BF16)
    wq_rot = _rope_rotate_cols(wq4[..., nope:])
    wqr = jnp.pad(wq_rot, ((0, 0), (0, 0), (0, 0), (nope, 128 - qd))).reshape(L, -1, N_HEADS * 128).astype(BF16)
    wkv4 = mla_w_ukv.reshape(L, -1, N_HEADS, 2 * HEAD_DIM)
    wk = jnp.pad(wkv4[..., :HEAD_DIM], ((0, 0), (0, 0), (0, 0), (0, 64))).reshape(L, -1, N_HEADS * 128).astype(BF16)
    wv = jnp.pad(wkv4[..., HEAD_DIM:], ((0, 0), (0, 0), (0, 0), (0, 64))).reshape(L, -1, N_HEADS * 128).astype(BF16)
    cos_t, sin_t = _rope_tables(T, Tc)
    scale = float(qd) ** -0.5 * math.log2(math.e)

    wup = _pad_rows(rwkv_w_up, 0, W).astype(BF16)
    aup = _pad_rows(rwkv_a_up, 32, W).astype(BF16)
    gup = _pad_rows(rwkv_g_up, 64, W).astype(BF16)
    gbias = jnp.pad(mlstm_gate_b, ((0, 0), (GATE_LANE0, W - GATE_LANE0 - mlstm_gate_b.shape[1])))

    xa = jnp.concatenate([ctx, x], axis=1)
    r1 = lambda a: a.reshape(1, -1)

    for l in range(L):
        mods = mods_all[l]
        xa2, hmix = _ffn_call(xa.reshape(B * Ta, D), mods, r1(norm_pre[l, 0]), r1(norm_post[l, 0]),
                              r1(norm_pre[l, 1]), wg[l, 0], wu[l, 0], wd[l, 0],
                              ta=Ta, tc=Tc, n_batch=B, koff=0, emit_hmix=True)
        xa = xa2.reshape(B, Ta, D)
        z = _inproj_call(hmix.reshape(B, Ta, D), w_in_re[l], cw[l], tc=Tc)

        s5f, s5b = _s5_call(z, lre[l], lim[l], ldt[l], wbre[l], wbim[l], wcre[l], wcim[l],
                            n_ctx=n_ctx, n_all=n_all)

        q, k, v = _mla_proj_call(z, r1(mla_q_norm[l]), r1(mla_kv_norm[l]), wq[l], wqr[l], wk[l], wv[l],
                                 cos_t, sin_t, scale=scale)
        attn = _attn_call(q, k, v, tc=Tc)

        rf, rb, rg, rbon, mf, mb = _recur_call(
            z, wup[l], aup[l], gup[l], rwkv_w0[l][:, None, :], rwkv_a0[l][:, None, :],
            r1(rwkv_k_k[l]), r1(rwkv_k_a[l]), r1(rwkv_r_k[l]), gbias[l:l + 1], n_ctx=n_ctx, n_all=n_all)

        xa = _outproj_call(xa, mods, s5f, s5b, z, attn, rf, rb, rg, rbon,
                           mf, mb, r1(s5_d[l]), s5_w_glu[l].astype(BF16), r1(s5_b_glu[l]), r1(rwkv_ln_w[l]),
                           r1(rwkv_ln_b[l]), r1(mlstm_norm[l]), r1(norm_post[l, 1]), wout[l], tc=Tc)

        xa2, _ = _ffn_call(xa.reshape(B * Ta, D), mods, r1(norm_pre[l, 2]), r1(norm_post[l, 2]),
                           r1(norm_pre[l, 1]), wg[l, 1], wu[l, 1], wd[l, 1],
                           ta=Ta, tc=Tc, n_batch=B, koff=6, emit_hmix=False)
        xa = xa2.reshape(B, Ta, D)

    return xa[:, Tc:, :]
```

```python
import functools
import math

import numpy as np
import jax
import jax.numpy as jnp
from jax import lax
from jax.experimental import pallas as pl
from jax.experimental.pallas import tpu as pltpu

F32 = jnp.float32
BF16 = jnp.bfloat16

GROUP_W = 256
HEAD_DIM = 64
N_HEADS = GROUP_W // HEAD_DIM
CHUNK = 64
S5_CHUNK = 128
N_MOD = 9
NORM_EPS = 1e-6
RWKV_GN_EPS = HEAD_DIM * 1e-5
GRID_W = 64
ROPE_BASE = 10000.0
ROPE_DIM = 32
ROPE_AXIS = 16
S5_P = 16
S5_STATE = 64
MACARON = 0.5
VMEM_LIMIT_BYTES = 56 * 1024 * 1024

ZB = 256
Z_MLA, Z_RKV, Z_S5, Z_LORA, Z_MQK, Z_MVO = 0, 3, 6, 7, 8, 10
Z_GATE = 2
GATE_LANE0 = 128
Z_NBLK = 12
Z_COLS = Z_NBLK * ZB


def _nn(a, b):
    return lax.dot_general(a, b, (((1,), (0,)), ((), ())), preferred_element_type=F32)


def _nt(a, b):
    return lax.dot_general(a, b, (((1,), (1,)), ((), ())), preferred_element_type=F32)


def _tn(a, b):
    return lax.dot_general(a, b, (((0,), (0,)), ((), ())), preferred_element_type=F32)


def _split2(x):
    hi = x.astype(BF16)
    lo = (x - hi.astype(F32)).astype(BF16)
    return hi, lo


def _split3(x):
    p1 = x.astype(BF16)
    r = x - p1.astype(F32)
    p2 = r.astype(BF16)
    p3 = (r - p2.astype(F32)).astype(BF16)
    return p1, p2, p3


def _mm3(dotf, a, b):
    ah, al = _split2(a)
    bh, bl = _split2(b)
    return dotf(ah, bh) + (dotf(ah, bl) + dotf(al, bh))


def _mm1(dotf, a, b):
    return dotf(a.astype(BF16), b.astype(BF16))


_mmn = _mm1


def _mm_exact_rhs(a, e):
    eb = e.astype(BF16)
    p1, p2, p3 = _split3(a)
    return _nn(p1, eb) + (_nn(p2, eb) + _nn(p3, eb))


def _mm_exact_lhs(e, a):
    eb = e.astype(BF16)
    p1, p2, p3 = _split3(a)
    return _nn(eb, p1) + (_nn(eb, p2) + _nn(eb, p3))


def _rms(x, g):
    return x * lax.rsqrt(jnp.mean(x * x, axis=-1, keepdims=True) + NORM_EPS) * g


def _sigmoid(x):
    return 0.5 * jnp.tanh(0.5 * x) + 0.5


def _group_tri_masks(tri_scr, rows):
    r = _iota((rows, rows), 0)
    c = _iota((rows, rows), 1)
    same = (r // CHUNK) == (c // CHUNK)
    tri_scr[0] = (same & (c <= r)).astype(BF16)
    tri_scr[1] = (same & (c >= r)).astype(BF16)


def _softplus(x):
    return jnp.maximum(x, 0.0) + jnp.log(1.0 + jnp.exp(-jnp.abs(x)))


def _iota(shape, dim):
    return lax.broadcasted_iota(jnp.int32, shape, dim)


def _head_masks():
    lane = _iota((1, GROUP_W), 1)
    return [lane // HEAD_DIM == h for h in range(N_HEADS)]


def _block_diag_mask():
    r = _iota((GROUP_W, GROUP_W), 0) // HEAD_DIM
    c = _iota((GROUP_W, GROUP_W), 1) // HEAD_DIM
    return (r == c).astype(F32)


def _row_stack(x, hms):
    xb = x.astype(BF16)
    return jnp.concatenate([jnp.where(m, xb, 0.0) for m in hms], axis=0)


def _pick_tile(n, target, mult=16):
    best = None
    for t in range(mult, min(n, target) + 1, mult):
        if n % t == 0:
            best = t
    if best is None:
        raise ValueError(f"no tile for {n}")
    return best


def _cparams(*sem):
    return pltpu.CompilerParams(dimension_semantics=sem, vmem_limit_bytes=VMEM_LIMIT_BYTES)


def _ada_kernel(c_ref, w_ref, b_ref, o_ref):
    c = c_ref[...]
    s = c * _sigmoid(c)
    o_ref[0] = _mm3(_nn, s, w_ref[0]) + b_ref[0]


def _ada_call(cvec, w_ada, b_ada):
    L, D, N = w_ada.shape
    R = cvec.shape[0]
    tn = _pick_tile(N, 1152, 128)
    return pl.pallas_call(
        _ada_kernel,
        out_shape=jax.ShapeDtypeStruct((L, R, N), F32),
        grid=(L, N // tn),
        in_specs=[pl.BlockSpec((R, D), lambda l, j: (0, 0)),
                  pl.BlockSpec((1, D, tn), lambda l, j: (l, 0, j)),
                  pl.BlockSpec((1, 1, tn), lambda l, j: (l, 0, j))],
        out_specs=pl.BlockSpec((1, R, tn), lambda l, j: (l, 0, j)),
        compiler_params=_cparams("parallel", "parallel"),
        name="ada_mod",
    )(cvec, w_ada, b_ada.reshape(L, 1, N))


FFN_COLS = 256


def _ffn_kernel(xn_ref, xp_ref, mods_ref, gpre_ref, gpost_ref, gmix_ref, wg_ref, wu_ref, wd_ref,
                o_ref, hmix_ref, h_scr, acc_scr, *, tm, tiles_per_batch, tc, n_batch, koff, emit_hmix,
                n_tiles):
    i = pl.program_id(0)
    f = pl.program_id(1)
    d = xn_ref.shape[-1]
    fd = wg_ref.shape[1]

    def mod_of(tile):
        b = tile // tiles_per_batch
        t0 = (tile % tiles_per_batch) * tm
        is_ctx = (t0 + _iota((tm, 1), 0)) < tc

        def mod(fn):
            row = lambda r: (lambda k: mods_ref[pl.ds(r, 1), pl.ds(k * d, d)])
            return jnp.where(is_ctx, fn(row(n_batch)), fn(row(b)))
        return mod

    unit = lambda x: x * lax.rsqrt(jnp.mean(x * x, axis=-1, keepdims=True) + NORM_EPS)

    def pre_norm(tile, x):
        mod = mod_of(tile)
        gain = mod(lambda m: gpre_ref[...] * (1.0 + m(koff + 1)))
        return (unit(x) * gain + mod(lambda m: m(koff))).astype(BF16)

    def finish(tile, acc, x):
        mod = mod_of(tile)
        xn = x + unit(acc) * mod(lambda m: (MACARON * m(koff + 2)) * gpost_ref[...])
        o_ref[...] = xn
        if emit_hmix:
            gain = mod(lambda m: gmix_ref[...] * (1.0 + m(4)))
            hmix_ref[...] = (unit(xn) * gain + mod(lambda m: m(3))).astype(BF16)
        else:
            hmix_ref[...] = jnp.zeros_like(hmix_ref)

    def hidden_cols(hb, lo, hi):
        acc = None
        for c0 in range(lo, hi, FFN_COLS):
            g = _nn(hb, wg_ref[:, c0:c0 + FFN_COLS])
            u = _nn(hb, wu_ref[:, c0:c0 + FFN_COLS])
            t = _nn((g * _sigmoid(g) * u).astype(BF16), wd_ref[c0:c0 + FFN_COLS, :])
            acc = t if acc is None else acc + t
        return acc

    @pl.when(jnp.logical_and(i == 0, f == 0))
    def _():
        h_scr[0] = pre_norm(0, xp_ref[...])
        acc_scr[1] = jnp.zeros((tm, d), F32)

    for s in range(2):
        tile = 2 * i + s

        @pl.when(jnp.logical_and(tile < n_tiles, f == s))
        def _(s=s, tile=tile):
            finish(jnp.maximum(tile - 1, 0), acc_scr[1 - s], xp_ref[...])
            h_scr[1 - s] = pre_norm(jnp.minimum(tile + 1, n_tiles - 1), xn_ref[...])
            acc_scr[s] = hidden_cols(h_scr[s], 0, fd)

    @pl.when(jnp.logical_and(2 * i == n_tiles, f == 0))
    def _():
        finish(n_tiles - 1, acc_scr[1], xp_ref[...])


def _ffn_call(xa2, mods, gpre, gpost, gmix, wg, wu, wd, *, ta, tc, n_batch, koff, emit_hmix):
    M, D = xa2.shape
    Fd = wg.shape[1]
    assert Fd % FFN_COLS == 0
    tm = _pick_tile(ta, 544)
    n_tiles = M // tm
    assert n_tiles % 2 == 0
    kern = functools.partial(_ffn_kernel, tm=tm, tiles_per_batch=ta // tm, tc=tc, n_batch=n_batch,
                             koff=koff, emit_hmix=emit_hmix, n_tiles=n_tiles)
    hm_rows = tm if emit_hmix else 16
    last = n_tiles - 1
    resident = lambda a: pl.BlockSpec(a.shape, lambda i, f: (0, 0), pipeline_mode=pl.Buffered(1))
    vec = pl.BlockSpec((1, D), lambda i, f: (0, 0))
    return pl.pallas_call(
        kern,
        out_shape=(jax.ShapeDtypeStruct((M, D), F32),
                   jax.ShapeDtypeStruct((M if emit_hmix else 16 * n_tiles, D), BF16)),
        grid=(n_tiles // 2 + 1, 2),
        in_specs=[pl.BlockSpec((tm, D), lambda i, f: (jnp.minimum(2 * i + f + 1, last), 0)),
                  pl.BlockSpec((tm, D), lambda i, f: (jnp.clip(2 * i + f - 1, 0, last), 0)),
                  pl.BlockSpec(mods.shape, lambda i, f: (0, 0)),
                  vec, vec, vec, resident(wg), resident(wu), resident(wd)],
        out_specs=(pl.BlockSpec((tm, D), lambda i, f: (jnp.clip(2 * i + f - 1, 0, last), 0)),
                   pl.BlockSpec((hm_rows, D), lambda i, f: (jnp.clip(2 * i + f - 1, 0, last), 0))),
        scratch_shapes=[pltpu.VMEM((2, tm, D), BF16), pltpu.VMEM((2, tm, D), F32)],
        compiler_params=_cparams("arbitrary", "arbitrary"),
        name="half_ffn",
    )(xa2, xa2, mods, gpre, gpost, gmix, wg, wu, wd)


def _inproj_kernel(h_ref, w_ref, cw_ref, o_ref, *, tc, conv_lo, conv_hi, silu_lo, silu_hi):
    nb = pl.program_id(1)
    ta = h_ref.shape[1]
    is_conv = ((nb >= conv_lo[0]) & (nb < conv_hi[0])) | ((nb >= conv_lo[1]) & (nb < conv_hi[1]))
    is_silu = (nb >= silu_lo) & (nb < silu_hi)

    @pl.when(jnp.logical_not(is_conv))
    def _():
        o_ref[0] = _nn(h_ref[0], w_ref[...])

    def conv():
        z = _nn(h_ref[0], w_ref[...])
        row = _iota((ta, 1), 0)
        zp = jnp.where((row == 0) | (row == tc), 0.0, pltpu.roll(z, 1, 0))
        zn = jnp.where((row == tc - 1) | (row == ta - 1), 0.0, pltpu.roll(z, ta - 1, 0))
        cw = cw_ref[0]
        return cw[3:4] + zp * cw[0:1] + z * cw[1:2] + zn * cw[2:3]

    @pl.when(is_conv & jnp.logical_not(is_silu))
    def _():
        o_ref[0] = conv()

    @pl.when(is_conv & is_silu)
    def _():
        y = conv()
        o_ref[0] = y * _sigmoid(y)


def _inproj_call(hmix3, w_re, cw, *, tc):
    B, Ta, D = hmix3.shape
    kern = functools.partial(_inproj_kernel, tc=tc, conv_lo=(Z_RKV, Z_MQK), conv_hi=(Z_RKV + 3, Z_MQK + 2),
                             silu_lo=Z_MQK, silu_hi=Z_MQK + 2)
    return pl.pallas_call(
        kern,
        out_shape=jax.ShapeDtypeStruct((B, Ta, Z_COLS), F32),
        grid=(B, Z_NBLK),
        in_specs=[pl.BlockSpec((1, Ta, D), lambda b, n: (b, 0, 0)),
                  pl.BlockSpec((D, ZB), lambda b, n: (0, n)),
                  pl.BlockSpec((1, 8, ZB), lambda b, n: (n, 0, 0))],
        out_specs=pl.BlockSpec((1, Ta, ZB), lambda b, n: (b, 0, n)),
        compiler_params=_cparams("parallel", "arbitrary"),
        name="in_proj",
    )(hmix3, w_re, cw)


def _bwd_chunk(i, n_ctx, n_all):
    return jnp.where(i < n_ctx, n_ctx - 1 - i, n_all - 1 - (i - n_ctx))


def _s5_kernel(uf_ref, ub_ref, lre_ref, lim_ref, ldt_ref, wbre_ref, wbim_ref, wcre_ref, wcim_ref,
               yf_ref, yb_ref, wb_scr, coef_scr, st_scr, rel_scr, x_scr, *, lc, nb, scan_unroll):
    i = pl.program_id(0)
    gn = lre_ref.shape[-1]

    @pl.when(i == 0)
    def _():
        for d in range(2):
            dt = jnp.exp(ldt_ref[d])
            lre = lre_ref[d]
            lim = lim_ref[d]
            mag = jnp.exp(lre * dt)
            ar = mag * jnp.cos(lim * dt)
            ai = mag * jnp.sin(lim * dt)
            den = lre * lre + lim * lim
            fr = ((ar - 1.0) * lre + ai * lim) / den
            fi = (ai * lre - (ar - 1.0) * lim) / den
            coef_scr[d, 0:nb, :] = jnp.broadcast_to(ar, (nb, gn))
            coef_scr[d, nb:2 * nb, :] = jnp.broadcast_to(ai, (nb, gn))
            wre = wbre_ref[d]
            wim = wbim_ref[d]
            wb_scr[d, :, 0:gn] = (wre * fr - wim * fi).astype(BF16)
            wb_scr[d, :, gn:2 * gn] = (wim * fr + wre * fi).astype(BF16)
        st_scr[...] = jnp.zeros_like(st_scr)

    half = 128
    u_refs = (uf_ref, ub_ref)
    y_refs = (yf_ref, yb_ref)

    for d in range(2):
        for b in range(nb):
            for s in range(2):
                rel_scr[d, s, pl.ds(b, lc, stride=nb), :] = u_refs[d][b, :, s * half:(s + 1) * half]
    for d in range(2):
        u_tm = jnp.concatenate([rel_scr[d, 0], rel_scr[d, 1]], axis=1).astype(BF16)
        x_scr[d] = _nn(u_tm, wb_scr[d])

    for d in range(2):
        def body(t, carry, d=d):
            sr, si = carry
            ar = coef_scr[d, 0:nb, :]
            ai = coef_scr[d, nb:2 * nb, :]
            tt = t if d == 0 else lc - 1 - t
            r0 = pl.multiple_of(tt * nb, nb)
            xr = x_scr[d, pl.ds(r0, nb), 0:gn]
            xi = x_scr[d, pl.ds(r0, nb), gn:2 * gn]
            nsr = ar * sr - ai * si + xr
            nsi = ar * si + ai * sr + xi
            x_scr[d, pl.ds(r0, nb), 0:gn] = nsr
            x_scr[d, pl.ds(r0, nb), gn:2 * gn] = nsi
            return nsr, nsi

        sr, si = lax.fori_loop(0, lc, body, (st_scr[d, 0:nb, :], st_scr[d, nb:2 * nb, :]), unroll=scan_unroll)
        st_scr[d, 0:nb, :] = sr
        st_scr[d, nb:2 * nb, :] = si
        y = (_nn(x_scr[d, :, 0:gn].astype(BF16), wcre_ref[d])
             - _nn(x_scr[d, :, gn:2 * gn].astype(BF16), wcim_ref[d]))
        rel_scr[d, 0] = y[:, 0:half]
        rel_scr[d, 1] = y[:, half:2 * half]
    for d in range(2):
        for b in range(nb):
            for s in range(2):
                c0 = b * 2 * half + s * half
                y_refs[d][:, c0:c0 + half] = rel_scr[d, s, pl.ds(b, lc, stride=nb), :]


def _s5_call(z, lre, lim, ldt, wbre, wbim, wcre, wcim, *, n_ctx, n_all):
    B, Ta, _ = z.shape
    W = GROUP_W
    lc = S5_CHUNK
    assert (n_ctx * CHUNK) % lc == 0 and (n_all * CHUNK) % lc == 0
    n_ctx, n_all = n_ctx * CHUNK // lc, n_all * CHUNK // lc
    gn = lre.shape[-1]
    blk = lc * B
    kern = functools.partial(_s5_kernel, lc=lc, nb=B, scan_unroll=True)
    full = lambda a: pl.BlockSpec(a.shape, lambda i: (0,) * a.ndim)
    out = jax.ShapeDtypeStruct((Ta, B * W), F32)
    return pl.pallas_call(
        kern,
        out_shape=(out, out),
        grid=(n_all,),
        in_specs=[pl.BlockSpec((B, lc, W), lambda i: (0, i, Z_S5)),
                  pl.BlockSpec((B, lc, W), lambda i: (0, _bwd_chunk(i, n_ctx, n_all), Z_S5)),
                  full(lre), full(lim), full(ldt), full(wbre), full(wbim), full(wcre), full(wcim)],
        out_specs=(pl.BlockSpec((lc, B * W), lambda i: (i, 0)),
                   pl.BlockSpec((lc, B * W), lambda i: (_bwd_chunk(i, n_ctx, n_all), 0))),
        scratch_shapes=[pltpu.VMEM((2, W, 2 * gn), BF16),
                        pltpu.VMEM((2, 2 * B, gn), F32),
                        pltpu.VMEM((2, 2 * B, gn), F32),
                        pltpu.VMEM((2, 2, blk, 128), F32),
                        pltpu.VMEM((2, blk, 2 * gn), F32)],
        compiler_params=_cparams("arbitrary"),
        name="s5_scan",
    )(z, z, lre, lim, ldt, wbre, wbim, wcre, wcim)


def _mla_proj_kernel(z_ref, qn_ref, kvn_ref, wq_ref, wqr_ref, wk_ref, wv_ref, cos_ref, sin_ref,
                     q_ref, k_ref, v_ref, *, scale):
    z = z_ref[0]
    cq = z[:, 0:256]
    ckv = z[:, 256:384]
    kr = z[:, 384:512]
    krr = z[:, 512:640]
    cos = cos_ref[...]
    sin = sin_ref[...]
    cqb = _rms(cq, qn_ref[...]).astype(BF16)
    ckvb = _rms(ckv, kvn_ref[...]).astype(BF16)
    q = _nn(cqb, wq_ref[...])
    qr = _nn(cqb, wqr_ref[...])
    kn = _nn(ckvb, wk_ref[...])
    krp = kr * cos + krr * sin
    for h in range(N_HEADS):
        sl = slice(h * 128, (h + 1) * 128)
        q_ref[0, h] = ((q[:, sl] * cos + qr[:, sl] * sin) * scale).astype(BF16)
        k_ref[0, h] = (kn[:, sl] + krp).astype(BF16)
    ones_pad = ((_iota((1, N_HEADS * 128), 1) % 128) >= HEAD_DIM).astype(F32)
    vv = _nn(ckvb, wv_ref[...]) + ones_pad
    for h in range(N_HEADS):
        v_ref[0, h] = vv[:, h * 128:(h + 1) * 128].astype(BF16)


def _mla_proj_call(z, qn, kvn, wq, wqr, wk, wv, cos_t, sin_t, *, scale):
    B, Ta, _ = z.shape
    tm = _pick_tile(Ta, 1088)
    kern = functools.partial(_mla_proj_kernel, scale=scale)
    full = lambda a: pl.BlockSpec(a.shape, lambda b, t: (0,) * a.ndim)
    return pl.pallas_call(
        kern,
        out_shape=(jax.ShapeDtypeStruct((B, N_HEADS, Ta, 128), BF16),
                   jax.ShapeDtypeStruct((B, N_HEADS, Ta, 128), BF16),
                   jax.ShapeDtypeStruct((B, N_HEADS, Ta, 128), BF16)),
        grid=(B, Ta // tm),
        in_specs=[pl.BlockSpec((1, tm, 3 * ZB), lambda b, t: (b, t, 0)),
                  full(qn), full(kvn), full(wq), full(wqr), full(wk), full(wv),
                  pl.BlockSpec((tm, 128), lambda b, t: (t, 0)),
                  pl.BlockSpec((tm, 128), lambda b, t: (t, 0))],
        out_specs=(pl.BlockSpec((1, N_HEADS, tm, 128), lambda b, t: (b, 0, t, 0)),
                   pl.BlockSpec((1, N_HEADS, tm, 128), lambda b, t: (b, 0, t, 0)),
                   pl.BlockSpec((1, N_HEADS, tm, 128), lambda b, t: (b, 0, t, 0))),
        compiler_params=_cparams("parallel", "parallel"),
        name="mla_proj",
    )(z, qn, kvn, wq, wqr, wk, wv, cos_t, sin_t)


def _attn_kernel(q_ref, k_ref, v_ref, o_ref, *, tc, ta, n_ctx_tiles, key_chunk):
    i = pl.program_id(1)
    lane = _iota((1, 128), 1)

    def write_out(acc):
        outs = [a * (1.0 / pltpu.roll(a, HEAD_DIM, 1)) for a in acc]
        for hp in range(N_HEADS // 2):
            pair = jnp.where(lane < HEAD_DIM, outs[2 * hp], pltpu.roll(outs[2 * hp + 1], HEAD_DIM, 1))
            o_ref[0, :, hp * 128:(hp + 1) * 128] = pair.astype(o_ref.dtype)

    @pl.when(i < n_ctx_tiles)
    def _():
        acc = []
        for j in range(N_HEADS):
            s = _nt(q_ref[0, j], k_ref[0, j, 0:tc, :])
            p = jnp.exp2(s - jnp.max(s, axis=-1, keepdims=True))
            acc.append(_nn(p.astype(BF16), v_ref[0, j, 0:tc, :]))
        write_out(acc)

    @pl.when(i >= n_ctx_tiles)
    def _():
        bounds = [0, tc] + list(range(tc + key_chunk, ta + 1, key_chunk))
        units = [(c, j) for c in range(len(bounds) - 1) for j in range(N_HEADS)]
        qs = [q_ref[0, j] for j in range(N_HEADS)]
        score = lambda c, j: _nt(qs[j], k_ref[0, j, bounds[c]:bounds[c + 1], :])
        m = [None] * N_HEADS
        acc = [None] * N_HEADS

        def weighted_values(pend):
            c, j, pb, alpha = pend
            pv = _nn(pb, v_ref[0, j, bounds[c]:bounds[c + 1], :])
            acc[j] = pv if alpha is None else alpha * acc[j] + pv

        pending = None
        s_next = score(*units[0])
        for idx, (c, j) in enumerate(units):
            s = s_next
            if idx + 1 < len(units):
                s_next = score(*units[idx + 1])
            mc = jnp.max(s, axis=-1, keepdims=True)
            if c == 0:
                alpha = None
                m[j] = mc
                p = jnp.exp2(s - mc)
            else:
                m_new = jnp.maximum(m[j], mc)
                alpha = jnp.exp2(m[j] - m_new)
                p = jnp.exp2(s - m_new)
                m[j] = m_new
            if pending is not None:
                weighted_values(pending)
            pending = (c, j, p.astype(BF16), alpha)
        weighted_values(pending)
        write_out(acc)


def _attn_call(q, k, v, *, tc):
    B, H, Ta, _ = q.shape
    tq = _pick_tile(math.gcd(tc, Ta), 256)
    kern = functools.partial(_attn_kernel, tc=tc, ta=Ta, n_ctx_tiles=tc // tq,
                             key_chunk=_pick_tile(Ta - tc, 1024, 128))
    return pl.pallas_call(
        kern,
        out_shape=jax.ShapeDtypeStruct((B, Ta, GROUP_W), BF16),
        grid=(B, Ta // tq),
        in_specs=[pl.BlockSpec((1, H, tq, 128), lambda b, i: (b, 0, i, 0)),
                  pl.BlockSpec((1, H, Ta, 128), lambda b, i: (b, 0, 0, 0)),
                  pl.BlockSpec((1, H, Ta, 128), lambda b, i: (b, 0, 0, 0))],
        out_specs=pl.BlockSpec((1, tq, GROUP_W), lambda b, i: (b, i, 0)),
        compiler_params=_cparams("parallel", "arbitrary"),
        name="mla_attn",
    )(q, k, v)


def _chunk_masks(d):
    L = CHUNK
    row = _iota((L, GROUP_W), 0)
    s_idx = _iota((L, GROUP_W), 1) % L
    tr = _iota((L, L), 0)
    tcol = _iota((L, L), 1)
    if d == 0:
        return (tcol <= tr).astype(F32), s_idx < row, s_idx <= row, s_idx == row
    return (tcol >= tr).astype(F32), s_idx > row, s_idx >= row, s_idx == row


def _run_stages(*generators):
    live = list(generators)
    while live:
        for g in list(live):
            try:
                next(g)
            except StopIteration:
                live.remove(g)


def _rwkv_kernel(*refs, cps):
    _run_stages(_rwkv_stages(*refs, cps=cps))


def _rwkv_stages(rf_ref, rb_ref, lf_ref, lb_ref, wup_ref, aup_ref, gup_ref, w0_ref, a0_ref,
                 kk_ref, ka_ref, rk_ref, yf_ref, yb_ref, g_ref, bon_ref, s_scr, tri_scr, bdb_scr, *, cps):
    i = pl.program_id(1)
    L = CHUNK
    W = GROUP_W

    @pl.when(i == 0)
    def _():
        s_scr[...] = jnp.zeros_like(s_scr)
        _group_tri_masks(tri_scr, cps * L)
        bdb_scr[...] = _block_diag_mask().astype(BF16)

    hms = _head_masks()
    bd = _block_diag_mask()
    bdb = bdb_scr[...]
    rs = lambda x: _row_stack(x, hms)
    bd_b = bd > 0.5
    bdiag = lambda x: jnp.where(bd_b, jnp.concatenate([x.astype(BF16)] * N_HEADS, axis=0), 0.0)

    per_dir = []
    for d, (r_ref, l_ref) in enumerate(((rf_ref, lf_ref), (rb_ref, lb_ref))):
        rkv = r_ref[0]
        lora = l_ref[0]
        r = rkv[:, 0:W]
        k = rkv[:, W:2 * W]
        v = rkv[:, 2 * W:3 * W]
        _, strict, incl, eye = _chunk_masks(d)
        tri = tri_scr[d]

        lw = -math.exp(-0.5) * _sigmoid(w0_ref[d] + _mm1(_nn, jnp.tanh(lora), wup_ref[d]))
        a = _sigmoid(a0_ref[d] + _mm1(_nn, lora, aup_ref[d]))
        kkv = k * kk_ref[...]
        kkn = kkv * lax.rsqrt(jnp.maximum(_mm1(_nn, kkv * kkv, bdb), 1e-24))
        keff = k * (1.0 + (a - 1.0) * ka_ref[...])
        kka = kkn * a
        lw_hi, lw_lo = _split2(lw)
        cum2 = _nn(tri, jnp.concatenate([lw_hi, lw_lo], axis=1))
        cum = cum2[:, 0:W] + cum2[:, W:2 * W]
        e_dn = jnp.exp(-cum)
        per_dir.append(dict(v=v, lw=lw, cum=cum, kka=kka, keff=keff, strict=strict, incl=incl, eye=eye,
                            al=-kkn * jnp.exp(cum - lw), rt=r * jnp.exp(cum), bh=kka * e_dn, kh=keff * e_dn))
        if d == 0:
            g_ref[0] = _mm1(_nn, _sigmoid(lora), gup_ref[...])
            bon_ref[0] = _mm1(_nn, r * k * rk_ref[...], bdb) * v
        yield

    chains = []
    for c in range(cps):
        for d in range(2):
            pd = per_dir[d]
            sl = slice(c * L, (c + 1) * L)
            tot = jnp.sum(pd["lw"][sl], axis=0, keepdims=True)
            e_tc = jnp.exp(tot - pd["cum"][sl])
            ch = dict(d=d, c=c, al=pd["al"][sl], rt=pd["rt"][sl], v=pd["v"][sl], tot=tot,
                      bt=pd["kka"][sl] * e_tc, kt=pd["keff"][sl] * e_tc)
            ch["rsv"] = rs(ch["v"])
            a_all = _mm1(_nt, jnp.concatenate([ch["al"], ch["rt"]], axis=0),
                         jnp.concatenate([rs(pd["bh"][sl]), rs(pd["kh"][sl])], axis=0))
            ch["a_ab"] = jnp.where(pd["strict"], a_all[0:L, 0:W], 0.0)
            ch["a_ak"] = jnp.where(pd["strict"], a_all[0:L, W:2 * W], 0.0)
            ch["a_rb"] = jnp.where(pd["incl"], a_all[L:2 * L, 0:W], 0.0)
            ch["a_rk"] = jnp.where(pd["incl"], a_all[L:2 * L, W:2 * W], 0.0)
            ch["p"] = jnp.where(pd["eye"], 1.0, 0.0) + ch["a_ab"]
            chains.append(ch)
        yield

    for ch in chains:
        ch["sq"] = _mmn(_nn, ch["a_ab"], bdiag(ch["a_ab"]))
        ch["zk"] = _mm1(_nn, ch["a_ak"], ch["rsv"])
        ch["y0k"] = _mm1(_nn, ch["a_rk"], ch["rsv"])
    yield
    n_sq = int(math.log2(L)) - 1
    for it in range(n_sq):
        for ch in chains:
            if it < n_sq - 1:
                ps = _mmn(_nn, jnp.concatenate([ch["p"], ch["sq"]], axis=0), bdiag(ch["sq"]))
                ch["p"] = ch["p"] + ps[0:L]
                ch["sq"] = ps[L:2 * L]
            else:
                ch["p"] = ch["p"] + _mmn(_nn, ch["p"], bdiag(ch["sq"]))
        yield
    for ch in chains:
        pu = _mm1(_nn, ch["p"], jnp.concatenate([rs(ch["al"]), rs(ch["zk"])], axis=1))
        ch["w"], ch["uk"] = pu[:, 0:W], pu[:, W:2 * W]
    yield
    for ch in chains:
        gy = _mm1(_nn, ch["a_rb"], jnp.concatenate([rs(ch["w"]), rs(ch["uk"])], axis=1))
        ch["g"] = ch["rt"] + gy[:, 0:W]
        ch["y0"] = gy[:, W:2 * W] + ch["y0k"]
        ch["decay"] = jnp.exp(ch["tot"])
        ch["pm"] = bd * _mm1(_tn, ch["w"], ch["bt"])
        ch["q0"] = bd * _mm1(_tn, jnp.concatenate([ch["uk"], ch["v"]], axis=0),
                             jnp.concatenate([ch["bt"], ch["kt"]], axis=0))
    yield

    by_key = {(ch["d"], ch["c"]): ch for ch in chains}
    st = [s_scr[0], s_scr[1]]
    ys = [[None] * cps, [None] * cps]
    for step in range(cps):
        for d in range(2):
            c = step if d == 0 else cps - 1 - step
            ch = by_key[(d, c)]
            ys[d][c] = _mm1(_nt, ch["g"], st[d]) + ch["y0"]
            st[d] = st[d] * ch["decay"] + (_mm1(_nn, st[d], ch["pm"]) + ch["q0"])
        yield
    for d, y_ref in enumerate((yf_ref, yb_ref)):
        s_scr[d] = st[d]
        y_ref[0] = jnp.concatenate(ys[d], axis=0)


def _chunks_per_step(n_ctx, n_all):
    for cps in (4, 2, 1):
        if n_ctx % cps == 0 and (n_all - n_ctx) % cps == 0:
            return cps


def _rwkv_call(z, wup, aup, gup, w0, a0, kk, ka, rk, *, n_ctx, n_all):
    B, Ta, _ = z.shape
    W = GROUP_W
    cps = _chunks_per_step(n_ctx, n_all)
    rows = cps * CHUNK
    gc, ga = n_ctx // cps, n_all // cps
    full = lambda a: pl.BlockSpec(a.shape, lambda b, i: (0,) * a.ndim)
    fwd = lambda blk: (lambda b, i: (b, i, blk))
    bwd = lambda blk: (lambda b, i: (b, _bwd_chunk(i, gc, ga), blk))
    out = jax.ShapeDtypeStruct((B, Ta, W), F32)
    return pl.pallas_call(
        functools.partial(_rwkv_kernel, cps=cps),
        out_shape=(out, out, out, out),
        grid=(B, ga),
        in_specs=[pl.BlockSpec((1, rows, 3 * W), fwd(Z_RKV // 3)),
                  pl.BlockSpec((1, rows, 3 * W), bwd(Z_RKV // 3)),
                  pl.BlockSpec((1, rows, W), fwd(Z_LORA)),
                  pl.BlockSpec((1, rows, W), bwd(Z_LORA)),
                  full(wup), full(aup), full(gup), full(w0), full(a0), full(kk), full(ka), full(rk)],
        out_specs=(pl.BlockSpec((1, rows, W), fwd(0)), pl.BlockSpec((1, rows, W), bwd(0)),
                   pl.BlockSpec((1, rows, W), fwd(0)), pl.BlockSpec((1, rows, W), fwd(0))),
        scratch_shapes=[pltpu.VMEM((2, W, W), F32), pltpu.VMEM((2, rows, rows), BF16), pltpu.VMEM((W, W), BF16)],
        compiler_params=_cparams("parallel", "arbitrary"),
        name="rwkv7",
    )(z, z, z, z, wup, aup, gup, w0, a0, kk, ka, rk)


def _mlstm_kernel(*refs, cps):
    _run_stages(_mlstm_stages(*refs, cps=cps))


def _mlstm_stages(qf_ref, qb_ref, vf_ref, vb_ref, gf_ref, gb_ref, gbias_ref,
                  yf_ref, yb_ref, c_scr, n_scr, m_scr, tri_scr, exp_scr, bdb_scr, *, cps):
    i = pl.program_id(1)
    L = CHUNK
    W = GROUP_W

    @pl.when(i == 0)
    def _():
        c_scr[...] = jnp.zeros_like(c_scr)
        n_scr[...] = jnp.zeros_like(n_scr)
        m_scr[...] = jnp.zeros_like(m_scr)
        _group_tri_masks(tri_scr, cps * L)
        bdb_scr[...] = _block_diag_mask().astype(BF16)
        ci = _iota((W, W), 0)
        cj = _iota((W, W), 1) // HEAD_DIM
        for d in range(2):
            col = GATE_LANE0 + d * 2 * N_HEADS
            exp_scr[d, :, 0:W] = (ci == col + cj).astype(BF16)
            exp_scr[d, :, W:2 * W] = (ci == col + N_HEADS + cj).astype(BF16)

    hms = _head_masks()
    bd = _block_diag_mask()
    bdb = bdb_scr[...]
    rs = lambda x: _row_stack(x, hms)
    neg_inf = -jnp.inf

    def sum2(x, eb):
        hi, lo = _split2(x)
        return _nn(hi, eb) + _nn(lo, eb)

    per_dir = []
    for d, (q_ref, v_ref, g_ref) in enumerate(((qf_ref, vf_ref, gf_ref), (qb_ref, vb_ref, gb_ref))):
        qk = q_ref[0]
        g = g_ref[0] + gbias_ref[...]
        _, strict, incl, eye = _chunk_masks(d)
        tri = tri_scr[d]
        gates = sum2(g, exp_scr[d])
        li = gates[:, 0:W]
        lf = -_softplus(-gates[:, W:2 * W])
        lf_hi, lf_lo = _split2(lf)
        b2 = _nn(tri, jnp.concatenate([lf_hi, lf_lo], axis=1))
        per_dir.append(dict(q=qk[:, 0:W], k=qk[:, W:2 * W] * (HEAD_DIM ** -0.5), v=v_ref[0][:, 0:W],
                            li=li, bcol=b2[:, 0:W] + b2[:, W:2 * W], incl=incl, eye=eye))
        yield

    chains = []
    for c in range(cps):
        for d in range(2):
            pd = per_dir[d]
            sl = slice(c * L, (c + 1) * L)
            q, k, v, li, bcol = pd["q"][sl], pd["k"][sl], pd["v"][sl], pd["li"][sl], pd["bcol"][sl]
            brow = jnp.sum(jnp.where(pd["eye"], bcol, 0.0), axis=0, keepdims=True)
            lirow = jnp.sum(jnp.where(pd["eye"], li, 0.0), axis=0, keepdims=True)
            logd = jnp.where(pd["incl"], bcol - brow + lirow, neg_inf)
            mx = jnp.zeros((L, W), F32)
            for hm in hms:
                mh = jnp.max(jnp.where(hm, logd, neg_inf), axis=1, keepdims=True)
                mx = jnp.where(hm, mh, mx)
            blast = bcol[L - 1:L, :] if d == 0 else bcol[0:1, :]
            lwc = blast - bcol + li
            mlw = jnp.max(lwc, axis=0, keepdims=True)
            kw = k * jnp.exp(lwc - mlw)
            chains.append(dict(d=d, c=c, q=q, v=v, bcol=bcol, mx=mx, blast=blast, mlw=mlw, kw=kw,
                               dexp=jnp.exp(logd - mx), rsk=rs(k), rsv=rs(v),
                               nu0=jnp.sum(kw, axis=0, keepdims=True)))
        yield
    for ch in chains:
        ch["sp"] = _mm1(_nt, ch["q"], ch["rsk"]) * ch["dexp"]
        ch["cu0"] = bd * _mm1(_tn, ch["v"], ch["kw"])
    yield
    for ch in chains:
        ch["num0"] = _mm1(_nn, ch["sp"], ch["rsv"])
        ch["den0"] = _mm1(_nn, ch["sp"], bdb)
    yield

    by_key = {(ch["d"], ch["c"]): ch for ch in chains}
    c_st = [c_scr[0], c_scr[1]]
    n_row = [n_scr[0, 0:1, :], n_scr[1, 0:1, :]]
    m_row = [m_scr[0, 0:1, :], m_scr[1, 0:1, :]]
    ys = [[None] * cps, [None] * cps]
    for step in range(cps):
        for d in range(2):
            c = step if d == 0 else cps - 1 - step
            ch = by_key[(d, c)]
            inter = ch["bcol"] + m_row[d]
            mt = jnp.maximum(inter, ch["mx"])
            f_in = jnp.exp(ch["mx"] - mt)
            w_int = jnp.exp(inter - mt)
            num = f_in * ch["num0"] + w_int * _mm1(_nt, ch["q"], c_st[d])
            den = f_in * ch["den0"] + w_int * _mm1(_nn, ch["q"] * n_row[d], bdb)
            ys[d][c] = num / jnp.maximum(jnp.abs(den), jnp.exp(-mt))
            m_new = jnp.maximum(ch["blast"] + m_row[d], ch["mlw"])
            sc = jnp.exp(ch["blast"] + m_row[d] - m_new)
            e2 = jnp.exp(ch["mlw"] - m_new)
            c_st[d] = sc * c_st[d] + e2 * ch["cu0"]
            n_row[d] = sc * n_row[d] + e2 * ch["nu0"]
            m_row[d] = m_new
        yield
    for d, y_ref in enumerate((yf_ref, yb_ref)):
        c_scr[d] = c_st[d]
        n_scr[d, 0:1, :] = n_row[d]
        m_scr[d, 0:1, :] = m_row[d]
        y_ref[0] = jnp.concatenate(ys[d], axis=0)


def _mlstm_call(z, gbias, *, n_ctx, n_all):
    B, Ta, _ = z.shape
    W = GROUP_W
    cps = _chunks_per_step(n_ctx, n_all)
    L = cps * CHUNK
    n_ctx, n_all = n_ctx // cps, n_all // cps
    fwd = lambda blk: (lambda b, i: (b, i, blk))
    bwd = lambda blk: (lambda b, i: (b, _bwd_chunk(i, n_ctx, n_all), blk))
    out = jax.ShapeDtypeStruct((B, Ta, W), F32)
    return pl.pallas_call(
        functools.partial(_mlstm_kernel, cps=cps),
        out_shape=(out, out),
        grid=(B, n_all),
        in_specs=[pl.BlockSpec((1, L, 2 * W), fwd(Z_MQK // 2)),
                  pl.BlockSpec((1, L, 2 * W), bwd(Z_MQK // 2)),
                  pl.BlockSpec((1, L, 2 * W), fwd(Z_MVO // 2)),
                  pl.BlockSpec((1, L, 2 * W), bwd(Z_MVO // 2)),
                  pl.BlockSpec((1, L, W), fwd(Z_GATE)),
                  pl.BlockSpec((1, L, W), bwd(Z_GATE)),
                  pl.BlockSpec(gbias.shape, lambda b, i: (0, 0))],
        out_specs=(pl.BlockSpec((1, L, W), fwd(0)), pl.BlockSpec((1, L, W), bwd(0))),
        scratch_shapes=[pltpu.VMEM((2, W, W), F32), pltpu.VMEM((2, 8, W), F32), pltpu.VMEM((2, 8, W), F32),
                        pltpu.VMEM((2, L, L), BF16), pltpu.VMEM((2, W, 2 * W), BF16), pltpu.VMEM((W, W), BF16)],
        compiler_params=_cparams("parallel", "arbitrary"),
        name="mlstm",
    )(z, z, z, z, z, z, gbias)


def _recur_kernel(*refs, cps):
    r_in, m_in = refs[0:12], refs[12:19]
    r_out, m_out = refs[19:23], refs[23:25]
    r_scr, m_scr = refs[25:28], refs[28:34]
    _run_stages(_rwkv_stages(*r_in, *r_out, *r_scr, cps=cps),
                _mlstm_stages(*m_in, *m_out, *m_scr, cps=cps))


def _recur_call(z, wup, aup, gup, w0, a0, kk, ka, rk, gbias, *, n_ctx, n_all):
    B, Ta, _ = z.shape
    W = GROUP_W
    cps = _chunks_per_step(n_ctx, n_all)
    rows = cps * CHUNK
    gc, ga = n_ctx // cps, n_all // cps
    full = lambda a: pl.BlockSpec(a.shape, lambda b, i: (0,) * a.ndim)
    fwd = lambda blk: (lambda b, i: (b, i, blk))
    bwd = lambda blk: (lambda b, i: (b, _bwd_chunk(i, gc, ga), blk))
    tok = lambda width, imap: pl.BlockSpec((1, rows, width), imap)
    out = jax.ShapeDtypeStruct((B, Ta, W), F32)
    return pl.pallas_call(
        functools.partial(_recur_kernel, cps=cps),
        out_shape=(out,) * 6,
        grid=(B, ga),
        in_specs=[tok(3 * W, fwd(Z_RKV // 3)), tok(3 * W, bwd(Z_RKV // 3)),
                  tok(W, fwd(Z_LORA)), tok(W, bwd(Z_LORA)),
                  full(wup), full(aup), full(gup), full(w0), full(a0), full(kk), full(ka), full(rk),
                  tok(2 * W, fwd(Z_MQK // 2)), tok(2 * W, bwd(Z_MQK // 2)),
                  tok(2 * W, fwd(Z_MVO // 2)), tok(2 * W, bwd(Z_MVO // 2)),
                  tok(W, fwd(Z_GATE)), tok(W, bwd(Z_GATE)), full(gbias)],
        out_specs=(tok(W, fwd(0)), tok(W, bwd(0)), tok(W, fwd(0)), tok(W, fwd(0)),
                   tok(W, fwd(0)), tok(W, bwd(0))),
        scratch_shapes=[pltpu.VMEM((2, W, W), F32), pltpu.VMEM((2, rows, rows), BF16), pltpu.VMEM((W, W), BF16),
                        pltpu.VMEM((2, W, W), F32), pltpu.VMEM((2, 8, W), F32), pltpu.VMEM((2, 8, W), F32),
                        pltpu.VMEM((2, rows, rows), BF16), pltpu.VMEM((2, W, 2 * W), BF16),
                        pltpu.VMEM((W, W), BF16)],
        compiler_params=_cparams("parallel", "arbitrary"),
        name="rwkv_mlstm",
    )(z, z, z, z, wup, aup, gup, w0, a0, kk, ka, rk, z, z, z, z, z, z, gbias)


def _head_norm(y, bd, eps):
    bdb = bd.astype(BF16)

    def head_mean(x):
        hi, lo = _split2(x)
        return (_nn(hi, bdb) + _nn(lo, bdb)) * (1.0 / HEAD_DIM)

    yc = y - head_mean(y)
    return yc * lax.rsqrt(head_mean(yc * yc) + eps)


def _outproj_kernel(x_ref, mods_ref, s5f_ref, s5b_ref, u_ref, at_ref, rf_ref, rb_ref, rg_ref, rbon_ref,
                    mf_ref, mb_ref, vo_ref, s5d_ref, wglu_ref, bglu_ref, lnw_ref, lnb_ref, mnw_ref,
                    gpost_ref, wout_ref, o_ref, *, tm, tc, n_batch):
    b = pl.program_id(0)
    t = pl.program_id(1)
    d = x_ref.shape[-1]
    W = GROUP_W
    is_ctx = (t * tm + _iota((tm, 1), 0)) < tc
    gate = jnp.where(is_ctx, mods_ref[pl.ds(n_batch, 1), pl.ds(5 * d, d)],
                     mods_ref[pl.ds(b, 1), pl.ds(5 * d, d)])
    bd = _block_diag_mask()

    y = s5f_ref[...] + s5b_ref[...] + s5d_ref[...] * u_ref[0]
    zg = 0.5 * y * (1.0 + jnp.tanh(math.sqrt(2.0 / math.pi) * (y + 0.044715 * (y * y * y))))
    s5o = zg * _sigmoid(_mm1(_nn, zg, wglu_ref[...]) + bglu_ref[...])

    yr = _head_norm(rf_ref[0] + rb_ref[0], bd, RWKV_GN_EPS)
    rwo = (yr * lnw_ref[...] + lnb_ref[...] + rbon_ref[0]) * rg_ref[0]

    ym = _head_norm(mf_ref[0] + mb_ref[0], bd, NORM_EPS)
    mlo = ym * mnw_ref[...] * _sigmoid(vo_ref[0])

    cat = jnp.concatenate([s5o.astype(BF16), at_ref[0], rwo.astype(BF16), mlo.astype(BF16)], axis=1)
    yx = _nn(cat, wout_ref[...])
    o_ref[0] = x_ref[0] + gate * _rms(yx, gpost_ref[...])


def _outproj_call(xa, mods, s5f, s5b, z, attn, rf, rb, rg, rbon, mf, mb,
                  s5d, wglu, bglu, lnw, lnb, mnw, gpost, wout, *, tc):
    B, Ta, D = xa.shape
    W = GROUP_W
    tm = _pick_tile(Ta, 544)
    kern = functools.partial(_outproj_kernel, tm=tm, tc=tc, n_batch=B)
    full = lambda a: pl.BlockSpec(a.shape, lambda b, t: (0,) * a.ndim)
    tok = pl.BlockSpec((1, tm, W), lambda b, t: (b, t, 0))
    tmaj = pl.BlockSpec((tm, W), lambda b, t: (t, b))
    return pl.pallas_call(
        kern,
        out_shape=jax.ShapeDtypeStruct((B, Ta, D), F32),
        grid=(B, Ta // tm),
        in_specs=[pl.BlockSpec((1, tm, D), lambda b, t: (b, t, 0)), full(mods),
                  tmaj, tmaj, pl.BlockSpec((1, tm, W), lambda b, t: (b, t, Z_S5)),
                  tok, tok, tok, tok, tok, tok, tok,
                  pl.BlockSpec((1, tm, W), lambda b, t: (b, t, Z_MVO + 1)),
                  full(s5d), full(wglu), full(bglu), full(lnw), full(lnb), full(mnw), full(gpost), full(wout)],
        out_specs=pl.BlockSpec((1, tm, D), lambda b, t: (b, t, 0)),
        compiler_params=_cparams("parallel", "parallel"),
        name="mix_out",
    )(xa, mods, s5f, s5b, z, attn, rf, rb, rg, rbon, mf, mb, z, s5d, wglu, bglu, lnw, lnb, mnw, gpost, wout)


def _rope_rotate_cols(w):
    h = ROPE_AXIS // 2
    return jnp.concatenate([-w[..., h:2 * h], w[..., 0:h], -w[..., 3 * h:4 * h], w[..., 2 * h:3 * h]], axis=-1)


def _inproj_relayout(w_in):
    L, D, _ = w_in.shape
    o_s5, o_mla, o_rw, o_ml = 0, 256, 672, 1568
    seg = lambda a, n: w_in[:, :, a:a + n]
    zer = lambda n: jnp.zeros((L, D, n), w_in.dtype)
    k_rope = seg(o_mla + 384, ROPE_DIM)
    parts = [seg(o_mla, 256), seg(o_mla + 256, 128),
             zer(64), k_rope, zer(32),
             zer(64), _rope_rotate_cols(k_rope), zer(32), seg(o_ml + 1024, 16), zer(128 - 16),
             seg(o_rw, 768), seg(o_s5, 256),
             seg(o_rw + 768, 128), zer(128),
             seg(o_ml, 512), seg(o_ml + 512, 512)]
    out = jnp.concatenate(parts, axis=2)
    assert out.shape[2] == Z_COLS
    return out


def _rope_tables(T, tc):
    rows = T // GRID_W
    r_idx, c_idx = jnp.meshgrid(jnp.arange(rows), jnp.arange(GRID_W), indexing='ij')
    inv_freq = 1.0 / (ROPE_BASE ** (jnp.arange(0, ROPE_AXIS, 2, dtype=F32) / ROPE_AXIS))
    ang_r = r_idx.reshape(-1, 1).astype(F32) * inv_freq
    ang_c = c_idx.reshape(-1, 1).astype(F32) * inv_freq
    ang = jnp.concatenate([ang_r, ang_r, ang_c, ang_c], axis=-1)
    cos = jnp.concatenate([jnp.ones((tc, ROPE_DIM), F32), jnp.cos(ang)], axis=0)
    sin = jnp.concatenate([jnp.zeros((tc, ROPE_DIM), F32), jnp.sin(ang)], axis=0)
    ta = T + tc
    cos_t = jnp.concatenate([jnp.ones((ta, 64), F32), cos, jnp.zeros((ta, 32), F32)], axis=1)
    sin_t = jnp.concatenate([jnp.zeros((ta, 64), F32), sin, jnp.zeros((ta, 32), F32)], axis=1)
    return cos_t, sin_t


def _pad_rows(w, r0, total):
    pad = [(0, 0)] * (w.ndim - 2) + [(r0, total - r0 - w.shape[-2]), (0, 0)]
    return jnp.pad(w, pad)


def kernel(x, c, ctx, c_ctx, w_ada, b_ada, norm_pre, norm_post, ffn_w_gate, ffn_w_up, ffn_w_down, w_in, w_out, s5_lam_re, s5_lam_im, s5_log_dt, s5_b_re, s5_b_im, s5_c_re, s5_c_im, s5_d, s5_w_glu, s5_b_glu, mla_q_norm, mla_kv_norm, mla_w_uq, mla_w_ukv, rwkv_conv_w, rwkv_conv_b, rwkv_w0, rwkv_w_up, rwkv_a0, rwkv_a_up, rwkv_g_up, rwkv_k_k, rwkv_k_a, rwkv_r_k, rwkv_ln_w, rwkv_ln_b, mlstm_conv_w, mlstm_conv_b, mlstm_gate_b, mlstm_norm):
    B, T, D = x.shape
    Tc = ctx.shape[1]
    Ta = T + Tc
    L = w_ada.shape[0]
    W = GROUP_W
    assert T % CHUNK == 0 and Tc % CHUNK == 0 and B % 8 == 0 and B <= 8
    n_ctx, n_all = Tc // CHUNK, Ta // CHUNK

    rows = 16
    cvec = jnp.concatenate([c, c_ctx[None, :], jnp.zeros((rows - B - 1, D), F32)], axis=0)
    mods_all = _ada_call(cvec, w_ada, b_ada)

    w_in_re = _inproj_relayout(w_in).astype(BF16)
    cw = jnp.zeros((L, Z_NBLK, 8, ZB), F32)
    rc = jnp.concatenate([rwkv_conv_w, rwkv_conv_b[:, None, :]], axis=1).reshape(L, 4, 3, ZB).transpose(0, 2, 1, 3)
    mc = jnp.concatenate([mlstm_conv_w, mlstm_conv_b[:, None, :]], axis=1).reshape(L, 4, 2, ZB).transpose(0, 2, 1, 3)
    cw = cw.at[:, Z_RKV:Z_RKV + 3, 0:4].set(rc).at[:, Z_MQK:Z_MQK + 2, 0:4].set(mc)

    wg = ffn_w_gate.astype(BF16)
    wu = ffn_w_up.astype(BF16)
    wd = ffn_w_down.astype(BF16)
    wout = w_out.astype(BF16)

    G = s5_lam_re.shape[2]
    N = s5_lam_re.shape[3]
    eye_g = jnp.eye(G, dtype=F32)
    lre = s5_lam_re.reshape(L, 2, 1, G * N)
    lim = s5_lam_im.reshape(L, 2, 1, G * N)
    ldt = jnp.repeat(s5_log_dt, N, axis=-1).reshape(L, 2, 1, G * N)
    wbre = jnp.einsum('ldgnp,gh->ldgphn', s5_b_re, eye_g).reshape(L, 2, G * S5_P, G * N)
    wbim = jnp.einsum('ldgnp,gh->ldgphn', s5_b_im, eye_g).reshape(L, 2, G * S5_P, G * N)
    wcre = jnp.einsum('ldgpn,gh->ldgnhp', s5_c_re, eye_g).reshape(L, 2, G * N, G * S5_P).astype(BF16)
    wcim = jnp.einsum('ldgpn,gh->ldgnhp', s5_c_im, eye_g).reshape(L, 2, G * N, G * S5_P).astype(BF16)

    nope = HEAD_DIM
    qd = nope + ROPE_DIM
    wq4 = mla_w_uq.reshape(L, -1, N_HEADS, qd)
    wq = jnp.pad(wq4, ((0, 0), (0, 0), (0, 0), (0, 128 - qd))).reshape(L, -1, N_HEADS * 128).astype(BF16)
    wq_rot = _rope_rotate_cols(wq4[..., nope:])
    wqr = jnp.pad(wq_rot, ((0, 0), (0, 0), (0, 0), (nope, 128 - qd))).reshape(L, -1, N_HEADS * 128).astype(BF16)
    wkv4 = mla_w_ukv.reshape(L, -1, N_HEADS, 2 * HEAD_DIM)
    wk = jnp.pad(wkv4[..., :HEAD_DIM], ((0, 0), (0, 0), (0, 0), (0, 64))).reshape(L, -1, N_HEADS * 128).astype(BF16)
    wv = jnp.pad(wkv4[..., HEAD_DIM:], ((0, 0), (0, 0), (0, 0), (0, 64))).reshape(L, -1, N_HEADS * 128).astype(BF16)
    cos_t, sin_t = _rope_tables(T, Tc)
    scale = float(qd) ** -0.5 * math.log2(math.e)

    wup = _pad_rows(rwkv_w_up, 0, W)
    aup = _pad_rows(rwkv_a_up, 32, W)
    gup = _pad_rows(rwkv_g_up, 64, W)
    gbias = jnp.pad(mlstm_gate_b, ((0, 0), (GATE_LANE0, W - GATE_LANE0 - mlstm_gate_b.shape[1])))

    xa = jnp.concatenate([ctx, x], axis=1)
    r1 = lambda a: a.reshape(1, -1)

    for l in range(L):
        mods = mods_all[l]
        xa2, hmix = _ffn_call(xa.reshape(B * Ta, D), mods, r1(norm_pre[l, 0]), r1(norm_post[l, 0]),
                              r1(norm_pre[l, 1]), wg[l, 0], wu[l, 0], wd[l, 0],
                              ta=Ta, tc=Tc, n_batch=B, koff=0, emit_hmix=True)
        xa = xa2.reshape(B, Ta, D)
        z = _inproj_call(hmix.reshape(B, Ta, D), w_in_re[l], cw[l], tc=Tc)

        s5f, s5b = _s5_call(z, lre[l], lim[l], ldt[l], wbre[l], wbim[l], wcre[l], wcim[l],
                            n_ctx=n_ctx, n_all=n_all)

        q, k, v = _mla_proj_call(z, r1(mla_q_norm[l]), r1(mla_kv_norm[l]), wq[l], wqr[l], wk[l], wv[l],
                                 cos_t, sin_t, scale=scale)
        attn = _attn_call(q, k, v, tc=Tc)

        rf, rb, rg, rbon, mf, mb = _recur_call(
            z, wup[l], aup[l], gup[l], rwkv_w0[l][:, None, :], rwkv_a0[l][:, None, :],
            r1(rwkv_k_k[l]), r1(rwkv_k_a[l]), r1(rwkv_r_k[l]), gbias[l:l + 1], n_ctx=n_ctx, n_all=n_all)

        xa = _outproj_call(xa, mods, s5f, s5b, z, attn, rf, rb, rg, rbon,
                           mf, mb, r1(s5_d[l]), s5_w_glu[l].astype(BF16), r1(s5_b_glu[l]), r1(rwkv_ln_w[l]),
                           r1(rwkv_ln_b[l]), r1(mlstm_norm[l]), r1(norm_post[l, 1]), wout[l], tc=Tc)

        xa2, _ = _ffn_call(xa.reshape(B * Ta, D), mods, r1(norm_pre[l, 2]), r1(norm_post[l, 2]),
                           r1(norm_pre[l, 1]), wg[l, 1], wu[l, 1], wd[l, 1],
                           ta=Ta, tc=Tc, n_batch=B, koff=6, emit_hmix=False)
        xa = xa2.reshape(B, Ta, D)

    return xa[:, Tc:, :]
```

```python
import functools
import math

import jax
import jax.numpy as jnp
from jax import lax
from jax.experimental import pallas as pl
from jax.experimental.pallas import tpu as pltpu

F32 = jnp.float32
BF16 = jnp.bfloat16

GROUP_W = 256
HEAD_DIM = 64
N_HEADS = GROUP_W // HEAD_DIM
CHUNK = 64
S5_CHUNK = 128
NORM_EPS = 1e-6
RWKV_GN_EPS = HEAD_DIM * 1e-5
GRID_W = 64
ROPE_BASE = 10000.0
ROPE_DIM = 32
ROPE_AXIS = 16
S5_P = 16
MACARON = 0.5
VMEM_LIMIT_BYTES = 56 * 1024 * 1024

ZB = 256
Z_RKV, Z_S5, Z_LORA, Z_MQK, Z_MVO = 3, 6, 7, 8, 10
Z_GATE = 2
GATE_LANE0 = 128
Z_NBLK = 12
Z_COLS = Z_NBLK * ZB


def _nn(a, b):
    return lax.dot_general(a, b, (((1,), (0,)), ((), ())), preferred_element_type=F32)


def _nt(a, b):
    return lax.dot_general(a, b, (((1,), (1,)), ((), ())), preferred_element_type=F32)


def _tn(a, b):
    return lax.dot_general(a, b, (((0,), (0,)), ((), ())), preferred_element_type=F32)


def _split2(x):
    hi = x.astype(BF16)
    lo = (x - hi.astype(F32)).astype(BF16)
    return hi, lo


def _mm3(dotf, a, b):
    ah, al = _split2(a)
    bh, bl = _split2(b)
    return dotf(ah, bh) + (dotf(ah, bl) + dotf(al, bh))


def _mm1(dotf, a, b):
    return dotf(a.astype(BF16), b.astype(BF16))


_mmn = _mm1


def _rms(x, g):
    return x * lax.rsqrt(jnp.mean(x * x, axis=-1, keepdims=True) + NORM_EPS) * g


def _sigmoid(x):
    return 0.5 * jnp.tanh(0.5 * x) + 0.5


def _group_tri_masks(tri_scr, rows):
    r = _iota((rows, rows), 0)
    c = _iota((rows, rows), 1)
    same = (r // CHUNK) == (c // CHUNK)
    tri_scr[0] = (same & (c <= r)).astype(BF16)
    tri_scr[1] = (same & (c >= r)).astype(BF16)


def _softplus(x):
    return jnp.maximum(x, 0.0) + jnp.log(1.0 + jnp.exp(-jnp.abs(x)))


def _iota(shape, dim):
    return lax.broadcasted_iota(jnp.int32, shape, dim)


def _head_masks():
    lane = _iota((1, GROUP_W), 1)
    return [lane // HEAD_DIM == h for h in range(N_HEADS)]


def _block_diag_mask():
    r = _iota((GROUP_W, GROUP_W), 0) // HEAD_DIM
    c = _iota((GROUP_W, GROUP_W), 1) // HEAD_DIM
    return (r == c).astype(F32)


def _row_stack(x, hms):
    xb = x.astype(BF16)
    return jnp.concatenate([jnp.where(m, xb, 0.0) for m in hms], axis=0)


def _pick_tile(n, target, mult=16):
    best = None
    for t in range(mult, min(n, target) + 1, mult):
        if n % t == 0:
            best = t
    if best is None:
        raise ValueError(f"no tile for {n}")
    return best


def _cparams(*sem):
    return pltpu.CompilerParams(dimension_semantics=sem, vmem_limit_bytes=VMEM_LIMIT_BYTES)


def _ada_kernel(c_ref, w_ref, b_ref, o_ref):
    c = c_ref[...]
    s = c * _sigmoid(c)
    o_ref[0] = _mm3(_nn, s, w_ref[0]) + b_ref[0]


def _ada_call(cvec, w_ada, b_ada):
    L, D, N = w_ada.shape
    R = cvec.shape[0]
    tn = _pick_tile(N, 1152, 128)
    return pl.pallas_call(
        _ada_kernel,
        out_shape=jax.ShapeDtypeStruct((L, R, N), F32),
        grid=(L, N // tn),
        in_specs=[pl.BlockSpec((R, D), lambda l, j: (0, 0)),
                  pl.BlockSpec((1, D, tn), lambda l, j: (l, 0, j)),
                  pl.BlockSpec((1, 1, tn), lambda l, j: (l, 0, j))],
        out_specs=pl.BlockSpec((1, R, tn), lambda l, j: (l, 0, j)),
        compiler_params=_cparams("parallel", "parallel"),
        name="ada_mod",
    )(cvec, w_ada, b_ada.reshape(L, 1, N))


FFN_COLS = 256


def _ffn_kernel(xn_ref, xp_ref, mods_ref, gpre_ref, gpost_ref, gmix_ref, wg_ref, wu_ref, wd_ref,
                o_ref, hmix_ref, h_scr, acc_scr, *, tm, tiles_per_batch, tc, n_batch, koff, emit_hmix,
                n_tiles):
    i = pl.program_id(0)
    f = pl.program_id(1)
    d = xn_ref.shape[-1]
    fd = wg_ref.shape[1]

    def mod_of(tile):
        b = tile // tiles_per_batch
        t0 = (tile % tiles_per_batch) * tm
        is_ctx = (t0 + _iota((tm, 1), 0)) < tc

        def mod(fn):
            row = lambda r: (lambda k: mods_ref[pl.ds(r, 1), pl.ds(k * d, d)])
            return jnp.where(is_ctx, fn(row(n_batch)), fn(row(b)))
        return mod

    unit = lambda x: x * lax.rsqrt(jnp.mean(x * x, axis=-1, keepdims=True) + NORM_EPS)

    def pre_norm(tile, x):
        mod = mod_of(tile)
        gain = mod(lambda m: gpre_ref[...] * (1.0 + m(koff + 1)))
        return (unit(x) * gain + mod(lambda m: m(koff))).astype(BF16)

    def finish(tile, acc, x):
        mod = mod_of(tile)
        xn = x + unit(acc) * mod(lambda m: (MACARON * m(koff + 2)) * gpost_ref[...])
        o_ref[...] = xn
        if emit_hmix:
            gain = mod(lambda m: gmix_ref[...] * (1.0 + m(4)))
            hmix_ref[...] = (unit(xn) * gain + mod(lambda m: m(3))).astype(BF16)
        else:
            hmix_ref[...] = jnp.zeros_like(hmix_ref)

    def hidden_cols(hb, lo, hi):
        acc = None
        for c0 in range(lo, hi, FFN_COLS):
            g = _nn(hb, wg_ref[:, c0:c0 + FFN_COLS])
            u = _nn(hb, wu_ref[:, c0:c0 + FFN_COLS])
            t = _nn((g * _sigmoid(g) * u).astype(BF16), wd_ref[c0:c0 + FFN_COLS, :])
            acc = t if acc is None else acc + t
        return acc

    @pl.when(jnp.logical_and(i == 0, f == 0))
    def _():
        h_scr[0] = pre_norm(0, xp_ref[...])
        acc_scr[1] = jnp.zeros((tm, d), F32)

    for s in range(2):
        tile = 2 * i + s

        @pl.when(jnp.logical_and(tile < n_tiles, f == s))
        def _(s=s, tile=tile):
            finish(jnp.maximum(tile - 1, 0), acc_scr[1 - s], xp_ref[...])
            h_scr[1 - s] = pre_norm(jnp.minimum(tile + 1, n_tiles - 1), xn_ref[...])
            acc_scr[s] = hidden_cols(h_scr[s], 0, fd)

    @pl.when(jnp.logical_and(2 * i == n_tiles, f == 0))
    def _():
        finish(n_tiles - 1, acc_scr[1], xp_ref[...])


def _ffn_call(xa2, mods, gpre, gpost, gmix, wg, wu, wd, *, ta, tc, n_batch, koff, emit_hmix):
    M, D = xa2.shape
    Fd = wg.shape[1]
    assert Fd % FFN_COLS == 0
    tm = _pick_tile(ta, 544)
    n_tiles = M // tm
    assert n_tiles % 2 == 0
    kern = functools.partial(_ffn_kernel, tm=tm, tiles_per_batch=ta // tm, tc=tc, n_batch=n_batch,
                             koff=koff, emit_hmix=emit_hmix, n_tiles=n_tiles)
    hm_rows = tm if emit_hmix else 16
    last = n_tiles - 1
    resident = lambda a: pl.BlockSpec(a.shape, lambda i, f: (0, 0), pipeline_mode=pl.Buffered(1))
    vec = pl.BlockSpec((1, D), lambda i, f: (0, 0))
    return pl.pallas_call(
        kern,
        out_shape=(jax.ShapeDtypeStruct((M, D), F32),
                   jax.ShapeDtypeStruct((M if emit_hmix else 16 * n_tiles, D), BF16)),
        grid=(n_tiles // 2 + 1, 2),
        in_specs=[pl.BlockSpec((tm, D), lambda i, f: (jnp.minimum(2 * i + f + 1, last), 0)),
                  pl.BlockSpec((tm, D), lambda i, f: (jnp.clip(2 * i + f - 1, 0, last), 0)),
                  pl.BlockSpec(mods.shape, lambda i, f: (0, 0)),
                  vec, vec, vec, resident(wg), resident(wu), resident(wd)],
        out_specs=(pl.BlockSpec((tm, D), lambda i, f: (jnp.clip(2 * i + f - 1, 0, last), 0)),
                   pl.BlockSpec((hm_rows, D), lambda i, f: (jnp.clip(2 * i + f - 1, 0, last), 0))),
        scratch_shapes=[pltpu.VMEM((2, tm, D), BF16), pltpu.VMEM((2, tm, D), F32)],
        compiler_params=_cparams("arbitrary", "arbitrary"),
        name="half_ffn",
    )(xa2, xa2, mods, gpre, gpost, gmix, wg, wu, wd)


def _inproj_kernel(h_ref, w_ref, cw_ref, o_ref, *, tc, conv_lo, conv_hi, silu_lo, silu_hi):
    nb = pl.program_id(1)
    ta = h_ref.shape[1]
    is_conv = ((nb >= conv_lo[0]) & (nb < conv_hi[0])) | ((nb >= conv_lo[1]) & (nb < conv_hi[1]))
    is_silu = (nb >= silu_lo) & (nb < silu_hi)

    @pl.when(jnp.logical_not(is_conv))
    def _():
        o_ref[0] = _nn(h_ref[0], w_ref[...])

    def conv():
        z = _nn(h_ref[0], w_ref[...])
        row = _iota((ta, 1), 0)
        zp = jnp.where((row == 0) | (row == tc), 0.0, pltpu.roll(z, 1, 0))
        zn = jnp.where((row == tc - 1) | (row == ta - 1), 0.0, pltpu.roll(z, ta - 1, 0))
        cw = cw_ref[0]
        return cw[3:4] + zp * cw[0:1] + z * cw[1:2] + zn * cw[2:3]

    @pl.when(is_conv & jnp.logical_not(is_silu))
    def _():
        o_ref[0] = conv()

    @pl.when(is_conv & is_silu)
    def _():
        y = conv()
        o_ref[0] = y * _sigmoid(y)


def _inproj_call(hmix3, w_re, cw, *, tc):
    B, Ta, D = hmix3.shape
    kern = functools.partial(_inproj_kernel, tc=tc, conv_lo=(Z_RKV, Z_MQK), conv_hi=(Z_RKV + 3, Z_MQK + 2),
                             silu_lo=Z_MQK, silu_hi=Z_MQK + 2)
    return pl.pallas_call(
        kern,
        out_shape=jax.ShapeDtypeStruct((B, Ta, Z_COLS), F32),
        grid=(B, Z_NBLK),
        in_specs=[pl.BlockSpec((1, Ta, D), lambda b, n: (b, 0, 0)),
                  pl.BlockSpec((D, ZB), lambda b, n: (0, n)),
                  pl.BlockSpec((1, 8, ZB), lambda b, n: (n, 0, 0))],
        out_specs=pl.BlockSpec((1, Ta, ZB), lambda b, n: (b, 0, n)),
        compiler_params=_cparams("parallel", "arbitrary"),
        name="in_proj",
    )(hmix3, w_re, cw)


def _bwd_chunk(i, n_ctx, n_all):
    return jnp.where(i < n_ctx, n_ctx - 1 - i, n_all - 1 - (i - n_ctx))


def _s5_kernel(uf_ref, ub_ref, lre_ref, lim_ref, ldt_ref, wbre_ref, wbim_ref, wcre_ref, wcim_ref,
               yf_ref, yb_ref, wb_scr, coef_scr, st_scr, rel_scr, x_scr, *, lc, nb, scan_unroll):
    i = pl.program_id(0)
    gn = lre_ref.shape[-1]

    @pl.when(i == 0)
    def _():
        for d in range(2):
            dt = jnp.exp(ldt_ref[d])
            lre = lre_ref[d]
            lim = lim_ref[d]
            mag = jnp.exp(lre * dt)
            ar = mag * jnp.cos(lim * dt)
            ai = mag * jnp.sin(lim * dt)
            den = lre * lre + lim * lim
            fr = ((ar - 1.0) * lre + ai * lim) / den
            fi = (ai * lre - (ar - 1.0) * lim) / den
            coef_scr[d, 0:nb, :] = jnp.broadcast_to(ar, (nb, gn))
            coef_scr[d, nb:2 * nb, :] = jnp.broadcast_to(ai, (nb, gn))
            wre = wbre_ref[d]
            wim = wbim_ref[d]
            wb_scr[d, :, 0:gn] = (wre * fr - wim * fi).astype(BF16)
            wb_scr[d, :, gn:2 * gn] = (wim * fr + wre * fi).astype(BF16)
        st_scr[...] = jnp.zeros_like(st_scr)

    half = 128
    u_refs = (uf_ref, ub_ref)
    y_refs = (yf_ref, yb_ref)

    for d in range(2):
        for b in range(nb):
            for s in range(2):
                rel_scr[d, s, pl.ds(b, lc, stride=nb), :] = u_refs[d][b, :, s * half:(s + 1) * half]
    for d in range(2):
        u_tm = jnp.concatenate([rel_scr[d, 0], rel_scr[d, 1]], axis=1).astype(BF16)
        x_scr[d] = _nn(u_tm, wb_scr[d])

    for d in range(2):
        def body(t, carry, d=d):
            sr, si = carry
            ar = coef_scr[d, 0:nb, :]
            ai = coef_scr[d, nb:2 * nb, :]
            tt = t if d == 0 else lc - 1 - t
            r0 = pl.multiple_of(tt * nb, nb)
            xr = x_scr[d, pl.ds(r0, nb), 0:gn]
            xi = x_scr[d, pl.ds(r0, nb), gn:2 * gn]
            nsr = ar * sr - ai * si + xr
            nsi = ar * si + ai * sr + xi
            x_scr[d, pl.ds(r0, nb), 0:gn] = nsr
            x_scr[d, pl.ds(r0, nb), gn:2 * gn] = nsi
            return nsr, nsi

        sr, si = lax.fori_loop(0, lc, body, (st_scr[d, 0:nb, :], st_scr[d, nb:2 * nb, :]), unroll=scan_unroll)
        st_scr[d, 0:nb, :] = sr
        st_scr[d, nb:2 * nb, :] = si
        y = (_nn(x_scr[d, :, 0:gn].astype(BF16), wcre_ref[d])
             - _nn(x_scr[d, :, gn:2 * gn].astype(BF16), wcim_ref[d]))
        rel_scr[d, 0] = y[:, 0:half]
        rel_scr[d, 1] = y[:, half:2 * half]
    for d in range(2):
        for b in range(nb):
            for s in range(2):
                c0 = b * 2 * half + s * half
                y_refs[d][:, c0:c0 + half] = rel_scr[d, s, pl.ds(b, lc, stride=nb), :]


def _s5_call(z, lre, lim, ldt, wbre, wbim, wcre, wcim, *, n_ctx, n_all):
    B, Ta, _ = z.shape
    W = GROUP_W
    lc = S5_CHUNK
    assert (n_ctx * CHUNK) % lc == 0 and (n_all * CHUNK) % lc == 0
    n_ctx, n_all = n_ctx * CHUNK // lc, n_all * CHUNK // lc
    gn = lre.shape[-1]
    blk = lc * B
    kern = functools.partial(_s5_kernel, lc=lc, nb=B, scan_unroll=True)
    full = lambda a: pl.BlockSpec(a.shape, lambda i: (0,) * a.ndim)
    out = jax.ShapeDtypeStruct((Ta, B * W), F32)
    return pl.pallas_call(
        kern,
        out_shape=(out, out),
        grid=(n_all,),
        in_specs=[pl.BlockSpec((B, lc, W), lambda i: (0, i, Z_S5)),
                  pl.BlockSpec((B, lc, W), lambda i: (0, _bwd_chunk(i, n_ctx, n_all), Z_S5)),
                  full(lre), full(lim), full(ldt), full(wbre), full(wbim), full(wcre), full(wcim)],
        out_specs=(pl.BlockSpec((lc, B * W), lambda i: (i, 0)),
                   pl.BlockSpec((lc, B * W), lambda i: (_bwd_chunk(i, n_ctx, n_all), 0))),
        scratch_shapes=[pltpu.VMEM((2, W, 2 * gn), BF16),
                        pltpu.VMEM((2, 2 * B, gn), F32),
                        pltpu.VMEM((2, 2 * B, gn), F32),
                        pltpu.VMEM((2, 2, blk, 128), F32),
                        pltpu.VMEM((2, blk, 2 * gn), F32)],
        compiler_params=_cparams("arbitrary"),
        name="s5_scan",
    )(z, z, lre, lim, ldt, wbre, wbim, wcre, wcim)


def _mla_proj_kernel(z_ref, qn_ref, kvn_ref, wq_ref, wqr_ref, wk_ref, wv_ref, cos_ref, sin_ref,
                     q_ref, k_ref, v_ref, *, scale):
    z = z_ref[0]
    cq = z[:, 0:256]
    ckv = z[:, 256:384]
    kr = z[:, 384:512]
    krr = z[:, 512:640]
    cos = cos_ref[...]
    sin = sin_ref[...]
    cqb = _rms(cq, qn_ref[...]).astype(BF16)
    ckvb = _rms(ckv, kvn_ref[...]).astype(BF16)
    q = _nn(cqb, wq_ref[...])
    qr = _nn(cqb, wqr_ref[...])
    kn = _nn(ckvb, wk_ref[...])
    krp = kr * cos + krr * sin
    for h in range(N_HEADS):
        sl = slice(h * 128, (h + 1) * 128)
        q_ref[0, h] = ((q[:, sl] * cos + qr[:, sl] * sin) * scale).astype(BF16)
        k_ref[0, h] = (kn[:, sl] + krp).astype(BF16)
    ones_pad = ((_iota((1, N_HEADS * 128), 1) % 128) >= HEAD_DIM).astype(F32)
    vv = _nn(ckvb, wv_ref[...]) + ones_pad
    for h in range(N_HEADS):
        v_ref[0, h] = vv[:, h * 128:(h + 1) * 128].astype(BF16)


def _mla_proj_call(z, qn, kvn, wq, wqr, wk, wv, cos_t, sin_t, *, scale):
    B, Ta, _ = z.shape
    tm = _pick_tile(Ta, 1088)
    kern = functools.partial(_mla_proj_kernel, scale=scale)
    full = lambda a: pl.BlockSpec(a.shape, lambda b, t: (0,) * a.ndim)
    return pl.pallas_call(
        kern,
        out_shape=(jax.ShapeDtypeStruct((B, N_HEADS, Ta, 128), BF16),
                   jax.ShapeDtypeStruct((B, N_HEADS, Ta, 128), BF16),
                   jax.ShapeDtypeStruct((B, N_HEADS, Ta, 128), BF16)),
        grid=(B, Ta // tm),
        in_specs=[pl.BlockSpec((1, tm, 3 * ZB), lambda b, t: (b, t, 0)),
                  full(qn), full(kvn), full(wq), full(wqr), full(wk), full(wv),
                  pl.BlockSpec((tm, 128), lambda b, t: (t, 0)),
                  pl.BlockSpec((tm, 128), lambda b, t: (t, 0))],
        out_specs=(pl.BlockSpec((1, N_HEADS, tm, 128), lambda b, t: (b, 0, t, 0)),
                   pl.BlockSpec((1, N_HEADS, tm, 128), lambda b, t: (b, 0, t, 0)),
                   pl.BlockSpec((1, N_HEADS, tm, 128), lambda b, t: (b, 0, t, 0))),
        compiler_params=_cparams("parallel", "parallel"),
        name="mla_proj",
    )(z, qn, kvn, wq, wqr, wk, wv, cos_t, sin_t)


def _attn_kernel(q_ref, k_ref, v_ref, o_ref, *, tc, ta, n_ctx_tiles, key_chunk):
    i = pl.program_id(1)
    lane = _iota((1, 128), 1)

    def write_out(acc):
        outs = [a * (1.0 / pltpu.roll(a, HEAD_DIM, 1)) for a in acc]
        for hp in range(N_HEADS // 2):
            pair = jnp.where(lane < HEAD_DIM, outs[2 * hp], pltpu.roll(outs[2 * hp + 1], HEAD_DIM, 1))
            o_ref[0, :, hp * 128:(hp + 1) * 128] = pair.astype(o_ref.dtype)

    @pl.when(i < n_ctx_tiles)
    def _():
        acc = []
        for j in range(N_HEADS):
            s = _nt(q_ref[0, j], k_ref[0, j, 0:tc, :])
            p = jnp.exp2(s - jnp.max(s, axis=-1, keepdims=True))
            acc.append(_nn(p.astype(BF16), v_ref[0, j, 0:tc, :]))
        write_out(acc)

    @pl.when(i >= n_ctx_tiles)
    def _():
        bounds = [0, tc] + list(range(tc + key_chunk, ta + 1, key_chunk))
        units = [(c, j) for c in range(len(bounds) - 1) for j in range(N_HEADS)]
        qs = [q_ref[0, j] for j in range(N_HEADS)]
        score = lambda c, j: _nt(qs[j], k_ref[0, j, bounds[c]:bounds[c + 1], :])
        m = [None] * N_HEADS
        acc = [None] * N_HEADS

        def weighted_values(pend):
            c, j, pb, alpha = pend
            pv = _nn(pb, v_ref[0, j, bounds[c]:bounds[c + 1], :])
            acc[j] = pv if alpha is None else alpha * acc[j] + pv

        pending = None
        s_next = score(*units[0])
        for idx, (c, j) in enumerate(units):
            s = s_next
            if idx + 1 < len(units):
                s_next = score(*units[idx + 1])
            mc = jnp.max(s, axis=-1, keepdims=True)
            if c == 0:
                alpha = None
                m[j] = mc
                p = jnp.exp2(s - mc)
            else:
                m_new = jnp.maximum(m[j], mc)
                alpha = jnp.exp2(m[j] - m_new)
                p = jnp.exp2(s - m_new)
                m[j] = m_new
            if pending is not None:
                weighted_values(pending)
            pending = (c, j, p.astype(BF16), alpha)
        weighted_values(pending)
        write_out(acc)


def _attn_call(q, k, v, *, tc):
    B, H, Ta, _ = q.shape
    tq = _pick_tile(math.gcd(tc, Ta), 256)
    kern = functools.partial(_attn_kernel, tc=tc, ta=Ta, n_ctx_tiles=tc // tq,
                             key_chunk=_pick_tile(Ta - tc, 1024, 128))
    return pl.pallas_call(
        kern,
        out_shape=jax.ShapeDtypeStruct((B, Ta, GROUP_W), BF16),
        grid=(B, Ta // tq),
        in_specs=[pl.BlockSpec((1, H, tq, 128), lambda b, i: (b, 0, i, 0)),
                  pl.BlockSpec((1, H, Ta, 128), lambda b, i: (b, 0, 0, 0)),
                  pl.BlockSpec((1, H, Ta, 128), lambda b, i: (b, 0, 0, 0))],
        out_specs=pl.BlockSpec((1, tq, GROUP_W), lambda b, i: (b, i, 0)),
        compiler_params=_cparams("parallel", "arbitrary"),
        name="mla_attn",
    )(q, k, v)


def _chunk_masks(d):
    L = CHUNK
    row = _iota((L, GROUP_W), 0)
    s_idx = _iota((L, GROUP_W), 1) % L
    if d == 0:
        return s_idx < row, s_idx <= row, s_idx == row
    return s_idx > row, s_idx >= row, s_idx == row


def _run_stages(*generators):
    live = list(generators)
    while live:
        for g in list(live):
            try:
                next(g)
            except StopIteration:
                live.remove(g)


def _rwkv_stages(rf_ref, rb_ref, lf_ref, lb_ref, wup_ref, aup_ref, gup_ref, w0_ref, a0_ref,
                 kk_ref, ka_ref, rk_ref, yf_ref, yb_ref, g_ref, bon_ref, s_scr, tri_scr, bdb_scr, *, cps):
    i = pl.program_id(1)
    L = CHUNK
    W = GROUP_W

    @pl.when(i == 0)
    def _():
        s_scr[...] = jnp.zeros_like(s_scr)
        _group_tri_masks(tri_scr, cps * L)
        bdb_scr[...] = _block_diag_mask().astype(BF16)

    hms = _head_masks()
    bd = _block_diag_mask()
    bdb = bdb_scr[...]
    rs = lambda x: _row_stack(x, hms)
    bd_b = bd > 0.5
    bdiag = lambda x: jnp.where(bd_b, jnp.concatenate([x.astype(BF16)] * N_HEADS, axis=0), 0.0)

    per_dir = []
    for d, (r_ref, l_ref) in enumerate(((rf_ref, lf_ref), (rb_ref, lb_ref))):
        rkv = r_ref[0]
        lora = l_ref[0]
        r = rkv[:, 0:W]
        k = rkv[:, W:2 * W]
        v = rkv[:, 2 * W:3 * W]
        strict, incl, eye = _chunk_masks(d)
        tri = tri_scr[d]

        lw = -math.exp(-0.5) * _sigmoid(w0_ref[d] + _mm1(_nn, jnp.tanh(lora), wup_ref[d]))
        a = _sigmoid(a0_ref[d] + _mm1(_nn, lora, aup_ref[d]))
        kkv = k * kk_ref[...]
        kkn = kkv * lax.rsqrt(jnp.maximum(_mm1(_nn, kkv * kkv, bdb), 1e-24))
        keff = k * (1.0 + (a - 1.0) * ka_ref[...])
        kka = kkn * a
        lw_hi, lw_lo = _split2(lw)
        cum2 = _nn(tri, jnp.concatenate([lw_hi, lw_lo], axis=1))
        cum = cum2[:, 0:W] + cum2[:, W:2 * W]
        e_dn = jnp.exp(-cum)
        per_dir.append(dict(v=v, lw=lw, cum=cum, kka=kka, keff=keff, strict=strict, incl=incl, eye=eye,
                            al=-kkn * jnp.exp(cum - lw), rt=r * jnp.exp(cum), bh=kka * e_dn, kh=keff * e_dn))
        if d == 0:
            g_ref[0] = _mm1(_nn, _sigmoid(lora), gup_ref[...])
            bon_ref[0] = _mm1(_nn, r * k * rk_ref[...], bdb) * v
        yield

    chains = []
    for c in range(cps):
        for d in range(2):
            pd = per_dir[d]
            sl = slice(c * L, (c + 1) * L)
            tot = jnp.sum(pd["lw"][sl], axis=0, keepdims=True)
            e_tc = jnp.exp(tot - pd["cum"][sl])
            ch = dict(d=d, c=c, al=pd["al"][sl], rt=pd["rt"][sl], v=pd["v"][sl], tot=tot,
                      bt=pd["kka"][sl] * e_tc, kt=pd["keff"][sl] * e_tc)
            ch["rsv"] = rs(ch["v"])
            a_all = _mm1(_nt, jnp.concatenate([ch["al"], ch["rt"]], axis=0),
                         jnp.concatenate([rs(pd["bh"][sl]), rs(pd["kh"][sl])], axis=0))
            ch["a_ab"] = jnp.where(pd["strict"], a_all[0:L, 0:W], 0.0)
            ch["a_ak"] = jnp.where(pd["strict"], a_all[0:L, W:2 * W], 0.0)
            ch["a_rb"] = jnp.where(pd["incl"], a_all[L:2 * L, 0:W], 0.0)
            ch["a_rk"] = jnp.where(pd["incl"], a_all[L:2 * L, W:2 * W], 0.0)
            ch["p"] = jnp.where(pd["eye"], 1.0, 0.0) + ch["a_ab"]
            chains.append(ch)
        yield

    for ch in chains:
        ch["sq"] = _mmn(_nn, ch["a_ab"], bdiag(ch["a_ab"]))
        ch["zk"] = _mm1(_nn, ch["a_ak"], ch["rsv"])
        ch["y0k"] = _mm1(_nn, ch["a_rk"], ch["rsv"])
    yield
    n_sq = int(math.log2(L)) - 1
    for it in range(n_sq):
        for ch in chains:
            if it < n_sq - 1:
                ps = _mmn(_nn, jnp.concatenate([ch["p"], ch["sq"]], axis=0), bdiag(ch["sq"]))
                ch["p"] = ch["p"] + ps[0:L]
                ch["sq"] = ps[L:2 * L]
            else:
                ch["p"] = ch["p"] + _mmn(_nn, ch["p"], bdiag(ch["sq"]))
        yield
    for ch in chains:
        pu = _mm1(_nn, ch["p"], jnp.concatenate([rs(ch["al"]), rs(ch["zk"])], axis=1))
        ch["w"], ch["uk"] = pu[:, 0:W], pu[:, W:2 * W]
    yield
    for ch in chains:
        gy = _mm1(_nn, ch["a_rb"], jnp.concatenate([rs(ch["w"]), rs(ch["uk"])], axis=1))
        ch["g"] = ch["rt"] + gy[:, 0:W]
        ch["y0"] = gy[:, W:2 * W] + ch["y0k"]
        ch["decay"] = jnp.exp(ch["tot"])
        ch["pm"] = bd * _mm1(_tn, ch["w"], ch["bt"])
        ch["q0"] = bd * _mm1(_tn, jnp.concatenate([ch["uk"], ch["v"]], axis=0),
                             jnp.concatenate([ch["bt"], ch["kt"]], axis=0))
    yield

    by_key = {(ch["d"], ch["c"]): ch for ch in chains}
    st = [s_scr[0], s_scr[1]]
    ys = [[None] * cps, [None] * cps]
    for step in range(cps):
        for d in range(2):
            c = step if d == 0 else cps - 1 - step
            ch = by_key[(d, c)]
            ys[d][c] = _mm1(_nt, ch["g"], st[d]) + ch["y0"]
            st[d] = st[d] * ch["decay"] + (_mm1(_nn, st[d], ch["pm"]) + ch["q0"])
        yield
    for d, y_ref in enumerate((yf_ref, yb_ref)):
        s_scr[d] = st[d]
        y_ref[0] = jnp.concatenate(ys[d], axis=0)


def _chunks_per_step(n_ctx, n_all):
    for cps in (4, 2, 1):
        if n_ctx % cps == 0 and (n_all - n_ctx) % cps == 0:
            return cps


def _mlstm_stages(qf_ref, qb_ref, vf_ref, vb_ref, gf_ref, gb_ref, gbias_ref,
                  yf_ref, yb_ref, c_scr, n_scr, m_scr, tri_scr, exp_scr, bdb_scr, *, cps):
    i = pl.program_id(1)
    L = CHUNK
    W = GROUP_W

    @pl.when(i == 0)
    def _():
        c_scr[...] = jnp.zeros_like(c_scr)
        n_scr[...] = jnp.zeros_like(n_scr)
        m_scr[...] = jnp.zeros_like(m_scr)
        _group_tri_masks(tri_scr, cps * L)
        bdb_scr[...] = _block_diag_mask().astype(BF16)
        ci = _iota((W, W), 0)
        cj = _iota((W, W), 1) // HEAD_DIM
        for d in range(2):
            col = GATE_LANE0 + d * 2 * N_HEADS
            exp_scr[d, :, 0:W] = (ci == col + cj).astype(BF16)
            exp_scr[d, :, W:2 * W] = (ci == col + N_HEADS + cj).astype(BF16)

    hms = _head_masks()
    bd = _block_diag_mask()
    bdb = bdb_scr[...]
    rs = lambda x: _row_stack(x, hms)
    neg_inf = -jnp.inf

    def sum2(x, eb):
        hi, lo = _split2(x)
        return _nn(hi, eb) + _nn(lo, eb)

    per_dir = []
    for d, (q_ref, v_ref, g_ref) in enumerate(((qf_ref, vf_ref, gf_ref), (qb_ref, vb_ref, gb_ref))):
        qk = q_ref[0]
        g = g_ref[0] + gbias_ref[...]
        strict, incl, eye = _chunk_masks(d)
        tri = tri_scr[d]
        gates = sum2(g, exp_scr[d])
        li = gates[:, 0:W]
        lf = -_softplus(-gates[:, W:2 * W])
        lf_hi, lf_lo = _split2(lf)
        b2 = _nn(tri, jnp.concatenate([lf_hi, lf_lo], axis=1))
        per_dir.append(dict(q=qk[:, 0:W], k=qk[:, W:2 * W] * (HEAD_DIM ** -0.5), v=v_ref[0][:, 0:W],
                            li=li, bcol=b2[:, 0:W] + b2[:, W:2 * W], incl=incl, eye=eye))
        yield

    chains = []
    for c in range(cps):
        for d in range(2):
            pd = per_dir[d]
            sl = slice(c * L, (c + 1) * L)
            q, k, v, li, bcol = pd["q"][sl], pd["k"][sl], pd["v"][sl], pd["li"][sl], pd["bcol"][sl]
            brow = jnp.sum(jnp.where(pd["eye"], bcol, 0.0), axis=0, keepdims=True)
            lirow = jnp.sum(jnp.where(pd["eye"], li, 0.0), axis=0, keepdims=True)
            logd = jnp.where(pd["incl"], bcol - brow + lirow, neg_inf)
            mx = jnp.zeros((L, W), F32)
            for hm in hms:
                mh = jnp.max(jnp.where(hm, logd, neg_inf), axis=1, keepdims=True)
                mx = jnp.where(hm, mh, mx)
            blast = bcol[L - 1:L, :] if d == 0 else bcol[0:1, :]
            lwc = blast - bcol + li
            mlw = jnp.max(lwc, axis=0, keepdims=True)
            kw = k * jnp.exp(lwc - mlw)
            chains.append(dict(d=d, c=c, q=q, v=v, bcol=bcol, mx=mx, blast=blast, mlw=mlw, kw=kw,
                               dexp=jnp.exp(logd - mx), rsk=rs(k), rsv=rs(v),
                               nu0=jnp.sum(kw, axis=0, keepdims=True)))
        yield
    for ch in chains:
        ch["sp"] = _mm1(_nt, ch["q"], ch["rsk"]) * ch["dexp"]
        ch["cu0"] = bd * _mm1(_tn, ch["v"], ch["kw"])
    yield
    for ch in chains:
        ch["num0"] = _mm1(_nn, ch["sp"], ch["rsv"])
        ch["den0"] = _mm1(_nn, ch["sp"], bdb)
    yield

    by_key = {(ch["d"], ch["c"]): ch for ch in chains}
    c_st = [c_scr[0], c_scr[1]]
    n_row = [n_scr[0, 0:1, :], n_scr[1, 0:1, :]]
    m_row = [m_scr[0, 0:1, :], m_scr[1, 0:1, :]]
    ys = [[None] * cps, [None] * cps]
    for step in range(cps):
        for d in range(2):
            c = step if d == 0 else cps - 1 - step
            ch = by_key[(d, c)]
            inter = ch["bcol"] + m_row[d]
            mt = jnp.maximum(inter, ch["mx"])
            f_in = jnp.exp(ch["mx"] - mt)
            w_int = jnp.exp(inter - mt)
            num = f_in * ch["num0"] + w_int * _mm1(_nt, ch["q"], c_st[d])
            den = f_in * ch["den0"] + w_int * _mm1(_nn, ch["q"] * n_row[d], bdb)
            ys[d][c] = num / jnp.maximum(jnp.abs(den), jnp.exp(-mt))
            m_new = jnp.maximum(ch["blast"] + m_row[d], ch["mlw"])
            sc = jnp.exp(ch["blast"] + m_row[d] - m_new)
            e2 = jnp.exp(ch["mlw"] - m_new)
            c_st[d] = sc * c_st[d] + e2 * ch["cu0"]
            n_row[d] = sc * n_row[d] + e2 * ch["nu0"]
            m_row[d] = m_new
        yield
    for d, y_ref in enumerate((yf_ref, yb_ref)):
        c_scr[d] = c_st[d]
        n_scr[d, 0:1, :] = n_row[d]
        m_scr[d, 0:1, :] = m_row[d]
        y_ref[0] = jnp.concatenate(ys[d], axis=0)


def _recur_kernel(*refs, cps):
    r_in, m_in = refs[0:12], refs[12:19]
    r_out, m_out = refs[19:23], refs[23:25]
    r_scr, m_scr = refs[25:28], refs[28:34]
    _run_stages(_rwkv_stages(*r_in, *r_out, *r_scr, cps=cps),
                _mlstm_stages(*m_in, *m_out, *m_scr, cps=cps))


def _recur_call(z, wup, aup, gup, w0, a0, kk, ka, rk, gbias, *, n_ctx, n_all):
    B, Ta, _ = z.shape
    W = GROUP_W
    cps = _chunks_per_step(n_ctx, n_all)
    rows = cps * CHUNK
    gc, ga = n_ctx // cps, n_all // cps
    full = lambda a: pl.BlockSpec(a.shape, lambda b, i: (0,) * a.ndim)
    fwd = lambda blk: (lambda b, i: (b, i, blk))
    bwd = lambda blk: (lambda b, i: (b, _bwd_chunk(i, gc, ga), blk))
    tok = lambda width, imap: pl.BlockSpec((1, rows, width), imap)
    out = jax.ShapeDtypeStruct((B, Ta, W), F32)
    return pl.pallas_call(
        functools.partial(_recur_kernel, cps=cps),
        out_shape=(out,) * 6,
        grid=(B, ga),
        in_specs=[tok(3 * W, fwd(Z_RKV // 3)), tok(3 * W, bwd(Z_RKV // 3)),
                  tok(W, fwd(Z_LORA)), tok(W, bwd(Z_LORA)),
                  full(wup), full(aup), full(gup), full(w0), full(a0), full(kk), full(ka), full(rk),
                  tok(2 * W, fwd(Z_MQK // 2)), tok(2 * W, bwd(Z_MQK // 2)),
                  tok(2 * W, fwd(Z_MVO // 2)), tok(2 * W, bwd(Z_MVO // 2)),
                  tok(W, fwd(Z_GATE)), tok(W, bwd(Z_GATE)), full(gbias)],
        out_specs=(tok(W, fwd(0)), tok(W, bwd(0)), tok(W, fwd(0)), tok(W, fwd(0)),
                   tok(W, fwd(0)), tok(W, bwd(0))),
        scratch_shapes=[pltpu.VMEM((2, W, W), F32), pltpu.VMEM((2, rows, rows), BF16), pltpu.VMEM((W, W), BF16),
                        pltpu.VMEM((2, W, W), F32), pltpu.VMEM((2, 8, W), F32), pltpu.VMEM((2, 8, W), F32),
                        pltpu.VMEM((2, rows, rows), BF16), pltpu.VMEM((2, W, 2 * W), BF16),
                        pltpu.VMEM((W, W), BF16)],
        compiler_params=_cparams("parallel", "arbitrary"),
        name="rwkv_mlstm",
    )(z, z, z, z, wup, aup, gup, w0, a0, kk, ka, rk, z, z, z, z, z, z, gbias)


def _head_norm(y, bd, eps):
    bdb = bd.astype(BF16)

    def head_mean(x):
        hi, lo = _split2(x)
        return (_nn(hi, bdb) + _nn(lo, bdb)) * (1.0 / HEAD_DIM)

    yc = y - head_mean(y)
    return yc * lax.rsqrt(head_mean(yc * yc) + eps)


def _outproj_kernel(x_ref, mods_ref, s5f_ref, s5b_ref, u_ref, at_ref, rf_ref, rb_ref, rg_ref, rbon_ref,
                    mf_ref, mb_ref, vo_ref, s5d_ref, wglu_ref, bglu_ref, lnw_ref, lnb_ref, mnw_ref,
                    gpost_ref, wout_ref, o_ref, *, tm, tc, n_batch):
    b = pl.program_id(0)
    t = pl.program_id(1)
    d = x_ref.shape[-1]
    W = GROUP_W
    is_ctx = (t * tm + _iota((tm, 1), 0)) < tc
    gate = jnp.where(is_ctx, mods_ref[pl.ds(n_batch, 1), pl.ds(5 * d, d)],
                     mods_ref[pl.ds(b, 1), pl.ds(5 * d, d)])
    bd = _block_diag_mask()

    y = s5f_ref[...] + s5b_ref[...] + s5d_ref[...] * u_ref[0]
    zg = 0.5 * y * (1.0 + jnp.tanh(math.sqrt(2.0 / math.pi) * (y + 0.044715 * (y * y * y))))
    s5o = zg * _sigmoid(_mm1(_nn, zg, wglu_ref[...]) + bglu_ref[...])

    yr = _head_norm(rf_ref[0] + rb_ref[0], bd, RWKV_GN_EPS)
    rwo = (yr * lnw_ref[...] + lnb_ref[...] + rbon_ref[0]) * rg_ref[0]

    ym = _head_norm(mf_ref[0] + mb_ref[0], bd, NORM_EPS)
    mlo = ym * mnw_ref[...] * _sigmoid(vo_ref[0])

    cat = jnp.concatenate([s5o.astype(BF16), at_ref[0], rwo.astype(BF16), mlo.astype(BF16)], axis=1)
    yx = _nn(cat, wout_ref[...])
    o_ref[0] = x_ref[0] + gate * _rms(yx, gpost_ref[...])


def _outproj_call(xa, mods, s5f, s5b, z, attn, rf, rb, rg, rbon, mf, mb,
                  s5d, wglu, bglu, lnw, lnb, mnw, gpost, wout, *, tc):
    B, Ta, D = xa.shape
    W = GROUP_W
    tm = _pick_tile(Ta, 544)
    kern = functools.partial(_outproj_kernel, tm=tm, tc=tc, n_batch=B)
    full = lambda a: pl.BlockSpec(a.shape, lambda b, t: (0,) * a.ndim)
    tok = pl.BlockSpec((1, tm, W), lambda b, t: (b, t, 0))
    tmaj = pl.BlockSpec((tm, W), lambda b, t: (t, b))
    return pl.pallas_call(
        kern,
        out_shape=jax.ShapeDtypeStruct((B, Ta, D), F32),
        grid=(B, Ta // tm),
        in_specs=[pl.BlockSpec((1, tm, D), lambda b, t: (b, t, 0)), full(mods),
                  tmaj, tmaj, pl.BlockSpec((1, tm, W), lambda b, t: (b, t, Z_S5)),
                  tok, tok, tok, tok, tok, tok, tok,
                  pl.BlockSpec((1, tm, W), lambda b, t: (b, t, Z_MVO + 1)),
                  full(s5d), full(wglu), full(bglu), full(lnw), full(lnb), full(mnw), full(gpost), full(wout)],
        out_specs=pl.BlockSpec((1, tm, D), lambda b, t: (b, t, 0)),
        compiler_params=_cparams("parallel", "parallel"),
        name="mix_out",
    )(xa, mods, s5f, s5b, z, attn, rf, rb, rg, rbon, mf, mb, z, s5d, wglu, bglu, lnw, lnb, mnw, gpost, wout)


def _rope_rotate_cols(w):
    h = ROPE_AXIS // 2
    return jnp.concatenate([-w[..., h:2 * h], w[..., 0:h], -w[..., 3 * h:4 * h], w[..., 2 * h:3 * h]], axis=-1)


def _inproj_relayout(w_in):
    L, D, _ = w_in.shape
    o_s5, o_mla, o_rw, o_ml = 0, 256, 672, 1568
    seg = lambda a, n: w_in[:, :, a:a + n]
    zer = lambda n: jnp.zeros((L, D, n), w_in.dtype)
    k_rope = seg(o_mla + 384, ROPE_DIM)
    parts = [seg(o_mla, 256), seg(o_mla + 256, 128),
             zer(64), k_rope, zer(32),
             zer(64), _rope_rotate_cols(k_rope), zer(32), seg(o_ml + 1024, 16), zer(128 - 16),
             seg(o_rw, 768), seg(o_s5, 256),
             seg(o_rw + 768, 128), zer(128),
             seg(o_ml, 512), seg(o_ml + 512, 512)]
    out = jnp.concatenate(parts, axis=2)
    assert out.shape[2] == Z_COLS
    return out


def _rope_tables(T, tc):
    rows = T // GRID_W
    r_idx, c_idx = jnp.meshgrid(jnp.arange(rows), jnp.arange(GRID_W), indexing='ij')
    inv_freq = 1.0 / (ROPE_BASE ** (jnp.arange(0, ROPE_AXIS, 2, dtype=F32) / ROPE_AXIS))
    ang_r = r_idx.reshape(-1, 1).astype(F32) * inv_freq
    ang_c = c_idx.reshape(-1, 1).astype(F32) * inv_freq
    ang = jnp.concatenate([ang_r, ang_r, ang_c, ang_c], axis=-1)
    cos = jnp.concatenate([jnp.ones((tc, ROPE_DIM), F32), jnp.cos(ang)], axis=0)
    sin = jnp.concatenate([jnp.zeros((tc, ROPE_DIM), F32), jnp.sin(ang)], axis=0)
    ta = T + tc
    cos_t = jnp.concatenate([jnp.ones((ta, 64), F32), cos, jnp.zeros((ta, 32), F32)], axis=1)
    sin_t = jnp.concatenate([jnp.zeros((ta, 64), F32), sin, jnp.zeros((ta, 32), F32)], axis=1)
    return cos_t, sin_t


def _pad_rows(w, r0, total):
    pad = [(0, 0)] * (w.ndim - 2) + [(r0, total - r0 - w.shape[-2]), (0, 0)]
    return jnp.pad(w, pad)


def kernel(x, c, ctx, c_ctx, w_ada, b_ada, norm_pre, norm_post, ffn_w_gate, ffn_w_up, ffn_w_down, w_in, w_out, s5_lam_re, s5_lam_im, s5_log_dt, s5_b_re, s5_b_im, s5_c_re, s5_c_im, s5_d, s5_w_glu, s5_b_glu, mla_q_norm, mla_kv_norm, mla_w_uq, mla_w_ukv, rwkv_conv_w, rwkv_conv_b, rwkv_w0, rwkv_w_up, rwkv_a0, rwkv_a_up, rwkv_g_up, rwkv_k_k, rwkv_k_a, rwkv_r_k, rwkv_ln_w, rwkv_ln_b, mlstm_conv_w, mlstm_conv_b, mlstm_gate_b, mlstm_norm):
    B, T, D = x.shape
    Tc = ctx.shape[1]
    Ta = T + Tc
    L = w_ada.shape[0]
    W = GROUP_W
    assert T % CHUNK == 0 and Tc % CHUNK == 0 and B % 8 == 0 and B <= 8
    n_ctx, n_all = Tc // CHUNK, Ta // CHUNK

    rows = 16
    cvec = jnp.concatenate([c, c_ctx[None, :], jnp.zeros((rows - B - 1, D), F32)], axis=0)
    mods_all = _ada_call(cvec, w_ada, b_ada)

    w_in_re = _inproj_relayout(w_in).astype(BF16)
    cw = jnp.zeros((L, Z_NBLK, 8, ZB), F32)
    rc = jnp.concatenate([rwkv_conv_w, rwkv_conv_b[:, None, :]], axis=1).reshape(L, 4, 3, ZB).transpose(0, 2, 1, 3)
    mc = jnp.concatenate([mlstm_conv_w, mlstm_conv_b[:, None, :]], axis=1).reshape(L, 4, 2, ZB).transpose(0, 2, 1, 3)
    cw = cw.at[:, Z_RKV:Z_RKV + 3, 0:4].set(rc).at[:, Z_MQK:Z_MQK + 2, 0:4].set(mc)

    wg = ffn_w_gate.astype(BF16)
    wu = ffn_w_up.astype(BF16)
    wd = ffn_w_down.astype(BF16)
    wout = w_out.astype(BF16)

    G = s5_lam_re.shape[2]
    N = s5_lam_re.shape[3]
    eye_g = jnp.eye(G, dtype=F32)
    lre = s5_lam_re.reshape(L, 2, 1, G * N)
    lim = s5_lam_im.reshape(L, 2, 1, G * N)
    ldt = jnp.repeat(s5_log_dt, N, axis=-1).reshape(L, 2, 1, G * N)
    wbre = jnp.einsum('ldgnp,gh->ldgphn', s5_b_re, eye_g).reshape(L, 2, G * S5_P, G * N)
    wbim = jnp.einsum('ldgnp,gh->ldgphn', s5_b_im, eye_g).reshape(L, 2, G * S5_P, G * N)
    wcre = jnp.einsum('ldgpn,gh->ldgnhp', s5_c_re, eye_g).reshape(L, 2, G * N, G * S5_P).astype(BF16)
    wcim = jnp.einsum('ldgpn,gh->ldgnhp', s5_c_im, eye_g).reshape(L, 2, G * N, G * S5_P).astype(BF16)

    nope = HEAD_DIM
    qd = nope + ROPE_DIM
    wq4 = mla_w_uq.reshape(L, -1, N_HEADS, qd)
    wq = jnp.pad(wq4, ((0, 0), (0, 0), (0, 0), (0, 128 - qd))).reshape(L, -1, N_HEADS * 128).astype(BF16)
    wq_rot = _rope_rotate_cols(wq4[..., nope:])
    wqr = jnp.pad(wq_rot, ((0, 0), (0, 0), (0, 0), (nope, 128 - qd))).reshape(L, -1, N_HEADS * 128).astype(BF16)
    wkv4 = mla_w_ukv.reshape(L, -1, N_HEADS, 2 * HEAD_DIM)
    wk = jnp.pad(wkv4[..., :HEAD_DIM], ((0, 0), (0, 0), (0, 0), (0, 64))).reshape(L, -1, N_HEADS * 128).astype(BF16)
    wv = jnp.pad(wkv4[..., HEAD_DIM:], ((0, 0), (0, 0), (0, 0), (0, 64))).reshape(L, -1, N_HEADS * 128).astype(BF16)
    cos_t, sin_t = _rope_tables(T, Tc)
    scale = float(qd) ** -0.5 * math.log2(math.e)

    wup = _pad_rows(rwkv_w_up, 0, W)
    aup = _pad_rows(rwkv_a_up, 32, W)
    gup = _pad_rows(rwkv_g_up, 64, W)
    gbias = jnp.pad(mlstm_gate_b, ((0, 0), (GATE_LANE0, W - GATE_LANE0 - mlstm_gate_b.shape[1])))

    xa = jnp.concatenate([ctx, x], axis=1)
    r1 = lambda a: a.reshape(1, -1)

    for l in range(L):
        mods = mods_all[l]
        xa2, hmix = _ffn_call(xa.reshape(B * Ta, D), mods, r1(norm_pre[l, 0]), r1(norm_post[l, 0]),
                              r1(norm_pre[l, 1]), wg[l, 0], wu[l, 0], wd[l, 0],
                              ta=Ta, tc=Tc, n_batch=B, koff=0, emit_hmix=True)
        xa = xa2.reshape(B, Ta, D)
        z = _inproj_call(hmix.reshape(B, Ta, D), w_in_re[l], cw[l], tc=Tc)

        s5f, s5b = _s5_call(z, lre[l], lim[l], ldt[l], wbre[l], wbim[l], wcre[l], wcim[l],
                            n_ctx=n_ctx, n_all=n_all)

        q, k, v = _mla_proj_call(z, r1(mla_q_norm[l]), r1(mla_kv_norm[l]), wq[l], wqr[l], wk[l], wv[l],
                                 cos_t, sin_t, scale=scale)
        attn = _attn_call(q, k, v, tc=Tc)

        rf, rb, rg, rbon, mf, mb = _recur_call(
            z, wup[l], aup[l], gup[l], rwkv_w0[l][:, None, :], rwkv_a0[l][:, None, :],
            r1(rwkv_k_k[l]), r1(rwkv_k_a[l]), r1(rwkv_r_k[l]), gbias[l:l + 1], n_ctx=n_ctx, n_all=n_all)

        xa = _outproj_call(xa, mods, s5f, s5b, z, attn, rf, rb, rg, rbon,
                           mf, mb, r1(s5_d[l]), s5_w_glu[l].astype(BF16), r1(s5_b_glu[l]), r1(rwkv_ln_w[l]),
                           r1(rwkv_ln_b[l]), r1(mlstm_norm[l]), r1(norm_post[l, 1]), wout[l], tc=Tc)

        xa2, _ = _ffn_call(xa.reshape(B * Ta, D), mods, r1(norm_pre[l, 2]), r1(norm_post[l, 2]),
                           r1(norm_pre[l, 1]), wg[l, 1], wu[l, 1], wd[l, 1],
                           ta=Ta, tc=Tc, n_batch=B, koff=6, emit_hmix=False)
        xa = xa2.reshape(B, Ta, D)

    return xa[:, Tc:, :]
```

```python
import functools
import math

import jax
import jax.numpy as jnp
from jax import lax
from jax.experimental import pallas as pl
from jax.experimental.pallas import tpu as pltpu

F32 = jnp.float32
BF16 = jnp.bfloat16

GROUP_W = 256
HEAD_DIM = 64
N_HEADS = GROUP_W // HEAD_DIM
CHUNK = 64
S5_CHUNK = 128
NORM_EPS = 1e-6
RWKV_GN_EPS = HEAD_DIM * 1e-5
GRID_W = 64
ROPE_BASE = 10000.0
ROPE_DIM = 32
ROPE_AXIS = 16
S5_P = 16
MACARON = 0.5
VMEM_LIMIT_BYTES = 56 * 1024 * 1024

ZB = 256
Z_RKV, Z_S5, Z_LORA, Z_MQK, Z_MVO = 3, 6, 7, 8, 10
Z_GATE = 2
GATE_LANE0 = 128
Z_NBLK = 12
Z_COLS = Z_NBLK * ZB


def _nn(a, b):
    return lax.dot_general(a, b, (((1,), (0,)), ((), ())), preferred_element_type=F32)


def _nt(a, b):
    return lax.dot_general(a, b, (((1,), (1,)), ((), ())), preferred_element_type=F32)


def _tn(a, b):
    return lax.dot_general(a, b, (((0,), (0,)), ((), ())), preferred_element_type=F32)


def _split2(x):
    hi = x.astype(BF16)
    lo = (x - hi.astype(F32)).astype(BF16)
    return hi, lo


def _mm3(dotf, a, b):
    ah, al = _split2(a)
    bh, bl = _split2(b)
    return dotf(ah, bh) + (dotf(ah, bl) + dotf(al, bh))


def _mm1(dotf, a, b):
    return dotf(a.astype(BF16), b.astype(BF16))


_mmn = _mm1


def _rms(x, g):
    return x * lax.rsqrt(jnp.mean(x * x, axis=-1, keepdims=True) + NORM_EPS) * g


def _sigmoid(x):
    return 0.5 * jnp.tanh(0.5 * x) + 0.5


def _group_tri_masks(tri_scr, rows):
    r = _iota((rows, rows), 0)
    c = _iota((rows, rows), 1)
    same = (r // CHUNK) == (c // CHUNK)
    tri_scr[0] = (same & (c <= r)).astype(BF16)
    tri_scr[1] = (same & (c >= r)).astype(BF16)


def _softplus(x):
    return jnp.maximum(x, 0.0) + jnp.log(1.0 + jnp.exp(-jnp.abs(x)))


def _iota(shape, dim):
    return lax.broadcasted_iota(jnp.int32, shape, dim)


def _head_masks():
    lane = _iota((1, GROUP_W), 1)
    return [lane // HEAD_DIM == h for h in range(N_HEADS)]


def _block_diag_mask():
    r = _iota((GROUP_W, GROUP_W), 0) // HEAD_DIM
    c = _iota((GROUP_W, GROUP_W), 1) // HEAD_DIM
    return (r == c).astype(F32)


def _row_stack(x, hms):
    xb = x.astype(BF16)
    return jnp.concatenate([jnp.where(m, xb, 0.0) for m in hms], axis=0)


def _pick_tile(n, target, mult=16):
    best = None
    for t in range(mult, min(n, target) + 1, mult):
        if n % t == 0:
            best = t
    if best is None:
        raise ValueError(f"no tile for {n}")
    return best


def _cparams(*sem):
    return pltpu.CompilerParams(dimension_semantics=sem, vmem_limit_bytes=VMEM_LIMIT_BYTES)


def _ada_kernel(c_ref, w_ref, b_ref, o_ref):
    c = c_ref[...]
    s = c * _sigmoid(c)
    o_ref[0] = _mm3(_nn, s, w_ref[0]) + b_ref[0]


def _ada_call(cvec, w_ada, b_ada):
    L, D, N = w_ada.shape
    R = cvec.shape[0]
    tn = _pick_tile(N, 1152, 128)
    return pl.pallas_call(
        _ada_kernel,
        out_shape=jax.ShapeDtypeStruct((L, R, N), F32),
        grid=(L, N // tn),
        in_specs=[pl.BlockSpec((R, D), lambda l, j: (0, 0)),
                  pl.BlockSpec((1, D, tn), lambda l, j: (l, 0, j)),
                  pl.BlockSpec((1, 1, tn), lambda l, j: (l, 0, j))],
        out_specs=pl.BlockSpec((1, R, tn), lambda l, j: (l, 0, j)),
        compiler_params=_cparams("parallel", "parallel"),
        name="ada_mod",
    )(cvec, w_ada, b_ada.reshape(L, 1, N))


FFN_COLS = 256


def _ffn_kernel(xn_ref, xp_ref, mods_ref, gpre_ref, gpost_ref, gmix_ref, wg_ref, wu_ref, wd_ref,
                o_ref, hmix_ref, h_scr, acc_scr, *, tm, tiles_per_batch, tc, n_batch, koff, emit_hmix,
                n_tiles):
    i = pl.program_id(0)
    f = pl.program_id(1)
    d = xn_ref.shape[-1]
    fd = wg_ref.shape[1]

    def mod_of(tile):
        b = tile // tiles_per_batch
        t0 = (tile % tiles_per_batch) * tm
        is_ctx = (t0 + _iota((tm, 1), 0)) < tc

        def mod(fn):
            row = lambda r: (lambda k: mods_ref[pl.ds(r, 1), pl.ds(k * d, d)])
            return jnp.where(is_ctx, fn(row(n_batch)), fn(row(b)))
        return mod

    unit = lambda x: x * lax.rsqrt(jnp.mean(x * x, axis=-1, keepdims=True) + NORM_EPS)

    def pre_norm(tile, x):
        mod = mod_of(tile)
        gain = mod(lambda m: gpre_ref[...] * (1.0 + m(koff + 1)))
        return (unit(x) * gain + mod(lambda m: m(koff))).astype(BF16)

    def finish(tile, acc, x):
        mod = mod_of(tile)
        xn = x + unit(acc) * mod(lambda m: (MACARON * m(koff + 2)) * gpost_ref[...])
        o_ref[...] = xn
        if emit_hmix:
            gain = mod(lambda m: gmix_ref[...] * (1.0 + m(4)))
            hmix_ref[...] = (unit(xn) * gain + mod(lambda m: m(3))).astype(BF16)
        else:
            hmix_ref[...] = jnp.zeros_like(hmix_ref)

    def hidden_cols(hb, lo, hi):
        acc = None
        for c0 in range(lo, hi, FFN_COLS):
            g = _nn(hb, wg_ref[:, c0:c0 + FFN_COLS])
            u = _nn(hb, wu_ref[:, c0:c0 + FFN_COLS])
            t = _nn((g * _sigmoid(g) * u).astype(BF16), wd_ref[c0:c0 + FFN_COLS, :])
            acc = t if acc is None else acc + t
        return acc

    @pl.when(jnp.logical_and(i == 0, f == 0))
    def _():
        h_scr[0] = pre_norm(0, xp_ref[...])
        acc_scr[1] = jnp.zeros((tm, d), F32)

    for s in range(2):
        tile = 2 * i + s

        @pl.when(jnp.logical_and(tile < n_tiles, f == s))
        def _(s=s, tile=tile):
            finish(jnp.maximum(tile - 1, 0), acc_scr[1 - s], xp_ref[...])
            h_scr[1 - s] = pre_norm(jnp.minimum(tile + 1, n_tiles - 1), xn_ref[...])
            acc_scr[s] = hidden_cols(h_scr[s], 0, fd)

    @pl.when(jnp.logical_and(2 * i == n_tiles, f == 0))
    def _():
        finish(n_tiles - 1, acc_scr[1], xp_ref[...])


def _ffn_call(xa2, mods, gpre, gpost, gmix, wg, wu, wd, *, ta, tc, n_batch, koff, emit_hmix):
    M, D = xa2.shape
    Fd = wg.shape[1]
    assert Fd % FFN_COLS == 0
    tm = _pick_tile(ta, 544)
    n_tiles = M // tm
    assert n_tiles % 2 == 0
    kern = functools.partial(_ffn_kernel, tm=tm, tiles_per_batch=ta // tm, tc=tc, n_batch=n_batch,
                             koff=koff, emit_hmix=emit_hmix, n_tiles=n_tiles)
    hm_rows = tm if emit_hmix else 16
    last = n_tiles - 1
    resident = lambda a: pl.BlockSpec(a.shape, lambda i, f: (0, 0), pipeline_mode=pl.Buffered(1))
    vec = pl.BlockSpec((1, D), lambda i, f: (0, 0))
    return pl.pallas_call(
        kern,
        out_shape=(jax.ShapeDtypeStruct((M, D), F32),
                   jax.ShapeDtypeStruct((M if emit_hmix else 16 * n_tiles, D), BF16)),
        grid=(n_tiles // 2 + 1, 2),
        in_specs=[pl.BlockSpec((tm, D), lambda i, f: (jnp.minimum(2 * i + f + 1, last), 0)),
                  pl.BlockSpec((tm, D), lambda i, f: (jnp.clip(2 * i + f - 1, 0, last), 0)),
                  pl.BlockSpec(mods.shape, lambda i, f: (0, 0)),
                  vec, vec, vec, resident(wg), resident(wu), resident(wd)],
        out_specs=(pl.BlockSpec((tm, D), lambda i, f: (jnp.clip(2 * i + f - 1, 0, last), 0)),
                   pl.BlockSpec((hm_rows, D), lambda i, f: (jnp.clip(2 * i + f - 1, 0, last), 0))),
        scratch_shapes=[pltpu.VMEM((2, tm, D), BF16), pltpu.VMEM((2, tm, D), F32)],
        compiler_params=_cparams("arbitrary", "arbitrary"),
        name="half_ffn",
    )(xa2, xa2, mods, gpre, gpost, gmix, wg, wu, wd)


def _inproj_kernel(h_ref, w_ref, cw_ref, o_ref, *, tc, conv_lo, conv_hi, silu_lo, silu_hi):
    nb = pl.program_id(1)
    ta = h_ref.shape[1]
    is_conv = ((nb >= conv_lo[0]) & (nb < conv_hi[0])) | ((nb >= conv_lo[1]) & (nb < conv_hi[1]))
    is_silu = (nb >= silu_lo) & (nb < silu_hi)

    @pl.when(jnp.logical_not(is_conv))
    def _():
        o_ref[0] = _nn(h_ref[0], w_ref[...])

    def conv():
        z = _nn(h_ref[0], w_ref[...])
        row = _iota((ta, 1), 0)
        zp = jnp.where((row == 0) | (row == tc), 0.0, pltpu.roll(z, 1, 0))
        zn = jnp.where((row == tc - 1) | (row == ta - 1), 0.0, pltpu.roll(z, ta - 1, 0))
        cw = cw_ref[0]
        return cw[3:4] + zp * cw[0:1] + z * cw[1:2] + zn * cw[2:3]

    @pl.when(is_conv & jnp.logical_not(is_silu))
    def _():
        o_ref[0] = conv()

    @pl.when(is_conv & is_silu)
    def _():
        y = conv()
        o_ref[0] = y * _sigmoid(y)


def _inproj_call(hmix3, w_re, cw, *, tc):
    B, Ta, D = hmix3.shape
    kern = functools.partial(_inproj_kernel, tc=tc, conv_lo=(Z_RKV, Z_MQK), conv_hi=(Z_RKV + 3, Z_MQK + 2),
                             silu_lo=Z_MQK, silu_hi=Z_MQK + 2)
    return pl.pallas_call(
        kern,
        out_shape=jax.ShapeDtypeStruct((B, Ta, Z_COLS), F32),
        grid=(B, Z_NBLK),
        in_specs=[pl.BlockSpec((1, Ta, D), lambda b, n: (b, 0, 0)),
                  pl.BlockSpec((D, ZB), lambda b, n: (0, n)),
                  pl.BlockSpec((1, 8, ZB), lambda b, n: (n, 0, 0))],
        out_specs=pl.BlockSpec((1, Ta, ZB), lambda b, n: (b, 0, n)),
        compiler_params=_cparams("parallel", "arbitrary"),
        name="in_proj",
    )(hmix3, w_re, cw)


def _bwd_chunk(i, n_ctx, n_all):
    return jnp.where(i < n_ctx, n_ctx - 1 - i, n_all - 1 - (i - n_ctx))


def _s5_kernel(uf_ref, ub_ref, lre_ref, lim_ref, ldt_ref, wbre_ref, wbim_ref, wcre_ref, wcim_ref,
               yf_ref, yb_ref, wb_scr, coef_scr, st_scr, rel_scr, x_scr, *, lc, nb, scan_unroll):
    i = pl.program_id(0)
    gn = lre_ref.shape[-1]

    @pl.when(i == 0)
    def _():
        for d in range(2):
            dt = jnp.exp(ldt_ref[d])
            lre = lre_ref[d]
            lim = lim_ref[d]
            mag = jnp.exp(lre * dt)
            ar = mag * jnp.cos(lim * dt)
            ai = mag * jnp.sin(lim * dt)
            den = lre * lre + lim * lim
            fr = ((ar - 1.0) * lre + ai * lim) / den
            fi = (ai * lre - (ar - 1.0) * lim) / den
            coef_scr[d, 0:nb, :] = jnp.broadcast_to(ar, (nb, gn))
            coef_scr[d, nb:2 * nb, :] = jnp.broadcast_to(ai, (nb, gn))
            wre = wbre_ref[d]
            wim = wbim_ref[d]
            wb_scr[d, :, 0:gn] = (wre * fr - wim * fi).astype(BF16)
            wb_scr[d, :, gn:2 * gn] = (wim * fr + wre * fi).astype(BF16)
        st_scr[...] = jnp.zeros_like(st_scr)

    half = 128
    u_refs = (uf_ref, ub_ref)
    y_refs = (yf_ref, yb_ref)

    for d in range(2):
        for b in range(nb):
            for s in range(2):
                rel_scr[d, s, pl.ds(b, lc, stride=nb), :] = u_refs[d][b, :, s * half:(s + 1) * half]
    for d in range(2):
        u_tm = jnp.concatenate([rel_scr[d, 0], rel_scr[d, 1]], axis=1).astype(BF16)
        x_scr[d] = _nn(u_tm, wb_scr[d])

    for d in range(2):
        def body(t, carry, d=d):
            sr, si = carry
            ar = coef_scr[d, 0:nb, :]
            ai = coef_scr[d, nb:2 * nb, :]
            tt = t if d == 0 else lc - 1 - t
            r0 = pl.multiple_of(tt * nb, nb)
            xr = x_scr[d, pl.ds(r0, nb), 0:gn]
            xi = x_scr[d, pl.ds(r0, nb), gn:2 * gn]
            nsr = ar * sr - ai * si + xr
            nsi = ar * si + ai * sr + xi
            x_scr[d, pl.ds(r0, nb), 0:gn] = nsr
            x_scr[d, pl.ds(r0, nb), gn:2 * gn] = nsi
            return nsr, nsi

        sr, si = lax.fori_loop(0, lc, body, (st_scr[d, 0:nb, :], st_scr[d, nb:2 * nb, :]), unroll=scan_unroll)
        st_scr[d, 0:nb, :] = sr
        st_scr[d, nb:2 * nb, :] = si
        y = (_nn(x_scr[d, :, 0:gn].astype(BF16), wcre_ref[d])
             - _nn(x_scr[d, :, gn:2 * gn].astype(BF16), wcim_ref[d]))
        rel_scr[d, 0] = y[:, 0:half]
        rel_scr[d, 1] = y[:, half:2 * half]
    for d in range(2):
        for b in range(nb):
            for s in range(2):
                c0 = b * 2 * half + s * half
                y_refs[d][:, c0:c0 + half] = rel_scr[d, s, pl.ds(b, lc, stride=nb), :]


def _s5_call(z, lre, lim, ldt, wbre, wbim, wcre, wcim, *, n_ctx, n_all):
    B, Ta, _ = z.shape
    W = GROUP_W
    lc = S5_CHUNK
    assert (n_ctx * CHUNK) % lc == 0 and (n_all * CHUNK) % lc == 0
    n_ctx, n_all = n_ctx * CHUNK // lc, n_all * CHUNK // lc
    gn = lre.shape[-1]
    blk = lc * B
    kern = functools.partial(_s5_kernel, lc=lc, nb=B, scan_unroll=True)
    full = lambda a: pl.BlockSpec(a.shape, lambda i: (0,) * a.ndim)
    out = jax.ShapeDtypeStruct((Ta, B * W), F32)
    return pl.pallas_call(
        kern,
        out_shape=(out, out),
        grid=(n_all,),
        in_specs=[pl.BlockSpec((B, lc, W), lambda i: (0, i, Z_S5)),
                  pl.BlockSpec((B, lc, W), lambda i: (0, _bwd_chunk(i, n_ctx, n_all), Z_S5)),
                  full(lre), full(lim), full(ldt), full(wbre), full(wbim), full(wcre), full(wcim)],
        out_specs=(pl.BlockSpec((lc, B * W), lambda i: (i, 0)),
                   pl.BlockSpec((lc, B * W), lambda i: (_bwd_chunk(i, n_ctx, n_all), 0))),
        scratch_shapes=[pltpu.VMEM((2, W, 2 * gn), BF16),
                        pltpu.VMEM((2, 2 * B, gn), F32),
                        pltpu.VMEM((2, 2 * B, gn), F32),
                        pltpu.VMEM((2, 2, blk, 128), F32),
                        pltpu.VMEM((2, blk, 2 * gn), F32)],
        compiler_params=_cparams("arbitrary"),
        name="s5_scan",
    )(z, z, lre, lim, ldt, wbre, wbim, wcre, wcim)


def _mla_proj_kernel(z_ref, qn_ref, kvn_ref, wq_ref, wqr_ref, wk_ref, wv_ref, cos_ref, sin_ref,
                     q_ref, k_ref, v_ref, *, scale):
    z = z_ref[0]
    cq = z[:, 0:256]
    ckv = z[:, 256:384]
    kr = z[:, 384:512]
    krr = z[:, 512:640]
    cos = cos_ref[...]
    sin = sin_ref[...]
    cqb = _rms(cq, qn_ref[...]).astype(BF16)
    ckvb = _rms(ckv, kvn_ref[...]).astype(BF16)
    q = _nn(cqb, wq_ref[...])
    qr = _nn(cqb, wqr_ref[...])
    kn = _nn(ckvb, wk_ref[...])
    krp = kr * cos + krr * sin
    for h in range(N_HEADS):
        sl = slice(h * 128, (h + 1) * 128)
        q_ref[0, h] = ((q[:, sl] * cos + qr[:, sl] * sin) * scale).astype(BF16)
        k_ref[0, h] = (kn[:, sl] + krp).astype(BF16)
    ones_pad = ((_iota((1, N_HEADS * 128), 1) % 128) >= HEAD_DIM).astype(F32)
    vv = _nn(ckvb, wv_ref[...]) + ones_pad
    for h in range(N_HEADS):
        v_ref[0, h] = vv[:, h * 128:(h + 1) * 128].astype(BF16)


def _mla_proj_call(z, qn, kvn, wq, wqr, wk, wv, cos_t, sin_t, *, scale):
    B, Ta, _ = z.shape
    tm = _pick_tile(Ta, 1088)
    kern = functools.partial(_mla_proj_kernel, scale=scale)
    full = lambda a: pl.BlockSpec(a.shape, lambda b, t: (0,) * a.ndim)
    return pl.pallas_call(
        kern,
        out_shape=(jax.ShapeDtypeStruct((B, N_HEADS, Ta, 128), BF16),
                   jax.ShapeDtypeStruct((B, N_HEADS, Ta, 128), BF16),
                   jax.ShapeDtypeStruct((B, N_HEADS, Ta, 128), BF16)),
        grid=(B, Ta // tm),
        in_specs=[pl.BlockSpec((1, tm, 3 * ZB), lambda b, t: (b, t, 0)),
                  full(qn), full(kvn), full(wq), full(wqr), full(wk), full(wv),
                  pl.BlockSpec((tm, 128), lambda b, t: (t, 0)),
                  pl.BlockSpec((tm, 128), lambda b, t: (t, 0))],
        out_specs=(pl.BlockSpec((1, N_HEADS, tm, 128), lambda b, t: (b, 0, t, 0)),
                   pl.BlockSpec((1, N_HEADS, tm, 128), lambda b, t: (b, 0, t, 0)),
                   pl.BlockSpec((1, N_HEADS, tm, 128), lambda b, t: (b, 0, t, 0))),
        compiler_params=_cparams("parallel", "parallel"),
        name="mla_proj",
    )(z, qn, kvn, wq, wqr, wk, wv, cos_t, sin_t)


def _attn_kernel(q_ref, k_ref, v_ref, o_ref, *, tc, ta, n_ctx_tiles, key_chunk):
    i = pl.program_id(1)
    lane = _iota((1, 128), 1)

    def write_out(acc):
        outs = [a * (1.0 / pltpu.roll(a, HEAD_DIM, 1)) for a in acc]
        for hp in range(N_HEADS // 2):
            pair = jnp.where(lane < HEAD_DIM, outs[2 * hp], pltpu.roll(outs[2 * hp + 1], HEAD_DIM, 1))
            o_ref[0, :, hp * 128:(hp + 1) * 128] = pair.astype(o_ref.dtype)

    @pl.when(i < n_ctx_tiles)
    def _():
        acc = []
        for j in range(N_HEADS):
            s = _nt(q_ref[0, j], k_ref[0, j, 0:tc, :])
            p = jnp.exp2(s - jnp.max(s, axis=-1, keepdims=True))
            acc.append(_nn(p.astype(BF16), v_ref[0, j, 0:tc, :]))
        write_out(acc)

    @pl.when(i >= n_ctx_tiles)
    def _():
        bounds = [0, tc] + list(range(tc + key_chunk, ta + 1, key_chunk))
        units = [(c, j) for c in range(len(bounds) - 1) for j in range(N_HEADS)]
        qs = [q_ref[0, j] for j in range(N_HEADS)]
        score = lambda c, j: _nt(qs[j], k_ref[0, j, bounds[c]:bounds[c + 1], :])
        m = [None] * N_HEADS
        acc = [None] * N_HEADS

        def weighted_values(pend):
            c, j, pb, alpha = pend
            pv = _nn(pb, v_ref[0, j, bounds[c]:bounds[c + 1], :])
            acc[j] = pv if alpha is None else alpha * acc[j] + pv

        pending = None
        s_next = score(*units[0])
        for idx, (c, j) in enumerate(units):
            s = s_next
            if idx + 1 < len(units):
                s_next = score(*units[idx + 1])
            mc = jnp.max(s, axis=-1, keepdims=True)
            if c == 0:
                alpha = None
                m[j] = mc
                p = jnp.exp2(s - mc)
            else:
                m_new = jnp.maximum(m[j], mc)
                alpha = jnp.exp2(m[j] - m_new)
                p = jnp.exp2(s - m_new)
                m[j] = m_new
            if pending is not None:
                weighted_values(pending)
            pending = (c, j, p.astype(BF16), alpha)
        weighted_values(pending)
        write_out(acc)


def _attn_call(q, k, v, *, tc):
    B, H, Ta, _ = q.shape
    tq = _pick_tile(math.gcd(tc, Ta), 256)
    kern = functools.partial(_attn_kernel, tc=tc, ta=Ta, n_ctx_tiles=tc // tq,
                             key_chunk=_pick_tile(Ta - tc, 1024, 128))
    return pl.pallas_call(
        kern,
        out_shape=jax.ShapeDtypeStruct((B, Ta, GROUP_W), BF16),
        grid=(B, Ta // tq),
        in_specs=[pl.BlockSpec((1, H, tq, 128), lambda b, i: (b, 0, i, 0)),
                  pl.BlockSpec((1, H, Ta, 128), lambda b, i: (b, 0, 0, 0)),
                  pl.BlockSpec((1, H, Ta, 128), lambda b, i: (b, 0, 0, 0))],
        out_specs=pl.BlockSpec((1, tq, GROUP_W), lambda b, i: (b, i, 0)),
        compiler_params=_cparams("parallel", "arbitrary"),
        name="mla_attn",
    )(q, k, v)


def _chunk_masks(d):
    L = CHUNK
    row = _iota((L, GROUP_W), 0)
    s_idx = _iota((L, GROUP_W), 1) % L
    if d == 0:
        return s_idx < row, s_idx <= row, s_idx == row
    return s_idx > row, s_idx >= row, s_idx == row


def _run_stages(*generators):
    live = list(generators)
    while live:
        for g in list(live):
            try:
                next(g)
            except StopIteration:
                live.remove(g)


def _rwkv_stages(rf_ref, rb_ref, lf_ref, lb_ref, wup_ref, aup_ref, gup_ref, w0_ref, a0_ref,
                 kk_ref, ka_ref, rk_ref, yf_ref, yb_ref, g_ref, bon_ref, s_scr, tri_scr, bdb_scr, *, cps):
    i = pl.program_id(1)
    L = CHUNK
    W = GROUP_W

    @pl.when(i == 0)
    def _():
        s_scr[...] = jnp.zeros_like(s_scr)
        _group_tri_masks(tri_scr, cps * L)
        bdb_scr[...] = _block_diag_mask().astype(BF16)

    hms = _head_masks()
    bd = _block_diag_mask()
    bdb = bdb_scr[...]
    rs = lambda x: _row_stack(x, hms)
    bd_b = bd > 0.5
    bdiag = lambda x: jnp.where(bd_b, jnp.concatenate([x.astype(BF16)] * N_HEADS, axis=0), 0.0)

    per_dir = []
    for d, (r_ref, l_ref) in enumerate(((rf_ref, lf_ref), (rb_ref, lb_ref))):
        rkv = r_ref[0]
        lora = l_ref[0]
        r = rkv[:, 0:W]
        k = rkv[:, W:2 * W]
        v = rkv[:, 2 * W:3 * W]
        strict, incl, eye = _chunk_masks(d)
        tri = tri_scr[d]

        lw = -math.exp(-0.5) * _sigmoid(w0_ref[d] + _mm1(_nn, jnp.tanh(lora), wup_ref[d]))
        a = _sigmoid(a0_ref[d] + _mm1(_nn, lora, aup_ref[d]))
        kkv = k * kk_ref[...]
        kkn = kkv * lax.rsqrt(jnp.maximum(_mm1(_nn, kkv * kkv, bdb), 1e-24))
        keff = k * (1.0 + (a - 1.0) * ka_ref[...])
        kka = kkn * a
        lw_hi, lw_lo = _split2(lw)
        cum2 = _nn(tri, jnp.concatenate([lw_hi, lw_lo], axis=1))
        cum = cum2[:, 0:W] + cum2[:, W:2 * W]
        e_dn = jnp.exp(-cum)
        per_dir.append(dict(v=v, lw=lw, cum=cum, kka=kka, keff=keff, strict=strict, incl=incl, eye=eye,
                            al=-kkn * jnp.exp(cum - lw), rt=r * jnp.exp(cum), bh=kka * e_dn, kh=keff * e_dn))
        if d == 0:
            g_ref[0] = _mm1(_nn, _sigmoid(lora), gup_ref[...])
            bon_ref[0] = _mm1(_nn, r * k * rk_ref[...], bdb) * v
        yield

    chains = []
    for c in range(cps):
        for d in range(2):
            pd = per_dir[d]
            sl = slice(c * L, (c + 1) * L)
            tot = jnp.sum(pd["lw"][sl], axis=0, keepdims=True)
            e_tc = jnp.exp(tot - pd["cum"][sl])
            ch = dict(d=d, c=c, al=pd["al"][sl], rt=pd["rt"][sl], v=pd["v"][sl], tot=tot,
                      bt=pd["kka"][sl] * e_tc, kt=pd["keff"][sl] * e_tc)
            ch["rsv"] = rs(ch["v"])
            a_all = _mm1(_nt, jnp.concatenate([ch["al"], ch["rt"]], axis=0),
                         jnp.concatenate([rs(pd["bh"][sl]), rs(pd["kh"][sl])], axis=0))
            ch["a_ab"] = jnp.where(pd["strict"], a_all[0:L, 0:W], 0.0)
            ch["a_ak"] = jnp.where(pd["strict"], a_all[0:L, W:2 * W], 0.0)
            ch["a_rb"] = jnp.where(pd["incl"], a_all[L:2 * L, 0:W], 0.0)
            ch["a_rk"] = jnp.where(pd["incl"], a_all[L:2 * L, W:2 * W], 0.0)
            ch["p"] = jnp.where(pd["eye"], 1.0, 0.0) + ch["a_ab"]
            chains.append(ch)
        yield

    for ch in chains:
        ch["sq"] = _mmn(_nn, ch["a_ab"], bdiag(ch["a_ab"]))
        ch["zk"] = _mm1(_nn, ch["a_ak"], ch["rsv"])
        ch["y0k"] = _mm1(_nn, ch["a_rk"], ch["rsv"])
    yield
    n_sq = int(math.log2(L)) - 1
    for it in range(n_sq):
        for ch in chains:
            if it < n_sq - 1:
                ps = _mmn(_nn, jnp.concatenate([ch["p"], ch["sq"]], axis=0), bdiag(ch["sq"]))
                ch["p"] = ch["p"] + ps[0:L]
                ch["sq"] = ps[L:2 * L]
            else:
                ch["p"] = ch["p"] + _mmn(_nn, ch["p"], bdiag(ch["sq"]))
        yield
    for ch in chains:
        pu = _mm1(_nn, ch["p"], jnp.concatenate([rs(ch["al"]), rs(ch["zk"])], axis=1))
        ch["w"], ch["uk"] = pu[:, 0:W], pu[:, W:2 * W]
    yield
    for ch in chains:
        gy = _mm1(_nn, ch["a_rb"], jnp.concatenate([rs(ch["w"]), rs(ch["uk"])], axis=1))
        ch["g"] = ch["rt"] + gy[:, 0:W]
        ch["y0"] = gy[:, W:2 * W] + ch["y0k"]
        ch["decay"] = jnp.exp(ch["tot"])
        ch["pm"] = bd * _mm1(_tn, ch["w"], ch["bt"])
        ch["q0"] = bd * _mm1(_tn, jnp.concatenate([ch["uk"], ch["v"]], axis=0),
                             jnp.concatenate([ch["bt"], ch["kt"]], axis=0))
    yield

    by_key = {(ch["d"], ch["c"]): ch for ch in chains}
    st = [s_scr[0], s_scr[1]]
    ys = [[None] * cps, [None] * cps]
    for step in range(cps):
        for d in range(2):
            c = step if d == 0 else cps - 1 - step
            ch = by_key[(d, c)]
            ys[d][c] = _mm1(_nt, ch["g"], st[d]) + ch["y0"]
            st[d] = st[d] * ch["decay"] + (_mm1(_nn, st[d], ch["pm"]) + ch["q0"])
        yield
    for d, y_ref in enumerate((yf_ref, yb_ref)):
        s_scr[d] = st[d]
        y_ref[0] = jnp.concatenate(ys[d], axis=0)


def _chunks_per_step(n_ctx, n_all):
    for cps in (4, 2, 1):
        if n_ctx % cps == 0 and (n_all - n_ctx) % cps == 0:
            return cps


def _mlstm_stages(qf_ref, qb_ref, vf_ref, vb_ref, gf_ref, gb_ref, gbias_ref,
                  yf_ref, yb_ref, c_scr, n_scr, m_scr, tri_scr, exp_scr, bdb_scr, *, cps):
    i = pl.program_id(1)
    L = CHUNK
    W = GROUP_W

    @pl.when(i == 0)
    def _():
        c_scr[...] = jnp.zeros_like(c_scr)
        n_scr[...] = jnp.zeros_like(n_scr)
        m_scr[...] = jnp.zeros_like(m_scr)
        _group_tri_masks(tri_scr, cps * L)
        bdb_scr[...] = _block_diag_mask().astype(BF16)
        ci = _iota((W, W), 0)
        cj = _iota((W, W), 1) // HEAD_DIM
        for d in range(2):
            col = GATE_LANE0 + d * 2 * N_HEADS
            exp_scr[d, :, 0:W] = (ci == col + cj).astype(BF16)
            exp_scr[d, :, W:2 * W] = (ci == col + N_HEADS + cj).astype(BF16)

    hms = _head_masks()
    bd = _block_diag_mask()
    bdb = bdb_scr[...]
    rs = lambda x: _row_stack(x, hms)
    neg_inf = -jnp.inf

    def sum2(x, eb):
        hi, lo = _split2(x)
        return _nn(hi, eb) + _nn(lo, eb)

    per_dir = []
    for d, (q_ref, v_ref, g_ref) in enumerate(((qf_ref, vf_ref, gf_ref), (qb_ref, vb_ref, gb_ref))):
        qk = q_ref[0]
        g = g_ref[0] + gbias_ref[...]
        strict, incl, eye = _chunk_masks(d)
        tri = tri_scr[d]
        gates = sum2(g, exp_scr[d])
        li = gates[:, 0:W]
        lf = -_softplus(-gates[:, W:2 * W])
        lf_hi, lf_lo = _split2(lf)
        b2 = _nn(tri, jnp.concatenate([lf_hi, lf_lo], axis=1))
        per_dir.append(dict(q=qk[:, 0:W], k=qk[:, W:2 * W] * (HEAD_DIM ** -0.5), v=v_ref[0][:, 0:W],
                            li=li, bcol=b2[:, 0:W] + b2[:, W:2 * W], incl=incl, eye=eye))
        yield

    chains = []
    for c in range(cps):
        for d in range(2):
            pd = per_dir[d]
            sl = slice(c * L, (c + 1) * L)
            q, k, v, li, bcol = pd["q"][sl], pd["k"][sl], pd["v"][sl], pd["li"][sl], pd["bcol"][sl]
            brow = jnp.sum(jnp.where(pd["eye"], bcol, 0.0), axis=0, keepdims=True)
            lirow = jnp.sum(jnp.where(pd["eye"], li, 0.0), axis=0, keepdims=True)
            logd = jnp.where(pd["incl"], bcol - brow + lirow, neg_inf)
            mx = jnp.zeros((L, W), F32)
            for hm in hms:
                mh = jnp.max(jnp.where(hm, logd, neg_inf), axis=1, keepdims=True)
                mx = jnp.where(hm, mh, mx)
            blast = bcol[L - 1:L, :] if d == 0 else bcol[0:1, :]
            lwc = blast - bcol + li
            mlw = jnp.max(lwc, axis=0, keepdims=True)
            kw = k * jnp.exp(lwc - mlw)
            chains.append(dict(d=d, c=c, q=q, v=v, bcol=bcol, mx=mx, blast=blast, mlw=mlw, kw=kw,
                               dexp=jnp.exp(logd - mx), rsk=rs(k), rsv=rs(v),
                               nu0=jnp.sum(kw, axis=0, keepdims=True)))
        yield
    for ch in chains:
        ch["sp"] = _mm1(_nt, ch["q"], ch["rsk"]) * ch["dexp"]
        ch["cu0"] = bd * _mm1(_tn, ch["v"], ch["kw"])
    yield
    for ch in chains:
        ch["num0"] = _mm1(_nn, ch["sp"], ch["rsv"])
        ch["den0"] = _mm1(_nn, ch["sp"], bdb)
    yield

    by_key = {(ch["d"], ch["c"]): ch for ch in chains}
    c_st = [c_scr[0], c_scr[1]]
    n_row = [n_scr[0, 0:1, :], n_scr[1, 0:1, :]]
    m_row = [m_scr[0, 0:1, :], m_scr[1, 0:1, :]]
    ys = [[None] * cps, [None] * cps]
    for step in range(cps):
        for d in range(2):
            c = step if d == 0 else cps - 1 - step
            ch = by_key[(d, c)]
            inter = ch["bcol"] + m_row[d]
            mt = jnp.maximum(inter, ch["mx"])
            f_in = jnp.exp(ch["mx"] - mt)
            w_int = jnp.exp(inter - mt)
            num = f_in * ch["num0"] + w_int * _mm1(_nt, ch["q"], c_st[d])
            den = f_in * ch["den0"] + w_int * _mm1(_nn, ch["q"] * n_row[d], bdb)
            ys[d][c] = num / jnp.maximum(jnp.abs(den), jnp.exp(-mt))
            m_new = jnp.maximum(ch["blast"] + m_row[d], ch["mlw"])
            sc = jnp.exp(ch["blast"] + m_row[d] - m_new)
            e2 = jnp.exp(ch["mlw"] - m_new)
            c_st[d] = sc * c_st[d] + e2 * ch["cu0"]
            n_row[d] = sc * n_row[d] + e2 * ch["nu0"]
            m_row[d] = m_new
        yield
    for d, y_ref in enumerate((yf_ref, yb_ref)):
        c_scr[d] = c_st[d]
        n_scr[d, 0:1, :] = n_row[d]
        m_scr[d, 0:1, :] = m_row[d]
        y_ref[0] = jnp.concatenate(ys[d], axis=0)


def _recur_kernel(*refs, cps):
    r_in, m_in = refs[0:12], refs[12:19]
    r_out, m_out = refs[19:23], refs[23:25]
    r_scr, m_scr = refs[25:28], refs[28:34]
    _run_stages(_rwkv_stages(*r_in, *r_out, *r_scr, cps=cps),
                _mlstm_stages(*m_in, *m_out, *m_scr, cps=cps))


def _recur_call(z, wup, aup, gup, w0, a0, kk, ka, rk, gbias, *, n_ctx, n_all):
    B, Ta, _ = z.shape
    W = GROUP_W
    cps = _chunks_per_step(n_ctx, n_all)
    rows = cps * CHUNK
    gc, ga = n_ctx // cps, n_all // cps
    full = lambda a: pl.BlockSpec(a.shape, lambda b, i: (0,) * a.ndim)
    fwd = lambda blk: (lambda b, i: (b, i, blk))
    bwd = lambda blk: (lambda b, i: (b, _bwd_chunk(i, gc, ga), blk))
    tok = lambda width, imap: pl.BlockSpec((1, rows, width), imap)
    out = jax.ShapeDtypeStruct((B, Ta, W), F32)
    return pl.pallas_call(
        functools.partial(_recur_kernel, cps=cps),
        out_shape=(out,) * 6,
        grid=(B, ga),
        in_specs=[tok(3 * W, fwd(Z_RKV // 3)), tok(3 * W, bwd(Z_RKV // 3)),
                  tok(W, fwd(Z_LORA)), tok(W, bwd(Z_LORA)),
                  full(wup), full(aup), full(gup), full(w0), full(a0), full(kk), full(ka), full(rk),
                  tok(2 * W, fwd(Z_MQK // 2)), tok(2 * W, bwd(Z_MQK // 2)),
                  tok(2 * W, fwd(Z_MVO // 2)), tok(2 * W, bwd(Z_MVO // 2)),
                  tok(W, fwd(Z_GATE)), tok(W, bwd(Z_GATE)), full(gbias)],
        out_specs=(tok(W, fwd(0)), tok(W, bwd(0)), tok(W, fwd(0)), tok(W, fwd(0)),
                   tok(W, fwd(0)), tok(W, bwd(0))),
        scratch_shapes=[pltpu.VMEM((2, W, W), F32), pltpu.VMEM((2, rows, rows), BF16), pltpu.VMEM((W, W), BF16),
                        pltpu.VMEM((2, W, W), F32), pltpu.VMEM((2, 8, W), F32), pltpu.VMEM((2, 8, W), F32),
                        pltpu.VMEM((2, rows, rows), BF16), pltpu.VMEM((2, W, 2 * W), BF16),
                        pltpu.VMEM((W, W), BF16)],
        compiler_params=_cparams("parallel", "arbitrary"),
        name="rwkv_mlstm",
    )(z, z, z, z, wup, aup, gup, w0, a0, kk, ka, rk, z, z, z, z, z, z, gbias)


def _head_norm(y, bd, eps):
    bdb = bd.astype(BF16)

    def head_mean(x):
        hi, lo = _split2(x)
        return (_nn(hi, bdb) + _nn(lo, bdb)) * (1.0 / HEAD_DIM)

    yc = y - head_mean(y)
    return yc * lax.rsqrt(head_mean(yc * yc) + eps)


def _outproj_kernel(x_ref, mods_ref, s5f_ref, s5b_ref, u_ref, at_ref, rf_ref, rb_ref, rg_ref, rbon_ref,
                    mf_ref, mb_ref, vo_ref, s5d_ref, wglu_ref, bglu_ref, lnw_ref, lnb_ref, mnw_ref,
                    gpost_ref, wout_ref, o_ref, *, tm, tc, n_batch):
    b = pl.program_id(0)
    t = pl.program_id(1)
    d = x_ref.shape[-1]
    W = GROUP_W
    is_ctx = (t * tm + _iota((tm, 1), 0)) < tc
    gate = jnp.where(is_ctx, mods_ref[pl.ds(n_batch, 1), pl.ds(5 * d, d)],
                     mods_ref[pl.ds(b, 1), pl.ds(5 * d, d)])
    bd = _block_diag_mask()

    y = s5f_ref[...] + s5b_ref[...] + s5d_ref[...] * u_ref[0]
    zg = 0.5 * y * (1.0 + jnp.tanh(math.sqrt(2.0 / math.pi) * (y + 0.044715 * (y * y * y))))
    s5o = zg * _sigmoid(_mm1(_nn, zg, wglu_ref[...]) + bglu_ref[...])

    yr = _head_norm(rf_ref[0] + rb_ref[0], bd, RWKV_GN_EPS)
    rwo = (yr * lnw_ref[...] + lnb_ref[...] + rbon_ref[0]) * rg_ref[0]

    ym = _head_norm(mf_ref[0] + mb_ref[0], bd, NORM_EPS)
    mlo = ym * mnw_ref[...] * _sigmoid(vo_ref[0])

    cat = jnp.concatenate([s5o.astype(BF16), at_ref[0], rwo.astype(BF16), mlo.astype(BF16)], axis=1)
    yx = _nn(cat, wout_ref[...])
    o_ref[0] = x_ref[0] + gate * _rms(yx, gpost_ref[...])


def _outproj_call(xa, mods, s5f, s5b, z, attn, rf, rb, rg, rbon, mf, mb,
                  s5d, wglu, bglu, lnw, lnb, mnw, gpost, wout, *, tc):
    B, Ta, D = xa.shape
    W = GROUP_W
    tm = _pick_tile(Ta, 544)
    kern = functools.partial(_outproj_kernel, tm=tm, tc=tc, n_batch=B)
    full = lambda a: pl.BlockSpec(a.shape, lambda b, t: (0,) * a.ndim)
    tok = pl.BlockSpec((1, tm, W), lambda b, t: (b, t, 0))
    tmaj = pl.BlockSpec((tm, W), lambda b, t: (t, b))
    return pl.pallas_call(
        kern,
        out_shape=jax.ShapeDtypeStruct((B, Ta, D), F32),
        grid=(B, Ta // tm),
        in_specs=[pl.BlockSpec((1, tm, D), lambda b, t: (b, t, 0)), full(mods),
                  tmaj, tmaj, pl.BlockSpec((1, tm, W), lambda b, t: (b, t, Z_S5)),
                  tok, tok, tok, tok, tok, tok, tok,
                  pl.BlockSpec((1, tm, W), lambda b, t: (b, t, Z_MVO + 1)),
                  full(s5d), full(wglu), full(bglu), full(lnw), full(lnb), full(mnw), full(gpost), full(wout)],
        out_specs=pl.BlockSpec((1, tm, D), lambda b, t: (b, t, 0)),
        compiler_params=_cparams("parallel", "parallel"),
        name="mix_out",
    )(xa, mods, s5f, s5b, z, attn, rf, rb, rg, rbon, mf, mb, z, s5d, wglu, bglu, lnw, lnb, mnw, gpost, wout)


def _rope_rotate_cols(w):
    h = ROPE_AXIS // 2
    return jnp.concatenate([-w[..., h:2 * h], w[..., 0:h], -w[..., 3 * h:4 * h], w[..., 2 * h:3 * h]], axis=-1)


def _inproj_relayout(w_in):
    L, D, _ = w_in.shape
    o_s5, o_mla, o_rw, o_ml = 0, 256, 672, 1568
    seg = lambda a, n: w_in[:, :, a:a + n]
    zer = lambda n: jnp.zeros((L, D, n), w_in.dtype)
    k_rope = seg(o_mla + 384, ROPE_DIM)
    parts = [seg(o_mla, 256), seg(o_mla + 256, 128),
             zer(64), k_rope, zer(32),
             zer(64), _rope_rotate_cols(k_rope), zer(32), seg(o_ml + 1024, 16), zer(128 - 16),
             seg(o_rw, 768), seg(o_s5, 256),
             seg(o_rw + 768, 128), zer(128),
             seg(o_ml, 512), seg(o_ml + 512, 512)]
    out = jnp.concatenate(parts, axis=2)
    assert out.shape[2] == Z_COLS
    return out


def _rope_tables(T, tc):
    rows = T // GRID_W
    r_idx, c_idx = jnp.meshgrid(jnp.arange(rows), jnp.arange(GRID_W), indexing='ij')
    inv_freq = 1.0 / (ROPE_BASE ** (jnp.arange(0, ROPE_AXIS, 2, dtype=F32) / ROPE_AXIS))
    ang_r = r_idx.reshape(-1, 1).astype(F32) * inv_freq
    ang_c = c_idx.reshape(-1, 1).astype(F32) * inv_freq
    ang = jnp.concatenate([ang_r, ang_r, ang_c, ang_c], axis=-1)
    cos = jnp.concatenate([jnp.ones((tc, ROPE_DIM), F32), jnp.cos(ang)], axis=0)
    sin = jnp.concatenate([jnp.zeros((tc, ROPE_DIM), F32), jnp.sin(ang)], axis=0)
    ta = T + tc
    cos_t = jnp.concatenate([jnp.ones((ta, 64), F32), cos, jnp.zeros((ta, 32), F32)], axis=1)
    sin_t = jnp.concatenate([jnp.zeros((ta, 64), F32), sin, jnp.zeros((ta, 32), F32)], axis=1)
    return cos_t, sin_t


def _pad_rows(w, r0, total):
    pad = [(0, 0)] * (w.ndim - 2) + [(r0, total - r0 - w.shape[-2]), (0, 0)]
    return jnp.pad(w, pad)


def kernel(x, c, ctx, c_ctx, w_ada, b_ada, norm_pre, norm_post, ffn_w_gate, ffn_w_up, ffn_w_down, w_in, w_out, s5_lam_re, s5_lam_im, s5_log_dt, s5_b_re, s5_b_im, s5_c_re, s5_c_im, s5_d, s5_w_glu, s5_b_glu, mla_q_norm, mla_kv_norm, mla_w_uq, mla_w_ukv, rwkv_conv_w, rwkv_conv_b, rwkv_w0, rwkv_w_up, rwkv_a0, rwkv_a_up, rwkv_g_up, rwkv_k_k, rwkv_k_a, rwkv_r_k, rwkv_ln_w, rwkv_ln_b, mlstm_conv_w, mlstm_conv_b, mlstm_gate_b, mlstm_norm):
    B, T, D = x.shape
    Tc = ctx.shape[1]
    Ta = T + Tc
    L = w_ada.shape[0]
    W = GROUP_W
    assert T % CHUNK == 0 and Tc % CHUNK == 0 and B % 8 == 0 and B <= 8
    n_ctx, n_all = Tc // CHUNK, Ta // CHUNK

    rows = 16
    cvec = jnp.concatenate([c, c_ctx[None, :], jnp.zeros((rows - B - 1, D), F32)], axis=0)
    mods_all = _ada_call(cvec, w_ada, b_ada)

    w_in_re = _inproj_relayout(w_in.astype(BF16))
    cw = jnp.zeros((L, Z_NBLK, 8, ZB), F32)
    rc = jnp.concatenate([rwkv_conv_w, rwkv_conv_b[:, None, :]], axis=1).reshape(L, 4, 3, ZB).transpose(0, 2, 1, 3)
    mc = jnp.concatenate([mlstm_conv_w, mlstm_conv_b[:, None, :]], axis=1).reshape(L, 4, 2, ZB).transpose(0, 2, 1, 3)
    cw = cw.at[:, Z_RKV:Z_RKV + 3, 0:4].set(rc).at[:, Z_MQK:Z_MQK + 2, 0:4].set(mc)

    wg = ffn_w_gate.astype(BF16)
    wu = ffn_w_up.astype(BF16)
    wd = ffn_w_down.astype(BF16)
    wout = w_out.astype(BF16)

    G = s5_lam_re.shape[2]
    N = s5_lam_re.shape[3]
    eye_g = jnp.eye(G, dtype=F32)
    lre = s5_lam_re.reshape(L, 2, 1, G * N)
    lim = s5_lam_im.reshape(L, 2, 1, G * N)
    ldt = jnp.repeat(s5_log_dt, N, axis=-1).reshape(L, 2, 1, G * N)
    wbre = jnp.einsum('ldgnp,gh->ldgphn', s5_b_re, eye_g).reshape(L, 2, G * S5_P, G * N)
    wbim = jnp.einsum('ldgnp,gh->ldgphn', s5_b_im, eye_g).reshape(L, 2, G * S5_P, G * N)
    wcre = jnp.einsum('ldgpn,gh->ldgnhp', s5_c_re, eye_g).reshape(L, 2, G * N, G * S5_P).astype(BF16)
    wcim = jnp.einsum('ldgpn,gh->ldgnhp', s5_c_im, eye_g).reshape(L, 2, G * N, G * S5_P).astype(BF16)

    nope = HEAD_DIM
    qd = nope + ROPE_DIM
    wq4 = mla_w_uq.reshape(L, -1, N_HEADS, qd)
    wq = jnp.pad(wq4, ((0, 0), (0, 0), (0, 0), (0, 128 - qd))).reshape(L, -1, N_HEADS * 128).astype(BF16)
    wq_rot = _rope_rotate_cols(wq4[..., nope:])
    wqr = jnp.pad(wq_rot, ((0, 0), (0, 0), (0, 0), (nope, 128 - qd))).reshape(L, -1, N_HEADS * 128).astype(BF16)
    wkv4 = mla_w_ukv.reshape(L, -1, N_HEADS, 2 * HEAD_DIM)
    wk = jnp.pad(wkv4[..., :HEAD_DIM], ((0, 0), (0, 0), (0, 0), (0, 64))).reshape(L, -1, N_HEADS * 128).astype(BF16)
    wv = jnp.pad(wkv4[..., HEAD_DIM:], ((0, 0), (0, 0), (0, 0), (0, 64))).reshape(L, -1, N_HEADS * 128).astype(BF16)
    cos_t, sin_t = _rope_tables(T, Tc)
    scale = float(qd) ** -0.5 * math.log2(math.e)

    wup = _pad_rows(rwkv_w_up, 0, W).astype(BF16)
    aup = _pad_rows(rwkv_a_up, 32, W).astype(BF16)
    gup = _pad_rows(rwkv_g_up, 64, W).astype(BF16)
    gbias = jnp.pad(mlstm_gate_b, ((0, 0), (GATE_LANE0, W - GATE_LANE0 - mlstm_gate_b.shape[1])))

    xa = jnp.concatenate([ctx, x], axis=1)
    r1 = lambda a: a.reshape(1, -1)

    for l in range(L):
        mods = mods_all[l]
        xa2, hmix = _ffn_call(xa.reshape(B * Ta, D), mods, r1(norm_pre[l, 0]), r1(norm_post[l, 0]),
                              r1(norm_pre[l, 1]), wg[l, 0], wu[l, 0], wd[l, 0],
                              ta=Ta, tc=Tc, n_batch=B, koff=0, emit_hmix=True)
        xa = xa2.reshape(B, Ta, D)
        z = _inproj_call(hmix.reshape(B, Ta, D), w_in_re[l], cw[l], tc=Tc)

        s5f, s5b = _s5_call(z, lre[l], lim[l], ldt[l], wbre[l], wbim[l], wcre[l], wcim[l],
                            n_ctx=n_ctx, n_all=n_all)

        q, k, v = _mla_proj_call(z, r1(mla_q_norm[l]), r1(mla_kv_norm[l]), wq[l], wqr[l], wk[l], wv[l],
                                 cos_t, sin_t, scale=scale)
        attn = _attn_call(q, k, v, tc=Tc)

        rf, rb, rg, rbon, mf, mb = _recur_call(
            z, wup[l], aup[l], gup[l], rwkv_w0[l][:, None, :], rwkv_a0[l][:, None, :],
            r1(rwkv_k_k[l]), r1(rwkv_k_a[l]), r1(rwkv_r_k[l]), gbias[l:l + 1], n_ctx=n_ctx, n_all=n_all)

        xa = _outproj_call(xa, mods, s5f, s5b, z, attn, rf, rb, rg, rbon,
                           mf, mb, r1(s5_d[l]), s5_w_glu[l].astype(BF16), r1(s5_b_glu[l]), r1(rwkv_ln_w[l]),
                           r1(rwkv_ln_b[l]), r1(mlstm_norm[l]), r1(norm_post[l, 1]), wout[l], tc=Tc)

        xa2, _ = _ffn_call(xa.reshape(B * Ta, D), mods, r1(norm_pre[l, 2]), r1(norm_post[l, 2]),
                           r1(norm_pre[l, 1]), wg[l, 1], wu[l, 1], wd[l, 1],
                           ta=Ta, tc=Tc, n_batch=B, koff=6, emit_hmix=False)
        xa = xa2.reshape(B, Ta, D)

    return xa[:, Tc:, :]
```
